```python
import math
import jax
import jax.numpy as jnp
from jax import lax
import numpy as np

D_MODEL = 1024
BATCH = 16
SEQ = 2048
DEPTH = 2

EPS = 1e-6
PLE_DIM = 256
N_EVEN = (DEPTH + 1) // 2
N_ODD = DEPTH // 2

MOBA_HEADS = 8
MOBA_HEAD_DIM = 64
MOBA_WIDTH = MOBA_HEADS * MOBA_HEAD_DIM
MOBA_BLOCK = 256
MOBA_TOPK = 3
MOBA_QBLOCK = 128
ALIBI_MAX_BIAS = 8.0

SSM_INNER = D_MODEL
SSM_HEAD_DIM = 64
SSM_HEADS = SSM_INNER // SSM_HEAD_DIM
SSM_STATE = 128
SSM_GROUPS = 2
SSM_CONV = 4
SSM_CHUNK = 128
SSM_CONV_DIM = SSM_INNER + 2 * SSM_GROUPS * SSM_STATE

EVEN_IN = 3 * MOBA_WIDTH + SSM_INNER + SSM_CONV_DIM + SSM_HEADS
EVEN_MIX = MOBA_WIDTH + SSM_INNER

GDN_K_HEADS = 8
GDN_V_HEADS = 16
GDN_KEY_DIM = 128
GDN_VALUE_DIM = 128
GDN_QK_WIDTH = GDN_K_HEADS * GDN_KEY_DIM
GDN_V_WIDTH = GDN_V_HEADS * GDN_VALUE_DIM
GDN_CONV = 4
GDN_CHUNK = 64
GDN_CONV_DIM = 2 * GDN_QK_WIDTH + GDN_V_WIDTH
ODD_IN = GDN_CONV_DIM + GDN_V_WIDTH + 2 * GDN_V_HEADS

D_FF = 2816
FFN_CONV = 3

kernel_name = 'hybrid_moba_mamba2_gdn_convffn_ple'


def rmsnorm(x, gain):
    xf = x.astype(jnp.float32)
    y = xf * lax.rsqrt(jnp.mean(xf * xf, axis=-1, keepdims=True) + EPS)
    return (y * gain.astype(jnp.float32)).astype(x.dtype)


def l2norm(x):
    return x * lax.rsqrt(jnp.sum(x * x, axis=-1, keepdims=True) + EPS)


def causal_dwconv(x, w):
    k, c = w.shape
    return lax.conv_general_dilated(
        x, w.astype(x.dtype)[:, None, :], window_strides=(1,), padding=[(k - 1, 0)],
        dimension_numbers=('NWC', 'WIO', 'NWC'), feature_group_count=c)


def moba_attention(q, k, v, slopes):
    s_len, h, dh = q.shape
    nb = -(-s_len // MOBA_BLOCK)
    pad = nb * MOBA_BLOCK - s_len
    kb = jnp.pad(k, ((0, pad), (0, 0), (0, 0))).reshape(nb, MOBA_BLOCK, h, dh).transpose(2, 0, 1, 3)
    vb = jnp.pad(v, ((0, pad), (0, 0), (0, 0))).reshape(nb, MOBA_BLOCK, h, dh).transpose(2, 0, 1, 3)
    scale = dh ** -0.5
    n_sel = min(MOBA_TOPK, nb - 1)
    blk_of = jnp.arange(s_len) // MOBA_BLOCK
    heads = jnp.arange(h)
    if n_sel > 0:
        counts = jnp.minimum(MOBA_BLOCK, s_len - jnp.arange(nb) * MOBA_BLOCK).astype(jnp.float32)
        kbar = kb.astype(jnp.float32).sum(axis=2) / counts[None, :, None]
        gate = jnp.einsum('shd,hnd->hsn', q.astype(jnp.float32), kbar)
        past = jnp.arange(nb)[None, :] < blk_of[:, None]
        gate = jnp.where(past[None], gate, -jnp.inf)
        _, sel = lax.top_k(gate, n_sel)
        sel_valid = sel < blk_of[None, :, None]

    def one_block(c):
        q0 = c * MOBA_QBLOCK
        qc = lax.dynamic_slice_in_dim(q, q0, MOBA_QBLOCK, axis=0)
        qpos = q0 + jnp.arange(MOBA_QBLOCK)
        own = q0 // MOBA_BLOCK
        k_own = lax.dynamic_index_in_dim(kb, own, axis=1, keepdims=False)
        v_own = lax.dynamic_index_in_dim(vb, own, axis=1, keepdims=False)
        dist_own = (qpos[:, None] - (own * MOBA_BLOCK + jnp.arange(MOBA_BLOCK))[None, :]).astype(jnp.float32)
        s_own = jnp.einsum('qhd,hkd->hqk', qc, k_own, preferred_element_type=jnp.float32) * scale \
            - slopes[:, None, None] * dist_own
        s_own = jnp.where(dist_own >= 0, s_own, -jnp.inf)
        if n_sel == 0:
            pr = jax.nn.softmax(s_own, axis=-1).astype(v.dtype)
            return jnp.einsum('hqk,hkd->qhd', pr, v_own)
        sel_c = lax.dynamic_slice_in_dim(sel, q0, MOBA_QBLOCK, axis=1)
        valid_c = lax.dynamic_slice_in_dim(sel_valid, q0, MOBA_QBLOCK, axis=1)
        k_sel = kb[heads[:, None, None], sel_c]
        v_sel = vb[heads[:, None, None], sel_c]
        kpos = sel_c[..., None] * MOBA_BLOCK + jnp.arange(MOBA_BLOCK)
        dist_sel = (qpos[None, :, None, None] - kpos).astype(jnp.float32)
        s_sel = jnp.einsum('qhd,hqjkd->hqjk', qc, k_sel, preferred_element_type=jnp.float32) * scale \
            - slopes[:, None, None, None] * dist_sel
        s_sel = jnp.where(valid_c[..., None], s_sel, -jnp.inf)
        s_all = jnp.concatenate([s_own, s_sel.reshape(h, MOBA_QBLOCK, n_sel * MOBA_BLOCK)], axis=-1)
        pr = jax.nn.softmax(s_all, axis=-1).astype(v.dtype)
        p_own = pr[..., :MOBA_BLOCK]
        p_sel = pr[..., MOBA_BLOCK:].reshape(h, MOBA_QBLOCK, n_sel, MOBA_BLOCK)
        return jnp.einsum('hqk,hkd->qhd', p_own, v_own) + jnp.einsum('hqjk,hqjkd->qhd', p_sel, v_sel)

    out = lax.map(one_block, jnp.arange(s_len // MOBA_QBLOCK))
    return out.reshape(s_len, h, dh)


def ssd_scan(x, dta, bm, cm):
    b, s_len, h, p = x.shape
    g, n = bm.shape[2], bm.shape[3]
    r = h // g
    L = SSM_CHUNK
    nc = s_len // L
    xc = x.reshape(b, nc, L, g, r, p)
    bc = bm.reshape(b, nc, L, g, n)
    cc = cm.reshape(b, nc, L, g, n)
    a = dta.reshape(b, nc, L, g, r).transpose(0, 3, 4, 1, 2)
    acum = jnp.cumsum(a, axis=-1)
    causal = jnp.tril(jnp.ones((L, L), dtype=bool))
    decay = jnp.exp(jnp.where(causal, acum[..., :, None] - acum[..., None, :], -jnp.inf))
    cb = jnp.einsum('bclgn,bcsgn->bgcls', cc, bc)
    y_diag = jnp.einsum('bgrcls,bcsgrp->bclgrp', cb[:, :, None] * decay, xc)
    decay_states = jnp.exp(acum[..., -1:] - acum)
    states = jnp.einsum('bcsgn,bgrcs,bcsgrp->bcgrpn', bc, decay_states, xc)
    chunk_decay = jnp.exp(acum[..., -1])

    def step(hs, inp):
        st, dec = inp
        return hs * dec[..., None, None] + st, hs

    h0 = jnp.zeros((b, g, r, p, n), jnp.float32)
    _, prev = lax.scan(step, h0, (jnp.moveaxis(states, 1, 0), jnp.moveaxis(chunk_decay, 3, 0)))
    y_off = jnp.einsum('bclgn,cbgrpn,bgrcl->bclgrp', cc, prev, jnp.exp(acum))
    return (y_diag + y_off).reshape(b, s_len, h, p)


def mamba2_heads(z, xbc, dt_raw, conv_w, conv_b, dt_bias, a_log, d_skip, norm_w):
    b, s_len, _ = z.shape
    xbc = jax.nn.silu(causal_dwconv(xbc, conv_w) + conv_b.astype(xbc.dtype))
    xs, bm, cm = jnp.split(xbc, [SSM_INNER, SSM_INNER + SSM_GROUPS * SSM_STATE], axis=-1)
    xs = xs.reshape(b, s_len, SSM_HEADS, SSM_HEAD_DIM).astype(jnp.float32)
    bm = bm.reshape(b, s_len, SSM_GROUPS, SSM_STATE).astype(jnp.float32)
    cm = cm.reshape(b, s_len, SSM_GROUPS, SSM_STATE).astype(jnp.float32)
    dt = jax.nn.softplus(dt_raw.astype(jnp.float32) + dt_bias.astype(jnp.float32))
    a = -jnp.exp(a_log.astype(jnp.float32))
    y = ssd_scan(xs * dt[..., None], dt * a, bm, cm) + d_skip.astype(jnp.float32)[:, None] * xs
    yg = (y.reshape(b, s_len, SSM_INNER) * jax.nn.silu(z.astype(jnp.float32)))
    yg = yg.reshape(b, s_len, SSM_GROUPS, SSM_INNER // SSM_GROUPS)
    yg = yg * lax.rsqrt(jnp.mean(yg * yg, axis=-1, keepdims=True) + EPS)
    return (yg.reshape(b, s_len, SSM_INNER) * norm_w.astype(jnp.float32)).astype(z.dtype)


def even_mixer(h, w_in, q_norm, k_norm, conv_w, conv_b, dt_bias, a_log, d_skip, ssm_norm, w_out, slopes):
    b, s_len, _ = h.shape
    proj = h @ w_in
    cuts = [MOBA_WIDTH, 2 * MOBA_WIDTH, 3 * MOBA_WIDTH, 3 * MOBA_WIDTH + SSM_INNER,
            3 * MOBA_WIDTH + SSM_INNER + SSM_CONV_DIM]
    q, k, v, z, xbc, dt_raw = jnp.split(proj, cuts, axis=-1)
    q = rmsnorm(q.reshape(b, s_len, MOBA_HEADS, MOBA_HEAD_DIM), q_norm)
    k = rmsnorm(k.reshape(b, s_len, MOBA_HEADS, MOBA_HEAD_DIM), k_norm)
    v = v.reshape(b, s_len, MOBA_HEADS, MOBA_HEAD_DIM)
    attn = lax.map(lambda t: moba_attention(t[0], t[1], t[2], slopes), (q, k, v))
    ssm = mamba2_heads(z, xbc, dt_raw, conv_w, conv_b, dt_bias, a_log, d_skip, ssm_norm)
    return jnp.concatenate([attn.reshape(b, s_len, MOBA_WIDTH), ssm], axis=-1) @ w_out


def gated_delta_rule(q, k, v, g, beta):
    b, s_len, h, dk = q.shape
    dv = v.shape[-1]
    L = GDN_CHUNK
    nc = s_len // L
    def chunks(t):
        return jnp.moveaxis(t.reshape(b, nc, L, h, *t.shape[3:]), 3, 1)
    qc = chunks(q * dk ** -0.5)
    kc = chunks(k)
    vc = chunks(v)
    bc = chunks(beta)
    gc = jnp.cumsum(chunks(g), axis=-1)
    tril = jnp.tril(jnp.ones((L, L), dtype=bool))
    strict = jnp.tril(jnp.ones((L, L), dtype=bool), -1)
    decay = jnp.exp(jnp.where(tril, gc[..., :, None] - gc[..., None, :], -jnp.inf))
    kbeta = kc * bc[..., None]
    a_mat = jnp.where(strict, jnp.einsum('bhcid,bhcjd->bhcij', kbeta, kc) * decay, 0.0)
    rhs = jnp.concatenate([vc * bc[..., None], kbeta * jnp.exp(gc)[..., None]], axis=-1)
    sol = lax.linalg.triangular_solve(a_mat, rhs, left_side=True, lower=True, unit_diagonal=True)
    u = sol[..., :dv]
    w = sol[..., dv:]
    qk = jnp.einsum('bhcid,bhcjd->bhcij', qc, kc) * decay
    q_dec = qc * jnp.exp(gc)[..., None]
    g_last = gc[..., -1]
    k_dec = kc * jnp.exp(g_last[..., None] - gc)[..., None]

    def step(st, inp):
        qk_i, qd_i, w_i, u_i, kd_i, gl_i = inp
        v_new = u_i - jnp.einsum('bhld,bhdv->bhlv', w_i, st)
        o = jnp.einsum('bhld,bhdv->bhlv', qd_i, st) + jnp.einsum('bhij,bhjv->bhiv', qk_i, v_new)
        st = st * jnp.exp(gl_i)[..., None, None] + jnp.einsum('bhld,bhlv->bhdv', kd_i, v_new)
        return st, o

    xs = tuple(jnp.moveaxis(t, 2, 0) for t in (qk, q_dec, w, u, k_dec, g_last))
    _, o = lax.scan(step, jnp.zeros((b, h, dk, dv), jnp.float32), xs)
    return o.transpose(1, 0, 3, 2, 4).reshape(b, s_len, h, dv)


def gdn_mixer(h, w_in, conv_w, dt_bias, a_log, norm_w, w_out):
    b, s_len, _ = h.shape
    proj = h @ w_in
    qkv, z, beta_raw, a_raw = jnp.split(
        proj, [GDN_CONV_DIM, GDN_CONV_DIM + GDN_V_WIDTH, GDN_CONV_DIM + GDN_V_WIDTH + GDN_V_HEADS], axis=-1)
    qkv = jax.nn.silu(causal_dwconv(qkv, conv_w)).astype(jnp.float32)
    q, k, v = jnp.split(qkv, [GDN_QK_WIDTH, 2 * GDN_QK_WIDTH], axis=-1)
    rep = GDN_V_HEADS // GDN_K_HEADS
    q = jnp.repeat(l2norm(q.reshape(b, s_len, GDN_K_HEADS, GDN_KEY_DIM)), rep, axis=2)
    k = jnp.repeat(l2norm(k.reshape(b, s_len, GDN_K_HEADS, GDN_KEY_DIM)), rep, axis=2)
    v = v.reshape(b, s_len, GDN_V_HEADS, GDN_VALUE_DIM)
    beta = jax.nn.sigmoid(beta_raw.astype(jnp.float32))
    g = -jnp.exp(a_log.astype(jnp.float32)) * jax.nn.softplus(a_raw.astype(jnp.float32) + dt_bias.astype(jnp.float32))
    o = gated_delta_rule(q, k, v, g, beta)
    o = o * lax.rsqrt(jnp.mean(o * o, axis=-1, keepdims=True) + EPS) * norm_w.astype(jnp.float32)
    o = o * jax.nn.silu(z.astype(jnp.float32).reshape(b, s_len, GDN_V_HEADS, GDN_VALUE_DIM))
    return o.reshape(b, s_len, GDN_V_WIDTH).astype(h.dtype) @ w_out


def conv_ffn(h, w_gate, w_up, conv_w, conv_b, w_down):
    gte = causal_dwconv(h @ w_gate, conv_w) + conv_b.astype(h.dtype)
    return (jax.nn.silu(gte) * (h @ w_up)) @ w_down


def per_layer_embedding(x, p_i, norm_w, w_proj, w_gate):
    return jax.nn.sigmoid(rmsnorm(x, norm_w) @ w_gate) * (p_i @ w_proj)


def setup_inputs(seed: int = 0) -> dict:
    key = jax.random.key(seed)
    ks = iter(jax.random.split(key, 40))
    f32 = jnp.float32

    def normal(shape, s=1.0):
        return s * jax.random.normal(next(ks), shape, f32)

    def dense(shape):
        return normal(shape, shape[-2] ** -0.5)

    def gain(shape):
        return 1.0 + normal(shape, 0.02)

    def dt_bias(shape):
        u = jax.random.uniform(next(ks), shape, f32)
        dt = jnp.exp(u * (math.log(0.1) - math.log(1e-3)) + math.log(1e-3))
        return dt + jnp.log(-jnp.expm1(-dt))

    def a_log(shape):
        return jnp.log(jax.random.uniform(next(ks), shape, f32, 1.0, 16.0))

    return {
        'x': normal((BATCH, SEQ, D_MODEL)),
        'p': normal((DEPTH, BATCH, SEQ, PLE_DIM)),
        'norm_mix': gain((DEPTH, D_MODEL)),
        'norm_ffn': gain((DEPTH, D_MODEL)),
        'norm_ple': gain((DEPTH, D_MODEL)),
        'w_in_even': dense((N_EVEN, D_MODEL, EVEN_IN)),
        'moba_q_norm': gain((N_EVEN, MOBA_HEAD_DIM)),
        'moba_k_norm': gain((N_EVEN, MOBA_HEAD_DIM)),
        'ssm_conv_w': normal((N_EVEN, SSM_CONV, SSM_CONV_DIM), SSM_CONV ** -0.5),
        'ssm_conv_b': normal((N_EVEN, SSM_CONV_DIM), 0.02),
        'ssm_dt_bias': dt_bias((N_EVEN, SSM_HEADS)),
        'ssm_a_log': a_log((N_EVEN, SSM_HEADS)),
        'ssm_d': gain((N_EVEN, SSM_HEADS)),
        'ssm_norm': gain((N_EVEN, SSM_INNER)),
        'w_out_even': dense((N_EVEN, EVEN_MIX, D_MODEL)),
        'w_in_odd': dense((N_ODD, D_MODEL, ODD_IN)),
        'gdn_conv_w': normal((N_ODD, GDN_CONV, GDN_CONV_DIM), GDN_CONV ** -0.5),
        'gdn_dt_bias': dt_bias((N_ODD, GDN_V_HEADS)),
        'gdn_a_log': a_log((N_ODD, GDN_V_HEADS)),
        'gdn_norm': gain((N_ODD, GDN_VALUE_DIM)),
        'w_out_odd': dense((N_ODD, GDN_V_WIDTH, D_MODEL)),
        'ffn_w_gate': dense((DEPTH, D_MODEL, D_FF)),
        'ffn_w_up': dense((DEPTH, D_MODEL, D_FF)),
        'ffn_conv_w': normal((DEPTH, FFN_CONV, D_FF), FFN_CONV ** -0.5),
        'ffn_conv_b': normal((DEPTH, D_FF), 0.02),
        'ffn_w_down': dense((DEPTH, D_FF, D_MODEL)),
        'ple_w_proj': dense((DEPTH, PLE_DIM, D_MODEL)),
        'ple_w_gate': dense((DEPTH, D_MODEL, D_MODEL)),
    }


def reference(x, p, norm_mix, norm_ffn, norm_ple, w_in_even, moba_q_norm, moba_k_norm,
              ssm_conv_w, ssm_conv_b, ssm_dt_bias, ssm_a_log, ssm_d, ssm_norm, w_out_even,
              w_in_odd, gdn_conv_w, gdn_dt_bias, gdn_a_log, gdn_norm, w_out_odd,
              ffn_w_gate, ffn_w_up, ffn_conv_w, ffn_conv_b, ffn_w_down, ple_w_proj, ple_w_gate):
    slopes = jnp.exp2(-ALIBI_MAX_BIAS * jnp.arange(1, MOBA_HEADS + 1, dtype=jnp.float32) / MOBA_HEADS)
    for i in range(DEPTH):
        j = i // 2
        h = rmsnorm(x, norm_mix[i])
        if i % 2 == 0:
            x = x + even_mixer(h, w_in_even[j], moba_q_norm[j], moba_k_norm[j], ssm_conv_w[j],
                               ssm_conv_b[j], ssm_dt_bias[j], ssm_a_log[j], ssm_d[j], ssm_norm[j],
                               w_out_even[j], slopes)
        else:
            x = x + gdn_mixer(h, w_in_odd[j], gdn_conv_w[j], gdn_dt_bias[j], gdn_a_log[j],
                              gdn_norm[j], w_out_odd[j])
        h = rmsnorm(x, norm_ffn[i])
        x = x + conv_ffn(h, ffn_w_gate[i], ffn_w_up[i], ffn_conv_w[i], ffn_conv_b[i], ffn_w_down[i])
        x = x + per_layer_embedding(x, p[i], norm_ple[i], ple_w_proj[i], ple_w_gate[i])
    return x
```

```python
import functools

import jax
import jax.numpy as jnp
from jax import lax
from jax.experimental import pallas as pl
from jax.experimental.pallas import tpu as pltpu

F32 = jnp.float32
BF16 = jnp.bfloat16
HIGHEST = lax.Precision.HIGHEST

EPS = 1e-6
NEG = -1e30
LANES = 128
CONV_HALO = 8

MOBA_HEADS = 8
MOBA_HEAD_DIM = 64
MOBA_BLOCK = 256
MOBA_TOPK = 3
ALIBI_MAX_BIAS = 8.0

SSM_HEADS = 16
SSM_HEAD_DIM = 64
SSM_STATE = 128
SSM_GROUPS = 2
SSM_CONV = 4
SSM_CHUNK = 128

GDN_K_HEADS = 8
GDN_V_HEADS = 16
GDN_DIM = 128
GDN_CONV = 4
GDN_CHUNK = 64

FFN_CONV = 3
FFN_HALO = 16
FFN_COLS = 256

VMEM_LIMIT = 56 * 1024 * 1024


def _rms_rows(x, gain):
    return x * lax.rsqrt(jnp.mean(x * x, axis=-1, keepdims=True) + EPS) * gain


def _sigmoid(x):
    return 1.0 / (1.0 + jnp.exp(-x))


def _silu(x):
    return x * _sigmoid(x)


def _softplus(x):
    return jnp.maximum(x, 0.0) + jnp.log(1.0 + jnp.exp(-jnp.abs(x)))


def _dot(a, b):
    return jnp.dot(a, b, preferred_element_type=F32)


def _dot_nt(a, b):
    return lax.dot_general(a, b, (((1,), (1,)), ((), ())), preferred_element_type=F32)


def _dot_exact(a, b):
    return jnp.dot(a, b, precision=HIGHEST, preferred_element_type=F32)


def _const_spec(shape):
    zeros = (0,) * len(shape)
    return pl.BlockSpec(shape, lambda *_: zeros, pipeline_mode=pl.Buffered(1))


def _params(semantics):
    return pltpu.CompilerParams(dimension_semantics=semantics, vmem_limit_bytes=VMEM_LIMIT)


def _causal_conv(ext, w_ref, width, halo):
    y = w_ref[width - 1:width, :] * ext[halo:]
    for shift in range(1, width):
        y = y + w_ref[width - 1 - shift:width - shift, :] * pltpu.roll(ext, shift, axis=0)[halo:]
    return y


def _inproj_even_kernel(x_ref, g_ref, wq_ref, wk_ref, wv_ref, wz_ref, wxbc_ref, wdt_ref,
                        q_ref, k_ref, v_ref, z_ref, xbc_ref, dt_ref):
    hn = _rms_rows(x_ref[...], g_ref[...]).astype(BF16)
    q_ref[...] = _dot(hn, wq_ref[...]).astype(BF16)
    k_ref[...] = _dot(hn, wk_ref[...]).astype(BF16)
    v_ref[...] = _dot(hn, wv_ref[...]).astype(BF16)
    z_ref[...] = _dot(hn, wz_ref[...]).astype(BF16)
    xbc_ref[...] = _dot(hn, wxbc_ref[...]).astype(BF16)
    dt_ref[...] = _dot(hn, wdt_ref[...])


def _inproj_even(x2, gain, wq, wk, wv, wz, wxbc, wdt, tm=512):
    t, d = x2.shape
    outs = [(wq.shape[1], BF16), (wk.shape[1], BF16), (wv.shape[1], BF16),
            (wz.shape[1], BF16), (wxbc.shape[1], BF16), (wdt.shape[1], F32)]
    return pl.pallas_call(
        _inproj_even_kernel,
        grid=(t // tm,),
        in_specs=[pl.BlockSpec((tm, d), lambda i: (i, 0)), _const_spec(gain.shape)]
        + [_const_spec(w.shape) for w in (wq, wk, wv, wz, wxbc, wdt)],
        out_specs=[pl.BlockSpec((tm, n), lambda i: (i, 0)) for n, _ in outs],
        out_shape=[jax.ShapeDtypeStruct((t, n), dt) for n, dt in outs],
        compiler_params=_params(("parallel",)),
        name="inproj_even",
    )(x2, gain, wq, wk, wv, wz, wxbc, wdt)


def _moba_kernel(slopes_ref, q_ref, k_ref, vt_ref, qg_ref, kg_ref, ot_ref, kn_s, kbar_s, selb_s):
    h = pl.program_id(1)
    qi = pl.program_id(2)
    seq = k_ref.shape[2]
    nb = seq // MOBA_BLOCK

    @pl.when(qi == 0)
    def _():
        kn = _rms_rows(k_ref[0, 0].astype(F32), kg_ref[...])
        kn_s[...] = kn.astype(BF16)
        kbar_s[...] = jnp.mean(kn.reshape(nb, MOBA_BLOCK, MOBA_HEAD_DIM), axis=1)

    slope = slopes_ref[h]
    qn = _rms_rows(q_ref[0, 0].astype(F32), qg_ref[...])

    gate = lax.dot_general(kbar_s[...], qn, (((1,), (1,)), ((), ())),
                           precision=HIGHEST, preferred_element_type=F32)
    blk = lax.broadcasted_iota(jnp.int32, gate.shape, 0)
    rank = jnp.zeros(gate.shape, F32)
    for m in range(nb):
        gm = gate[m:m + 1, :]
        beats = (gm > gate) | ((gm == gate) & (m < blk))
        rank = rank + jnp.where(beats, jnp.where(m < qi, 1.0, 0.0), 0.0)
    selected = (blk < qi) & (rank < float(min(MOBA_TOPK, nb - 1)))
    selb_s[...] = jnp.where(selected, 0.0, NEG)

    qs = (qn * (MOBA_HEAD_DIM ** -0.5)).astype(BF16)
    kidx = lax.broadcasted_iota(jnp.int32, (MOBA_BLOCK, MOBA_BLOCK), 0)
    qidx = lax.broadcasted_iota(jnp.int32, (MOBA_BLOCK, MOBA_BLOCK), 1)
    rel = (kidx - qidx).astype(F32)
    alibi = slope * rel

    q0 = pl.multiple_of(qi * MOBA_BLOCK, MOBA_BLOCK)
    s = _dot_nt(kn_s[pl.ds(q0, MOBA_BLOCK), :], qs) + alibi
    s = jnp.where(rel <= 0.0, s, NEG)
    m_run = jnp.max(s, axis=0, keepdims=True)
    p = jnp.exp(s - m_run)
    l_run = jnp.sum(p, axis=0, keepdims=True)
    acc = _dot(vt_ref[0, 0, :, pl.ds(q0, MOBA_BLOCK)], p.astype(BF16))

    def past_block(j, carry):
        m_run, l_run, acc = carry
        k0 = pl.multiple_of(j * MOBA_BLOCK, MOBA_BLOCK)
        offset = slope * ((j - qi) * MOBA_BLOCK).astype(F32)
        s = _dot_nt(kn_s[pl.ds(k0, MOBA_BLOCK), :], qs) + alibi + (selb_s[pl.ds(j, 1), :] + offset)
        m_new = jnp.maximum(m_run, jnp.max(s, axis=0, keepdims=True))
        alpha = jnp.exp(m_run - m_new)
        p = jnp.exp(s - m_new)
        l_new = alpha * l_run + jnp.sum(p, axis=0, keepdims=True)
        acc_new = alpha * acc + _dot(vt_ref[0, 0, :, pl.ds(k0, MOBA_BLOCK)], p.astype(BF16))
        return m_new, l_new, acc_new

    m_run, l_run, acc = lax.fori_loop(0, qi, past_block, (m_run, l_run, acc))
    ot_ref[0, 0] = (acc / l_run).astype(BF16)


def _moba(q_hm, k_hm, vt_hm, q_gain, k_gain, slopes):
    b, h, s, dh = q_hm.shape
    nq = s // MOBA_BLOCK
    return pl.pallas_call(
        _moba_kernel,
        grid=(b, h, nq),
        in_specs=[
            pl.BlockSpec(memory_space=pltpu.SMEM),
            pl.BlockSpec((1, 1, MOBA_BLOCK, dh), lambda bi, hi, qi: (bi, hi, qi, 0)),
            pl.BlockSpec((1, 1, s, dh), lambda bi, hi, qi: (bi, hi, 0, 0)),
            pl.BlockSpec((1, 1, dh, s), lambda bi, hi, qi: (bi, hi, 0, 0)),
            pl.BlockSpec((1, dh), lambda bi, hi, qi: (0, 0)),
            pl.BlockSpec((1, dh), lambda bi, hi, qi: (0, 0)),
        ],
        out_specs=pl.BlockSpec((1, 1, dh, MOBA_BLOCK), lambda bi, hi, qi: (bi, hi, 0, qi)),
        out_shape=jax.ShapeDtypeStruct((b, h, dh, s), BF16),
        scratch_shapes=[
            pltpu.VMEM((s, dh), BF16),
            pltpu.VMEM((s // MOBA_BLOCK, dh), F32),
            pltpu.VMEM((s // MOBA_BLOCK, MOBA_BLOCK), F32),
        ],
        compiler_params=_params(("parallel", "parallel", "arbitrary")),
        name="moba_attention",
    )(slopes, q_hm, k_hm, vt_hm, q_gain, k_gain)


def _ssd_kernel(xbc_ref, z_ref, dt_ref, cw_ref, cb_ref, dtb_ref, alog_ref, dskip_ref, nw_ref,
                out_ref, hist_s, state_s):
    c = pl.program_id(1)
    L = SSM_CHUNK
    inner = SSM_HEADS * SSM_HEAD_DIM
    gw = SSM_STATE
    pair_w = 2 * SSM_HEAD_DIM

    @pl.when(c == 0)
    def _():
        hist_s[...] = jnp.zeros_like(hist_s)
        state_s[...] = jnp.zeros_like(state_s)

    cur = xbc_ref[0].astype(F32)
    ext = jnp.concatenate([hist_s[...], cur], axis=0)
    hist_s[...] = cur[L - CONV_HALO:, :]
    act = _silu(_causal_conv(ext, cw_ref, SSM_CONV, CONV_HALO) + cb_ref[...])
    xs = act[:, :inner]

    dt = _softplus(dt_ref[0] + dtb_ref[...])
    a = dt * (-jnp.exp(alog_ref[...]))
    row = lax.broadcasted_iota(jnp.int32, (L, L), 0)
    col = lax.broadcasted_iota(jnp.int32, (L, L), 1)
    causal = row >= col
    acum = _dot_exact(jnp.where(causal, 1.0, 0.0), a)
    acum_t = acum.T

    hrow = lax.broadcasted_iota(jnp.int32, (LANES, inner), 0)
    hcol = lax.broadcasted_iota(jnp.int32, (LANES, inner), 1)
    widen = jnp.where(hcol // SSM_HEAD_DIM == hrow, 1.0, 0.0)
    dt_w = _dot_exact(dt, widen)
    acum_w = _dot_exact(acum, widen)
    last_w = acum_w[L - 1:L, :]
    xdt = xs * dt_w
    xdt_bf = xdt.astype(BF16)
    xdt_dec_bf = (xdt * jnp.exp(last_w - acum_w)).astype(BF16)
    exp_acum_w = jnp.exp(acum_w)
    chunk_decay_w = jnp.exp(last_w)
    lane = lax.broadcasted_iota(jnp.int32, (L, pair_w), 1)
    first_head = lane < SSM_HEAD_DIM

    heads_per_group = SSM_HEADS // SSM_GROUPS
    ys = []
    for g in range(SSM_GROUPS):
        bg = act[:, inner + g * gw:inner + (g + 1) * gw]
        cg = act[:, inner + SSM_GROUPS * gw + g * gw:inner + SSM_GROUPS * gw + (g + 1) * gw]
        cg_bf = cg.astype(BF16)
        cb = _dot_nt(cg_bf, bg.astype(BF16))
        bg_t = bg.T.astype(BF16)
        for pp in range(heads_per_group // 2):
            h0 = g * heads_per_group + 2 * pp
            pi = h0 // 2
            sl = slice(h0 * SSM_HEAD_DIM, h0 * SSM_HEAD_DIM + pair_w)
            yd = []
            for hh in (h0, h0 + 1):
                diff = jnp.broadcast_to(acum[:, hh:hh + 1], (L, L)) - acum_t[hh:hh + 1, :]
                decay = jnp.exp(jnp.where(causal, diff, NEG))
                yd.append(_dot((cb * decay).astype(BF16), xdt_bf[:, sl]))
            y_diag = jnp.where(first_head, yd[0], yd[1])
            prev = state_s[pi]
            y_off = _dot(cg_bf, prev.astype(BF16)) * exp_acum_w[:, sl]
            state_s[pi] = prev * chunk_decay_w[:, sl] + _dot(bg_t, xdt_dec_bf[:, sl])
            ys.append(y_diag + y_off + dskip_ref[:, sl] * xs[:, sl])
    y = jnp.concatenate(ys, axis=1)

    zf = z_ref[0].astype(F32)
    yg = y * _silu(zf)
    gsz = inner // SSM_GROUPS
    parts = []
    for g in range(SSM_GROUPS):
        seg = yg[:, g * gsz:(g + 1) * gsz]
        parts.append(seg * lax.rsqrt(jnp.mean(seg * seg, axis=-1, keepdims=True) + EPS))
    out_ref[0] = (jnp.concatenate(parts, axis=1) * nw_ref[...]).astype(BF16)


def _ssd(xbc, z, dt, conv_w, conv_b, dt_bias, a_log, d_skip_w, norm_w):
    b, s, cd = xbc.shape
    inner = z.shape[2]
    nc = s // SSM_CHUNK
    return pl.pallas_call(
        _ssd_kernel,
        grid=(b, nc),
        in_specs=[
            pl.BlockSpec((1, SSM_CHUNK, cd), lambda bi, ci: (bi, ci, 0)),
            pl.BlockSpec((1, SSM_CHUNK, inner), lambda bi, ci: (bi, ci, 0)),
            pl.BlockSpec((1, SSM_CHUNK, LANES), lambda bi, ci: (bi, ci, 0)),
            _const_spec(conv_w.shape), _const_spec(conv_b.shape), _const_spec(dt_bias.shape),
            _const_spec(a_log.shape), _const_spec(d_skip_w.shape), _const_spec(norm_w.shape),
        ],
        out_specs=pl.BlockSpec((1, SSM_CHUNK, inner), lambda bi, ci: (bi, ci, 0)),
        out_shape=jax.ShapeDtypeStruct((b, s, inner), BF16),
        scratch_shapes=[
            pltpu.VMEM((CONV_HALO, cd), F32),
            pltpu.VMEM((SSM_HEADS // 2, SSM_STATE, 2 * SSM_HEAD_DIM), F32),
        ],
        compiler_params=_params(("parallel", "arbitrary")),
        name="ssd_heads",
    )(xbc, z, dt, conv_w, conv_b, dt_bias, a_log, d_skip_w, norm_w)


def _outproj_even_kernel(x_ref, a_ref, s_ref, wa_ref, ws_ref, o_ref):
    o_ref[...] = x_ref[...] + _dot(a_ref[...], wa_ref[...]) + _dot(s_ref[...], ws_ref[...])


def _outproj_even(x2, attn, ssm, wa, ws, tm=512):
    t, d = x2.shape
    return pl.pallas_call(
        _outproj_even_kernel,
        grid=(t // tm,),
        in_specs=[pl.BlockSpec((tm, d), lambda i: (i, 0)),
                  pl.BlockSpec((tm, attn.shape[1]), lambda i: (i, 0)),
                  pl.BlockSpec((tm, ssm.shape[1]), lambda i: (i, 0)),
                  _const_spec(wa.shape), _const_spec(ws.shape)],
        out_specs=pl.BlockSpec((tm, d), lambda i: (i, 0)),
        out_shape=jax.ShapeDtypeStruct((t, d), F32),
        compiler_params=_params(("parallel",)),
        name="outproj_even",
    )(x2, attn, ssm, wa, ws)


def _outproj_odd_kernel(x_ref, o_in_ref, w_ref, o_ref):
    o_ref[...] = x_ref[...] + _dot(o_in_ref[...], w_ref[...])


def _outproj_odd(x2, o_in, w, tm=512):
    t, d = x2.shape
    return pl.pallas_call(
        _outproj_odd_kernel,
        grid=(t // tm,),
        in_specs=[pl.BlockSpec((tm, d), lambda i: (i, 0)),
                  pl.BlockSpec((tm, o_in.shape[1]), lambda i: (i, 0)),
                  _const_spec(w.shape)],
        out_specs=pl.BlockSpec((tm, d), lambda i: (i, 0)),
        out_shape=jax.ShapeDtypeStruct((t, d), F32),
        compiler_params=_params(("parallel",)),
        name="outproj_odd",
    )(x2, o_in, w)


def _ffn_ple_kernel(x_ref, halo_ref, p_ref, gf_ref, wg_ref, wu_ref, cw_ref, cb_ref, wd_ref,
                    gp_ref, wpg_ref, wpp_ref, o_ref, *, tm, seq):
    i = pl.program_id(0)
    x = x_ref[...]
    keep_halo = jnp.where((i * tm) % seq == 0, 0.0, 1.0)
    halo = halo_ref[...] * keep_halo
    hn = _rms_rows(jnp.concatenate([halo, x], axis=0), gf_ref[...]).astype(BF16)
    hn_cur = hn[FFN_HALO:]

    d_ff = wg_ref.shape[1]
    acc = x
    for j in range(d_ff // FFN_COLS):
        cs = slice(j * FFN_COLS, (j + 1) * FFN_COLS)
        g = _dot(hn, wg_ref[:, cs])
        gc = cb_ref[:, cs] + cw_ref[FFN_CONV - 1:FFN_CONV, cs] * g[FFN_HALO:]
        for shift in range(1, FFN_CONV):
            gc = gc + cw_ref[FFN_CONV - 1 - shift:FFN_CONV - shift, cs] * pltpu.roll(g, shift, axis=0)[FFN_HALO:]
        u = _dot(hn_cur, wu_ref[:, cs])
        acc = acc + _dot((_silu(gc) * u).astype(BF16), wd_ref[cs, :])
    x1 = acc
    hp = _rms_rows(x1, gp_ref[...]).astype(BF16)
    gate = _sigmoid(_dot(hp, wpg_ref[...]))
    o_ref[...] = x1 + gate * _dot(p_ref[...].astype(BF16), wpp_ref[...])


def _ffn_ple(x2, p2, seq, gain_ffn, wg, wu, conv_w, conv_b, wd, gain_ple, wpg, wpp, tm=512):
    t, d = x2.shape
    halo_blocks = tm // FFN_HALO
    consts = (gain_ffn, wg, wu, conv_w, conv_b, wd, gain_ple, wpg, wpp)
    return pl.pallas_call(
        functools.partial(_ffn_ple_kernel, tm=tm, seq=seq),
        grid=(t // tm,),
        in_specs=[pl.BlockSpec((tm, d), lambda i: (i, 0)),
                  pl.BlockSpec((FFN_HALO, d), lambda i: (jnp.maximum(i * halo_blocks - 1, 0), 0)),
                  pl.BlockSpec((tm, p2.shape[1]), lambda i: (i, 0))]
        + [_const_spec(c.shape) for c in consts],
        out_specs=pl.BlockSpec((tm, d), lambda i: (i, 0)),
        out_shape=jax.ShapeDtypeStruct((t, d), F32),
        compiler_params=_params(("parallel",)),
        name="ffn_ple",
    )(x2, x2, p2, *consts)


def _inproj_odd_kernel(x_ref, g_ref, wqkv_ref, wz_ref, wba_ref, qkv_ref, z_ref, ba_ref):
    hn = _rms_rows(x_ref[...], g_ref[...]).astype(BF16)
    qkv_ref[...] = _dot(hn, wqkv_ref[...]).astype(BF16)
    z_ref[...] = _dot(hn, wz_ref[...]).astype(BF16)
    ba_ref[...] = _dot(hn, wba_ref[...])


def _inproj_odd(x2, gain, wqkv, wz, wba, tm=256):
    t, d = x2.shape
    outs = [(wqkv.shape[1], BF16), (wz.shape[1], BF16), (wba.shape[1], F32)]
    return pl.pallas_call(
        _inproj_odd_kernel,
        grid=(t // tm,),
        in_specs=[pl.BlockSpec((tm, d), lambda i: (i, 0)), _const_spec(gain.shape)]
        + [_const_spec(w.shape) for w in (wqkv, wz, wba)],
        out_specs=[pl.BlockSpec((tm, n), lambda i: (i, 0)) for n, _ in outs],
        out_shape=[jax.ShapeDtypeStruct((t, n), dt) for n, dt in outs],
        compiler_params=_params(("parallel",)),
        name="inproj_odd",
    )(x2, gain, wqkv, wz, wba)


def _unit_lower_inverse(a_strict):
    L = a_strict.shape[0]
    row = lax.broadcasted_iota(jnp.int32, (L, L), 0)
    col = lax.broadcasted_iota(jnp.int32, (L, L), 1)
    eye = jnp.where(row == col, 1.0, 0.0)
    base = 8
    same = (row // base) == (col // base)
    n = -jnp.where(same, a_strict, 0.0)
    t = eye + n
    pw = _dot_exact(n, n)
    t = t + _dot_exact(t, pw)
    pw = _dot_exact(pw, pw)
    t = t + _dot_exact(t, pw)
    size = base
    while size < L:
        wider = (row // (2 * size)) == (col // (2 * size))
        off = jnp.where(wider & jnp.logical_not(same), a_strict, 0.0)
        t = t - _dot_exact(t, _dot_exact(off, t))
        same = wider
        size *= 2
    return t


def _gdn_kernel(qkv_ref, z_ref, ba_ref, cw_ref, dtb_ref, alog_ref, nw_ref, out_ref, hist_s, state_s):
    c = pl.program_id(1)
    L = GDN_CHUNK
    dk = GDN_DIM
    qk_w = GDN_K_HEADS * dk
    rep = GDN_V_HEADS // GDN_K_HEADS

    @pl.when(c == 0)
    def _():
        hist_s[...] = jnp.zeros_like(hist_s)
        state_s[...] = jnp.zeros_like(state_s)

    cur = qkv_ref[0].astype(F32)
    ext = jnp.concatenate([hist_s[...], cur], axis=0)
    hist_s[...] = cur[L - CONV_HALO:, :]
    act = _silu(_causal_conv(ext, cw_ref, GDN_CONV, CONV_HALO))

    ba = ba_ref[0]
    beta = _sigmoid(ba)
    g = -jnp.exp(alog_ref[...]) * _softplus(ba + dtb_ref[...])
    row = lax.broadcasted_iota(jnp.int32, (L, L), 0)
    col = lax.broadcasted_iota(jnp.int32, (L, L), 1)
    lower = row >= col
    strict = row > col
    gc = _dot_exact(jnp.where(lower, 1.0, 0.0), g)
    gc_t = gc.T

    def l2n(t):
        return t * lax.rsqrt(jnp.sum(t * t, axis=-1, keepdims=True) + EPS)

    qs, ks, kts = [], [], []
    for kh in range(GDN_K_HEADS):
        qs.append(l2n(act[:, kh * dk:(kh + 1) * dk]) * (dk ** -0.5))
        kn = l2n(act[:, qk_w + kh * dk:qk_w + (kh + 1) * dk])
        ks.append(kn)
        kts.append(kn.T)

    zf = z_ref[0].astype(F32)
    for hv in range(GDN_V_HEADS):
        kh = hv // rep
        q_h, k_h, kt_h = qs[kh], ks[kh], kts[kh]
        v_h = act[:, 2 * qk_w + hv * dk:2 * qk_w + (hv + 1) * dk]
        beta_c = beta[:, hv:hv + 1]
        gc_c = gc[:, GDN_V_HEADS + hv:GDN_V_HEADS + hv + 1]
        gc_r = gc_t[GDN_V_HEADS + hv:GDN_V_HEADS + hv + 1, :]
        g_last = gc_r[:, L - 1:L]
        decay = jnp.exp(jnp.where(lower, gc_c - gc_r, NEG))
        kbeta = k_h * beta_c
        kt_bf = kt_h.astype(BF16)
        a_mat = jnp.where(strict, _dot(kbeta.astype(BF16), kt_bf) * decay, 0.0)
        t_inv = _unit_lower_inverse(a_mat)
        egc = jnp.exp(gc_c)
        rhs = jnp.concatenate([v_h * beta_c, kbeta * egc], axis=1)
        sol = _dot_exact(t_inv, rhs)
        u = sol[:, :dk]
        w = sol[:, dk:]
        qk = _dot(q_h.astype(BF16), kt_bf) * decay
        st = state_s[hv]
        st_bf = st.astype(BF16)
        v_new = u - _dot(w.astype(BF16), st_bf)
        v_new_bf = v_new.astype(BF16)
        o = _dot((q_h * egc).astype(BF16), st_bf) + _dot(qk.astype(BF16), v_new_bf)
        k_dec_t = kt_h * jnp.exp(g_last - gc_r)
        state_s[hv] = st * jnp.exp(g_last) + _dot(k_dec_t.astype(BF16), v_new_bf)
        o = o * lax.rsqrt(jnp.mean(o * o, axis=-1, keepdims=True) + EPS) * nw_ref[...]
        sl = slice(hv * dk, (hv + 1) * dk)
        out_ref[0, :, sl] = (o * _silu(zf[:, sl])).astype(BF16)


def _gdn(qkv, z, ba, conv_w, dt_bias_pad, a_log_pad, norm_w):
    b, s, cd = qkv.shape
    vw = z.shape[2]
    nc = s // GDN_CHUNK
    return pl.pallas_call(
        _gdn_kernel,
        grid=(b, nc),
        in_specs=[
            pl.BlockSpec((1, GDN_CHUNK, cd), lambda bi, ci: (bi, ci, 0)),
            pl.BlockSpec((1, GDN_CHUNK, vw), lambda bi, ci: (bi, ci, 0)),
            pl.BlockSpec((1, GDN_CHUNK, LANES), lambda bi, ci: (bi, ci, 0)),
            _const_spec(conv_w.shape), _const_spec(dt_bias_pad.shape),
            _const_spec(a_log_pad.shape), _const_spec(norm_w.shape),
        ],
        out_specs=pl.BlockSpec((1, GDN_CHUNK, vw), lambda bi, ci: (bi, ci, 0)),
        out_shape=jax.ShapeDtypeStruct((b, s, vw), BF16),
        scratch_shapes=[
            pltpu.VMEM((CONV_HALO, cd), F32),
            pltpu.VMEM((GDN_V_HEADS, GDN_DIM, GDN_DIM), F32),
        ],
        compiler_params=_params(("parallel", "arbitrary")),
        name="gated_deltanet",
    )(qkv, z, ba, conv_w, dt_bias_pad, a_log_pad, norm_w)


def _pad_lanes(v, offset=0):
    out = jnp.zeros((1, LANES), F32)
    return out.at[0, offset:offset + v.shape[0]].set(v.astype(F32))


def _row(v):
    return v.astype(F32).reshape(1, -1)


def _even_mixer(x2, batch, seq, norm_w, w_in, q_norm, k_norm, conv_w, conv_b, dt_bias, a_log, d_skip,
                ssm_norm, w_out, slopes):
    mw = MOBA_HEADS * MOBA_HEAD_DIM
    inner = SSM_HEADS * SSM_HEAD_DIM
    conv_dim = inner + 2 * SSM_GROUPS * SSM_STATE
    w = w_in.astype(BF16)
    c0, c1, c2, c3, c4 = mw, 2 * mw, 3 * mw, 3 * mw + inner, 3 * mw + inner + conv_dim
    wdt = jnp.pad(w[:, c4:], ((0, 0), (0, LANES - SSM_HEADS)))
    q, k, v, z, xbc, dt = _inproj_even(x2, _row(norm_w), w[:, :c0], w[:, c0:c1], w[:, c1:c2],
                                       w[:, c2:c3], w[:, c3:c4], wdt)
    q_hm = q.reshape(batch, seq, MOBA_HEADS, MOBA_HEAD_DIM).transpose(0, 2, 1, 3)
    k_hm = k.reshape(batch, seq, MOBA_HEADS, MOBA_HEAD_DIM).transpose(0, 2, 1, 3)
    vt_hm = v.reshape(batch, seq, MOBA_HEADS, MOBA_HEAD_DIM).transpose(0, 2, 3, 1)
    ot = _moba(q_hm, k_hm, vt_hm, _row(q_norm), _row(k_norm), slopes)
    attn = ot.transpose(0, 3, 1, 2).reshape(batch * seq, mw)

    ssm = _ssd(xbc.reshape(batch, seq, conv_dim), z.reshape(batch, seq, inner),
               dt.reshape(batch, seq, LANES), conv_w.astype(F32), _row(conv_b),
               _pad_lanes(dt_bias), _pad_lanes(a_log), _row(jnp.repeat(d_skip, SSM_HEAD_DIM)),
               _row(ssm_norm))
    wo = w_out.astype(BF16)
    return _outproj_even(x2, attn, ssm.reshape(batch * seq, inner), wo[:mw], wo[mw:])


def _gdn_mixer(x2, batch, seq, norm_w, w_in, conv_w, dt_bias, a_log, gdn_norm, w_out):
    conv_dim = 2 * GDN_K_HEADS * GDN_DIM + GDN_V_HEADS * GDN_DIM
    vw = GDN_V_HEADS * GDN_DIM
    w = w_in.astype(BF16)
    wba = jnp.pad(w[:, conv_dim + vw:], ((0, 0), (0, LANES - 2 * GDN_V_HEADS)))
    qkv, z, ba = _inproj_odd(x2, _row(norm_w), w[:, :conv_dim], w[:, conv_dim:conv_dim + vw], wba)
    o = _gdn(qkv.reshape(batch, seq, conv_dim), z.reshape(batch, seq, vw), ba.reshape(batch, seq, LANES),
             conv_w.astype(F32), _pad_lanes(dt_bias, GDN_V_HEADS), _pad_lanes(a_log, GDN_V_HEADS),
             _row(gdn_norm))
    return _outproj_odd(x2, o.reshape(batch * seq, vw), w_out.astype(BF16))


def kernel(x, p, norm_mix, norm_ffn, norm_ple, w_in_even, moba_q_norm, moba_k_norm, ssm_conv_w, ssm_conv_b, ssm_dt_bias, ssm_a_log, ssm_d, ssm_norm, w_out_even, w_in_odd, gdn_conv_w, gdn_dt_bias, gdn_a_log, gdn_norm, w_out_odd, ffn_w_gate, ffn_w_up, ffn_conv_w, ffn_conv_b, ffn_w_down, ple_w_proj, ple_w_gate):
    batch, seq, d = x.shape
    depth = p.shape[0]
    slopes = jnp.exp2(-ALIBI_MAX_BIAS * jnp.arange(1, MOBA_HEADS + 1, dtype=F32) / MOBA_HEADS)
    x2 = x.reshape(batch * seq, d)
    for i in range(depth):
        j = i // 2
        if i % 2 == 0:
            x2 = _even_mixer(x2, batch, seq, norm_mix[i], w_in_even[j], moba_q_norm[j], moba_k_norm[j],
                             ssm_conv_w[j], ssm_conv_b[j], ssm_dt_bias[j], ssm_a_log[j], ssm_d[j],
                             ssm_norm[j], w_out_even[j], slopes)
        else:
            x2 = _gdn_mixer(x2, batch, seq, norm_mix[i], w_in_odd[j], gdn_conv_w[j], gdn_dt_bias[j],
                            gdn_a_log[j], gdn_norm[j], w_out_odd[j])
        x2 = _ffn_ple(x2, p[i].reshape(batch * seq, -1), seq, _row(norm_ffn[i]),
                      ffn_w_gate[i].astype(BF16), ffn_w_up[i].astype(BF16), ffn_conv_w[i].astype(F32),
                      _row(ffn_conv_b[i]), ffn_w_down[i].astype(BF16), _row(norm_ple[i]),
                      ple_w_gate[i].astype(BF16), ple_w_proj[i].astype(BF16))
    return x2.reshape(batch, seq, d)
```

```python
import functools

import jax
import jax.numpy as jnp
from jax import lax
from jax.experimental import pallas as pl
from jax.experimental.pallas import tpu as pltpu

F32 = jnp.float32
BF16 = jnp.bfloat16
HIGHEST = lax.Precision.HIGHEST

EPS = 1e-6
NEG = -1e30
LANES = 128
CONV_HALO = 8

MOBA_HEADS = 8
MOBA_HEAD_DIM = 64
MOBA_BLOCK = 256
MOBA_TOPK = 3
ALIBI_MAX_BIAS = 8.0

SSM_HEADS = 16
SSM_HEAD_DIM = 64
SSM_STATE = 128
SSM_GROUPS = 2
SSM_CONV = 4
SSM_CHUNK = 128

GDN_K_HEADS = 8
GDN_V_HEADS = 16
GDN_DIM = 128
GDN_CONV = 4
GDN_CHUNK = 64

FFN_CONV = 3
FFN_HALO = 16
FFN_COLS = 256

VMEM_LIMIT = 56 * 1024 * 1024


def _rms_rows(x, gain):
    return x * lax.rsqrt(jnp.mean(x * x, axis=-1, keepdims=True) + EPS) * gain


def _sigmoid(x):
    return 1.0 / (1.0 + jnp.exp(-x))


def _silu(x):
    return x * _sigmoid(x)


def _softplus(x):
    return jnp.maximum(x, 0.0) + jnp.log(1.0 + jnp.exp(-jnp.abs(x)))


def _dot(a, b):
    return jnp.dot(a, b, preferred_element_type=F32)


def _dot_nt(a, b):
    return lax.dot_general(a, b, (((1,), (1,)), ((), ())), preferred_element_type=F32)


def _dot_exact(a, b):
    return jnp.dot(a, b, precision=HIGHEST, preferred_element_type=F32)


def _const_spec(shape):
    zeros = (0,) * len(shape)
    return pl.BlockSpec(shape, lambda *_: zeros, pipeline_mode=pl.Buffered(1))


def _params(semantics):
    return pltpu.CompilerParams(dimension_semantics=semantics, vmem_limit_bytes=VMEM_LIMIT)


def _causal_conv(ext, w_ref, width, halo):
    y = w_ref[width - 1:width, :] * ext[halo:]
    for shift in range(1, width):
        y = y + w_ref[width - 1 - shift:width - shift, :] * pltpu.roll(ext, shift, axis=0)[halo:]
    return y


def _inproj_even_kernel(x_ref, g_ref, wq_ref, wk_ref, wv_ref, wz_ref, wxbc_ref, wdt_ref,
                        q_ref, k_ref, v_ref, z_ref, xbc_ref, dt_ref):
    hn = _rms_rows(x_ref[...], g_ref[...]).astype(BF16)
    q_ref[...] = _dot(hn, wq_ref[...]).astype(BF16)
    k_ref[...] = _dot(hn, wk_ref[...]).astype(BF16)
    v_ref[...] = _dot(hn, wv_ref[...]).astype(BF16)
    z_ref[...] = _dot(hn, wz_ref[...]).astype(BF16)
    xbc_ref[...] = _dot(hn, wxbc_ref[...]).astype(BF16)
    dt_ref[...] = _dot(hn, wdt_ref[...])


def _inproj_even(x2, gain, wq, wk, wv, wz, wxbc, wdt, tm=512):
    t, d = x2.shape
    outs = [(wq.shape[1], BF16), (wk.shape[1], BF16), (wv.shape[1], BF16),
            (wz.shape[1], BF16), (wxbc.shape[1], BF16), (wdt.shape[1], F32)]
    return pl.pallas_call(
        _inproj_even_kernel,
        grid=(t // tm,),
        in_specs=[pl.BlockSpec((tm, d), lambda i: (i, 0)), _const_spec(gain.shape)]
        + [_const_spec(w.shape) for w in (wq, wk, wv, wz, wxbc, wdt)],
        out_specs=[pl.BlockSpec((tm, n), lambda i: (i, 0)) for n, _ in outs],
        out_shape=[jax.ShapeDtypeStruct((t, n), dt) for n, dt in outs],
        compiler_params=_params(("parallel",)),
        name="inproj_even",
    )(x2, gain, wq, wk, wv, wz, wxbc, wdt)


def _moba_kernel(slopes_ref, q_ref, k_ref, vt_ref, qg_ref, kg_ref, ot_ref, kn_s, kbar_s, selb_s):
    h = pl.program_id(1)
    qi = pl.program_id(2)
    seq = k_ref.shape[2]
    nb = seq // MOBA_BLOCK

    @pl.when(qi == 0)
    def _():
        kn = _rms_rows(k_ref[0, 0].astype(F32), kg_ref[...])
        kn_s[...] = kn.astype(BF16)
        kbar_s[...] = jnp.mean(kn.reshape(nb, MOBA_BLOCK, MOBA_HEAD_DIM), axis=1)

    slope = slopes_ref[h]
    qn = _rms_rows(q_ref[0, 0].astype(F32), qg_ref[...])

    gate = lax.dot_general(kbar_s[...], qn, (((1,), (1,)), ((), ())),
                           precision=HIGHEST, preferred_element_type=F32)
    blk = lax.broadcasted_iota(jnp.int32, gate.shape, 0)
    rank = jnp.zeros(gate.shape, F32)
    for m in range(nb):
        gm = gate[m:m + 1, :]
        beats = (gm > gate) | ((gm == gate) & (m < blk))
        rank = rank + jnp.where(beats, jnp.where(m < qi, 1.0, 0.0), 0.0)
    selected = (blk < qi) & (rank < float(min(MOBA_TOPK, nb - 1)))
    selb_s[...] = jnp.where(selected, 0.0, NEG)

    qs = (qn * (MOBA_HEAD_DIM ** -0.5)).astype(BF16)
    kidx = lax.broadcasted_iota(jnp.int32, (MOBA_BLOCK, MOBA_BLOCK), 0)
    qidx = lax.broadcasted_iota(jnp.int32, (MOBA_BLOCK, MOBA_BLOCK), 1)
    rel = (kidx - qidx).astype(F32)
    alibi = slope * rel

    q0 = pl.multiple_of(qi * MOBA_BLOCK, MOBA_BLOCK)
    s = _dot_nt(kn_s[pl.ds(q0, MOBA_BLOCK), :], qs) + alibi
    s = jnp.where(rel <= 0.0, s, NEG)
    m_run = jnp.max(s, axis=0, keepdims=True)
    p = jnp.exp(s - m_run)
    l_run = jnp.sum(p, axis=0, keepdims=True)
    acc = _dot(vt_ref[0, 0, :, pl.ds(q0, MOBA_BLOCK)], p.astype(BF16))

    def past_block(j, carry):
        m_run, l_run, acc = carry
        k0 = pl.multiple_of(j * MOBA_BLOCK, MOBA_BLOCK)
        offset = slope * ((j - qi) * MOBA_BLOCK).astype(F32)
        s = _dot_nt(kn_s[pl.ds(k0, MOBA_BLOCK), :], qs) + alibi + (selb_s[pl.ds(j, 1), :] + offset)
        m_new = jnp.maximum(m_run, jnp.max(s, axis=0, keepdims=True))
        alpha = jnp.exp(m_run - m_new)
        p = jnp.exp(s - m_new)
        l_new = alpha * l_run + jnp.sum(p, axis=0, keepdims=True)
        acc_new = alpha * acc + _dot(vt_ref[0, 0, :, pl.ds(k0, MOBA_BLOCK)], p.astype(BF16))
        return m_new, l_new, acc_new

    m_run, l_run, acc = lax.fori_loop(0, qi, past_block, (m_run, l_run, acc))
    ot_ref[0, 0] = (acc / l_run).astype(BF16)


def _moba(q_hm, k_hm, vt_hm, q_gain, k_gain, slopes):
    b, h, s, dh = q_hm.shape
    nq = s // MOBA_BLOCK
    return pl.pallas_call(
        _moba_kernel,
        grid=(b, h, nq),
        in_specs=[
            pl.BlockSpec(memory_space=pltpu.SMEM),
            pl.BlockSpec((1, 1, MOBA_BLOCK, dh), lambda bi, hi, qi: (bi, hi, qi, 0)),
            pl.BlockSpec((1, 1, s, dh), lambda bi, hi, qi: (bi, hi, 0, 0)),
            pl.BlockSpec((1, 1, dh, s), lambda bi, hi, qi: (bi, hi, 0, 0)),
            pl.BlockSpec((1, dh), lambda bi, hi, qi: (0, 0)),
            pl.BlockSpec((1, dh), lambda bi, hi, qi: (0, 0)),
        ],
        out_specs=pl.BlockSpec((1, 1, dh, MOBA_BLOCK), lambda bi, hi, qi: (bi, hi, 0, qi)),
        out_shape=jax.ShapeDtypeStruct((b, h, dh, s), BF16),
        scratch_shapes=[
            pltpu.VMEM((s, dh), BF16),
            pltpu.VMEM((s // MOBA_BLOCK, dh), F32),
            pltpu.VMEM((s // MOBA_BLOCK, MOBA_BLOCK), F32),
        ],
        compiler_params=_params(("parallel", "parallel", "arbitrary")),
        name="moba_attention",
    )(slopes, q_hm, k_hm, vt_hm, q_gain, k_gain)


def _ssd_kernel(xbc_ref, z_ref, dt_ref, cw_ref, cb_ref, dtb_ref, alog_ref, dskip_ref, nw_ref,
                out_ref, hist_s, state_s):
    c = pl.program_id(1)
    L = SSM_CHUNK
    inner = SSM_HEADS * SSM_HEAD_DIM
    gw = SSM_STATE
    pair_w = 2 * SSM_HEAD_DIM

    @pl.when(c == 0)
    def _():
        hist_s[...] = jnp.zeros_like(hist_s)
        state_s[...] = jnp.zeros_like(state_s)

    cur = xbc_ref[0].astype(F32)
    ext = jnp.concatenate([hist_s[...], cur], axis=0)
    hist_s[...] = cur[L - CONV_HALO:, :]
    act = _silu(_causal_conv(ext, cw_ref, SSM_CONV, CONV_HALO) + cb_ref[...])
    xs = act[:, :inner]

    dt = _softplus(dt_ref[0] + dtb_ref[...])
    a = dt * (-jnp.exp(alog_ref[...]))
    row = lax.broadcasted_iota(jnp.int32, (L, L), 0)
    col = lax.broadcasted_iota(jnp.int32, (L, L), 1)
    causal = row >= col
    acum = _dot_exact(jnp.where(causal, 1.0, 0.0), a)
    acum_t = acum.T

    hrow = lax.broadcasted_iota(jnp.int32, (LANES, inner), 0)
    hcol = lax.broadcasted_iota(jnp.int32, (LANES, inner), 1)
    widen = jnp.where(hcol // SSM_HEAD_DIM == hrow, 1.0, 0.0)
    dt_w = _dot_exact(dt, widen)
    acum_w = _dot_exact(acum, widen)
    last_w = acum_w[L - 1:L, :]
    xdt = xs * dt_w
    xdt_bf = xdt.astype(BF16)
    xdt_dec_bf = (xdt * jnp.exp(last_w - acum_w)).astype(BF16)
    exp_acum_w = jnp.exp(acum_w)
    chunk_decay_w = jnp.exp(last_w)
    lane = lax.broadcasted_iota(jnp.int32, (L, pair_w), 1)
    first_head = lane < SSM_HEAD_DIM

    heads_per_group = SSM_HEADS // SSM_GROUPS
    ys = []
    for g in range(SSM_GROUPS):
        bg = act[:, inner + g * gw:inner + (g + 1) * gw]
        cg = act[:, inner + SSM_GROUPS * gw + g * gw:inner + SSM_GROUPS * gw + (g + 1) * gw]
        cg_bf = cg.astype(BF16)
        cb = _dot_nt(cg_bf, bg.astype(BF16))
        bg_t = bg.T.astype(BF16)
        for pp in range(heads_per_group // 2):
            h0 = g * heads_per_group + 2 * pp
            pi = h0 // 2
            sl = slice(h0 * SSM_HEAD_DIM, h0 * SSM_HEAD_DIM + pair_w)
            yd = []
            for hh in (h0, h0 + 1):
                diff = jnp.broadcast_to(acum[:, hh:hh + 1], (L, L)) - acum_t[hh:hh + 1, :]
                decay = jnp.exp(jnp.where(causal, diff, NEG))
                yd.append(_dot((cb * decay).astype(BF16), xdt_bf[:, sl]))
            y_diag = jnp.where(first_head, yd[0], yd[1])
            prev = state_s[pi]
            y_off = _dot(cg_bf, prev.astype(BF16)) * exp_acum_w[:, sl]
            state_s[pi] = prev * chunk_decay_w[:, sl] + _dot(bg_t, xdt_dec_bf[:, sl])
            ys.append(y_diag + y_off + dskip_ref[:, sl] * xs[:, sl])
    y = jnp.concatenate(ys, axis=1)

    zf = z_ref[0].astype(F32)
    yg = y * _silu(zf)
    gsz = inner // SSM_GROUPS
    parts = []
    for g in range(SSM_GROUPS):
        seg = yg[:, g * gsz:(g + 1) * gsz]
        parts.append(seg * lax.rsqrt(jnp.mean(seg * seg, axis=-1, keepdims=True) + EPS))
    out_ref[0] = (jnp.concatenate(parts, axis=1) * nw_ref[...]).astype(BF16)


def _ssd(xbc, z, dt, conv_w, conv_b, dt_bias, a_log, d_skip_w, norm_w):
    b, s, cd = xbc.shape
    inner = z.shape[2]
    nc = s // SSM_CHUNK
    return pl.pallas_call(
        _ssd_kernel,
        grid=(b, nc),
        in_specs=[
            pl.BlockSpec((1, SSM_CHUNK, cd), lambda bi, ci: (bi, ci, 0)),
            pl.BlockSpec((1, SSM_CHUNK, inner), lambda bi, ci: (bi, ci, 0)),
            pl.BlockSpec((1, SSM_CHUNK, LANES), lambda bi, ci: (bi, ci, 0)),
            _const_spec(conv_w.shape), _const_spec(conv_b.shape), _const_spec(dt_bias.shape),
            _const_spec(a_log.shape), _const_spec(d_skip_w.shape), _const_spec(norm_w.shape),
        ],
        out_specs=pl.BlockSpec((1, SSM_CHUNK, inner), lambda bi, ci: (bi, ci, 0)),
        out_shape=jax.ShapeDtypeStruct((b, s, inner), BF16),
        scratch_shapes=[
            pltpu.VMEM((CONV_HALO, cd), F32),
            pltpu.VMEM((SSM_HEADS // 2, SSM_STATE, 2 * SSM_HEAD_DIM), F32),
        ],
        compiler_params=_params(("parallel", "arbitrary")),
        name="ssd_heads",
    )(xbc, z, dt, conv_w, conv_b, dt_bias, a_log, d_skip_w, norm_w)


def _outproj_even_kernel(x_ref, a_ref, s_ref, wa_ref, ws_ref, o_ref):
    o_ref[...] = x_ref[...] + _dot(a_ref[...], wa_ref[...]) + _dot(s_ref[...], ws_ref[...])


def _outproj_even(x2, attn, ssm, wa, ws, tm=512):
    t, d = x2.shape
    return pl.pallas_call(
        _outproj_even_kernel,
        grid=(t // tm,),
        in_specs=[pl.BlockSpec((tm, d), lambda i: (i, 0)),
                  pl.BlockSpec((tm, attn.shape[1]), lambda i: (i, 0)),
                  pl.BlockSpec((tm, ssm.shape[1]), lambda i: (i, 0)),
                  _const_spec(wa.shape), _const_spec(ws.shape)],
        out_specs=pl.BlockSpec((tm, d), lambda i: (i, 0)),
        out_shape=jax.ShapeDtypeStruct((t, d), F32),
        compiler_params=_params(("parallel",)),
        name="outproj_even",
    )(x2, attn, ssm, wa, ws)


def _outproj_odd_kernel(x_ref, o_in_ref, w_ref, o_ref):
    o_ref[...] = x_ref[...] + _dot(o_in_ref[...], w_ref[...])


def _outproj_odd(x2, o_in, w, tm=512):
    t, d = x2.shape
    return pl.pallas_call(
        _outproj_odd_kernel,
        grid=(t // tm,),
        in_specs=[pl.BlockSpec((tm, d), lambda i: (i, 0)),
                  pl.BlockSpec((tm, o_in.shape[1]), lambda i: (i, 0)),
                  _const_spec(w.shape)],
        out_specs=pl.BlockSpec((tm, d), lambda i: (i, 0)),
        out_shape=jax.ShapeDtypeStruct((t, d), F32),
        compiler_params=_params(("parallel",)),
        name="outproj_odd",
    )(x2, o_in, w)


def _ffn_ple_kernel(x_ref, halo_ref, p_ref, gf_ref, wg_ref, wu_ref, cw_ref, cb_ref, wd_ref,
                    gp_ref, wpg_ref, wpp_ref, o_ref, *, tm, seq):
    i = pl.program_id(0)
    x = x_ref[...]
    keep_halo = jnp.where((i * tm) % seq == 0, 0.0, 1.0)
    halo = halo_ref[...] * keep_halo
    hn = _rms_rows(jnp.concatenate([halo, x], axis=0), gf_ref[...]).astype(BF16)
    hn_cur = hn[FFN_HALO:]

    d_ff = wg_ref.shape[1]
    acc = x
    for j in range(d_ff // FFN_COLS):
        cs = slice(j * FFN_COLS, (j + 1) * FFN_COLS)
        g = _dot(hn, wg_ref[:, cs])
        gc = cb_ref[:, cs] + cw_ref[FFN_CONV - 1:FFN_CONV, cs] * g[FFN_HALO:]
        for shift in range(1, FFN_CONV):
            gc = gc + cw_ref[FFN_CONV - 1 - shift:FFN_CONV - shift, cs] * pltpu.roll(g, shift, axis=0)[FFN_HALO:]
        u = _dot(hn_cur, wu_ref[:, cs])
        acc = acc + _dot((_silu(gc) * u).astype(BF16), wd_ref[cs, :])
    x1 = acc
    hp = _rms_rows(x1, gp_ref[...]).astype(BF16)
    gate = _sigmoid(_dot(hp, wpg_ref[...]))
    o_ref[...] = x1 + gate * _dot(p_ref[...].astype(BF16), wpp_ref[...])


def _ffn_ple(x2, p2, seq, gain_ffn, wg, wu, conv_w, conv_b, wd, gain_ple, wpg, wpp, tm=512):
    t, d = x2.shape
    halo_blocks = tm // FFN_HALO
    consts = (gain_ffn, wg, wu, conv_w, conv_b, wd, gain_ple, wpg, wpp)
    return pl.pallas_call(
        functools.partial(_ffn_ple_kernel, tm=tm, seq=seq),
        grid=(t // tm,),
        in_specs=[pl.BlockSpec((tm, d), lambda i: (i, 0)),
                  pl.BlockSpec((FFN_HALO, d), lambda i: (jnp.maximum(i * halo_blocks - 1, 0), 0)),
                  pl.BlockSpec((tm, p2.shape[1]), lambda i: (i, 0))]
        + [_const_spec(c.shape) for c in consts],
        out_specs=pl.BlockSpec((tm, d), lambda i: (i, 0)),
        out_shape=jax.ShapeDtypeStruct((t, d), F32),
        compiler_params=_params(("parallel",)),
        name="ffn_ple",
    )(x2, x2, p2, *consts)


def _inproj_odd_kernel(x_ref, g_ref, wqkv_ref, wz_ref, wba_ref, qkv_ref, z_ref, ba_ref):
    hn = _rms_rows(x_ref[...], g_ref[...]).astype(BF16)
    qkv_ref[...] = _dot(hn, wqkv_ref[...]).astype(BF16)
    z_ref[...] = _dot(hn, wz_ref[...]).astype(BF16)
    ba_ref[...] = _dot(hn, wba_ref[...])


def _inproj_odd(x2, gain, wqkv, wz, wba, tm=256):
    t, d = x2.shape
    outs = [(wqkv.shape[1], BF16), (wz.shape[1], BF16), (wba.shape[1], F32)]
    return pl.pallas_call(
        _inproj_odd_kernel,
        grid=(t // tm,),
        in_specs=[pl.BlockSpec((tm, d), lambda i: (i, 0)), _const_spec(gain.shape)]
        + [_const_spec(w.shape) for w in (wqkv, wz, wba)],
        out_specs=[pl.BlockSpec((tm, n), lambda i: (i, 0)) for n, _ in outs],
        out_shape=[jax.ShapeDtypeStruct((t, n), dt) for n, dt in outs],
        compiler_params=_params(("parallel",)),
        name="inproj_odd",
    )(x2, gain, wqkv, wz, wba)


def _bdot(a, b):
    return _dot(a.astype(BF16), b.astype(BF16))


def _unit_lower_inverses(mats):
    L = mats[0].shape[0]
    row = lax.broadcasted_iota(jnp.int32, (L, L), 0)
    col = lax.broadcasted_iota(jnp.int32, (L, L), 1)
    eye = jnp.where(row == col, 1.0, 0.0)
    base = 8
    same = (row // base) == (col // base)
    ns = [-jnp.where(same, a, 0.0) for a in mats]
    ts = [eye + n for n in ns]
    pws = [_bdot(n, n) for n in ns]
    ts = [t + _bdot(t, pw) for t, pw in zip(ts, pws)]
    pws = [_bdot(pw, pw) for pw in pws]
    ts = [t + _bdot(t, pw) for t, pw in zip(ts, pws)]
    size = base
    while size < L:
        wider = (row // (2 * size)) == (col // (2 * size))
        offs = [jnp.where(wider & jnp.logical_not(same), a, 0.0) for a in mats]
        ys = [_bdot(off, t) for off, t in zip(offs, ts)]
        ts = [t - _bdot(t, y) for t, y in zip(ts, ys)]
        same = wider
        size *= 2
    return ts


def _gdn_kernel(qkv_ref, z_ref, ba_ref, cw_ref, dtb_ref, alog_ref, nw_ref, out_ref, hist_s, state_s):
    c = pl.program_id(1)
    L = GDN_CHUNK
    dk = GDN_DIM
    qk_w = GDN_K_HEADS * dk
    rep = GDN_V_HEADS // GDN_K_HEADS

    @pl.when(c == 0)
    def _():
        hist_s[...] = jnp.zeros_like(hist_s)
        state_s[...] = jnp.zeros_like(state_s)

    cur = qkv_ref[0].astype(F32)
    ext = jnp.concatenate([hist_s[...], cur], axis=0)
    hist_s[...] = cur[L - CONV_HALO:, :]
    act = _silu(_causal_conv(ext, cw_ref, GDN_CONV, CONV_HALO))

    ba = ba_ref[0]
    beta = _sigmoid(ba)
    g = -jnp.exp(alog_ref[...]) * _softplus(ba + dtb_ref[...])
    row = lax.broadcasted_iota(jnp.int32, (L, L), 0)
    col = lax.broadcasted_iota(jnp.int32, (L, L), 1)
    lower = row >= col
    strict = row > col
    gc = _dot_exact(jnp.where(lower, 1.0, 0.0), g)
    gc_t = gc.T

    def l2n(t):
        return t * lax.rsqrt(jnp.sum(t * t, axis=-1, keepdims=True) + EPS)

    qs, ks, kts, kk, qk0 = [], [], [], [], []
    for kh in range(GDN_K_HEADS):
        qn = l2n(act[:, kh * dk:(kh + 1) * dk]) * (dk ** -0.5)
        kn = l2n(act[:, qk_w + kh * dk:qk_w + (kh + 1) * dk])
        kt = kn.T
        prod = _bdot(jnp.concatenate([kn, qn], axis=0), kt)
        qs.append(qn)
        ks.append(kn)
        kts.append(kt)
        kk.append(prod[:L])
        qk0.append(prod[L:])

    heads = range(GDN_V_HEADS)
    beta_c = [beta[:, hv:hv + 1] for hv in heads]
    gc_c = [gc[:, GDN_V_HEADS + hv:GDN_V_HEADS + hv + 1] for hv in heads]
    gc_r = [gc_t[GDN_V_HEADS + hv:GDN_V_HEADS + hv + 1, :] for hv in heads]
    g_last = [r[:, L - 1:L] for r in gc_r]
    egc = [jnp.exp(cc) for cc in gc_c]
    decay = [jnp.exp(jnp.where(lower, cc - rr, NEG)) for cc, rr in zip(gc_c, gc_r)]
    a_mats = [jnp.where(strict, kk[hv // rep] * beta_c[hv] * decay[hv], 0.0) for hv in heads]
    t_inv = _unit_lower_inverses(a_mats)

    sols = []
    for hv in heads:
        v_h = act[:, 2 * qk_w + hv * dk:2 * qk_w + (hv + 1) * dk]
        kbeta = ks[hv // rep] * beta_c[hv]
        sols.append(_bdot(t_inv[hv], jnp.concatenate([v_h * beta_c[hv], kbeta * egc[hv]], axis=1)))

    states = [state_s[hv] for hv in heads]
    from_state = [_bdot(jnp.concatenate([sols[hv][:, dk:], qs[hv // rep] * egc[hv]], axis=0), states[hv])
                  for hv in heads]
    v_new = [sols[hv][:, :dk] - from_state[hv][:L] for hv in heads]
    from_v = [_bdot(jnp.concatenate([qk0[hv // rep] * decay[hv],
                                     kts[hv // rep] * jnp.exp(g_last[hv] - gc_r[hv])], axis=0), v_new[hv])
              for hv in heads]
    zf = z_ref[0].astype(F32)
    for hv in heads:
        state_s[hv] = states[hv] * jnp.exp(g_last[hv]) + from_v[hv][L:]
        o = from_state[hv][L:] + from_v[hv][:L]
        o = o * lax.rsqrt(jnp.mean(o * o, axis=-1, keepdims=True) + EPS) * nw_ref[...]
        sl = slice(hv * dk, (hv + 1) * dk)
        out_ref[0, :, sl] = (o * _silu(zf[:, sl])).astype(BF16)


def _gdn(qkv, z, ba, conv_w, dt_bias_pad, a_log_pad, norm_w):
    b, s, cd = qkv.shape
    vw = z.shape[2]
    nc = s // GDN_CHUNK
    return pl.pallas_call(
        _gdn_kernel,
        grid=(b, nc),
        in_specs=[
            pl.BlockSpec((1, GDN_CHUNK, cd), lambda bi, ci: (bi, ci, 0)),
            pl.BlockSpec((1, GDN_CHUNK, vw), lambda bi, ci: (bi, ci, 0)),
            pl.BlockSpec((1, GDN_CHUNK, LANES), lambda bi, ci: (bi, ci, 0)),
            _const_spec(conv_w.shape), _const_spec(dt_bias_pad.shape),
            _const_spec(a_log_pad.shape), _const_spec(norm_w.shape),
        ],
        out_specs=pl.BlockSpec((1, GDN_CHUNK, vw), lambda bi, ci: (bi, ci, 0)),
        out_shape=jax.ShapeDtypeStruct((b, s, vw), BF16),
        scratch_shapes=[
            pltpu.VMEM((CONV_HALO, cd), F32),
            pltpu.VMEM((GDN_V_HEADS, GDN_DIM, GDN_DIM), F32),
        ],
        compiler_params=_params(("parallel", "arbitrary")),
        name="gated_deltanet",
    )(qkv, z, ba, conv_w, dt_bias_pad, a_log_pad, norm_w)


def _pad_lanes(v, offset=0):
    out = jnp.zeros((1, LANES), F32)
    return out.at[0, offset:offset + v.shape[0]].set(v.astype(F32))


def _row(v):
    return v.astype(F32).reshape(1, -1)


def _even_mixer(x2, batch, seq, norm_w, w_in, q_norm, k_norm, conv_w, conv_b, dt_bias, a_log, d_skip,
                ssm_norm, w_out, slopes):
    mw = MOBA_HEADS * MOBA_HEAD_DIM
    inner = SSM_HEADS * SSM_HEAD_DIM
    conv_dim = inner + 2 * SSM_GROUPS * SSM_STATE
    w = w_in.astype(BF16)
    c0, c1, c2, c3, c4 = mw, 2 * mw, 3 * mw, 3 * mw + inner, 3 * mw + inner + conv_dim
    wdt = jnp.pad(w[:, c4:], ((0, 0), (0, LANES - SSM_HEADS)))
    q, k, v, z, xbc, dt = _inproj_even(x2, _row(norm_w), w[:, :c0], w[:, c0:c1], w[:, c1:c2],
                                       w[:, c2:c3], w[:, c3:c4], wdt)
    q_hm = q.reshape(batch, seq, MOBA_HEADS, MOBA_HEAD_DIM).transpose(0, 2, 1, 3)
    k_hm = k.reshape(batch, seq, MOBA_HEADS, MOBA_HEAD_DIM).transpose(0, 2, 1, 3)
    vt_hm = v.reshape(batch, seq, MOBA_HEADS, MOBA_HEAD_DIM).transpose(0, 2, 3, 1)
    ot = _moba(q_hm, k_hm, vt_hm, _row(q_norm), _row(k_norm), slopes)
    attn = ot.transpose(0, 3, 1, 2).reshape(batch * seq, mw)

    ssm = _ssd(xbc.reshape(batch, seq, conv_dim), z.reshape(batch, seq, inner),
               dt.reshape(batch, seq, LANES), conv_w.astype(F32), _row(conv_b),
               _pad_lanes(dt_bias), _pad_lanes(a_log), _row(jnp.repeat(d_skip, SSM_HEAD_DIM)),
               _row(ssm_norm))
    wo = w_out.astype(BF16)
    return _outproj_even(x2, attn, ssm.reshape(batch * seq, inner), wo[:mw], wo[mw:])


def _gdn_mixer(x2, batch, seq, norm_w, w_in, conv_w, dt_bias, a_log, gdn_norm, w_out):
    conv_dim = 2 * GDN_K_HEADS * GDN_DIM + GDN_V_HEADS * GDN_DIM
    vw = GDN_V_HEADS * GDN_DIM
    w = w_in.astype(BF16)
    wba = jnp.pad(w[:, conv_dim + vw:], ((0, 0), (0, LANES - 2 * GDN_V_HEADS)))
    qkv, z, ba = _inproj_odd(x2, _row(norm_w), w[:, :conv_dim], w[:, conv_dim:conv_dim + vw], wba)
    o = _gdn(qkv.reshape(batch, seq, conv_dim), z.reshape(batch, seq, vw), ba.reshape(batch, seq, LANES),
             conv_w.astype(F32), _pad_lanes(dt_bias, GDN_V_HEADS), _pad_lanes(a_log, GDN_V_HEADS),
             _row(gdn_norm))
    return _outproj_odd(x2, o.reshape(batch * seq, vw), w_out.astype(BF16))


def kernel(x, p, norm_mix, norm_ffn, norm_ple, w_in_even, moba_q_norm, moba_k_norm, ssm_conv_w, ssm_conv_b, ssm_dt_bias, ssm_a_log, ssm_d, ssm_norm, w_out_even, w_in_odd, gdn_conv_w, gdn_dt_bias, gdn_a_log, gdn_norm, w_out_odd, ffn_w_gate, ffn_w_up, ffn_conv_w, ffn_conv_b, ffn_w_down, ple_w_proj, ple_w_gate):
    batch, seq, d = x.shape
    depth = p.shape[0]
    slopes = jnp.exp2(-ALIBI_MAX_BIAS * jnp.arange(1, MOBA_HEADS + 1, dtype=F32) / MOBA_HEADS)
    x2 = x.reshape(batch * seq, d)
    for i in range(depth):
        j = i // 2
        if i % 2 == 0:
            x2 = _even_mixer(x2, batch, seq, norm_mix[i], w_in_even[j], moba_q_norm[j], moba_k_norm[j],
                             ssm_conv_w[j], ssm_conv_b[j], ssm_dt_bias[j], ssm_a_log[j], ssm_d[j],
                             ssm_norm[j], w_out_even[j], slopes)
        else:
            x2 = _gdn_mixer(x2, batch, seq, norm_mix[i], w_in_odd[j], gdn_conv_w[j], gdn_dt_bias[j],
                            gdn_a_log[j], gdn_norm[j], w_out_odd[j])
        x2 = _ffn_ple(x2, p[i].reshape(batch * seq, -1), seq, _row(norm_ffn[i]),
                      ffn_w_gate[i].astype(BF16), ffn_w_up[i].astype(BF16), ffn_conv_w[i].astype(F32),
                      _row(ffn_conv_b[i]), ffn_w_down[i].astype(BF16), _row(norm_ple[i]),
                      ple_w_gate[i].astype(BF16), ple_w_proj[i].astype(BF16))
    return x2.reshape(batch, seq, d)
```

```python
import functools

import jax
import jax.numpy as jnp
from jax import lax
from jax.experimental import pallas as pl
from jax.experimental.pallas import tpu as pltpu

F32 = jnp.float32
BF16 = jnp.bfloat16
HIGHEST = lax.Precision.HIGHEST

EPS = 1e-6
NEG = -1e30
LANES = 128
CONV_HALO = 8

MOBA_HEADS = 8
MOBA_HEAD_DIM = 64
MOBA_BLOCK = 256
MOBA_TOPK = 3
MOBA_HEADS_PER_STEP = 4
ALIBI_MAX_BIAS = 8.0

SSM_HEADS = 16
SSM_HEAD_DIM = 64
SSM_STATE = 128
SSM_GROUPS = 2
SSM_CONV = 4
SSM_CHUNK = 128

GDN_K_HEADS = 8
GDN_V_HEADS = 16
GDN_DIM = 128
GDN_CONV = 4
GDN_CHUNK = 64

FFN_CONV = 3
FFN_HALO = 16
FFN_COLS = 256

VMEM_LIMIT = 56 * 1024 * 1024


def _rms_rows(x, gain):
    return x * lax.rsqrt(jnp.mean(x * x, axis=-1, keepdims=True) + EPS) * gain


def _sigmoid(x):
    return 1.0 / (1.0 + jnp.exp(-x))


def _silu(x):
    return x * _sigmoid(x)


def _softplus(x):
    return jnp.maximum(x, 0.0) + jnp.log(1.0 + jnp.exp(-jnp.abs(x)))


def _dot(a, b):
    return jnp.dot(a, b, preferred_element_type=F32)


def _dot_nt(a, b):
    return lax.dot_general(a, b, (((1,), (1,)), ((), ())), preferred_element_type=F32)


def _dot_exact(a, b):
    return jnp.dot(a, b, precision=HIGHEST, preferred_element_type=F32)


def _const_spec(shape):
    zeros = (0,) * len(shape)
    return pl.BlockSpec(shape, lambda *_: zeros, pipeline_mode=pl.Buffered(1))


def _params(semantics):
    return pltpu.CompilerParams(dimension_semantics=semantics, vmem_limit_bytes=VMEM_LIMIT)


def _causal_conv(ext, w_ref, width, halo):
    y = w_ref[width - 1:width, :] * ext[halo:]
    for shift in range(1, width):
        y = y + w_ref[width - 1 - shift:width - shift, :] * pltpu.roll(ext, shift, axis=0)[halo:]
    return y


def _inproj_even_kernel(x_ref, g_ref, wq_ref, wk_ref, wv_ref, wz_ref, wxbc_ref, wdt_ref,
                        q_ref, k_ref, v_ref, z_ref, xbc_ref, dt_ref):
    hn = _rms_rows(x_ref[...], g_ref[...]).astype(BF16)
    q_ref[...] = _dot(hn, wq_ref[...]).astype(BF16)
    k_ref[...] = _dot(hn, wk_ref[...]).astype(BF16)
    v_ref[...] = _dot(hn, wv_ref[...]).astype(BF16)
    z_ref[...] = _dot(hn, wz_ref[...]).astype(BF16)
    xbc_ref[...] = _dot(hn, wxbc_ref[...]).astype(BF16)
    dt_ref[...] = _dot(hn, wdt_ref[...])


def _inproj_even(x2, gain, wq, wk, wv, wz, wxbc, wdt, tm=512):
    t, d = x2.shape
    outs = [(wq.shape[1], BF16), (wk.shape[1], BF16), (wv.shape[1], BF16),
            (wz.shape[1], BF16), (wxbc.shape[1], BF16), (wdt.shape[1], F32)]
    return pl.pallas_call(
        _inproj_even_kernel,
        grid=(t // tm,),
        in_specs=[pl.BlockSpec((tm, d), lambda i: (i, 0)), _const_spec(gain.shape)]
        + [_const_spec(w.shape) for w in (wq, wk, wv, wz, wxbc, wdt)],
        out_specs=[pl.BlockSpec((tm, n), lambda i: (i, 0)) for n, _ in outs],
        out_shape=[jax.ShapeDtypeStruct((t, n), dt) for n, dt in outs],
        compiler_params=_params(("parallel",)),
        name="inproj_even",
    )(x2, gain, wq, wk, wv, wz, wxbc, wdt)


def _moba_kernel(slopes_ref, q_ref, k_ref, v_ref, qg_ref, kg_ref, o_ref, kn_s, kbar_s, vt_s, selb_s):
    hp = pl.program_id(1)
    qi = pl.program_id(2)
    seq = k_ref.shape[1]
    nb = seq // MOBA_BLOCK
    dh = MOBA_HEAD_DIM
    hps = MOBA_HEADS_PER_STEP
    heads = range(hps)

    lanes = hps * dh
    same_head = (lax.broadcasted_iota(jnp.int32, (lanes, lanes), 0) // dh
                 == lax.broadcasted_iota(jnp.int32, (lanes, lanes), 1) // dh)
    head_ones = jnp.where(same_head, 1.0, 0.0).astype(BF16)

    def head_rms(x, gain):
        sq = x * x
        hi = sq.astype(BF16)
        lo = (sq - hi.astype(F32)).astype(BF16)
        ms = (_dot(hi, head_ones) + _dot(lo, head_ones)) * (1.0 / dh)
        return x * lax.rsqrt(ms + EPS) * gain

    @pl.when(qi == 0)
    def _():
        kn = head_rms(k_ref[0].astype(F32), kg_ref[...])
        kn_s[...] = kn.astype(BF16)
        kbar_s[...] = jnp.mean(kn.reshape(nb, MOBA_BLOCK, lanes), axis=1)
        vt_s[...] = v_ref[0].astype(F32).T.astype(BF16)

    qn = head_rms(q_ref[0].astype(F32), qg_ref[...])
    lane_head = lax.broadcasted_iota(jnp.int32, (MOBA_BLOCK, lanes), 1) // dh
    blk = lax.broadcasted_iota(jnp.int32, (nb, MOBA_BLOCK), 0)
    kbar_head = lax.broadcasted_iota(jnp.int32, (nb, lanes), 1) // dh
    kbar = kbar_s[...]
    kbar_rows = jnp.concatenate([jnp.where(kbar_head == hh, kbar, 0.0) for hh in heads], axis=0)
    gates = lax.dot_general(kbar_rows, qn, (((1,), (1,)), ((), ())),
                            precision=HIGHEST, preferred_element_type=F32)
    slopes, qs = [], []
    for hh in heads:
        slopes.append(slopes_ref[hp * hps + hh])
        q_h = jnp.where(lane_head == hh, qn, 0.0)
        gate = gates[hh * nb:(hh + 1) * nb, :]
        rank = jnp.zeros(gate.shape, F32)
        for m in range(nb):
            gm = gate[m:m + 1, :]
            beats = (gm > gate) | ((gm == gate) & (m < blk))
            rank = rank + jnp.where(beats, jnp.where(m < qi, 1.0, 0.0), 0.0)
        selected = (blk < qi) & (rank < float(min(MOBA_TOPK, nb - 1)))
        selb_s[hh] = jnp.where(selected, 0.0, NEG)
        qs.append((q_h * (dh ** -0.5)).astype(BF16))

    kidx = lax.broadcasted_iota(jnp.int32, (MOBA_BLOCK, MOBA_BLOCK), 0)
    qidx = lax.broadcasted_iota(jnp.int32, (MOBA_BLOCK, MOBA_BLOCK), 1)
    rel = (kidx - qidx).astype(F32)

    q0 = pl.multiple_of(qi * MOBA_BLOCK, MOBA_BLOCK)
    scores = [_dot_nt(kn_s[pl.ds(q0, MOBA_BLOCK), :], qs[hh]) for hh in heads]
    ms, ls, ps = [], [], []
    for hh in heads:
        s = jnp.where(rel <= 0.0, scores[hh] + slopes[hh] * rel, NEG)
        m_run = jnp.max(s, axis=0, keepdims=True)
        p = jnp.exp(s - m_run)
        ms.append(m_run)
        ls.append(jnp.sum(p, axis=0, keepdims=True))
        ps.append(p.astype(BF16))
    accs = [_dot(vt_s[hh * dh:(hh + 1) * dh, pl.ds(q0, MOBA_BLOCK)], ps[hh]) for hh in heads]
    carry = []
    for hh in heads:
        carry += [ms[hh], ls[hh], accs[hh]]

    def past_block(j, carry):
        k0 = pl.multiple_of(j * MOBA_BLOCK, MOBA_BLOCK)
        dist = ((j - qi) * MOBA_BLOCK).astype(F32)
        scores = [_dot_nt(kn_s[pl.ds(k0, MOBA_BLOCK), :], qs[hh]) for hh in heads]
        ms, ls, ps, alphas = [], [], [], []
        for hh in heads:
            m_run, l_run = carry[3 * hh], carry[3 * hh + 1]
            bias = selb_s[hh, pl.ds(j, 1), :] + slopes[hh] * dist
            s = scores[hh] + slopes[hh] * rel + bias
            m_new = jnp.maximum(m_run, jnp.max(s, axis=0, keepdims=True))
            alpha = jnp.exp(m_run - m_new)
            p = jnp.exp(s - m_new)
            ms.append(m_new)
            ls.append(alpha * l_run + jnp.sum(p, axis=0, keepdims=True))
            ps.append(p.astype(BF16))
            alphas.append(alpha)
        pv = [_dot(vt_s[hh * dh:(hh + 1) * dh, pl.ds(k0, MOBA_BLOCK)], ps[hh]) for hh in heads]
        out = []
        for hh in heads:
            out += [ms[hh], ls[hh], alphas[hh] * carry[3 * hh + 2] + pv[hh]]
        return tuple(out)

    carry = lax.fori_loop(0, qi, past_block, tuple(carry))
    ot = jnp.concatenate([carry[3 * hh + 2] / carry[3 * hh + 1] for hh in heads], axis=0)
    o_ref[0] = ot.T.astype(BF16)


def _moba(q, k, v, q_gain, k_gain, slopes):
    b, s, width = q.shape
    dh = MOBA_HEAD_DIM
    hps = MOBA_HEADS_PER_STEP
    lanes = hps * dh
    nq = s // MOBA_BLOCK
    nb = s // MOBA_BLOCK
    return pl.pallas_call(
        _moba_kernel,
        grid=(b, width // lanes, nq),
        in_specs=[
            pl.BlockSpec(memory_space=pltpu.SMEM),
            pl.BlockSpec((1, MOBA_BLOCK, lanes), lambda bi, hi, qi: (bi, qi, hi)),
            pl.BlockSpec((1, s, lanes), lambda bi, hi, qi: (bi, 0, hi)),
            pl.BlockSpec((1, s, lanes), lambda bi, hi, qi: (bi, 0, hi)),
            pl.BlockSpec((1, lanes), lambda bi, hi, qi: (0, 0)),
            pl.BlockSpec((1, lanes), lambda bi, hi, qi: (0, 0)),
        ],
        out_specs=pl.BlockSpec((1, MOBA_BLOCK, lanes), lambda bi, hi, qi: (bi, qi, hi)),
        out_shape=jax.ShapeDtypeStruct((b, s, width), BF16),
        scratch_shapes=[
            pltpu.VMEM((s, lanes), BF16),
            pltpu.VMEM((nb, lanes), F32),
            pltpu.VMEM((lanes, s), BF16),
            pltpu.VMEM((hps, nb, MOBA_BLOCK), F32),
        ],
        compiler_params=_params(("parallel", "parallel", "arbitrary")),
        name="moba_attention",
    )(slopes, q, k, v, q_gain, k_gain)


def _ssd_kernel(xbc_ref, z_ref, dt_ref, cw_ref, cb_ref, dtb_ref, alog_ref, dskip_ref, nw_ref,
                out_ref, hist_s, state_s):
    c = pl.program_id(1)
    L = SSM_CHUNK
    inner = SSM_HEADS * SSM_HEAD_DIM
    gw = SSM_STATE
    pair_w = 2 * SSM_HEAD_DIM

    @pl.when(c == 0)
    def _():
        hist_s[...] = jnp.zeros_like(hist_s)
        state_s[...] = jnp.zeros_like(state_s)

    cur = xbc_ref[0].astype(F32)
    ext = jnp.concatenate([hist_s[...], cur], axis=0)
    hist_s[...] = cur[L - CONV_HALO:, :]
    act = _silu(_causal_conv(ext, cw_ref, SSM_CONV, CONV_HALO) + cb_ref[...])
    xs = act[:, :inner]

    dt = _softplus(dt_ref[0] + dtb_ref[...])
    a = dt * (-jnp.exp(alog_ref[...]))
    row = lax.broadcasted_iota(jnp.int32, (L, L), 0)
    col = lax.broadcasted_iota(jnp.int32, (L, L), 1)
    causal = row >= col
    acum = _dot_exact(jnp.where(causal, 1.0, 0.0), a)
    acum_t = acum.T

    hrow = lax.broadcasted_iota(jnp.int32, (LANES, inner), 0)
    hcol = lax.broadcasted_iota(jnp.int32, (LANES, inner), 1)
    widen = jnp.where(hcol // SSM_HEAD_DIM == hrow, 1.0, 0.0)
    dt_w = _dot_exact(dt, widen)
    acum_w = _dot_exact(acum, widen)
    last_w = acum_w[L - 1:L, :]
    xdt = xs * dt_w
    xdt_bf = xdt.astype(BF16)
    xdt_dec_bf = (xdt * jnp.exp(last_w - acum_w)).astype(BF16)
    exp_acum_w = jnp.exp(acum_w)
    chunk_decay_w = jnp.exp(last_w)
    lane = lax.broadcasted_iota(jnp.int32, (L, pair_w), 1)
    first_head = lane < SSM_HEAD_DIM

    heads_per_group = SSM_HEADS // SSM_GROUPS
    ys = []
    for g in range(SSM_GROUPS):
        bg = act[:, inner + g * gw:inner + (g + 1) * gw]
        cg = act[:, inner + SSM_GROUPS * gw + g * gw:inner + SSM_GROUPS * gw + (g + 1) * gw]
        cg_bf = cg.astype(BF16)
        cb = _dot_nt(cg_bf, bg.astype(BF16))
        bg_t = bg.T.astype(BF16)
        for pp in range(heads_per_group // 2):
            h0 = g * heads_per_group + 2 * pp
            pi = h0 // 2
            sl = slice(h0 * SSM_HEAD_DIM, h0 * SSM_HEAD_DIM + pair_w)
            yd = []
            for hh in (h0, h0 + 1):
                diff = jnp.broadcast_to(acum[:, hh:hh + 1], (L, L)) - acum_t[hh:hh + 1, :]
                decay = jnp.exp(jnp.where(causal, diff, NEG))
                yd.append(_dot((cb * decay).astype(BF16), xdt_bf[:, sl]))
            y_diag = jnp.where(first_head, yd[0], yd[1])
            prev = state_s[pi]
            y_off = _dot(cg_bf, prev.astype(BF16)) * exp_acum_w[:, sl]
            state_s[pi] = prev * chunk_decay_w[:, sl] + _dot(bg_t, xdt_dec_bf[:, sl])
            ys.append(y_diag + y_off + dskip_ref[:, sl] * xs[:, sl])
    y = jnp.concatenate(ys, axis=1)

    zf = z_ref[0].astype(F32)
    yg = y * _silu(zf)
    gsz = inner // SSM_GROUPS
    parts = []
    for g in range(SSM_GROUPS):
        seg = yg[:, g * gsz:(g + 1) * gsz]
        parts.append(seg * lax.rsqrt(jnp.mean(seg * seg, axis=-1, keepdims=True) + EPS))
    out_ref[0] = (jnp.concatenate(parts, axis=1) * nw_ref[...]).astype(BF16)


def _ssd(xbc, z, dt, conv_w, conv_b, dt_bias, a_log, d_skip_w, norm_w):
    b, s, cd = xbc.shape
    inner = z.shape[2]
    nc = s // SSM_CHUNK
    return pl.pallas_call(
        _ssd_kernel,
        grid=(b, nc),
        in_specs=[
            pl.BlockSpec((1, SSM_CHUNK, cd), lambda bi, ci: (bi, ci, 0)),
            pl.BlockSpec((1, SSM_CHUNK, inner), lambda bi, ci: (bi, ci, 0)),
            pl.BlockSpec((1, SSM_CHUNK, LANES), lambda bi, ci: (bi, ci, 0)),
            _const_spec(conv_w.shape), _const_spec(conv_b.shape), _const_spec(dt_bias.shape),
            _const_spec(a_log.shape), _const_spec(d_skip_w.shape), _const_spec(norm_w.shape),
        ],
        out_specs=pl.BlockSpec((1, SSM_CHUNK, inner), lambda bi, ci: (bi, ci, 0)),
        out_shape=jax.ShapeDtypeStruct((b, s, inner), BF16),
        scratch_shapes=[
            pltpu.VMEM((CONV_HALO, cd), F32),
            pltpu.VMEM((SSM_HEADS // 2, SSM_STATE, 2 * SSM_HEAD_DIM), F32),
        ],
        compiler_params=_params(("parallel", "arbitrary")),
        name="ssd_heads",
    )(xbc, z, dt, conv_w, conv_b, dt_bias, a_log, d_skip_w, norm_w)


def _outproj_even_kernel(x_ref, a_ref, s_ref, wa_ref, ws_ref, o_ref):
    o_ref[...] = x_ref[...] + _dot(a_ref[...], wa_ref[...]) + _dot(s_ref[...], ws_ref[...])


def _outproj_even(x2, attn, ssm, wa, ws, tm=512):
    t, d = x2.shape
    return pl.pallas_call(
        _outproj_even_kernel,
        grid=(t // tm,),
        in_specs=[pl.BlockSpec((tm, d), lambda i: (i, 0)),
                  pl.BlockSpec((tm, attn.shape[1]), lambda i: (i, 0)),
                  pl.BlockSpec((tm, ssm.shape[1]), lambda i: (i, 0)),
                  _const_spec(wa.shape), _const_spec(ws.shape)],
        out_specs=pl.BlockSpec((tm, d), lambda i: (i, 0)),
        out_shape=jax.ShapeDtypeStruct((t, d), F32),
        compiler_params=_params(("parallel",)),
        name="outproj_even",
    )(x2, attn, ssm, wa, ws)


def _outproj_odd_kernel(x_ref, o_in_ref, w_ref, o_ref):
    o_ref[...] = x_ref[...] + _dot(o_in_ref[...], w_ref[...])


def _outproj_odd(x2, o_in, w, tm=512):
    t, d = x2.shape
    return pl.pallas_call(
        _outproj_odd_kernel,
        grid=(t // tm,),
        in_specs=[pl.BlockSpec((tm, d), lambda i: (i, 0)),
                  pl.BlockSpec((tm, o_in.shape[1]), lambda i: (i, 0)),
                  _const_spec(w.shape)],
        out_specs=pl.BlockSpec((tm, d), lambda i: (i, 0)),
        out_shape=jax.ShapeDtypeStruct((t, d), F32),
        compiler_params=_params(("parallel",)),
        name="outproj_odd",
    )(x2, o_in, w)


def _ffn_ple_kernel(x_ref, halo_ref, p_ref, gf_ref, wg_ref, wu_ref, cw_ref, cb_ref, wd_ref,
                    gp_ref, wpg_ref, wpp_ref, o_ref, *, tm, seq):
    i = pl.program_id(0)
    x = x_ref[...]
    keep_halo = jnp.where((i * tm) % seq == 0, 0.0, 1.0)
    halo = halo_ref[...] * keep_halo
    hn = _rms_rows(jnp.concatenate([halo, x], axis=0), gf_ref[...]).astype(BF16)
    hn_cur = hn[FFN_HALO:]

    d_ff = wg_ref.shape[1]
    acc = x
    for j in range(d_ff // FFN_COLS):
        cs = slice(j * FFN_COLS, (j + 1) * FFN_COLS)
        g = _dot(hn, wg_ref[:, cs])
        gc = cb_ref[:, cs] + cw_ref[FFN_CONV - 1:FFN_CONV, cs] * g[FFN_HALO:]
        for shift in range(1, FFN_CONV):
            gc = gc + cw_ref[FFN_CONV - 1 - shift:FFN_CONV - shift, cs] * pltpu.roll(g, shift, axis=0)[FFN_HALO:]
        u = _dot(hn_cur, wu_ref[:, cs])
        acc = acc + _dot((_silu(gc) * u).astype(BF16), wd_ref[cs, :])
    x1 = acc
    hp = _rms_rows(x1, gp_ref[...]).astype(BF16)
    gate = _sigmoid(_dot(hp, wpg_ref[...]))
    o_ref[...] = x1 + gate * _dot(p_ref[...].astype(BF16), wpp_ref[...])


def _ffn_ple(x2, p2, seq, gain_ffn, wg, wu, conv_w, conv_b, wd, gain_ple, wpg, wpp, tm=512):
    t, d = x2.shape
    halo_blocks = tm // FFN_HALO
    consts = (gain_ffn, wg, wu, conv_w, conv_b, wd, gain_ple, wpg, wpp)
    return pl.pallas_call(
        functools.partial(_ffn_ple_kernel, tm=tm, seq=seq),
        grid=(t // tm,),
        in_specs=[pl.BlockSpec((tm, d), lambda i: (i, 0)),
                  pl.BlockSpec((FFN_HALO, d), lambda i: (jnp.maximum(i * halo_blocks - 1, 0), 0)),
                  pl.BlockSpec((tm, p2.shape[1]), lambda i: (i, 0))]
        + [_const_spec(c.shape) for c in consts],
        out_specs=pl.BlockSpec((tm, d), lambda i: (i, 0)),
        out_shape=jax.ShapeDtypeStruct((t, d), F32),
        compiler_params=_params(("parallel",)),
        name="ffn_ple",
    )(x2, x2, p2, *consts)


def _inproj_odd_kernel(x_ref, g_ref, wqkv_ref, wz_ref, wba_ref, qkv_ref, z_ref, ba_ref):
    hn = _rms_rows(x_ref[...], g_ref[...]).astype(BF16)
    qkv_ref[...] = _dot(hn, wqkv_ref[...]).astype(BF16)
    z_ref[...] = _dot(hn, wz_ref[...]).astype(BF16)
    ba_ref[...] = _dot(hn, wba_ref[...])


def _inproj_odd(x2, gain, wqkv, wz, wba, tm=256):
    t, d = x2.shape
    outs = [(wqkv.shape[1], BF16), (wz.shape[1], BF16), (wba.shape[1], F32)]
    return pl.pallas_call(
        _inproj_odd_kernel,
        grid=(t // tm,),
        in_specs=[pl.BlockSpec((tm, d), lambda i: (i, 0)), _const_spec(gain.shape)]
        + [_const_spec(w.shape) for w in (wqkv, wz, wba)],
        out_specs=[pl.BlockSpec((tm, n), lambda i: (i, 0)) for n, _ in outs],
        out_shape=[jax.ShapeDtypeStruct((t, n), dt) for n, dt in outs],
        compiler_params=_params(("parallel",)),
        name="inproj_odd",
    )(x2, gain, wqkv, wz, wba)


def _bdot(a, b):
    return _dot(a.astype(BF16), b.astype(BF16))


def _unit_lower_inverses(mats):
    L = mats[0].shape[0]
    row = lax.broadcasted_iota(jnp.int32, (L, L), 0)
    col = lax.broadcasted_iota(jnp.int32, (L, L), 1)
    eye = jnp.where(row == col, 1.0, 0.0)
    base = 8
    same = (row // base) == (col // base)
    ns = [-jnp.where(same, a, 0.0) for a in mats]
    ts = [eye + n for n in ns]
    pws = [_bdot(n, n) for n in ns]
    ts = [t + _bdot(t, pw) for t, pw in zip(ts, pws)]
    pws = [_bdot(pw, pw) for pw in pws]
    ts = [t + _bdot(t, pw) for t, pw in zip(ts, pws)]
    size = base
    while size < L:
        wider = (row // (2 * size)) == (col // (2 * size))
        offs = [jnp.where(wider & jnp.logical_not(same), a, 0.0) for a in mats]
        ys = [_bdot(off, t) for off, t in zip(offs, ts)]
        ts = [t - _bdot(t, y) for t, y in zip(ts, ys)]
        same = wider
        size *= 2
    return ts


def _gdn_kernel(qkv_ref, z_ref, ba_ref, cw_ref, dtb_ref, alog_ref, nw_ref, out_ref, hist_s, state_s):
    c = pl.program_id(1)
    L = GDN_CHUNK
    dk = GDN_DIM
    qk_w = GDN_K_HEADS * dk
    rep = GDN_V_HEADS // GDN_K_HEADS

    @pl.when(c == 0)
    def _():
        hist_s[...] = jnp.zeros_like(hist_s)
        state_s[...] = jnp.zeros_like(state_s)

    cur = qkv_ref[0].astype(F32)
    ext = jnp.concatenate([hist_s[...], cur], axis=0)
    hist_s[...] = cur[L - CONV_HALO:, :]
    act = _silu(_causal_conv(ext, cw_ref, GDN_CONV, CONV_HALO))

    ba = ba_ref[0]
    beta = _sigmoid(ba)
    g = -jnp.exp(alog_ref[...]) * _softplus(ba + dtb_ref[...])
    row = lax.broadcasted_iota(jnp.int32, (L, L), 0)
    col = lax.broadcasted_iota(jnp.int32, (L, L), 1)
    lower = row >= col
    strict = row > col
    gc = _dot_exact(jnp.where(lower, 1.0, 0.0), g)
    gc_t = gc.T

    def l2n(t):
        return t * lax.rsqrt(jnp.sum(t * t, axis=-1, keepdims=True) + EPS)

    qs, ks, kts, kk, qk0 = [], [], [], [], []
    for kh in range(GDN_K_HEADS):
        qn = l2n(act[:, kh * dk:(kh + 1) * dk]) * (dk ** -0.5)
        kn = l2n(act[:, qk_w + kh * dk:qk_w + (kh + 1) * dk])
        kt = kn.T
        prod = _bdot(jnp.concatenate([kn, qn], axis=0), kt)
        qs.append(qn)
        ks.append(kn)
        kts.append(kt)
        kk.append(prod[:L])
        qk0.append(prod[L:])

    heads = range(GDN_V_HEADS)
    beta_c = [beta[:, hv:hv + 1] for hv in heads]
    gc_c = [gc[:, GDN_V_HEADS + hv:GDN_V_HEADS + hv + 1] for hv in heads]
    gc_r = [gc_t[GDN_V_HEADS + hv:GDN_V_HEADS + hv + 1, :] for hv in heads]
    g_last = [r[:, L - 1:L] for r in gc_r]
    egc = [jnp.exp(cc) for cc in gc_c]
    decay = [jnp.exp(jnp.where(lower, cc - rr, NEG)) for cc, rr in zip(gc_c, gc_r)]
    a_mats = [jnp.where(strict, kk[hv // rep] * beta_c[hv] * decay[hv], 0.0) for hv in heads]
    t_inv = _unit_lower_inverses(a_mats)

    sols = []
    for hv in heads:
        v_h = act[:, 2 * qk_w + hv * dk:2 * qk_w + (hv + 1) * dk]
        kbeta = ks[hv // rep] * beta_c[hv]
        sols.append(_bdot(t_inv[hv], jnp.concatenate([v_h * beta_c[hv], kbeta * egc[hv]], axis=1)))

    states = [state_s[hv] for hv in heads]
    from_state = [_bdot(jnp.concatenate([sols[hv][:, dk:], qs[hv // rep] * egc[hv]], axis=0), states[hv])
                  for hv in heads]
    v_new = [sols[hv][:, :dk] - from_state[hv][:L] for hv in heads]
    from_v = [_bdot(jnp.concatenate([qk0[hv // rep] * decay[hv],
                                     kts[hv // rep] * jnp.exp(g_last[hv] - gc_r[hv])], axis=0), v_new[hv])
              for hv in heads]
    zf = z_ref[0].astype(F32)
    for hv in heads:
        state_s[hv] = states[hv] * jnp.exp(g_last[hv]) + from_v[hv][L:]
        o = from_state[hv][L:] + from_v[hv][:L]
        o = o * lax.rsqrt(jnp.mean(o * o, axis=-1, keepdims=True) + EPS) * nw_ref[...]
        sl = slice(hv * dk, (hv + 1) * dk)
        out_ref[0, :, sl] = (o * _silu(zf[:, sl])).astype(BF16)


def _gdn(qkv, z, ba, conv_w, dt_bias_pad, a_log_pad, norm_w):
    b, s, cd = qkv.shape
    vw = z.shape[2]
    nc = s // GDN_CHUNK
    return pl.pallas_call(
        _gdn_kernel,
        grid=(b, nc),
        in_specs=[
            pl.BlockSpec((1, GDN_CHUNK, cd), lambda bi, ci: (bi, ci, 0)),
            pl.BlockSpec((1, GDN_CHUNK, vw), lambda bi, ci: (bi, ci, 0)),
            pl.BlockSpec((1, GDN_CHUNK, LANES), lambda bi, ci: (bi, ci, 0)),
            _const_spec(conv_w.shape), _const_spec(dt_bias_pad.shape),
            _const_spec(a_log_pad.shape), _const_spec(norm_w.shape),
        ],
        out_specs=pl.BlockSpec((1, GDN_CHUNK, vw), lambda bi, ci: (bi, ci, 0)),
        out_shape=jax.ShapeDtypeStruct((b, s, vw), BF16),
        scratch_shapes=[
            pltpu.VMEM((CONV_HALO, cd), F32),
            pltpu.VMEM((GDN_V_HEADS, GDN_DIM, GDN_DIM), F32),
        ],
        compiler_params=_params(("parallel", "arbitrary")),
        name="gated_deltanet",
    )(qkv, z, ba, conv_w, dt_bias_pad, a_log_pad, norm_w)


def _pad_lanes(v, offset=0):
    out = jnp.zeros((1, LANES), F32)
    return out.at[0, offset:offset + v.shape[0]].set(v.astype(F32))


def _row(v):
    return v.astype(F32).reshape(1, -1)


def _even_mixer(x2, batch, seq, norm_w, w_in, q_norm, k_norm, conv_w, conv_b, dt_bias, a_log, d_skip,
                ssm_norm, w_out, slopes):
    mw = MOBA_HEADS * MOBA_HEAD_DIM
    inner = SSM_HEADS * SSM_HEAD_DIM
    conv_dim = inner + 2 * SSM_GROUPS * SSM_STATE
    w = w_in.astype(BF16)
    c0, c1, c2, c3, c4 = mw, 2 * mw, 3 * mw, 3 * mw + inner, 3 * mw + inner + conv_dim
    wdt = jnp.pad(w[:, c4:], ((0, 0), (0, LANES - SSM_HEADS)))
    q, k, v, z, xbc, dt = _inproj_even(x2, _row(norm_w), w[:, :c0], w[:, c0:c1], w[:, c1:c2],
                                       w[:, c2:c3], w[:, c3:c4], wdt)
    attn = _moba(q.reshape(batch, seq, mw), k.reshape(batch, seq, mw), v.reshape(batch, seq, mw),
                 _row(jnp.tile(q_norm, MOBA_HEADS_PER_STEP)), _row(jnp.tile(k_norm, MOBA_HEADS_PER_STEP)),
                 slopes).reshape(batch * seq, mw)

    ssm = _ssd(xbc.reshape(batch, seq, conv_dim), z.reshape(batch, seq, inner),
               dt.reshape(batch, seq, LANES), conv_w.astype(F32), _row(conv_b),
               _pad_lanes(dt_bias), _pad_lanes(a_log), _row(jnp.repeat(d_skip, SSM_HEAD_DIM)),
               _row(ssm_norm))
    wo = w_out.astype(BF16)
    return _outproj_even(x2, attn, ssm.reshape(batch * seq, inner), wo[:mw], wo[mw:])


def _gdn_mixer(x2, batch, seq, norm_w, w_in, conv_w, dt_bias, a_log, gdn_norm, w_out):
    conv_dim = 2 * GDN_K_HEADS * GDN_DIM + GDN_V_HEADS * GDN_DIM
    vw = GDN_V_HEADS * GDN_DIM
    w = w_in.astype(BF16)
    wba = jnp.pad(w[:, conv_dim + vw:], ((0, 0), (0, LANES - 2 * GDN_V_HEADS)))
    qkv, z, ba = _inproj_odd(x2, _row(norm_w), w[:, :conv_dim], w[:, conv_dim:conv_dim + vw], wba)
    o = _gdn(qkv.reshape(batch, seq, conv_dim), z.reshape(batch, seq, vw), ba.reshape(batch, seq, LANES),
             conv_w.astype(F32), _pad_lanes(dt_bias, GDN_V_HEADS), _pad_lanes(a_log, GDN_V_HEADS),
             _row(gdn_norm))
    return _outproj_odd(x2, o.reshape(batch * seq, vw), w_out.astype(BF16))


def kernel(x, p, norm_mix, norm_ffn, norm_ple, w_in_even, moba_q_norm, moba_k_norm, ssm_conv_w, ssm_conv_b, ssm_dt_bias, ssm_a_log, ssm_d, ssm_norm, w_out_even, w_in_odd, gdn_conv_w, gdn_dt_bias, gdn_a_log, gdn_norm, w_out_odd, ffn_w_gate, ffn_w_up, ffn_conv_w, ffn_conv_b, ffn_w_down, ple_w_proj, ple_w_gate):
    batch, seq, d = x.shape
    depth = p.shape[0]
    slopes = jnp.exp2(-ALIBI_MAX_BIAS * jnp.arange(1, MOBA_HEADS + 1, dtype=F32) / MOBA_HEADS)
    x2 = x.reshape(batch * seq, d)
    for i in range(depth):
        j = i // 2
        if i % 2 == 0:
            x2 = _even_mixer(x2, batch, seq, norm_mix[i], w_in_even[j], moba_q_norm[j], moba_k_norm[j],
                             ssm_conv_w[j], ssm_conv_b[j], ssm_dt_bias[j], ssm_a_log[j], ssm_d[j],
                             ssm_norm[j], w_out_even[j], slopes)
        else:
            x2 = _gdn_mixer(x2, batch, seq, norm_mix[i], w_in_odd[j], gdn_conv_w[j], gdn_dt_bias[j],
                            gdn_a_log[j], gdn_norm[j], w_out_odd[j])
        x2 = _ffn_ple(x2, p[i].reshape(batch * seq, -1), seq, _row(norm_ffn[i]),
                      ffn_w_gate[i].astype(BF16), ffn_w_up[i].astype(BF16), ffn_conv_w[i].astype(F32),
                      _row(ffn_conv_b[i]), ffn_w_down[i].astype(BF16), _row(norm_ple[i]),
                      ple_w_gate[i].astype(BF16), ple_w_proj[i].astype(BF16))
    return x2.reshape(batch, seq, d)
```

```python
import functools

import jax
import jax.numpy as jnp
from jax import lax
from jax.experimental import pallas as pl
from jax.experimental.pallas import tpu as pltpu

F32 = jnp.float32
BF16 = jnp.bfloat16
HIGHEST = lax.Precision.HIGHEST

EPS = 1e-6
NEG = -1e30
LANES = 128
CONV_HALO = 8
CONV_COLS = 512

MOBA_HEADS = 8
MOBA_HEAD_DIM = 64
MOBA_BLOCK = 256
MOBA_TOPK = 3
MOBA_HEADS_PER_STEP = 4
ALIBI_MAX_BIAS = 8.0

SSM_HEADS = 16
SSM_HEAD_DIM = 64
SSM_STATE = 128
SSM_GROUPS = 2
SSM_CONV = 4
SSM_CHUNK = 128

GDN_K_HEADS = 8
GDN_V_HEADS = 16
GDN_DIM = 128
GDN_CONV = 4
GDN_CHUNK = 64

FFN_CONV = 3
FFN_HALO = 16
FFN_COLS = 256

VMEM_LIMIT = 56 * 1024 * 1024


def _rms_rows(x, gain):
    return x * lax.rsqrt(jnp.mean(x * x, axis=-1, keepdims=True) + EPS) * gain


def _sigmoid(x):
    return 0.5 * jnp.tanh(0.5 * x) + 0.5


def _silu(x):
    return x * _sigmoid(x)


def _softplus(x):
    return jnp.maximum(x, 0.0) + jnp.log(1.0 + jnp.exp(-jnp.abs(x)))


def _dot(a, b):
    return jnp.dot(a, b, preferred_element_type=F32)


def _dot_nt(a, b):
    return lax.dot_general(a, b, (((1,), (1,)), ((), ())), preferred_element_type=F32)


def _dot_exact(a, b):
    return jnp.dot(a, b, precision=HIGHEST, preferred_element_type=F32)


def _head_sums(x, head_w):
    rows, width = x.shape
    slab = 2 * LANES
    n = width // slab
    same = (lax.broadcasted_iota(jnp.int32, (slab, slab), 0) // head_w
            == lax.broadcasted_iota(jnp.int32, (slab, slab), 1) // head_w)
    ones = jnp.where(same, 1.0, 0.0).astype(BF16)
    stacked = jnp.concatenate([x[:, i * slab:(i + 1) * slab] for i in range(n)], axis=0)
    sums = _dot(stacked.astype(BF16), ones)
    return jnp.concatenate([sums[i * rows:(i + 1) * rows] for i in range(n)], axis=1)


def _const_spec(shape):
    zeros = (0,) * len(shape)
    return pl.BlockSpec(shape, lambda *_: zeros, pipeline_mode=pl.Buffered(1))


def _params(semantics):
    return pltpu.CompilerParams(dimension_semantics=semantics, vmem_limit_bytes=VMEM_LIMIT)


def _project_conv_silu(hn, w_ref, cw_ref, cb_ref, out_ref, hist_s, first, width):
    @pl.when(first)
    def _():
        hist_s[...] = jnp.zeros_like(hist_s)

    rows = hn.shape[0]
    for c0 in range(0, w_ref.shape[1], CONV_COLS):
        cs = slice(c0, c0 + CONV_COLS)
        cur = _dot(hn, w_ref[:, cs])
        ext = jnp.concatenate([hist_s[:, cs], cur], axis=0)
        hist_s[:, cs] = cur[rows - CONV_HALO:, :]
        y = cw_ref[width - 1:width, cs] * cur
        for shift in range(1, width):
            y = y + cw_ref[width - 1 - shift:width - shift, cs] * pltpu.roll(ext, shift, axis=0)[CONV_HALO:]
        if cb_ref is not None:
            y = y + cb_ref[:, cs]
        out_ref[:, cs] = _silu(y).astype(BF16)


def _inproj_even_kernel(x_ref, g_ref, wq_ref, wk_ref, wv_ref, wz_ref, wxbc_ref, wdt_ref, cw_ref, cb_ref,
                        q_ref, k_ref, v_ref, z_ref, xbc_ref, dt_ref, hist_s, *, tm, seq):
    first = (pl.program_id(0) * tm) % seq == 0
    hn = _rms_rows(x_ref[...], g_ref[...]).astype(BF16)
    q_ref[...] = _dot(hn, wq_ref[...]).astype(BF16)
    k_ref[...] = _dot(hn, wk_ref[...]).astype(BF16)
    v_ref[...] = _dot(hn, wv_ref[...]).astype(BF16)
    z_ref[...] = _dot(hn, wz_ref[...]).astype(BF16)
    _project_conv_silu(hn, wxbc_ref, cw_ref, cb_ref, xbc_ref, hist_s, first, SSM_CONV)
    dt_ref[...] = _dot(hn, wdt_ref[...])


def _inproj_even(x2, seq, gain, wq, wk, wv, wz, wxbc, wdt, conv_w, conv_b, tm=512):
    t, d = x2.shape
    outs = [(wq.shape[1], BF16), (wk.shape[1], BF16), (wv.shape[1], BF16),
            (wz.shape[1], BF16), (wxbc.shape[1], BF16), (wdt.shape[1], F32)]
    consts = (gain, wq, wk, wv, wz, wxbc, wdt, conv_w, conv_b)
    return pl.pallas_call(
        functools.partial(_inproj_even_kernel, tm=tm, seq=seq),
        grid=(t // tm,),
        in_specs=[pl.BlockSpec((tm, d), lambda i: (i, 0))] + [_const_spec(c.shape) for c in consts],
        out_specs=[pl.BlockSpec((tm, n), lambda i: (i, 0)) for n, _ in outs],
        out_shape=[jax.ShapeDtypeStruct((t, n), dt) for n, dt in outs],
        scratch_shapes=[pltpu.VMEM((CONV_HALO, wxbc.shape[1]), F32)],
        compiler_params=_params(("arbitrary",)),
        name="inproj_even",
    )(x2, *consts)


def _moba_kernel(slopes_ref, q_ref, k_ref, v_ref, qg_ref, kg_ref, o_ref, kn_s, kbar_s, vt_s, selb_s):
    hp = pl.program_id(1)
    qi = pl.program_id(2)
    seq = k_ref.shape[1]
    nb = seq // MOBA_BLOCK
    dh = MOBA_HEAD_DIM
    hps = MOBA_HEADS_PER_STEP
    heads = range(hps)

    lanes = hps * dh
    same_head = (lax.broadcasted_iota(jnp.int32, (lanes, lanes), 0) // dh
                 == lax.broadcasted_iota(jnp.int32, (lanes, lanes), 1) // dh)
    head_ones = jnp.where(same_head, 1.0, 0.0).astype(BF16)

    def head_rms(x, gain):
        sq = x * x
        hi = sq.astype(BF16)
        lo = (sq - hi.astype(F32)).astype(BF16)
        ms = (_dot(hi, head_ones) + _dot(lo, head_ones)) * (1.0 / dh)
        return x * lax.rsqrt(ms + EPS) * gain

    @pl.when(qi == 0)
    def _():
        kn = head_rms(k_ref[0].astype(F32), kg_ref[...])
        kn_s[...] = kn.astype(BF16)
        kbar_s[...] = jnp.mean(kn.reshape(nb, MOBA_BLOCK, lanes), axis=1)
        vt_s[...] = v_ref[0].astype(F32).T.astype(BF16)

    qn = head_rms(q_ref[0].astype(F32), qg_ref[...])
    lane_head = lax.broadcasted_iota(jnp.int32, (MOBA_BLOCK, lanes), 1) // dh
    blk = lax.broadcasted_iota(jnp.int32, (nb, MOBA_BLOCK), 0)
    kbar_head = lax.broadcasted_iota(jnp.int32, (nb, lanes), 1) // dh
    kbar = kbar_s[...]
    kbar_rows = jnp.concatenate([jnp.where(kbar_head == hh, kbar, 0.0) for hh in heads], axis=0)
    gates = lax.dot_general(kbar_rows, qn, (((1,), (1,)), ((), ())),
                            precision=HIGHEST, preferred_element_type=F32)
    slopes, qs = [], []
    for hh in heads:
        slopes.append(slopes_ref[hp * hps + hh])
        q_h = jnp.where(lane_head == hh, qn, 0.0)
        gate = gates[hh * nb:(hh + 1) * nb, :]
        rank = jnp.zeros(gate.shape, F32)
        for m in range(nb):
            gm = gate[m:m + 1, :]
            beats = (gm > gate) | ((gm == gate) & (m < blk))
            rank = rank + jnp.where(beats, jnp.where(m < qi, 1.0, 0.0), 0.0)
        selected = (blk < qi) & (rank < float(min(MOBA_TOPK, nb - 1)))
        selb_s[hh] = jnp.where(selected, 0.0, NEG)
        qs.append((q_h * (dh ** -0.5)).astype(BF16))

    kidx = lax.broadcasted_iota(jnp.int32, (MOBA_BLOCK, MOBA_BLOCK), 0)
    qidx = lax.broadcasted_iota(jnp.int32, (MOBA_BLOCK, MOBA_BLOCK), 1)
    rel = (kidx - qidx).astype(F32)

    q0 = pl.multiple_of(qi * MOBA_BLOCK, MOBA_BLOCK)
    scores = [_dot_nt(kn_s[pl.ds(q0, MOBA_BLOCK), :], qs[hh]) for hh in heads]
    ms, ls, ps = [], [], []
    for hh in heads:
        s = jnp.where(rel <= 0.0, scores[hh] + slopes[hh] * rel, NEG)
        m_run = jnp.max(s, axis=0, keepdims=True)
        p = jnp.exp(s - m_run)
        ms.append(m_run)
        ls.append(jnp.sum(p, axis=0, keepdims=True))
        ps.append(p.astype(BF16))
    accs = [_dot(vt_s[hh * dh:(hh + 1) * dh, pl.ds(q0, MOBA_BLOCK)], ps[hh]) for hh in heads]
    carry = []
    for hh in heads:
        carry += [ms[hh], ls[hh], accs[hh]]

    def past_block(j, carry):
        k0 = pl.multiple_of(j * MOBA_BLOCK, MOBA_BLOCK)
        dist = ((j - qi) * MOBA_BLOCK).astype(F32)
        scores = [_dot_nt(kn_s[pl.ds(k0, MOBA_BLOCK), :], qs[hh]) for hh in heads]
        ms, ls, ps, alphas = [], [], [], []
        for hh in heads:
            m_run, l_run = carry[3 * hh], carry[3 * hh + 1]
            bias = selb_s[hh, pl.ds(j, 1), :] + slopes[hh] * dist
            s = scores[hh] + slopes[hh] * rel + bias
            m_new = jnp.maximum(m_run, jnp.max(s, axis=0, keepdims=True))
            alpha = jnp.exp(m_run - m_new)
            p = jnp.exp(s - m_new)
            ms.append(m_new)
            ls.append(alpha * l_run + jnp.sum(p, axis=0, keepdims=True))
            ps.append(p.astype(BF16))
            alphas.append(alpha)
        pv = [_dot(vt_s[hh * dh:(hh + 1) * dh, pl.ds(k0, MOBA_BLOCK)], ps[hh]) for hh in heads]
        out = []
        for hh in heads:
            out += [ms[hh], ls[hh], alphas[hh] * carry[3 * hh + 2] + pv[hh]]
        return tuple(out)

    carry = lax.fori_loop(0, qi, past_block, tuple(carry))
    ot = jnp.concatenate([carry[3 * hh + 2] / carry[3 * hh + 1] for hh in heads], axis=0)
    o_ref[0] = ot.T.astype(BF16)


def _moba(q, k, v, q_gain, k_gain, slopes):
    b, s, width = q.shape
    dh = MOBA_HEAD_DIM
    hps = MOBA_HEADS_PER_STEP
    lanes = hps * dh
    nq = s // MOBA_BLOCK
    nb = s // MOBA_BLOCK
    return pl.pallas_call(
        _moba_kernel,
        grid=(b, width // lanes, nq),
        in_specs=[
            pl.BlockSpec(memory_space=pltpu.SMEM),
            pl.BlockSpec((1, MOBA_BLOCK, lanes), lambda bi, hi, qi: (bi, qi, hi)),
            pl.BlockSpec((1, s, lanes), lambda bi, hi, qi: (bi, 0, hi)),
            pl.BlockSpec((1, s, lanes), lambda bi, hi, qi: (bi, 0, hi)),
            pl.BlockSpec((1, lanes), lambda bi, hi, qi: (0, 0)),
            pl.BlockSpec((1, lanes), lambda bi, hi, qi: (0, 0)),
        ],
        out_specs=pl.BlockSpec((1, MOBA_BLOCK, lanes), lambda bi, hi, qi: (bi, qi, hi)),
        out_shape=jax.ShapeDtypeStruct((b, s, width), BF16),
        scratch_shapes=[
            pltpu.VMEM((s, lanes), BF16),
            pltpu.VMEM((nb, lanes), F32),
            pltpu.VMEM((lanes, s), BF16),
            pltpu.VMEM((hps, nb, MOBA_BLOCK), F32),
        ],
        compiler_params=_params(("parallel", "parallel", "arbitrary")),
        name="moba_attention",
    )(slopes, q, k, v, q_gain, k_gain)


def _ssd_kernel(xbc_ref, z_ref, dt_ref, dtb_ref, alog_ref, dskip_ref, nw_ref, out_ref, state_s):
    c = pl.program_id(1)
    L = SSM_CHUNK
    inner = SSM_HEADS * SSM_HEAD_DIM
    gw = SSM_STATE
    pair_w = 2 * SSM_HEAD_DIM

    @pl.when(c == 0)
    def _():
        state_s[...] = jnp.zeros_like(state_s)

    act = xbc_ref[0].astype(F32)
    xs = act[:, :inner]

    dt = _softplus(dt_ref[0] + dtb_ref[...])
    a = dt * (-jnp.exp(alog_ref[...]))
    row = lax.broadcasted_iota(jnp.int32, (L, L), 0)
    col = lax.broadcasted_iota(jnp.int32, (L, L), 1)
    causal = row >= col
    acum = _dot_exact(jnp.where(causal, 1.0, 0.0), a)
    acum_t = acum.T

    hrow = lax.broadcasted_iota(jnp.int32, (LANES, inner), 0)
    hcol = lax.broadcasted_iota(jnp.int32, (LANES, inner), 1)
    widen = jnp.where(hcol // SSM_HEAD_DIM == hrow, 1.0, 0.0)
    dt_w = _dot_exact(dt, widen)
    acum_w = _dot_exact(acum, widen)
    last_w = acum_w[L - 1:L, :]
    xdt = xs * dt_w
    xdt_bf = xdt.astype(BF16)
    xdt_dec_bf = (xdt * jnp.exp(last_w - acum_w)).astype(BF16)
    exp_acum_w = jnp.exp(acum_w)
    chunk_decay_w = jnp.exp(last_w)
    lane = lax.broadcasted_iota(jnp.int32, (L, pair_w), 1)
    first_head = lane < SSM_HEAD_DIM

    heads_per_group = SSM_HEADS // SSM_GROUPS
    ys = []
    for g in range(SSM_GROUPS):
        bg = act[:, inner + g * gw:inner + (g + 1) * gw]
        cg = act[:, inner + SSM_GROUPS * gw + g * gw:inner + SSM_GROUPS * gw + (g + 1) * gw]
        cg_bf = cg.astype(BF16)
        cb = _dot_nt(cg_bf, bg.astype(BF16))
        bg_t = bg.T.astype(BF16)
        for pp in range(heads_per_group // 2):
            h0 = g * heads_per_group + 2 * pp
            pi = h0 // 2
            sl = slice(h0 * SSM_HEAD_DIM, h0 * SSM_HEAD_DIM + pair_w)
            yd = []
            for hh in (h0, h0 + 1):
                diff = jnp.broadcast_to(acum[:, hh:hh + 1], (L, L)) - acum_t[hh:hh + 1, :]
                decay = jnp.exp(jnp.where(causal, diff, NEG))
                yd.append(_dot((cb * decay).astype(BF16), xdt_bf[:, sl]))
            y_diag = jnp.where(first_head, yd[0], yd[1])
            prev = state_s[pi]
            y_off = _dot(cg_bf, prev.astype(BF16)) * exp_acum_w[:, sl]
            state_s[pi] = prev * chunk_decay_w[:, sl] + _dot(bg_t, xdt_dec_bf[:, sl])
            ys.append(y_diag + y_off + dskip_ref[:, sl] * xs[:, sl])
    y = jnp.concatenate(ys, axis=1)

    zf = z_ref[0].astype(F32)
    yg = y * _silu(zf)
    gsz = inner // SSM_GROUPS
    parts = []
    for g in range(SSM_GROUPS):
        seg = yg[:, g * gsz:(g + 1) * gsz]
        parts.append(seg * lax.rsqrt(jnp.mean(seg * seg, axis=-1, keepdims=True) + EPS))
    out_ref[0] = (jnp.concatenate(parts, axis=1) * nw_ref[...]).astype(BF16)


def _ssd(xbc_act, z, dt, dt_bias, a_log, d_skip_w, norm_w):
    b, s, cd = xbc_act.shape
    inner = z.shape[2]
    nc = s // SSM_CHUNK
    return pl.pallas_call(
        _ssd_kernel,
        grid=(b, nc),
        in_specs=[
            pl.BlockSpec((1, SSM_CHUNK, cd), lambda bi, ci: (bi, ci, 0)),
            pl.BlockSpec((1, SSM_CHUNK, inner), lambda bi, ci: (bi, ci, 0)),
            pl.BlockSpec((1, SSM_CHUNK, LANES), lambda bi, ci: (bi, ci, 0)),
            _const_spec(dt_bias.shape), _const_spec(a_log.shape), _const_spec(d_skip_w.shape),
            _const_spec(norm_w.shape),
        ],
        out_specs=pl.BlockSpec((1, SSM_CHUNK, inner), lambda bi, ci: (bi, ci, 0)),
        out_shape=jax.ShapeDtypeStruct((b, s, inner), BF16),
        scratch_shapes=[pltpu.VMEM((SSM_HEADS // 2, SSM_STATE, 2 * SSM_HEAD_DIM), F32)],
        compiler_params=_params(("parallel", "arbitrary")),
        name="ssd_heads",
    )(xbc_act, z, dt, dt_bias, a_log, d_skip_w, norm_w)


def _outproj_even_kernel(x_ref, a_ref, s_ref, wa_ref, ws_ref, o_ref):
    o_ref[...] = x_ref[...] + _dot(a_ref[...], wa_ref[...]) + _dot(s_ref[...], ws_ref[...])


def _outproj_even(x2, attn, ssm, wa, ws, tm=512):
    t, d = x2.shape
    return pl.pallas_call(
        _outproj_even_kernel,
        grid=(t // tm,),
        in_specs=[pl.BlockSpec((tm, d), lambda i: (i, 0)),
                  pl.BlockSpec((tm, attn.shape[1]), lambda i: (i, 0)),
                  pl.BlockSpec((tm, ssm.shape[1]), lambda i: (i, 0)),
                  _const_spec(wa.shape), _const_spec(ws.shape)],
        out_specs=pl.BlockSpec((tm, d), lambda i: (i, 0)),
        out_shape=jax.ShapeDtypeStruct((t, d), F32),
        compiler_params=_params(("parallel",)),
        name="outproj_even",
    )(x2, attn, ssm, wa, ws)


def _outproj_odd_kernel(x_ref, o_in_ref, w_ref, o_ref):
    o_ref[...] = x_ref[...] + _dot(o_in_ref[...], w_ref[...])


def _outproj_odd(x2, o_in, w, tm=512):
    t, d = x2.shape
    return pl.pallas_call(
        _outproj_odd_kernel,
        grid=(t // tm,),
        in_specs=[pl.BlockSpec((tm, d), lambda i: (i, 0)),
                  pl.BlockSpec((tm, o_in.shape[1]), lambda i: (i, 0)),
                  _const_spec(w.shape)],
        out_specs=pl.BlockSpec((tm, d), lambda i: (i, 0)),
        out_shape=jax.ShapeDtypeStruct((t, d), F32),
        compiler_params=_params(("parallel",)),
        name="outproj_odd",
    )(x2, o_in, w)


def _ffn_ple_kernel(x_ref, halo_ref, p_ref, gf_ref, wg_ref, wu_ref, cw_ref, cb_ref, wd_ref,
                    gp_ref, wpg_ref, wpp_ref, o_ref, *, tm, seq):
    i = pl.program_id(0)
    x = x_ref[...]
    keep_halo = jnp.where((i * tm) % seq == 0, 0.0, 1.0)
    halo = halo_ref[...] * keep_halo
    hn = _rms_rows(jnp.concatenate([halo, x], axis=0), gf_ref[...]).astype(BF16)
    hn_cur = hn[FFN_HALO:]

    d_ff = wg_ref.shape[1]
    acc = x
    for j in range(d_ff // FFN_COLS):
        cs = slice(j * FFN_COLS, (j + 1) * FFN_COLS)
        g = _dot(hn, wg_ref[:, cs])
        gc = cb_ref[:, cs] + cw_ref[FFN_CONV - 1:FFN_CONV, cs] * g[FFN_HALO:]
        for shift in range(1, FFN_CONV):
            gc = gc + cw_ref[FFN_CONV - 1 - shift:FFN_CONV - shift, cs] * pltpu.roll(g, shift, axis=0)[FFN_HALO:]
        u = _dot(hn_cur, wu_ref[:, cs])
        acc = acc + _dot((_silu(gc) * u).astype(BF16), wd_ref[cs, :])
    x1 = acc
    hp = _rms_rows(x1, gp_ref[...]).astype(BF16)
    gate = _sigmoid(_dot(hp, wpg_ref[...]))
    o_ref[...] = x1 + gate * _dot(p_ref[...].astype(BF16), wpp_ref[...])


def _ffn_ple(x2, p2, seq, gain_ffn, wg, wu, conv_w, conv_b, wd, gain_ple, wpg, wpp, tm=512):
    t, d = x2.shape
    halo_blocks = tm // FFN_HALO
    consts = (gain_ffn, wg, wu, conv_w, conv_b, wd, gain_ple, wpg, wpp)
    return pl.pallas_call(
        functools.partial(_ffn_ple_kernel, tm=tm, seq=seq),
        grid=(t // tm,),
        in_specs=[pl.BlockSpec((tm, d), lambda i: (i, 0)),
                  pl.BlockSpec((FFN_HALO, d), lambda i: (jnp.maximum(i * halo_blocks - 1, 0), 0)),
                  pl.BlockSpec((tm, p2.shape[1]), lambda i: (i, 0))]
        + [_const_spec(c.shape) for c in consts],
        out_specs=pl.BlockSpec((tm, d), lambda i: (i, 0)),
        out_shape=jax.ShapeDtypeStruct((t, d), F32),
        compiler_params=_params(("parallel",)),
        name="ffn_ple",
    )(x2, x2, p2, *consts)


def _inproj_odd_kernel(x_ref, g_ref, wqkv_ref, wz_ref, wba_ref, cw_ref, qkv_ref, z_ref, ba_ref, hist_s,
                       *, tm, seq):
    first = (pl.program_id(0) * tm) % seq == 0
    hn = _rms_rows(x_ref[...], g_ref[...]).astype(BF16)
    _project_conv_silu(hn, wqkv_ref, cw_ref, None, qkv_ref, hist_s, first, GDN_CONV)
    z_ref[...] = _dot(hn, wz_ref[...]).astype(BF16)
    ba_ref[...] = _dot(hn, wba_ref[...])


def _inproj_odd(x2, seq, gain, wqkv, wz, wba, conv_w, tm=256):
    t, d = x2.shape
    outs = [(wqkv.shape[1], BF16), (wz.shape[1], BF16), (wba.shape[1], F32)]
    consts = (gain, wqkv, wz, wba, conv_w)
    return pl.pallas_call(
        functools.partial(_inproj_odd_kernel, tm=tm, seq=seq),
        grid=(t // tm,),
        in_specs=[pl.BlockSpec((tm, d), lambda i: (i, 0))] + [_const_spec(c.shape) for c in consts],
        out_specs=[pl.BlockSpec((tm, n), lambda i: (i, 0)) for n, _ in outs],
        out_shape=[jax.ShapeDtypeStruct((t, n), dt) for n, dt in outs],
        scratch_shapes=[pltpu.VMEM((CONV_HALO, wqkv.shape[1]), F32)],
        compiler_params=_params(("arbitrary",)),
        name="inproj_odd",
    )(x2, *consts)


def _bdot(a, b):
    return _dot(a.astype(BF16), b.astype(BF16))


def _unit_lower_inverses(mats):
    L = mats[0].shape[0]
    row = lax.broadcasted_iota(jnp.int32, (L, L), 0)
    col = lax.broadcasted_iota(jnp.int32, (L, L), 1)
    eye = jnp.where(row == col, 1.0, 0.0)
    base = 8
    same = (row // base) == (col // base)
    ns = [-jnp.where(same, a, 0.0) for a in mats]
    ts = [eye + n for n in ns]
    pws = [_bdot(n, n) for n in ns]
    ts = [t + _bdot(t, pw) for t, pw in zip(ts, pws)]
    pws = [_bdot(pw, pw) for pw in pws]
    ts = [t + _bdot(t, pw) for t, pw in zip(ts, pws)]
    size = base
    while size < L:
        wider = (row // (2 * size)) == (col // (2 * size))
        offs = [jnp.where(wider & jnp.logical_not(same), a, 0.0) for a in mats]
        ys = [_bdot(off, t) for off, t in zip(offs, ts)]
        ts = [t - _bdot(t, y) for t, y in zip(ts, ys)]
        same = wider
        size *= 2
    return ts


def _gdn_kernel(qkv_ref, z_ref, ba_ref, dtb_ref, alog_ref, nw_ref, out_ref, state_s):
    c = pl.program_id(1)
    L = GDN_CHUNK
    dk = GDN_DIM
    qk_w = GDN_K_HEADS * dk
    rep = GDN_V_HEADS // GDN_K_HEADS

    @pl.when(c == 0)
    def _():
        state_s[...] = jnp.zeros_like(state_s)

    act = qkv_ref[0].astype(F32)

    ba = ba_ref[0]
    beta = _sigmoid(ba)
    g = -jnp.exp(alog_ref[...]) * _softplus(ba + dtb_ref[...])
    row = lax.broadcasted_iota(jnp.int32, (L, L), 0)
    col = lax.broadcasted_iota(jnp.int32, (L, L), 1)
    lower = row >= col
    strict = row > col
    gc = _dot_exact(jnp.where(lower, 1.0, 0.0), g)
    gc_t = gc.T

    qk_act = act[:, :2 * qk_w]
    qk_n = qk_act * lax.rsqrt(_head_sums(qk_act * qk_act, dk) + EPS)

    qs, ks, kts, kk, qk0 = [], [], [], [], []
    for kh in range(GDN_K_HEADS):
        qn = qk_n[:, kh * dk:(kh + 1) * dk] * (dk ** -0.5)
        kn = qk_n[:, qk_w + kh * dk:qk_w + (kh + 1) * dk]
        kt = kn.T
        prod = _bdot(jnp.concatenate([kn, qn], axis=0), kt)
        qs.append(qn)
        ks.append(kn)
        kts.append(kt)
        kk.append(prod[:L])
        qk0.append(prod[L:])

    heads = range(GDN_V_HEADS)
    beta_c = [beta[:, hv:hv + 1] for hv in heads]
    gc_c = [gc[:, GDN_V_HEADS + hv:GDN_V_HEADS + hv + 1] for hv in heads]
    gc_r = [gc_t[GDN_V_HEADS + hv:GDN_V_HEADS + hv + 1, :] for hv in heads]
    g_last = [r[:, L - 1:L] for r in gc_r]
    egc = [jnp.exp(cc) for cc in gc_c]
    decay = [jnp.exp(jnp.where(lower, cc - rr, NEG)) for cc, rr in zip(gc_c, gc_r)]
    a_mats = [jnp.where(strict, kk[hv // rep] * beta_c[hv] * decay[hv], 0.0) for hv in heads]
    t_inv = _unit_lower_inverses(a_mats)

    sols = []
    for hv in heads:
        v_h = act[:, 2 * qk_w + hv * dk:2 * qk_w + (hv + 1) * dk]
        kbeta = ks[hv // rep] * beta_c[hv]
        sols.append(_bdot(t_inv[hv], jnp.concatenate([v_h * beta_c[hv], kbeta * egc[hv]], axis=1)))

    states = [state_s[hv] for hv in heads]
    from_state = [_bdot(jnp.concatenate([sols[hv][:, dk:], qs[hv // rep] * egc[hv]], axis=0), states[hv])
                  for hv in heads]
    v_new = [sols[hv][:, :dk] - from_state[hv][:L] for hv in heads]
    from_v = [_bdot(jnp.concatenate([qk0[hv // rep] * decay[hv],
                                     kts[hv // rep] * jnp.exp(g_last[hv] - gc_r[hv])], axis=0), v_new[hv])
              for hv in heads]
    for hv in heads:
        state_s[hv] = states[hv] * jnp.exp(g_last[hv]) + from_v[hv][L:]
    o = jnp.concatenate([from_state[hv][L:] + from_v[hv][:L] for hv in heads], axis=1)
    o = o * lax.rsqrt(_head_sums(o * o, dk) * (1.0 / dk) + EPS) * nw_ref[...]
    out_ref[0] = (o * _silu(z_ref[0].astype(F32))).astype(BF16)


def _gdn(qkv_act, z, ba, dt_bias_pad, a_log_pad, norm_w):
    b, s, cd = qkv_act.shape
    vw = z.shape[2]
    nc = s // GDN_CHUNK
    return pl.pallas_call(
        _gdn_kernel,
        grid=(b, nc),
        in_specs=[
            pl.BlockSpec((1, GDN_CHUNK, cd), lambda bi, ci: (bi, ci, 0)),
            pl.BlockSpec((1, GDN_CHUNK, vw), lambda bi, ci: (bi, ci, 0)),
            pl.BlockSpec((1, GDN_CHUNK, LANES), lambda bi, ci: (bi, ci, 0)),
            _const_spec(dt_bias_pad.shape), _const_spec(a_log_pad.shape), _const_spec(norm_w.shape),
        ],
        out_specs=pl.BlockSpec((1, GDN_CHUNK, vw), lambda bi, ci: (bi, ci, 0)),
        out_shape=jax.ShapeDtypeStruct((b, s, vw), BF16),
        scratch_shapes=[pltpu.VMEM((GDN_V_HEADS, GDN_DIM, GDN_DIM), F32)],
        compiler_params=_params(("parallel", "arbitrary")),
        name="gated_deltanet",
    )(qkv_act, z, ba, dt_bias_pad, a_log_pad, norm_w)


def _pad_lanes(v, offset=0):
    out = jnp.zeros((1, LANES), F32)
    return out.at[0, offset:offset + v.shape[0]].set(v.astype(F32))


def _row(v):
    return v.astype(F32).reshape(1, -1)


def _even_mixer(x2, batch, seq, norm_w, w_in, q_norm, k_norm, conv_w, conv_b, dt_bias, a_log, d_skip,
                ssm_norm, w_out, slopes):
    mw = MOBA_HEADS * MOBA_HEAD_DIM
    inner = SSM_HEADS * SSM_HEAD_DIM
    conv_dim = inner + 2 * SSM_GROUPS * SSM_STATE
    w = w_in.astype(BF16)
    c0, c1, c2, c3, c4 = mw, 2 * mw, 3 * mw, 3 * mw + inner, 3 * mw + inner + conv_dim
    wdt = jnp.pad(w[:, c4:], ((0, 0), (0, LANES - SSM_HEADS)))
    q, k, v, z, xbc, dt = _inproj_even(x2, seq, _row(norm_w), w[:, :c0], w[:, c0:c1], w[:, c1:c2],
                                       w[:, c2:c3], w[:, c3:c4], wdt, conv_w.astype(F32), _row(conv_b))
    attn = _moba(q.reshape(batch, seq, mw), k.reshape(batch, seq, mw), v.reshape(batch, seq, mw),
                 _row(jnp.tile(q_norm, MOBA_HEADS_PER_STEP)), _row(jnp.tile(k_norm, MOBA_HEADS_PER_STEP)),
                 slopes).reshape(batch * seq, mw)

    ssm = _ssd(xbc.reshape(batch, seq, conv_dim), z.reshape(batch, seq, inner),
               dt.reshape(batch, seq, LANES),
               _pad_lanes(dt_bias), _pad_lanes(a_log), _row(jnp.repeat(d_skip, SSM_HEAD_DIM)),
               _row(ssm_norm))
    wo = w_out.astype(BF16)
    return _outproj_even(x2, attn, ssm.reshape(batch * seq, inner), wo[:mw], wo[mw:])


def _gdn_mixer(x2, batch, seq, norm_w, w_in, conv_w, dt_bias, a_log, gdn_norm, w_out):
    conv_dim = 2 * GDN_K_HEADS * GDN_DIM + GDN_V_HEADS * GDN_DIM
    vw = GDN_V_HEADS * GDN_DIM
    w = w_in.astype(BF16)
    wba = jnp.pad(w[:, conv_dim + vw:], ((0, 0), (0, LANES - 2 * GDN_V_HEADS)))
    qkv, z, ba = _inproj_odd(x2, seq, _row(norm_w), w[:, :conv_dim], w[:, conv_dim:conv_dim + vw], wba,
                             conv_w.astype(F32))
    o = _gdn(qkv.reshape(batch, seq, conv_dim), z.reshape(batch, seq, vw), ba.reshape(batch, seq, LANES),
             _pad_lanes(dt_bias, GDN_V_HEADS), _pad_lanes(a_log, GDN_V_HEADS),
             _row(jnp.tile(gdn_norm, GDN_V_HEADS)))
    return _outproj_odd(x2, o.reshape(batch * seq, vw), w_out.astype(BF16))


def kernel(x, p, norm_mix, norm_ffn, norm_ple, w_in_even, moba_q_norm, moba_k_norm, ssm_conv_w, ssm_conv_b, ssm_dt_bias, ssm_a_log, ssm_d, ssm_norm, w_out_even, w_in_odd, gdn_conv_w, gdn_dt_bias, gdn_a_log, gdn_norm, w_out_odd, ffn_w_gate, ffn_w_up, ffn_conv_w, ffn_conv_b, ffn_w_down, ple_w_proj, ple_w_gate):
    batch, seq, d = x.shape
    depth = p.shape[0]
    slopes = jnp.exp2(-ALIBI_MAX_BIAS * jnp.arange(1, MOBA_HEADS + 1, dtype=F32) / MOBA_HEADS)
    x2 = x.reshape(batch * seq, d)
    for i in range(depth):
        j = i // 2
        if i % 2 == 0:
            x2 = _even_mixer(x2, batch, seq, norm_mix[i], w_in_even[j], moba_q_norm[j], moba_k_norm[j],
                             ssm_conv_w[j], ssm_conv_b[j], ssm_dt_bias[j], ssm_a_log[j], ssm_d[j],
                             ssm_norm[j], w_out_even[j], slopes)
        else:
            x2 = _gdn_mixer(x2, batch, seq, norm_mix[i], w_in_odd[j], gdn_conv_w[j], gdn_dt_bias[j],
                            gdn_a_log[j], gdn_norm[j], w_out_odd[j])
        x2 = _ffn_ple(x2, p[i].reshape(batch * seq, -1), seq, _row(norm_ffn[i]),
                      ffn_w_gate[i].astype(BF16), ffn_w_up[i].astype(BF16), ffn_conv_w[i].astype(F32),
                      _row(ffn_conv_b[i]), ffn_w_down[i].astype(BF16), _row(norm_ple[i]),
                      ple_w_gate[i].astype(BF16), ple_w_proj[i].astype(BF16))
    return x2.reshape(batch, seq, d)
```

```python
import functools

import jax
import jax.numpy as jnp
from jax import lax
from jax.experimental import pallas as pl
from jax.experimental.pallas import tpu as pltpu

F32 = jnp.float32
BF16 = jnp.bfloat16
HIGHEST = lax.Precision.HIGHEST

EPS = 1e-6
NEG = -1e30
LANES = 128
CONV_HALO = 8

MOBA_HEADS = 8
MOBA_HEAD_DIM = 64
MOBA_BLOCK = 256
MOBA_TOPK = 3
MOBA_HEADS_PER_STEP = 4
ALIBI_MAX_BIAS = 8.0

SSM_HEADS = 16
SSM_HEAD_DIM = 64
SSM_STATE = 128
SSM_GROUPS = 2
SSM_CONV = 4
SSM_CHUNK = 128

GDN_K_HEADS = 8
GDN_V_HEADS = 16
GDN_DIM = 128
GDN_CONV = 4
GDN_CHUNK = 64
GDN_SEQS_PER_STEP = 2

FFN_CONV = 3
FFN_HALO = 16
FFN_COLS = 256

VMEM_LIMIT = 56 * 1024 * 1024


def _rms_rows(x, gain):
    return x * lax.rsqrt(jnp.mean(x * x, axis=-1, keepdims=True) + EPS) * gain


def _sigmoid(x):
    return 0.5 * jnp.tanh(0.5 * x) + 0.5


def _silu(x):
    return x * _sigmoid(x)


def _softplus(x):
    return jnp.maximum(x, 0.0) + jnp.log(1.0 + jnp.exp(-jnp.abs(x)))


def _dot(a, b):
    return jnp.dot(a, b, preferred_element_type=F32)


def _dot_nt(a, b):
    return lax.dot_general(a, b, (((1,), (1,)), ((), ())), preferred_element_type=F32)


def _dot_exact(a, b):
    return jnp.dot(a, b, precision=HIGHEST, preferred_element_type=F32)


def _head_sums(x, head_w):
    rows, width = x.shape
    slab = 2 * LANES
    n = width // slab
    same = (lax.broadcasted_iota(jnp.int32, (slab, slab), 0) // head_w
            == lax.broadcasted_iota(jnp.int32, (slab, slab), 1) // head_w)
    ones = jnp.where(same, 1.0, 0.0).astype(BF16)
    stacked = jnp.concatenate([x[:, i * slab:(i + 1) * slab] for i in range(n)], axis=0)
    sums = _dot(stacked.astype(BF16), ones)
    return jnp.concatenate([sums[i * rows:(i + 1) * rows] for i in range(n)], axis=1)


def _const_spec(shape):
    zeros = (0,) * len(shape)
    return pl.BlockSpec(shape, lambda *_: zeros, pipeline_mode=pl.Buffered(1))


def _params(semantics):
    return pltpu.CompilerParams(dimension_semantics=semantics, vmem_limit_bytes=VMEM_LIMIT)


def _causal_conv(ext, w_ref, width, halo):
    y = w_ref[width - 1:width, :] * ext[halo:]
    for shift in range(1, width):
        y = y + w_ref[width - 1 - shift:width - shift, :] * pltpu.roll(ext, shift, axis=0)[halo:]
    return y


def _inproj_even_kernel(x_ref, g_ref, wq_ref, wk_ref, wv_ref, wz_ref, wxbc_ref, wdt_ref,
                        q_ref, k_ref, v_ref, z_ref, xbc_ref, dt_ref):
    hn = _rms_rows(x_ref[...], g_ref[...]).astype(BF16)
    q_ref[...] = _dot(hn, wq_ref[...]).astype(BF16)
    k_ref[...] = _dot(hn, wk_ref[...]).astype(BF16)
    v_ref[...] = _dot(hn, wv_ref[...]).astype(BF16)
    z_ref[...] = _dot(hn, wz_ref[...]).astype(BF16)
    xbc_ref[...] = _dot(hn, wxbc_ref[...]).astype(BF16)
    dt_ref[...] = _dot(hn, wdt_ref[...])


def _inproj_even(x2, gain, wq, wk, wv, wz, wxbc, wdt, tm=512):
    t, d = x2.shape
    outs = [(wq.shape[1], BF16), (wk.shape[1], BF16), (wv.shape[1], BF16),
            (wz.shape[1], BF16), (wxbc.shape[1], BF16), (wdt.shape[1], F32)]
    return pl.pallas_call(
        _inproj_even_kernel,
        grid=(t // tm,),
        in_specs=[pl.BlockSpec((tm, d), lambda i: (i, 0)), _const_spec(gain.shape)]
        + [_const_spec(w.shape) for w in (wq, wk, wv, wz, wxbc, wdt)],
        out_specs=[pl.BlockSpec((tm, n), lambda i: (i, 0)) for n, _ in outs],
        out_shape=[jax.ShapeDtypeStruct((t, n), dt) for n, dt in outs],
        compiler_params=_params(("parallel",)),
        name="inproj_even",
    )(x2, gain, wq, wk, wv, wz, wxbc, wdt)


def _moba_kernel(slopes_ref, q_ref, k_ref, v_ref, qg_ref, kg_ref, o_ref, kn_s, kbar_s, vt_s, selb_s):
    hp = pl.program_id(1)
    qi = pl.program_id(2)
    seq = k_ref.shape[1]
    nb = seq // MOBA_BLOCK
    dh = MOBA_HEAD_DIM
    hps = MOBA_HEADS_PER_STEP
    heads = range(hps)

    lanes = hps * dh
    same_head = (lax.broadcasted_iota(jnp.int32, (lanes, lanes), 0) // dh
                 == lax.broadcasted_iota(jnp.int32, (lanes, lanes), 1) // dh)
    head_ones = jnp.where(same_head, 1.0, 0.0).astype(BF16)

    def head_rms(x, gain):
        sq = x * x
        hi = sq.astype(BF16)
        lo = (sq - hi.astype(F32)).astype(BF16)
        ms = (_dot(hi, head_ones) + _dot(lo, head_ones)) * (1.0 / dh)
        return x * lax.rsqrt(ms + EPS) * gain

    @pl.when(qi == 0)
    def _():
        kn = head_rms(k_ref[0].astype(F32), kg_ref[...])
        kn_s[...] = kn.astype(BF16)
        kbar_s[...] = jnp.mean(kn.reshape(nb, MOBA_BLOCK, lanes), axis=1)
        vt_s[...] = v_ref[0].astype(F32).T.astype(BF16)

    qn = head_rms(q_ref[0].astype(F32), qg_ref[...])
    lane_head = lax.broadcasted_iota(jnp.int32, (MOBA_BLOCK, lanes), 1) // dh
    blk = lax.broadcasted_iota(jnp.int32, (nb, MOBA_BLOCK), 0)
    kbar_head = lax.broadcasted_iota(jnp.int32, (nb, lanes), 1) // dh
    kbar = kbar_s[...]
    kbar_rows = jnp.concatenate([jnp.where(kbar_head == hh, kbar, 0.0) for hh in heads], axis=0)
    gates = lax.dot_general(kbar_rows, qn, (((1,), (1,)), ((), ())),
                            precision=HIGHEST, preferred_element_type=F32)
    slopes, qs = [], []
    for hh in heads:
        slopes.append(slopes_ref[hp * hps + hh])
        q_h = jnp.where(lane_head == hh, qn, 0.0)
        gate = gates[hh * nb:(hh + 1) * nb, :]
        rank = jnp.zeros(gate.shape, F32)
        for m in range(nb):
            gm = gate[m:m + 1, :]
            beats = (gm > gate) | ((gm == gate) & (m < blk))
            rank = rank + jnp.where(beats, jnp.where(m < qi, 1.0, 0.0), 0.0)
        selected = (blk < qi) & (rank < float(min(MOBA_TOPK, nb - 1)))
        selb_s[hh] = jnp.where(selected, 0.0, NEG)
        qs.append((q_h * (dh ** -0.5)).astype(BF16))

    kidx = lax.broadcasted_iota(jnp.int32, (MOBA_BLOCK, MOBA_BLOCK), 0)
    qidx = lax.broadcasted_iota(jnp.int32, (MOBA_BLOCK, MOBA_BLOCK), 1)
    rel = (kidx - qidx).astype(F32)

    q0 = pl.multiple_of(qi * MOBA_BLOCK, MOBA_BLOCK)
    scores = [_dot_nt(kn_s[pl.ds(q0, MOBA_BLOCK), :], qs[hh]) for hh in heads]
    ms, ls, ps = [], [], []
    for hh in heads:
        s = jnp.where(rel <= 0.0, scores[hh] + slopes[hh] * rel, NEG)
        m_run = jnp.max(s, axis=0, keepdims=True)
        p = jnp.exp(s - m_run)
        ms.append(m_run)
        ls.append(jnp.sum(p, axis=0, keepdims=True))
        ps.append(p.astype(BF16))
    accs = [_dot(vt_s[hh * dh:(hh + 1) * dh, pl.ds(q0, MOBA_BLOCK)], ps[hh]) for hh in heads]
    carry = []
    for hh in heads:
        carry += [ms[hh], ls[hh], accs[hh]]

    def past_block(j, carry):
        k0 = pl.multiple_of(j * MOBA_BLOCK, MOBA_BLOCK)
        dist = ((j - qi) * MOBA_BLOCK).astype(F32)
        scores = [_dot_nt(kn_s[pl.ds(k0, MOBA_BLOCK), :], qs[hh]) for hh in heads]
        ms, ls, ps, alphas = [], [], [], []
        for hh in heads:
            m_run, l_run = carry[3 * hh], carry[3 * hh + 1]
            bias = selb_s[hh, pl.ds(j, 1), :] + slopes[hh] * dist
            s = scores[hh] + slopes[hh] * rel + bias
            m_new = jnp.maximum(m_run, jnp.max(s, axis=0, keepdims=True))
            alpha = jnp.exp(m_run - m_new)
            p = jnp.exp(s - m_new)
            ms.append(m_new)
            ls.append(alpha * l_run + jnp.sum(p, axis=0, keepdims=True))
            ps.append(p.astype(BF16))
            alphas.append(alpha)
        pv = [_dot(vt_s[hh * dh:(hh + 1) * dh, pl.ds(k0, MOBA_BLOCK)], ps[hh]) for hh in heads]
        out = []
        for hh in heads:
            out += [ms[hh], ls[hh], alphas[hh] * carry[3 * hh + 2] + pv[hh]]
        return tuple(out)

    carry = lax.fori_loop(0, qi, past_block, tuple(carry))
    ot = jnp.concatenate([carry[3 * hh + 2] / carry[3 * hh + 1] for hh in heads], axis=0)
    o_ref[0] = ot.T.astype(BF16)


def _moba(q, k, v, q_gain, k_gain, slopes):
    b, s, width = q.shape
    dh = MOBA_HEAD_DIM
    hps = MOBA_HEADS_PER_STEP
    lanes = hps * dh
    nq = s // MOBA_BLOCK
    nb = s // MOBA_BLOCK
    return pl.pallas_call(
        _moba_kernel,
        grid=(b, width // lanes, nq),
        in_specs=[
            pl.BlockSpec(memory_space=pltpu.SMEM),
            pl.BlockSpec((1, MOBA_BLOCK, lanes), lambda bi, hi, qi: (bi, qi, hi)),
            pl.BlockSpec((1, s, lanes), lambda bi, hi, qi: (bi, 0, hi)),
            pl.BlockSpec((1, s, lanes), lambda bi, hi, qi: (bi, 0, hi)),
            pl.BlockSpec((1, lanes), lambda bi, hi, qi: (0, 0)),
            pl.BlockSpec((1, lanes), lambda bi, hi, qi: (0, 0)),
        ],
        out_specs=pl.BlockSpec((1, MOBA_BLOCK, lanes), lambda bi, hi, qi: (bi, qi, hi)),
        out_shape=jax.ShapeDtypeStruct((b, s, width), BF16),
        scratch_shapes=[
            pltpu.VMEM((s, lanes), BF16),
            pltpu.VMEM((nb, lanes), F32),
            pltpu.VMEM((lanes, s), BF16),
            pltpu.VMEM((hps, nb, MOBA_BLOCK), F32),
        ],
        compiler_params=_params(("parallel", "parallel", "arbitrary")),
        name="moba_attention",
    )(slopes, q, k, v, q_gain, k_gain)


def _ssd_kernel(xbc_ref, z_ref, dt_ref, cw_ref, cb_ref, dtb_ref, alog_ref, dskip_ref, nw_ref,
                out_ref, hist_s, state_s):
    c = pl.program_id(1)
    L = SSM_CHUNK
    inner = SSM_HEADS * SSM_HEAD_DIM
    gw = SSM_STATE
    pair_w = 2 * SSM_HEAD_DIM

    @pl.when(c == 0)
    def _():
        hist_s[...] = jnp.zeros_like(hist_s)
        state_s[...] = jnp.zeros_like(state_s)

    cur = xbc_ref[0].astype(F32)
    ext = jnp.concatenate([hist_s[...], cur], axis=0)
    hist_s[...] = cur[L - CONV_HALO:, :]
    act = _silu(_causal_conv(ext, cw_ref, SSM_CONV, CONV_HALO) + cb_ref[...])
    xs = act[:, :inner]

    dt = _softplus(dt_ref[0] + dtb_ref[...])
    a = dt * (-jnp.exp(alog_ref[...]))
    row = lax.broadcasted_iota(jnp.int32, (L, L), 0)
    col = lax.broadcasted_iota(jnp.int32, (L, L), 1)
    causal = row >= col
    acum = _dot_exact(jnp.where(causal, 1.0, 0.0), a)
    acum_t = acum.T

    hrow = lax.broadcasted_iota(jnp.int32, (LANES, inner), 0)
    hcol = lax.broadcasted_iota(jnp.int32, (LANES, inner), 1)
    widen = jnp.where(hcol // SSM_HEAD_DIM == hrow, 1.0, 0.0)
    dt_w = _dot_exact(dt, widen)
    acum_w = _dot_exact(acum, widen)
    last_w = acum_w[L - 1:L, :]
    xdt = xs * dt_w
    xdt_bf = xdt.astype(BF16)
    xdt_dec_bf = (xdt * jnp.exp(last_w - acum_w)).astype(BF16)
    exp_acum_w = jnp.exp(acum_w)
    chunk_decay_w = jnp.exp(last_w)
    lane = lax.broadcasted_iota(jnp.int32, (L, pair_w), 1)
    first_head = lane < SSM_HEAD_DIM

    heads_per_group = SSM_HEADS // SSM_GROUPS
    ys = []
    for g in range(SSM_GROUPS):
        bg = act[:, inner + g * gw:inner + (g + 1) * gw]
        cg = act[:, inner + SSM_GROUPS * gw + g * gw:inner + SSM_GROUPS * gw + (g + 1) * gw]
        cg_bf = cg.astype(BF16)
        cb = _dot_nt(cg_bf, bg.astype(BF16))
        bg_t = bg.T.astype(BF16)
        for pp in range(heads_per_group // 2):
            h0 = g * heads_per_group + 2 * pp
            pi = h0 // 2
            sl = slice(h0 * SSM_HEAD_DIM, h0 * SSM_HEAD_DIM + pair_w)
            yd = []
            for hh in (h0, h0 + 1):
                diff = jnp.broadcast_to(acum[:, hh:hh + 1], (L, L)) - acum_t[hh:hh + 1, :]
                decay = jnp.exp(jnp.where(causal, diff, NEG))
                yd.append(_dot((cb * decay).astype(BF16), xdt_bf[:, sl]))
            y_diag = jnp.where(first_head, yd[0], yd[1])
            prev = state_s[pi]
            y_off = _dot(cg_bf, prev.astype(BF16)) * exp_acum_w[:, sl]
            state_s[pi] = prev * chunk_decay_w[:, sl] + _dot(bg_t, xdt_dec_bf[:, sl])
            ys.append(y_diag + y_off + dskip_ref[:, sl] * xs[:, sl])
    y = jnp.concatenate(ys, axis=1)

    zf = z_ref[0].astype(F32)
    yg = y * _silu(zf)
    gsz = inner // SSM_GROUPS
    parts = []
    for g in range(SSM_GROUPS):
        seg = yg[:, g * gsz:(g + 1) * gsz]
        parts.append(seg * lax.rsqrt(jnp.mean(seg * seg, axis=-1, keepdims=True) + EPS))
    out_ref[0] = (jnp.concatenate(parts, axis=1) * nw_ref[...]).astype(BF16)


def _ssd(xbc, z, dt, conv_w, conv_b, dt_bias, a_log, d_skip_w, norm_w):
    b, s, cd = xbc.shape
    inner = z.shape[2]
    nc = s // SSM_CHUNK
    return pl.pallas_call(
        _ssd_kernel,
        grid=(b, nc),
        in_specs=[
            pl.BlockSpec((1, SSM_CHUNK, cd), lambda bi, ci: (bi, ci, 0)),
            pl.BlockSpec((1, SSM_CHUNK, inner), lambda bi, ci: (bi, ci, 0)),
            pl.BlockSpec((1, SSM_CHUNK, LANES), lambda bi, ci: (bi, ci, 0)),
            _const_spec(conv_w.shape), _const_spec(conv_b.shape), _const_spec(dt_bias.shape),
            _const_spec(a_log.shape), _const_spec(d_skip_w.shape), _const_spec(norm_w.shape),
        ],
        out_specs=pl.BlockSpec((1, SSM_CHUNK, inner), lambda bi, ci: (bi, ci, 0)),
        out_shape=jax.ShapeDtypeStruct((b, s, inner), BF16),
        scratch_shapes=[
            pltpu.VMEM((CONV_HALO, cd), F32),
            pltpu.VMEM((SSM_HEADS // 2, SSM_STATE, 2 * SSM_HEAD_DIM), F32),
        ],
        compiler_params=_params(("parallel", "arbitrary")),
        name="ssd_heads",
    )(xbc, z, dt, conv_w, conv_b, dt_bias, a_log, d_skip_w, norm_w)


def _outproj_even_kernel(x_ref, a_ref, s_ref, wa_ref, ws_ref, o_ref):
    o_ref[...] = x_ref[...] + _dot(a_ref[...], wa_ref[...]) + _dot(s_ref[...], ws_ref[...])


def _outproj_even(x2, attn, ssm, wa, ws, tm=512):
    t, d = x2.shape
    return pl.pallas_call(
        _outproj_even_kernel,
        grid=(t // tm,),
        in_specs=[pl.BlockSpec((tm, d), lambda i: (i, 0)),
                  pl.BlockSpec((tm, attn.shape[1]), lambda i: (i, 0)),
                  pl.BlockSpec((tm, ssm.shape[1]), lambda i: (i, 0)),
                  _const_spec(wa.shape), _const_spec(ws.shape)],
        out_specs=pl.BlockSpec((tm, d), lambda i: (i, 0)),
        out_shape=jax.ShapeDtypeStruct((t, d), F32),
        compiler_params=_params(("parallel",)),
        name="outproj_even",
    )(x2, attn, ssm, wa, ws)


def _outproj_odd_kernel(x_ref, o_in_ref, w_ref, o_ref):
    o_ref[...] = x_ref[...] + _dot(o_in_ref[...], w_ref[...])


def _outproj_odd(x2, o_in, w, tm=512):
    t, d = x2.shape
    return pl.pallas_call(
        _outproj_odd_kernel,
        grid=(t // tm,),
        in_specs=[pl.BlockSpec((tm, d), lambda i: (i, 0)),
                  pl.BlockSpec((tm, o_in.shape[1]), lambda i: (i, 0)),
                  _const_spec(w.shape)],
        out_specs=pl.BlockSpec((tm, d), lambda i: (i, 0)),
        out_shape=jax.ShapeDtypeStruct((t, d), F32),
        compiler_params=_params(("parallel",)),
        name="outproj_odd",
    )(x2, o_in, w)


def _ffn_ple_kernel(x_ref, halo_ref, p_ref, gf_ref, wg_ref, wu_ref, cw_ref, cb_ref, wd_ref,
                    gp_ref, wpg_ref, wpp_ref, o_ref, *, tm, seq):
    i = pl.program_id(0)
    x = x_ref[...]
    keep_halo = jnp.where((i * tm) % seq == 0, 0.0, 1.0)
    halo = halo_ref[...] * keep_halo
    hn = _rms_rows(jnp.concatenate([halo, x], axis=0), gf_ref[...]).astype(BF16)
    hn_cur = hn[FFN_HALO:]

    d_ff = wg_ref.shape[1]
    acc = x
    for j in range(d_ff // FFN_COLS):
        cs = slice(j * FFN_COLS, (j + 1) * FFN_COLS)
        g = _dot(hn, wg_ref[:, cs])
        gc = cb_ref[:, cs] + cw_ref[FFN_CONV - 1:FFN_CONV, cs] * g[FFN_HALO:]
        for shift in range(1, FFN_CONV):
            gc = gc + cw_ref[FFN_CONV - 1 - shift:FFN_CONV - shift, cs] * pltpu.roll(g, shift, axis=0)[FFN_HALO:]
        u = _dot(hn_cur, wu_ref[:, cs])
        acc = acc + _dot((_silu(gc) * u).astype(BF16), wd_ref[cs, :])
    x1 = acc
    hp = _rms_rows(x1, gp_ref[...]).astype(BF16)
    gate = _sigmoid(_dot(hp, wpg_ref[...]))
    o_ref[...] = x1 + gate * _dot(p_ref[...].astype(BF16), wpp_ref[...])


def _ffn_ple(x2, p2, seq, gain_ffn, wg, wu, conv_w, conv_b, wd, gain_ple, wpg, wpp, tm=512):
    t, d = x2.shape
    halo_blocks = tm // FFN_HALO
    consts = (gain_ffn, wg, wu, conv_w, conv_b, wd, gain_ple, wpg, wpp)
    return pl.pallas_call(
        functools.partial(_ffn_ple_kernel, tm=tm, seq=seq),
        grid=(t // tm,),
        in_specs=[pl.BlockSpec((tm, d), lambda i: (i, 0)),
                  pl.BlockSpec((FFN_HALO, d), lambda i: (jnp.maximum(i * halo_blocks - 1, 0), 0)),
                  pl.BlockSpec((tm, p2.shape[1]), lambda i: (i, 0))]
        + [_const_spec(c.shape) for c in consts],
        out_specs=pl.BlockSpec((tm, d), lambda i: (i, 0)),
        out_shape=jax.ShapeDtypeStruct((t, d), F32),
        compiler_params=_params(("parallel",)),
        name="ffn_ple",
    )(x2, x2, p2, *consts)


def _inproj_odd_kernel(x_ref, g_ref, wqkv_ref, wz_ref, wba_ref, qkv_ref, z_ref, ba_ref):
    hn = _rms_rows(x_ref[...], g_ref[...]).astype(BF16)
    qkv_ref[...] = _dot(hn, wqkv_ref[...]).astype(BF16)
    z_ref[...] = _dot(hn, wz_ref[...]).astype(BF16)
    ba_ref[...] = _dot(hn, wba_ref[...])


def _inproj_odd(x2, gain, wqkv, wz, wba, tm=256):
    t, d = x2.shape
    outs = [(wqkv.shape[1], BF16), (wz.shape[1], BF16), (wba.shape[1], F32)]
    return pl.pallas_call(
        _inproj_odd_kernel,
        grid=(t // tm,),
        in_specs=[pl.BlockSpec((tm, d), lambda i: (i, 0)), _const_spec(gain.shape)]
        + [_const_spec(w.shape) for w in (wqkv, wz, wba)],
        out_specs=[pl.BlockSpec((tm, n), lambda i: (i, 0)) for n, _ in outs],
        out_shape=[jax.ShapeDtypeStruct((t, n), dt) for n, dt in outs],
        compiler_params=_params(("parallel",)),
        name="inproj_odd",
    )(x2, gain, wqkv, wz, wba)


def _bdot(a, b):
    return _dot(a.astype(BF16), b.astype(BF16))


def _unit_lower_inverses(mats):
    L = mats[0].shape[0]
    row = lax.broadcasted_iota(jnp.int32, (L, L), 0)
    col = lax.broadcasted_iota(jnp.int32, (L, L), 1)
    eye = jnp.where(row == col, 1.0, 0.0)
    base = 8
    same = (row // base) == (col // base)
    ns = [-jnp.where(same, a, 0.0) for a in mats]
    ts = [eye + n for n in ns]
    pws = [_bdot(n, n) for n in ns]
    ts = [t + _bdot(t, pw) for t, pw in zip(ts, pws)]
    pws = [_bdot(pw, pw) for pw in pws]
    ts = [t + _bdot(t, pw) for t, pw in zip(ts, pws)]
    size = base
    while size < L:
        wider = (row // (2 * size)) == (col // (2 * size))
        offs = [jnp.where(wider & jnp.logical_not(same), a, 0.0) for a in mats]
        ys = [_bdot(off, t) for off, t in zip(offs, ts)]
        ts = [t - _bdot(t, y) for t, y in zip(ts, ys)]
        same = wider
        size *= 2
    return ts


def _gdn_kernel(qkv_ref, z_ref, ba_ref, cw_ref, dtb_ref, alog_ref, nw_ref, out_ref, hist_s, state_s):
    c = pl.program_id(1)
    L = GDN_CHUNK
    dk = GDN_DIM
    qk_w = GDN_K_HEADS * dk
    rep = GDN_V_HEADS // GDN_K_HEADS
    nseq = qkv_ref.shape[0]

    @pl.when(c == 0)
    def _():
        hist_s[...] = jnp.zeros_like(hist_s)
        state_s[...] = jnp.zeros_like(state_s)

    row = lax.broadcasted_iota(jnp.int32, (L, L), 0)
    col = lax.broadcasted_iota(jnp.int32, (L, L), 1)
    lower = row >= col
    strict = row > col

    items = [(sq, hv) for sq in range(nseq) for hv in range(GDN_V_HEADS)]
    acts, qs, ks, kts, kk, qk0 = {}, {}, {}, {}, {}, {}
    beta_c, gc_c, gc_r = {}, {}, {}
    for sq in range(nseq):
        cur = qkv_ref[sq].astype(F32)
        ext = jnp.concatenate([hist_s[sq], cur], axis=0)
        hist_s[sq] = cur[L - CONV_HALO:, :]
        act = _silu(_causal_conv(ext, cw_ref, GDN_CONV, CONV_HALO))
        acts[sq] = act

        ba = ba_ref[sq]
        beta = _sigmoid(ba)
        g = -jnp.exp(alog_ref[...]) * _softplus(ba + dtb_ref[...])
        gc = _dot_exact(jnp.where(lower, 1.0, 0.0), g)
        gc_t = gc.T
        for hv in range(GDN_V_HEADS):
            beta_c[sq, hv] = beta[:, hv:hv + 1]
            gc_c[sq, hv] = gc[:, GDN_V_HEADS + hv:GDN_V_HEADS + hv + 1]
            gc_r[sq, hv] = gc_t[GDN_V_HEADS + hv:GDN_V_HEADS + hv + 1, :]

        qk_act = act[:, :2 * qk_w]
        qk_n = qk_act * lax.rsqrt(_head_sums(qk_act * qk_act, dk) + EPS)
        for kh in range(GDN_K_HEADS):
            qn = qk_n[:, kh * dk:(kh + 1) * dk] * (dk ** -0.5)
            kn = qk_n[:, qk_w + kh * dk:qk_w + (kh + 1) * dk]
            kt = kn.T
            prod = _bdot(jnp.concatenate([kn, qn], axis=0), kt)
            qs[sq, kh], ks[sq, kh], kts[sq, kh] = qn, kn, kt
            kk[sq, kh], qk0[sq, kh] = prod[:L], prod[L:]

    def khead(it):
        return it[0], it[1] // rep

    g_last = {it: gc_r[it][:, L - 1:L] for it in items}
    egc = {it: jnp.exp(gc_c[it]) for it in items}
    decay = {it: jnp.exp(jnp.where(lower, gc_c[it] - gc_r[it], NEG)) for it in items}
    a_mats = [jnp.where(strict, kk[khead(it)] * beta_c[it] * decay[it], 0.0) for it in items]
    t_inv = dict(zip(items, _unit_lower_inverses(a_mats)))

    sols = {}
    for it in items:
        sq, hv = it
        v_h = acts[sq][:, 2 * qk_w + hv * dk:2 * qk_w + (hv + 1) * dk]
        kbeta = ks[khead(it)] * beta_c[it]
        sols[it] = _bdot(t_inv[it], jnp.concatenate([v_h * beta_c[it], kbeta * egc[it]], axis=1))

    states = {it: state_s[it[0], it[1]] for it in items}
    from_state = {it: _bdot(jnp.concatenate([sols[it][:, dk:], qs[khead(it)] * egc[it]], axis=0), states[it])
                  for it in items}
    v_new = {it: sols[it][:, :dk] - from_state[it][:L] for it in items}
    from_v = {it: _bdot(jnp.concatenate([qk0[khead(it)] * decay[it],
                                         kts[khead(it)] * jnp.exp(g_last[it] - gc_r[it])], axis=0), v_new[it])
              for it in items}
    for it in items:
        state_s[it[0], it[1]] = states[it] * jnp.exp(g_last[it]) + from_v[it][L:]
    for sq in range(nseq):
        o = jnp.concatenate([from_state[sq, hv][L:] + from_v[sq, hv][:L] for hv in range(GDN_V_HEADS)],
                            axis=1)
        o = o * lax.rsqrt(_head_sums(o * o, dk) * (1.0 / dk) + EPS) * nw_ref[...]
        out_ref[sq] = (o * _silu(z_ref[sq].astype(F32))).astype(BF16)


def _gdn(qkv, z, ba, conv_w, dt_bias_pad, a_log_pad, norm_w):
    b, s, cd = qkv.shape
    vw = z.shape[2]
    nc = s // GDN_CHUNK
    nseq = GDN_SEQS_PER_STEP
    return pl.pallas_call(
        _gdn_kernel,
        grid=(b // nseq, nc),
        in_specs=[
            pl.BlockSpec((nseq, GDN_CHUNK, cd), lambda bi, ci: (bi, ci, 0)),
            pl.BlockSpec((nseq, GDN_CHUNK, vw), lambda bi, ci: (bi, ci, 0)),
            pl.BlockSpec((nseq, GDN_CHUNK, LANES), lambda bi, ci: (bi, ci, 0)),
            _const_spec(conv_w.shape), _const_spec(dt_bias_pad.shape),
            _const_spec(a_log_pad.shape), _const_spec(norm_w.shape),
        ],
        out_specs=pl.BlockSpec((nseq, GDN_CHUNK, vw), lambda bi, ci: (bi, ci, 0)),
        out_shape=jax.ShapeDtypeStruct((b, s, vw), BF16),
        scratch_shapes=[
            pltpu.VMEM((nseq, CONV_HALO, cd), F32),
            pltpu.VMEM((nseq, GDN_V_HEADS, GDN_DIM, GDN_DIM), F32),
        ],
        compiler_params=_params(("parallel", "arbitrary")),
        name="gated_deltanet",
    )(qkv, z, ba, conv_w, dt_bias_pad, a_log_pad, norm_w)


def _pad_lanes(v, offset=0):
    out = jnp.zeros((1, LANES), F32)
    return out.at[0, offset:offset + v.shape[0]].set(v.astype(F32))


def _row(v):
    return v.astype(F32).reshape(1, -1)


def _even_mixer(x2, batch, seq, norm_w, w_in, q_norm, k_norm, conv_w, conv_b, dt_bias, a_log, d_skip,
                ssm_norm, w_out, slopes):
    mw = MOBA_HEADS * MOBA_HEAD_DIM
    inner = SSM_HEADS * SSM_HEAD_DIM
    conv_dim = inner + 2 * SSM_GROUPS * SSM_STATE
    w = w_in.astype(BF16)
    c0, c1, c2, c3, c4 = mw, 2 * mw, 3 * mw, 3 * mw + inner, 3 * mw + inner + conv_dim
    wdt = jnp.pad(w[:, c4:], ((0, 0), (0, LANES - SSM_HEADS)))
    q, k, v, z, xbc, dt = _inproj_even(x2, _row(norm_w), w[:, :c0], w[:, c0:c1], w[:, c1:c2],
                                       w[:, c2:c3], w[:, c3:c4], wdt)
    attn = _moba(q.reshape(batch, seq, mw), k.reshape(batch, seq, mw), v.reshape(batch, seq, mw),
                 _row(jnp.tile(q_norm, MOBA_HEADS_PER_STEP)), _row(jnp.tile(k_norm, MOBA_HEADS_PER_STEP)),
                 slopes).reshape(batch * seq, mw)

    ssm = _ssd(xbc.reshape(batch, seq, conv_dim), z.reshape(batch, seq, inner),
               dt.reshape(batch, seq, LANES), conv_w.astype(F32), _row(conv_b),
               _pad_lanes(dt_bias), _pad_lanes(a_log), _row(jnp.repeat(d_skip, SSM_HEAD_DIM)),
               _row(ssm_norm))
    wo = w_out.astype(BF16)
    return _outproj_even(x2, attn, ssm.reshape(batch * seq, inner), wo[:mw], wo[mw:])


def _gdn_mixer(x2, batch, seq, norm_w, w_in, conv_w, dt_bias, a_log, gdn_norm, w_out):
    conv_dim = 2 * GDN_K_HEADS * GDN_DIM + GDN_V_HEADS * GDN_DIM
    vw = GDN_V_HEADS * GDN_DIM
    w = w_in.astype(BF16)
    wba = jnp.pad(w[:, conv_dim + vw:], ((0, 0), (0, LANES - 2 * GDN_V_HEADS)))
    qkv, z, ba = _inproj_odd(x2, _row(norm_w), w[:, :conv_dim], w[:, conv_dim:conv_dim + vw], wba)
    o = _gdn(qkv.reshape(batch, seq, conv_dim), z.reshape(batch, seq, vw), ba.reshape(batch, seq, LANES),
             conv_w.astype(F32), _pad_lanes(dt_bias, GDN_V_HEADS), _pad_lanes(a_log, GDN_V_HEADS),
             _row(jnp.tile(gdn_norm, GDN_V_HEADS)))
    return _outproj_odd(x2, o.reshape(batch * seq, vw), w_out.astype(BF16))


def kernel(x, p, norm_mix, norm_ffn, norm_ple, w_in_even, moba_q_norm, moba_k_norm, ssm_conv_w, ssm_conv_b, ssm_dt_bias, ssm_a_log, ssm_d, ssm_norm, w_out_even, w_in_odd, gdn_conv_w, gdn_dt_bias, gdn_a_log, gdn_norm, w_out_odd, ffn_w_gate, ffn_w_up, ffn_conv_w, ffn_conv_b, ffn_w_down, ple_w_proj, ple_w_gate):
    batch, seq, d = x.shape
    depth = p.shape[0]
    slopes = jnp.exp2(-ALIBI_MAX_BIAS * jnp.arange(1, MOBA_HEADS + 1, dtype=F32) / MOBA_HEADS)
    x2 = x.reshape(batch * seq, d)
    for i in range(depth):
        j = i // 2
        if i % 2 == 0:
            x2 = _even_mixer(x2, batch, seq, norm_mix[i], w_in_even[j], moba_q_norm[j], moba_k_norm[j],
                             ssm_conv_w[j], ssm_conv_b[j], ssm_dt_bias[j], ssm_a_log[j], ssm_d[j],
                             ssm_norm[j], w_out_even[j], slopes)
        else:
            x2 = _gdn_mixer(x2, batch, seq, norm_mix[i], w_in_odd[j], gdn_conv_w[j], gdn_dt_bias[j],
                            gdn_a_log[j], gdn_norm[j], w_out_odd[j])
        x2 = _ffn_ple(x2, p[i].reshape(batch * seq, -1), seq, _row(norm_ffn[i]),
                      ffn_w_gate[i].astype(BF16), ffn_w_up[i].astype(BF16), ffn_conv_w[i].astype(F32),
                      _row(ffn_conv_b[i]), ffn_w_down[i].astype(BF16), _row(norm_ple[i]),
                      ple_w_gate[i].astype(BF16), ple_w_proj[i].astype(BF16))
    return x2.reshape(batch, seq, d)
```

```python
import functools

import jax
import jax.numpy as jnp
from jax import lax
from jax.experimental import pallas as pl
from jax.experimental.pallas import tpu as pltpu

F32 = jnp.float32
BF16 = jnp.bfloat16
HIGHEST = lax.Precision.HIGHEST

EPS = 1e-6
NEG = -1e30
LOG2E = 1.4426950408889634
LANES = 128
CONV_HALO = 8

MOBA_HEADS = 8
MOBA_HEAD_DIM = 64
MOBA_BLOCK = 256
MOBA_TOPK = 3
MOBA_HEADS_PER_STEP = 4
ALIBI_MAX_BIAS = 8.0

SSM_HEADS = 16
SSM_HEAD_DIM = 64
SSM_STATE = 128
SSM_GROUPS = 2
SSM_CONV = 4
SSM_CHUNK = 128

GDN_K_HEADS = 8
GDN_V_HEADS = 16
GDN_DIM = 128
GDN_CONV = 4
GDN_CHUNK = 64
GDN_SEQS_PER_STEP = 2

FFN_CONV = 3
FFN_HALO = 16
FFN_COLS = 256

VMEM_LIMIT = 56 * 1024 * 1024


def _rms_rows(x, gain):
    return x * lax.rsqrt(jnp.mean(x * x, axis=-1, keepdims=True) + EPS) * gain


def _sigmoid(x):
    return 0.5 * jnp.tanh(0.5 * x) + 0.5


def _silu(x):
    return x * _sigmoid(x)


def _softplus(x):
    return jnp.maximum(x, 0.0) + jnp.log(1.0 + jnp.exp(-jnp.abs(x)))


def _dot(a, b):
    return jnp.dot(a, b, preferred_element_type=F32)


def _dot_nt(a, b):
    return lax.dot_general(a, b, (((1,), (1,)), ((), ())), preferred_element_type=F32)


def _dot_exact(a, b):
    return jnp.dot(a, b, precision=HIGHEST, preferred_element_type=F32)


def _head_sums(x, head_w):
    rows, width = x.shape
    slab = 2 * LANES
    n = width // slab
    same = (lax.broadcasted_iota(jnp.int32, (slab, slab), 0) // head_w
            == lax.broadcasted_iota(jnp.int32, (slab, slab), 1) // head_w)
    ones = jnp.where(same, 1.0, 0.0).astype(BF16)
    stacked = jnp.concatenate([x[:, i * slab:(i + 1) * slab] for i in range(n)], axis=0)
    sums = _dot(stacked.astype(BF16), ones)
    return jnp.concatenate([sums[i * rows:(i + 1) * rows] for i in range(n)], axis=1)


def _const_spec(shape):
    zeros = (0,) * len(shape)
    return pl.BlockSpec(shape, lambda *_: zeros, pipeline_mode=pl.Buffered(1))


def _params(semantics):
    return pltpu.CompilerParams(dimension_semantics=semantics, vmem_limit_bytes=VMEM_LIMIT)


def _causal_conv(ext, w_ref, width, halo):
    y = w_ref[width - 1:width, :] * ext[halo:]
    for shift in range(1, width):
        y = y + w_ref[width - 1 - shift:width - shift, :] * pltpu.roll(ext, shift, axis=0)[halo:]
    return y


def _inproj_even_kernel(x_ref, g_ref, wq_ref, wk_ref, wv_ref, wz_ref, wxbc_ref, wdt_ref,
                        q_ref, k_ref, v_ref, z_ref, xbc_ref, dt_ref):
    hn = _rms_rows(x_ref[...], g_ref[...]).astype(BF16)
    q_ref[...] = _dot(hn, wq_ref[...]).astype(BF16)
    k_ref[...] = _dot(hn, wk_ref[...]).astype(BF16)
    v_ref[...] = _dot(hn, wv_ref[...]).astype(BF16)
    z_ref[...] = _dot(hn, wz_ref[...]).astype(BF16)
    xbc_ref[...] = _dot(hn, wxbc_ref[...]).astype(BF16)
    dt_ref[...] = _dot(hn, wdt_ref[...])


def _inproj_even(x2, gain, wq, wk, wv, wz, wxbc, wdt, tm=512):
    t, d = x2.shape
    outs = [(wq.shape[1], BF16), (wk.shape[1], BF16), (wv.shape[1], BF16),
            (wz.shape[1], BF16), (wxbc.shape[1], BF16), (wdt.shape[1], F32)]
    return pl.pallas_call(
        _inproj_even_kernel,
        grid=(t // tm,),
        in_specs=[pl.BlockSpec((tm, d), lambda i: (i, 0)), _const_spec(gain.shape)]
        + [_const_spec(w.shape) for w in (wq, wk, wv, wz, wxbc, wdt)],
        out_specs=[pl.BlockSpec((tm, n), lambda i: (i, 0)) for n, _ in outs],
        out_shape=[jax.ShapeDtypeStruct((t, n), dt) for n, dt in outs],
        compiler_params=_params(("parallel",)),
        name="inproj_even",
    )(x2, gain, wq, wk, wv, wz, wxbc, wdt)


def _moba_kernel(slopes_ref, q_ref, k_ref, v_ref, qg_ref, kg_ref, o_ref, kn_s, kbar_s, vt_s):
    hp = pl.program_id(1)
    qi = pl.program_id(2)
    seq = k_ref.shape[1]
    nb = seq // MOBA_BLOCK
    dh = MOBA_HEAD_DIM
    hps = MOBA_HEADS_PER_STEP
    heads = range(hps)

    lanes = hps * dh
    same_head = (lax.broadcasted_iota(jnp.int32, (lanes, lanes), 0) // dh
                 == lax.broadcasted_iota(jnp.int32, (lanes, lanes), 1) // dh)
    head_ones = jnp.where(same_head, 1.0, 0.0).astype(BF16)

    def head_rms(x, gain):
        sq = x * x
        hi = sq.astype(BF16)
        lo = (sq - hi.astype(F32)).astype(BF16)
        ms = (_dot(hi, head_ones) + _dot(lo, head_ones)) * (1.0 / dh)
        return x * lax.rsqrt(ms + EPS) * gain

    @pl.when(qi == 0)
    def _():
        kn = head_rms(k_ref[0].astype(F32), kg_ref[...])
        kn_s[...] = kn.astype(BF16)
        kbar_s[...] = jnp.mean(kn.reshape(nb, MOBA_BLOCK, lanes), axis=1)
        vt_s[...] = v_ref[0].astype(F32).T.astype(BF16)

    def attend(own):
        qn = head_rms(q_ref[0].astype(F32), qg_ref[...])
        lane_head = lax.broadcasted_iota(jnp.int32, (MOBA_BLOCK, lanes), 1) // dh
        slopes = [slopes_ref[hp * hps + hh] for hh in heads]
        q_heads = [jnp.where(lane_head == hh, qn, 0.0) for hh in heads]
        qs = [(q_h * (dh ** -0.5 * LOG2E)).astype(BF16) for q_h in q_heads]
        scores = {(hh, j): _dot_nt(kn_s[j * MOBA_BLOCK:(j + 1) * MOBA_BLOCK, :], qs[hh])
                  for hh in heads for j in range(own + 1)}

        blk = lax.broadcasted_iota(jnp.int32, (nb, MOBA_BLOCK), 0)
        kbar_head = lax.broadcasted_iota(jnp.int32, (nb, lanes), 1) // dh
        kbar = kbar_s[...]
        kbar_rows = jnp.concatenate([jnp.where(kbar_head == hh, kbar, 0.0) for hh in heads], axis=0)
        gates = lax.dot_general(kbar_rows, qn, (((1,), (1,)), ((), ())),
                                precision=HIGHEST, preferred_element_type=F32)
        selb = []
        for hh in heads:
            gate = gates[hh * nb:(hh + 1) * nb, :]
            rank = jnp.zeros(gate.shape, F32)
            for m in range(own):
                gm = gate[m:m + 1, :]
                beats = (gm > gate) | ((gm == gate) & (m < blk))
                rank = rank + jnp.where(beats, 1.0, 0.0)
            selected = (blk < own) & (rank < float(min(MOBA_TOPK, nb - 1)))
            selb.append(jnp.where(selected, 0.0, NEG))

        kidx = lax.broadcasted_iota(jnp.int32, (MOBA_BLOCK, MOBA_BLOCK), 0)
        qidx = lax.broadcasted_iota(jnp.int32, (MOBA_BLOCK, MOBA_BLOCK), 1)
        rel = (kidx - qidx).astype(F32)
        probs, denoms = [], []
        for hh in heads:
            slope2 = slopes[hh] * LOG2E
            alibi = slope2 * rel
            tiles = [scores[hh, j] + alibi for j in range(own)]
            tiles.append(jnp.where(rel <= 0.0, scores[hh, own] + alibi, NEG))
            col_bias = [selb[hh][j:j + 1, :] + slope2 * float((j - own) * MOBA_BLOCK) for j in range(own)]
            col_bias.append(jnp.zeros((1, MOBA_BLOCK), F32))
            m = jnp.max(tiles[0], axis=0, keepdims=True) + col_bias[0]
            for t, cb in zip(tiles[1:], col_bias[1:]):
                m = jnp.maximum(m, jnp.max(t, axis=0, keepdims=True) + cb)
            ps = [jnp.exp2(t + (cb - m)) for t, cb in zip(tiles, col_bias)]
            denom = jnp.sum(ps[0], axis=0, keepdims=True)
            for p in ps[1:]:
                denom = denom + jnp.sum(p, axis=0, keepdims=True)
            probs.append(jnp.concatenate([p.astype(BF16) for p in ps], axis=0))
            denoms.append(denom)
        keys = (own + 1) * MOBA_BLOCK
        outs = [_dot(vt_s[hh * dh:(hh + 1) * dh, 0:keys], probs[hh]) / denoms[hh] for hh in heads]
        o_ref[0] = jnp.concatenate(outs, axis=0).T.astype(BF16)

    for own in range(nb):
        pl.when(qi == own)(functools.partial(attend, own))


def _moba(q, k, v, q_gain, k_gain, slopes):
    b, s, width = q.shape
    dh = MOBA_HEAD_DIM
    hps = MOBA_HEADS_PER_STEP
    lanes = hps * dh
    nq = s // MOBA_BLOCK
    nb = s // MOBA_BLOCK
    return pl.pallas_call(
        _moba_kernel,
        grid=(b, width // lanes, nq),
        in_specs=[
            pl.BlockSpec(memory_space=pltpu.SMEM),
            pl.BlockSpec((1, MOBA_BLOCK, lanes), lambda bi, hi, qi: (bi, qi, hi)),
            pl.BlockSpec((1, s, lanes), lambda bi, hi, qi: (bi, 0, hi)),
            pl.BlockSpec((1, s, lanes), lambda bi, hi, qi: (bi, 0, hi)),
            pl.BlockSpec((1, lanes), lambda bi, hi, qi: (0, 0)),
            pl.BlockSpec((1, lanes), lambda bi, hi, qi: (0, 0)),
        ],
        out_specs=pl.BlockSpec((1, MOBA_BLOCK, lanes), lambda bi, hi, qi: (bi, qi, hi)),
        out_shape=jax.ShapeDtypeStruct((b, s, width), BF16),
        scratch_shapes=[
            pltpu.VMEM((s, lanes), BF16),
            pltpu.VMEM((nb, lanes), F32),
            pltpu.VMEM((lanes, s), BF16),
        ],
        compiler_params=_params(("parallel", "parallel", "arbitrary")),
        name="moba_attention",
    )(slopes, q, k, v, q_gain, k_gain)


def _ssd_kernel(xbc_ref, z_ref, dt_ref, cw_ref, cb_ref, dtb_ref, alog_ref, dskip_ref, nw_ref,
                out_ref, hist_s, state_s):
    c = pl.program_id(1)
    L = SSM_CHUNK
    inner = SSM_HEADS * SSM_HEAD_DIM
    gw = SSM_STATE
    pair_w = 2 * SSM_HEAD_DIM

    @pl.when(c == 0)
    def _():
        hist_s[...] = jnp.zeros_like(hist_s)
        state_s[...] = jnp.zeros_like(state_s)

    cur = xbc_ref[0].astype(F32)
    ext = jnp.concatenate([hist_s[...], cur], axis=0)
    hist_s[...] = cur[L - CONV_HALO:, :]
    act = _silu(_causal_conv(ext, cw_ref, SSM_CONV, CONV_HALO) + cb_ref[...])
    xs = act[:, :inner]

    dt = _softplus(dt_ref[0] + dtb_ref[...])
    a = dt * (-jnp.exp(alog_ref[...]))
    row = lax.broadcasted_iota(jnp.int32, (L, L), 0)
    col = lax.broadcasted_iota(jnp.int32, (L, L), 1)
    causal = row >= col
    acum = _dot_exact(jnp.where(causal, 1.0, 0.0), a)
    acum_t = acum.T

    hrow = lax.broadcasted_iota(jnp.int32, (LANES, inner), 0)
    hcol = lax.broadcasted_iota(jnp.int32, (LANES, inner), 1)
    widen = jnp.where(hcol // SSM_HEAD_DIM == hrow, 1.0, 0.0)
    dt_w = _dot_exact(dt, widen)
    acum_w = _dot_exact(acum, widen)
    last_w = acum_w[L - 1:L, :]
    xdt = xs * dt_w
    xdt_bf = xdt.astype(BF16)
    xdt_dec_bf = (xdt * jnp.exp(last_w - acum_w)).astype(BF16)
    exp_acum_w = jnp.exp(acum_w)
    chunk_decay_w = jnp.exp(last_w)
    lane = lax.broadcasted_iota(jnp.int32, (L, pair_w), 1)
    first_head = lane < SSM_HEAD_DIM

    heads_per_group = SSM_HEADS // SSM_GROUPS
    ys = []
    for g in range(SSM_GROUPS):
        bg = act[:, inner + g * gw:inner + (g + 1) * gw]
        cg = act[:, inner + SSM_GROUPS * gw + g * gw:inner + SSM_GROUPS * gw + (g + 1) * gw]
        cg_bf = cg.astype(BF16)
        cb = _dot_nt(cg_bf, bg.astype(BF16))
        bg_t = bg.T.astype(BF16)
        for pp in range(heads_per_group // 2):
            h0 = g * heads_per_group + 2 * pp
            pi = h0 // 2
            sl = slice(h0 * SSM_HEAD_DIM, h0 * SSM_HEAD_DIM + pair_w)
            yd = []
            for hh in (h0, h0 + 1):
                diff = jnp.broadcast_to(acum[:, hh:hh + 1], (L, L)) - acum_t[hh:hh + 1, :]
                decay = jnp.exp(jnp.where(causal, diff, NEG))
                yd.append(_dot((cb * decay).astype(BF16), xdt_bf[:, sl]))
            y_diag = jnp.where(first_head, yd[0], yd[1])
            prev = state_s[pi]
            y_off = _dot(cg_bf, prev.astype(BF16)) * exp_acum_w[:, sl]
            state_s[pi] = prev * chunk_decay_w[:, sl] + _dot(bg_t, xdt_dec_bf[:, sl])
            ys.append(y_diag + y_off + dskip_ref[:, sl] * xs[:, sl])
    y = jnp.concatenate(ys, axis=1)

    zf = z_ref[0].astype(F32)
    yg = y * _silu(zf)
    gsz = inner // SSM_GROUPS
    parts = []
    for g in range(SSM_GROUPS):
        seg = yg[:, g * gsz:(g + 1) * gsz]
        parts.append(seg * lax.rsqrt(jnp.mean(seg * seg, axis=-1, keepdims=True) + EPS))
    out_ref[0] = (jnp.concatenate(parts, axis=1) * nw_ref[...]).astype(BF16)


def _ssd(xbc, z, dt, conv_w, conv_b, dt_bias, a_log, d_skip_w, norm_w):
    b, s, cd = xbc.shape
    inner = z.shape[2]
    nc = s // SSM_CHUNK
    return pl.pallas_call(
        _ssd_kernel,
        grid=(b, nc),
        in_specs=[
            pl.BlockSpec((1, SSM_CHUNK, cd), lambda bi, ci: (bi, ci, 0)),
            pl.BlockSpec((1, SSM_CHUNK, inner), lambda bi, ci: (bi, ci, 0)),
            pl.BlockSpec((1, SSM_CHUNK, LANES), lambda bi, ci: (bi, ci, 0)),
            _const_spec(conv_w.shape), _const_spec(conv_b.shape), _const_spec(dt_bias.shape),
            _const_spec(a_log.shape), _const_spec(d_skip_w.shape), _const_spec(norm_w.shape),
        ],
        out_specs=pl.BlockSpec((1, SSM_CHUNK, inner), lambda bi, ci: (bi, ci, 0)),
        out_shape=jax.ShapeDtypeStruct((b, s, inner), BF16),
        scratch_shapes=[
            pltpu.VMEM((CONV_HALO, cd), F32),
            pltpu.VMEM((SSM_HEADS // 2, SSM_STATE, 2 * SSM_HEAD_DIM), F32),
        ],
        compiler_params=_params(("parallel", "arbitrary")),
        name="ssd_heads",
    )(xbc, z, dt, conv_w, conv_b, dt_bias, a_log, d_skip_w, norm_w)


def _outproj_even_kernel(x_ref, a_ref, s_ref, wa_ref, ws_ref, o_ref):
    o_ref[...] = x_ref[...] + _dot(a_ref[...], wa_ref[...]) + _dot(s_ref[...], ws_ref[...])


def _outproj_even(x2, attn, ssm, wa, ws, tm=512):
    t, d = x2.shape
    return pl.pallas_call(
        _outproj_even_kernel,
        grid=(t // tm,),
        in_specs=[pl.BlockSpec((tm, d), lambda i: (i, 0)),
                  pl.BlockSpec((tm, attn.shape[1]), lambda i: (i, 0)),
                  pl.BlockSpec((tm, ssm.shape[1]), lambda i: (i, 0)),
                  _const_spec(wa.shape), _const_spec(ws.shape)],
        out_specs=pl.BlockSpec((tm, d), lambda i: (i, 0)),
        out_shape=jax.ShapeDtypeStruct((t, d), F32),
        compiler_params=_params(("parallel",)),
        name="outproj_even",
    )(x2, attn, ssm, wa, ws)


def _outproj_odd_kernel(x_ref, o_in_ref, w_ref, o_ref):
    o_ref[...] = x_ref[...] + _dot(o_in_ref[...], w_ref[...])


def _outproj_odd(x2, o_in, w, tm=512):
    t, d = x2.shape
    return pl.pallas_call(
        _outproj_odd_kernel,
        grid=(t // tm,),
        in_specs=[pl.BlockSpec((tm, d), lambda i: (i, 0)),
                  pl.BlockSpec((tm, o_in.shape[1]), lambda i: (i, 0)),
                  _const_spec(w.shape)],
        out_specs=pl.BlockSpec((tm, d), lambda i: (i, 0)),
        out_shape=jax.ShapeDtypeStruct((t, d), F32),
        compiler_params=_params(("parallel",)),
        name="outproj_odd",
    )(x2, o_in, w)


def _ffn_ple_kernel(x_ref, halo_ref, p_ref, gf_ref, wg_ref, wu_ref, cw_ref, cb_ref, wd_ref,
                    gp_ref, wpg_ref, wpp_ref, o_ref, *, tm, seq):
    i = pl.program_id(0)
    x = x_ref[...]
    keep_halo = jnp.where((i * tm) % seq == 0, 0.0, 1.0)
    halo = halo_ref[...] * keep_halo
    hn = _rms_rows(jnp.concatenate([halo, x], axis=0), gf_ref[...]).astype(BF16)
    hn_cur = hn[FFN_HALO:]

    d_ff = wg_ref.shape[1]
    acc = x
    for j in range(d_ff // FFN_COLS):
        cs = slice(j * FFN_COLS, (j + 1) * FFN_COLS)
        g = _dot(hn, wg_ref[:, cs])
        gc = cb_ref[:, cs] + cw_ref[FFN_CONV - 1:FFN_CONV, cs] * g[FFN_HALO:]
        for shift in range(1, FFN_CONV):
            gc = gc + cw_ref[FFN_CONV - 1 - shift:FFN_CONV - shift, cs] * pltpu.roll(g, shift, axis=0)[FFN_HALO:]
        u = _dot(hn_cur, wu_ref[:, cs])
        acc = acc + _dot((_silu(gc) * u).astype(BF16), wd_ref[cs, :])
    x1 = acc
    hp = _rms_rows(x1, gp_ref[...]).astype(BF16)
    gate = _sigmoid(_dot(hp, wpg_ref[...]))
    o_ref[...] = x1 + gate * _dot(p_ref[...].astype(BF16), wpp_ref[...])


def _ffn_ple(x2, p2, seq, gain_ffn, wg, wu, conv_w, conv_b, wd, gain_ple, wpg, wpp, tm=512):
    t, d = x2.shape
    halo_blocks = tm // FFN_HALO
    consts = (gain_ffn, wg, wu, conv_w, conv_b, wd, gain_ple, wpg, wpp)
    return pl.pallas_call(
        functools.partial(_ffn_ple_kernel, tm=tm, seq=seq),
        grid=(t // tm,),
        in_specs=[pl.BlockSpec((tm, d), lambda i: (i, 0)),
                  pl.BlockSpec((FFN_HALO, d), lambda i: (jnp.maximum(i * halo_blocks - 1, 0), 0)),
                  pl.BlockSpec((tm, p2.shape[1]), lambda i: (i, 0))]
        + [_const_spec(c.shape) for c in consts],
        out_specs=pl.BlockSpec((tm, d), lambda i: (i, 0)),
        out_shape=jax.ShapeDtypeStruct((t, d), F32),
        compiler_params=_params(("parallel",)),
        name="ffn_ple",
    )(x2, x2, p2, *consts)


def _inproj_odd_kernel(x_ref, g_ref, wqkv_ref, wz_ref, wba_ref, qkv_ref, z_ref, ba_ref):
    hn = _rms_rows(x_ref[...], g_ref[...]).astype(BF16)
    qkv_ref[...] = _dot(hn, wqkv_ref[...]).astype(BF16)
    z_ref[...] = _dot(hn, wz_ref[...]).astype(BF16)
    ba_ref[...] = _dot(hn, wba_ref[...])


def _inproj_odd(x2, gain, wqkv, wz, wba, tm=256):
    t, d = x2.shape
    outs = [(wqkv.shape[1], BF16), (wz.shape[1], BF16), (wba.shape[1], F32)]
    return pl.pallas_call(
        _inproj_odd_kernel,
        grid=(t // tm,),
        in_specs=[pl.BlockSpec((tm, d), lambda i: (i, 0)), _const_spec(gain.shape)]
        + [_const_spec(w.shape) for w in (wqkv, wz, wba)],
        out_specs=[pl.BlockSpec((tm, n), lambda i: (i, 0)) for n, _ in outs],
        out_shape=[jax.ShapeDtypeStruct((t, n), dt) for n, dt in outs],
        compiler_params=_params(("parallel",)),
        name="inproj_odd",
    )(x2, gain, wqkv, wz, wba)


def _bdot(a, b):
    return _dot(a.astype(BF16), b.astype(BF16))


def _unit_lower_inverses(mats):
    L = mats[0].shape[0]
    row = lax.broadcasted_iota(jnp.int32, (L, L), 0)
    col = lax.broadcasted_iota(jnp.int32, (L, L), 1)
    eye = jnp.where(row == col, 1.0, 0.0)
    base = 8
    same = (row // base) == (col // base)
    ns = [-jnp.where(same, a, 0.0) for a in mats]
    ts = [eye + n for n in ns]
    pws = [_bdot(n, n) for n in ns]
    ts = [t + _bdot(t, pw) for t, pw in zip(ts, pws)]
    pws = [_bdot(pw, pw) for pw in pws]
    ts = [t + _bdot(t, pw) for t, pw in zip(ts, pws)]
    size = base
    while size < L:
        wider = (row // (2 * size)) == (col // (2 * size))
        offs = [jnp.where(wider & jnp.logical_not(same), a, 0.0) for a in mats]
        ys = [_bdot(off, t) for off, t in zip(offs, ts)]
        ts = [t - _bdot(t, y) for t, y in zip(ts, ys)]
        same = wider
        size *= 2
    return ts


def _gdn_kernel(qkv_ref, z_ref, ba_ref, cw_ref, dtb_ref, alog_ref, nw_ref, out_ref, hist_s, state_s):
    c = pl.program_id(1)
    L = GDN_CHUNK
    dk = GDN_DIM
    qk_w = GDN_K_HEADS * dk
    rep = GDN_V_HEADS // GDN_K_HEADS
    nseq = qkv_ref.shape[0]

    @pl.when(c == 0)
    def _():
        hist_s[...] = jnp.zeros_like(hist_s)
        state_s[...] = jnp.zeros_like(state_s)

    row = lax.broadcasted_iota(jnp.int32, (L, L), 0)
    col = lax.broadcasted_iota(jnp.int32, (L, L), 1)
    lower = row >= col
    strict = row > col

    items = [(sq, hv) for sq in range(nseq) for hv in range(GDN_V_HEADS)]
    acts, qs, ks, kts, kk, qk0 = {}, {}, {}, {}, {}, {}
    beta_c, gc_c, gc_r = {}, {}, {}
    for sq in range(nseq):
        cur = qkv_ref[sq].astype(F32)
        ext = jnp.concatenate([hist_s[sq], cur], axis=0)
        hist_s[sq] = cur[L - CONV_HALO:, :]
        act = _silu(_causal_conv(ext, cw_ref, GDN_CONV, CONV_HALO))
        acts[sq] = act

        ba = ba_ref[sq]
        beta = _sigmoid(ba)
        g = -jnp.exp(alog_ref[...]) * _softplus(ba + dtb_ref[...])
        gc = _dot_exact(jnp.where(lower, 1.0, 0.0), g)
        gc_t = gc.T
        for hv in range(GDN_V_HEADS):
            beta_c[sq, hv] = beta[:, hv:hv + 1]
            gc_c[sq, hv] = gc[:, GDN_V_HEADS + hv:GDN_V_HEADS + hv + 1]
            gc_r[sq, hv] = gc_t[GDN_V_HEADS + hv:GDN_V_HEADS + hv + 1, :]

        qk_act = act[:, :2 * qk_w]
        qk_n = qk_act * lax.rsqrt(_head_sums(qk_act * qk_act, dk) + EPS)
        for kh in range(GDN_K_HEADS):
            qn = qk_n[:, kh * dk:(kh + 1) * dk] * (dk ** -0.5)
            kn = qk_n[:, qk_w + kh * dk:qk_w + (kh + 1) * dk]
            kt = kn.T
            prod = _bdot(jnp.concatenate([kn, qn], axis=0), kt)
            qs[sq, kh], ks[sq, kh], kts[sq, kh] = qn, kn, kt
            kk[sq, kh], qk0[sq, kh] = prod[:L], prod[L:]

    def khead(it):
        return it[0], it[1] // rep

    g_last = {it: gc_r[it][:, L - 1:L] for it in items}
    egc = {it: jnp.exp(gc_c[it]) for it in items}
    decay = {it: jnp.exp(jnp.where(lower, gc_c[it] - gc_r[it], NEG)) for it in items}
    a_mats = [jnp.where(strict, kk[khead(it)] * beta_c[it] * decay[it], 0.0) for it in items]
    t_inv = dict(zip(items, _unit_lower_inverses(a_mats)))

    sols = {}
    for it in items:
        sq, hv = it
        v_h = acts[sq][:, 2 * qk_w + hv * dk:2 * qk_w + (hv + 1) * dk]
        kbeta = ks[khead(it)] * beta_c[it]
        sols[it] = _bdot(t_inv[it], jnp.concatenate([v_h * beta_c[it], kbeta * egc[it]], axis=1))

    states = {it: state_s[it[0], it[1]] for it in items}
    from_state = {it: _bdot(jnp.concatenate([sols[it][:, dk:], qs[khead(it)] * egc[it]], axis=0), states[it])
                  for it in items}
    v_new = {it: sols[it][:, :dk] - from_state[it][:L] for it in items}
    from_v = {it: _bdot(jnp.concatenate([qk0[khead(it)] * decay[it],
                                         kts[khead(it)] * jnp.exp(g_last[it] - gc_r[it])], axis=0), v_new[it])
              for it in items}
    for it in items:
        state_s[it[0], it[1]] = states[it] * jnp.exp(g_last[it]) + from_v[it][L:]
    for sq in range(nseq):
        o = jnp.concatenate([from_state[sq, hv][L:] + from_v[sq, hv][:L] for hv in range(GDN_V_HEADS)],
                            axis=1)
        o = o * lax.rsqrt(_head_sums(o * o, dk) * (1.0 / dk) + EPS) * nw_ref[...]
        out_ref[sq] = (o * _silu(z_ref[sq].astype(F32))).astype(BF16)


def _gdn(qkv, z, ba, conv_w, dt_bias_pad, a_log_pad, norm_w):
    b, s, cd = qkv.shape
    vw = z.shape[2]
    nc = s // GDN_CHUNK
    nseq = GDN_SEQS_PER_STEP
    return pl.pallas_call(
        _gdn_kernel,
        grid=(b // nseq, nc),
        in_specs=[
            pl.BlockSpec((nseq, GDN_CHUNK, cd), lambda bi, ci: (bi, ci, 0)),
            pl.BlockSpec((nseq, GDN_CHUNK, vw), lambda bi, ci: (bi, ci, 0)),
            pl.BlockSpec((nseq, GDN_CHUNK, LANES), lambda bi, ci: (bi, ci, 0)),
            _const_spec(conv_w.shape), _const_spec(dt_bias_pad.shape),
            _const_spec(a_log_pad.shape), _const_spec(norm_w.shape),
        ],
        out_specs=pl.BlockSpec((nseq, GDN_CHUNK, vw), lambda bi, ci: (bi, ci, 0)),
        out_shape=jax.ShapeDtypeStruct((b, s, vw), BF16),
        scratch_shapes=[
            pltpu.VMEM((nseq, CONV_HALO, cd), F32),
            pltpu.VMEM((nseq, GDN_V_HEADS, GDN_DIM, GDN_DIM), F32),
        ],
        compiler_params=_params(("parallel", "arbitrary")),
        name="gated_deltanet",
    )(qkv, z, ba, conv_w, dt_bias_pad, a_log_pad, norm_w)


def _pad_lanes(v, offset=0):
    out = jnp.zeros((1, LANES), F32)
    return out.at[0, offset:offset + v.shape[0]].set(v.astype(F32))


def _row(v):
    return v.astype(F32).reshape(1, -1)


def _even_mixer(x2, batch, seq, norm_w, w_in, q_norm, k_norm, conv_w, conv_b, dt_bias, a_log, d_skip,
                ssm_norm, w_out, slopes):
    mw = MOBA_HEADS * MOBA_HEAD_DIM
    inner = SSM_HEADS * SSM_HEAD_DIM
    conv_dim = inner + 2 * SSM_GROUPS * SSM_STATE
    w = w_in.astype(BF16)
    c0, c1, c2, c3, c4 = mw, 2 * mw, 3 * mw, 3 * mw + inner, 3 * mw + inner + conv_dim
    wdt = jnp.pad(w[:, c4:], ((0, 0), (0, LANES - SSM_HEADS)))
    q, k, v, z, xbc, dt = _inproj_even(x2, _row(norm_w), w[:, :c0], w[:, c0:c1], w[:, c1:c2],
                                       w[:, c2:c3], w[:, c3:c4], wdt)
    attn = _moba(q.reshape(batch, seq, mw), k.reshape(batch, seq, mw), v.reshape(batch, seq, mw),
                 _row(jnp.tile(q_norm, MOBA_HEADS_PER_STEP)), _row(jnp.tile(k_norm, MOBA_HEADS_PER_STEP)),
                 slopes).reshape(batch * seq, mw)

    ssm = _ssd(xbc.reshape(batch, seq, conv_dim), z.reshape(batch, seq, inner),
               dt.reshape(batch, seq, LANES), conv_w.astype(F32), _row(conv_b),
               _pad_lanes(dt_bias), _pad_lanes(a_log), _row(jnp.repeat(d_skip, SSM_HEAD_DIM)),
               _row(ssm_norm))
    wo = w_out.astype(BF16)
    return _outproj_even(x2, attn, ssm.reshape(batch * seq, inner), wo[:mw], wo[mw:])


def _gdn_mixer(x2, batch, seq, norm_w, w_in, conv_w, dt_bias, a_log, gdn_norm, w_out):
    conv_dim = 2 * GDN_K_HEADS * GDN_DIM + GDN_V_HEADS * GDN_DIM
    vw = GDN_V_HEADS * GDN_DIM
    w = w_in.astype(BF16)
    wba = jnp.pad(w[:, conv_dim + vw:], ((0, 0), (0, LANES - 2 * GDN_V_HEADS)))
    qkv, z, ba = _inproj_odd(x2, _row(norm_w), w[:, :conv_dim], w[:, conv_dim:conv_dim + vw], wba)
    o = _gdn(qkv.reshape(batch, seq, conv_dim), z.reshape(batch, seq, vw), ba.reshape(batch, seq, LANES),
             conv_w.astype(F32), _pad_lanes(dt_bias, GDN_V_HEADS), _pad_lanes(a_log, GDN_V_HEADS),
             _row(jnp.tile(gdn_norm, GDN_V_HEADS)))
    return _outproj_odd(x2, o.reshape(batch * seq, vw), w_out.astype(BF16))


def kernel(x, p, norm_mix, norm_ffn, norm_ple, w_in_even, moba_q_norm, moba_k_norm, ssm_conv_w, ssm_conv_b, ssm_dt_bias, ssm_a_log, ssm_d, ssm_norm, w_out_even, w_in_odd, gdn_conv_w, gdn_dt_bias, gdn_a_log, gdn_norm, w_out_odd, ffn_w_gate, ffn_w_up, ffn_conv_w, ffn_conv_b, ffn_w_down, ple_w_proj, ple_w_gate):
    batch, seq, d = x.shape
    depth = p.shape[0]
    slopes = jnp.exp2(-ALIBI_MAX_BIAS * jnp.arange(1, MOBA_HEADS + 1, dtype=F32) / MOBA_HEADS)
    x2 = x.reshape(batch * seq, d)
    for i in range(depth):
        j = i // 2
        if i % 2 == 0:
            x2 = _even_mixer(x2, batch, seq, norm_mix[i], w_in_even[j], moba_q_norm[j], moba_k_norm[j],
                             ssm_conv_w[j], ssm_conv_b[j], ssm_dt_bias[j], ssm_a_log[j], ssm_d[j],
                             ssm_norm[j], w_out_even[j], slopes)
        else:
            x2 = _gdn_mixer(x2, batch, seq, norm_mix[i], w_in_odd[j], gdn_conv_w[j], gdn_dt_bias[j],
                            gdn_a_log[j], gdn_norm[j], w_out_odd[j])
        x2 = _ffn_ple(x2, p[i].reshape(batch * seq, -1), seq, _row(norm_ffn[i]),
                      ffn_w_gate[i].astype(BF16), ffn_w_up[i].astype(BF16), ffn_conv_w[i].astype(F32),
                      _row(ffn_conv_b[i]), ffn_w_down[i].astype(BF16), _row(norm_ple[i]),
                      ple_w_gate[i].astype(BF16), ple_w_proj[i].astype(BF16))
    return x2.reshape(batch, seq, d)
```

```python
import functools

import jax
import jax.numpy as jnp
from jax import lax
from jax.experimental import pallas as pl
from jax.experimental.pallas import tpu as pltpu

F32 = jnp.float32
BF16 = jnp.bfloat16
HIGHEST = lax.Precision.HIGHEST

EPS = 1e-6
NEG = -1e30
LOG2E = 1.4426950408889634
LANES = 128
CONV_HALO = 8

MOBA_HEADS = 8
MOBA_HEAD_DIM = 64
MOBA_BLOCK = 256
MOBA_TOPK = 3
MOBA_HEADS_PER_STEP = 4
ALIBI_MAX_BIAS = 8.0

SSM_HEADS = 16
SSM_HEAD_DIM = 64
SSM_STATE = 128
SSM_GROUPS = 2
SSM_CONV = 4
SSM_CHUNK = 128

GDN_K_HEADS = 8
GDN_V_HEADS = 16
GDN_DIM = 128
GDN_CONV = 4
GDN_CHUNK = 64
GDN_SEQS_PER_STEP = 2

FFN_CONV = 3
FFN_HALO = 16
FFN_COLS = 256

VMEM_LIMIT = 56 * 1024 * 1024


def _rms_rows(x, gain):
    return x * lax.rsqrt(jnp.mean(x * x, axis=-1, keepdims=True) + EPS) * gain


def _sigmoid(x):
    return 0.5 * jnp.tanh(0.5 * x) + 0.5


def _silu(x):
    return x * _sigmoid(x)


def _softplus(x):
    return jnp.maximum(x, 0.0) + jnp.log(1.0 + jnp.exp(-jnp.abs(x)))


def _dot(a, b):
    return jnp.dot(a, b, preferred_element_type=F32)


def _dot_nt(a, b):
    return lax.dot_general(a, b, (((1,), (1,)), ((), ())), preferred_element_type=F32)


def _bf16_terms(x):
    hi = x.astype(BF16)
    rest = x - hi.astype(F32)
    mid = rest.astype(BF16)
    lo = (rest - mid.astype(F32)).astype(BF16)
    return [hi, mid, lo]


def _select_rows_dot(sel, x):
    return _dot(jnp.concatenate([sel] * 3, axis=1), jnp.concatenate(_bf16_terms(x), axis=0))


def _select_cols_dot(x, sel):
    return _dot(jnp.concatenate(_bf16_terms(x), axis=1), jnp.concatenate([sel] * 3, axis=0))


def _head_sums(x, head_w):
    rows, width = x.shape
    slab = 2 * LANES
    n = width // slab
    same = (lax.broadcasted_iota(jnp.int32, (slab, slab), 0) // head_w
            == lax.broadcasted_iota(jnp.int32, (slab, slab), 1) // head_w)
    ones = jnp.where(same, 1.0, 0.0).astype(BF16)
    stacked = jnp.concatenate([x[:, i * slab:(i + 1) * slab] for i in range(n)], axis=0)
    sums = _dot(stacked.astype(BF16), ones)
    return jnp.concatenate([sums[i * rows:(i + 1) * rows] for i in range(n)], axis=1)


def _const_spec(shape):
    zeros = (0,) * len(shape)
    return pl.BlockSpec(shape, lambda *_: zeros, pipeline_mode=pl.Buffered(1))


def _params(semantics):
    return pltpu.CompilerParams(dimension_semantics=semantics, vmem_limit_bytes=VMEM_LIMIT)


def _causal_conv(ext, w_ref, width, halo):
    y = w_ref[width - 1:width, :] * ext[halo:]
    for shift in range(1, width):
        y = y + w_ref[width - 1 - shift:width - shift, :] * pltpu.roll(ext, shift, axis=0)[halo:]
    return y


def _inproj_even_kernel(x_ref, g_ref, wq_ref, wk_ref, wv_ref, wz_ref, wxbc_ref, wdt_ref,
                        q_ref, k_ref, v_ref, z_ref, xbc_ref, dt_ref):
    hn = _rms_rows(x_ref[...], g_ref[...]).astype(BF16)
    q_ref[...] = _dot(hn, wq_ref[...]).astype(BF16)
    k_ref[...] = _dot(hn, wk_ref[...]).astype(BF16)
    v_ref[...] = _dot(hn, wv_ref[...]).astype(BF16)
    z_ref[...] = _dot(hn, wz_ref[...]).astype(BF16)
    xbc_ref[...] = _dot(hn, wxbc_ref[...]).astype(BF16)
    dt_ref[...] = _dot(hn, wdt_ref[...])


def _inproj_even(x2, gain, wq, wk, wv, wz, wxbc, wdt, tm=512):
    t, d = x2.shape
    outs = [(wq.shape[1], BF16), (wk.shape[1], BF16), (wv.shape[1], BF16),
            (wz.shape[1], BF16), (wxbc.shape[1], BF16), (wdt.shape[1], F32)]
    return pl.pallas_call(
        _inproj_even_kernel,
        grid=(t // tm,),
        in_specs=[pl.BlockSpec((tm, d), lambda i: (i, 0)), _const_spec(gain.shape)]
        + [_const_spec(w.shape) for w in (wq, wk, wv, wz, wxbc, wdt)],
        out_specs=[pl.BlockSpec((tm, n), lambda i: (i, 0)) for n, _ in outs],
        out_shape=[jax.ShapeDtypeStruct((t, n), dt) for n, dt in outs],
        compiler_params=_params(("parallel",)),
        name="inproj_even",
    )(x2, gain, wq, wk, wv, wz, wxbc, wdt)


def _moba_kernel(slopes_ref, q_ref, k_ref, v_ref, qg_ref, kg_ref, o_ref, kn_s, kbar_s, vt_s):
    hp = pl.program_id(1)
    qi = pl.program_id(2)
    seq = k_ref.shape[1]
    nb = seq // MOBA_BLOCK
    dh = MOBA_HEAD_DIM
    hps = MOBA_HEADS_PER_STEP
    heads = range(hps)

    lanes = hps * dh
    same_head = (lax.broadcasted_iota(jnp.int32, (lanes, lanes), 0) // dh
                 == lax.broadcasted_iota(jnp.int32, (lanes, lanes), 1) // dh)
    head_ones = jnp.where(same_head, 1.0, 0.0).astype(BF16)

    def head_rms(x, gain):
        sq = x * x
        hi = sq.astype(BF16)
        lo = (sq - hi.astype(F32)).astype(BF16)
        ms = (_dot(hi, head_ones) + _dot(lo, head_ones)) * (1.0 / dh)
        return x * lax.rsqrt(ms + EPS) * gain

    @pl.when(qi == 0)
    def _():
        kn = head_rms(k_ref[0].astype(F32), kg_ref[...])
        kn_s[...] = kn.astype(BF16)
        kbar_s[...] = jnp.mean(kn.reshape(nb, MOBA_BLOCK, lanes), axis=1)
        vt_s[...] = v_ref[0].astype(F32).T.astype(BF16)

    def attend(own):
        qn = head_rms(q_ref[0].astype(F32), qg_ref[...])
        lane_head = lax.broadcasted_iota(jnp.int32, (MOBA_BLOCK, lanes), 1) // dh
        slopes = [slopes_ref[hp * hps + hh] for hh in heads]
        q_heads = [jnp.where(lane_head == hh, qn, 0.0) for hh in heads]
        qs = [(q_h * (dh ** -0.5 * LOG2E)).astype(BF16) for q_h in q_heads]
        scores = {(hh, j): _dot_nt(kn_s[j * MOBA_BLOCK:(j + 1) * MOBA_BLOCK, :], qs[hh])
                  for hh in heads for j in range(own + 1)}

        blk = lax.broadcasted_iota(jnp.int32, (nb, MOBA_BLOCK), 0)
        kbar_head = lax.broadcasted_iota(jnp.int32, (nb, lanes), 1) // dh
        kbar = kbar_s[...]
        kbar_rows = jnp.concatenate([jnp.where(kbar_head == hh, kbar, 0.0) for hh in heads], axis=0)
        gates = lax.dot_general(kbar_rows, qn, (((1,), (1,)), ((), ())),
                                precision=HIGHEST, preferred_element_type=F32)
        selb = []
        for hh in heads:
            gate = gates[hh * nb:(hh + 1) * nb, :]
            rank = jnp.zeros(gate.shape, F32)
            for m in range(own):
                gm = gate[m:m + 1, :]
                beats = (gm > gate) | ((gm == gate) & (m < blk))
                rank = rank + jnp.where(beats, 1.0, 0.0)
            selected = (blk < own) & (rank < float(min(MOBA_TOPK, nb - 1)))
            selb.append(jnp.where(selected, 0.0, NEG))

        kidx = lax.broadcasted_iota(jnp.int32, (MOBA_BLOCK, MOBA_BLOCK), 0)
        qidx = lax.broadcasted_iota(jnp.int32, (MOBA_BLOCK, MOBA_BLOCK), 1)
        rel = (kidx - qidx).astype(F32)
        probs, denoms = [], []
        for hh in heads:
            slope2 = slopes[hh] * LOG2E
            alibi = slope2 * rel
            tiles = [scores[hh, j] + alibi for j in range(own)]
            tiles.append(jnp.where(rel <= 0.0, scores[hh, own] + alibi, NEG))
            col_bias = [selb[hh][j:j + 1, :] + slope2 * float((j - own) * MOBA_BLOCK) for j in range(own)]
            col_bias.append(jnp.zeros((1, MOBA_BLOCK), F32))
            m = jnp.max(tiles[0], axis=0, keepdims=True) + col_bias[0]
            for t, cb in zip(tiles[1:], col_bias[1:]):
                m = jnp.maximum(m, jnp.max(t, axis=0, keepdims=True) + cb)
            ps = [jnp.exp2(t + (cb - m)) for t, cb in zip(tiles, col_bias)]
            denom = jnp.sum(ps[0], axis=0, keepdims=True)
            for p in ps[1:]:
                denom = denom + jnp.sum(p, axis=0, keepdims=True)
            probs.append(jnp.concatenate([p.astype(BF16) for p in ps], axis=0))
            denoms.append(denom)
        keys = (own + 1) * MOBA_BLOCK
        outs = [_dot(vt_s[hh * dh:(hh + 1) * dh, 0:keys], probs[hh]) / denoms[hh] for hh in heads]
        o_ref[0] = jnp.concatenate(outs, axis=0).T.astype(BF16)

    for own in range(nb):
        pl.when(qi == own)(functools.partial(attend, own))


def _moba(q, k, v, q_gain, k_gain, slopes):
    b, s, width = q.shape
    dh = MOBA_HEAD_DIM
    hps = MOBA_HEADS_PER_STEP
    lanes = hps * dh
    nq = s // MOBA_BLOCK
    nb = s // MOBA_BLOCK
    return pl.pallas_call(
        _moba_kernel,
        grid=(b, width // lanes, nq),
        in_specs=[
            pl.BlockSpec(memory_space=pltpu.SMEM),
            pl.BlockSpec((1, MOBA_BLOCK, lanes), lambda bi, hi, qi: (bi, qi, hi)),
            pl.BlockSpec((1, s, lanes), lambda bi, hi, qi: (bi, 0, hi)),
            pl.BlockSpec((1, s, lanes), lambda bi, hi, qi: (bi, 0, hi)),
            pl.BlockSpec((1, lanes), lambda bi, hi, qi: (0, 0)),
            pl.BlockSpec((1, lanes), lambda bi, hi, qi: (0, 0)),
        ],
        out_specs=pl.BlockSpec((1, MOBA_BLOCK, lanes), lambda bi, hi, qi: (bi, qi, hi)),
        out_shape=jax.ShapeDtypeStruct((b, s, width), BF16),
        scratch_shapes=[
            pltpu.VMEM((s, lanes), BF16),
            pltpu.VMEM((nb, lanes), F32),
            pltpu.VMEM((lanes, s), BF16),
        ],
        compiler_params=_params(("parallel", "parallel", "arbitrary")),
        name="moba_attention",
    )(slopes, q, k, v, q_gain, k_gain)


def _ssd_kernel(xbc_ref, z_ref, dt_ref, cw_ref, cb_ref, dtb_ref, alog_ref, dskip_ref, nw_ref,
                out_ref, hist_s, state_s):
    c = pl.program_id(1)
    L = SSM_CHUNK
    inner = SSM_HEADS * SSM_HEAD_DIM
    gw = SSM_STATE
    pair_w = 2 * SSM_HEAD_DIM

    @pl.when(c == 0)
    def _():
        hist_s[...] = jnp.zeros_like(hist_s)
        state_s[...] = jnp.zeros_like(state_s)

    cur = xbc_ref[0].astype(F32)
    ext = jnp.concatenate([hist_s[...], cur], axis=0)
    hist_s[...] = cur[L - CONV_HALO:, :]
    act = _silu(_causal_conv(ext, cw_ref, SSM_CONV, CONV_HALO) + cb_ref[...])
    xs = act[:, :inner]

    dt = _softplus(dt_ref[0] + dtb_ref[...])
    a = dt * (-jnp.exp(alog_ref[...]))
    row = lax.broadcasted_iota(jnp.int32, (L, L), 0)
    col = lax.broadcasted_iota(jnp.int32, (L, L), 1)
    causal = row >= col
    acum = _select_rows_dot(jnp.where(causal, 1.0, 0.0).astype(BF16), a)
    acum_t = acum.T

    hrow = lax.broadcasted_iota(jnp.int32, (LANES, inner), 0)
    hcol = lax.broadcasted_iota(jnp.int32, (LANES, inner), 1)
    widen = jnp.where(hcol // SSM_HEAD_DIM == hrow, 1.0, 0.0).astype(BF16)
    widened = _select_cols_dot(jnp.concatenate([dt, acum], axis=0), widen)
    dt_w = widened[:L]
    acum_w = widened[L:]
    last_w = acum_w[L - 1:L, :]
    xdt = xs * dt_w
    xdt_bf = xdt.astype(BF16)
    xdt_dec_bf = (xdt * jnp.exp(last_w - acum_w)).astype(BF16)
    exp_acum_w = jnp.exp(acum_w)
    chunk_decay_w = jnp.exp(last_w)
    lane = lax.broadcasted_iota(jnp.int32, (L, pair_w), 1)
    first_head = lane < SSM_HEAD_DIM

    heads_per_group = SSM_HEADS // SSM_GROUPS
    ys = []
    for g in range(SSM_GROUPS):
        bg = act[:, inner + g * gw:inner + (g + 1) * gw]
        cg = act[:, inner + SSM_GROUPS * gw + g * gw:inner + SSM_GROUPS * gw + (g + 1) * gw]
        cg_bf = cg.astype(BF16)
        cb = _dot_nt(cg_bf, bg.astype(BF16))
        bg_t = bg.T.astype(BF16)
        for pp in range(heads_per_group // 2):
            h0 = g * heads_per_group + 2 * pp
            pi = h0 // 2
            sl = slice(h0 * SSM_HEAD_DIM, h0 * SSM_HEAD_DIM + pair_w)
            yd = []
            for hh in (h0, h0 + 1):
                diff = jnp.broadcast_to(acum[:, hh:hh + 1], (L, L)) - acum_t[hh:hh + 1, :]
                decay = jnp.exp(jnp.where(causal, diff, NEG))
                yd.append(_dot((cb * decay).astype(BF16), xdt_bf[:, sl]))
            y_diag = jnp.where(first_head, yd[0], yd[1])
            prev = state_s[pi]
            y_off = _dot(cg_bf, prev.astype(BF16)) * exp_acum_w[:, sl]
            state_s[pi] = prev * chunk_decay_w[:, sl] + _dot(bg_t, xdt_dec_bf[:, sl])
            ys.append(y_diag + y_off + dskip_ref[:, sl] * xs[:, sl])
    y = jnp.concatenate(ys, axis=1)

    zf = z_ref[0].astype(F32)
    yg = y * _silu(zf)
    gsz = inner // SSM_GROUPS
    parts = []
    for g in range(SSM_GROUPS):
        seg = yg[:, g * gsz:(g + 1) * gsz]
        parts.append(seg * lax.rsqrt(jnp.mean(seg * seg, axis=-1, keepdims=True) + EPS))
    out_ref[0] = (jnp.concatenate(parts, axis=1) * nw_ref[...]).astype(BF16)


def _ssd(xbc, z, dt, conv_w, conv_b, dt_bias, a_log, d_skip_w, norm_w):
    b, s, cd = xbc.shape
    inner = z.shape[2]
    nc = s // SSM_CHUNK
    return pl.pallas_call(
        _ssd_kernel,
        grid=(b, nc),
        in_specs=[
            pl.BlockSpec((1, SSM_CHUNK, cd), lambda bi, ci: (bi, ci, 0)),
            pl.BlockSpec((1, SSM_CHUNK, inner), lambda bi, ci: (bi, ci, 0)),
            pl.BlockSpec((1, SSM_CHUNK, LANES), lambda bi, ci: (bi, ci, 0)),
            _const_spec(conv_w.shape), _const_spec(conv_b.shape), _const_spec(dt_bias.shape),
            _const_spec(a_log.shape), _const_spec(d_skip_w.shape), _const_spec(norm_w.shape),
        ],
        out_specs=pl.BlockSpec((1, SSM_CHUNK, inner), lambda bi, ci: (bi, ci, 0)),
        out_shape=jax.ShapeDtypeStruct((b, s, inner), BF16),
        scratch_shapes=[
            pltpu.VMEM((CONV_HALO, cd), F32),
            pltpu.VMEM((SSM_HEADS // 2, SSM_STATE, 2 * SSM_HEAD_DIM), F32),
        ],
        compiler_params=_params(("parallel", "arbitrary")),
        name="ssd_heads",
    )(xbc, z, dt, conv_w, conv_b, dt_bias, a_log, d_skip_w, norm_w)


def _outproj_even_kernel(x_ref, a_ref, s_ref, wa_ref, ws_ref, o_ref):
    o_ref[...] = x_ref[...] + _dot(a_ref[...], wa_ref[...]) + _dot(s_ref[...], ws_ref[...])


def _outproj_even(x2, attn, ssm, wa, ws, tm=1024):
    t, d = x2.shape
    return pl.pallas_call(
        _outproj_even_kernel,
        grid=(t // tm,),
        in_specs=[pl.BlockSpec((tm, d), lambda i: (i, 0)),
                  pl.BlockSpec((tm, attn.shape[1]), lambda i: (i, 0)),
                  pl.BlockSpec((tm, ssm.shape[1]), lambda i: (i, 0)),
                  _const_spec(wa.shape), _const_spec(ws.shape)],
        out_specs=pl.BlockSpec((tm, d), lambda i: (i, 0)),
        out_shape=jax.ShapeDtypeStruct((t, d), F32),
        compiler_params=_params(("parallel",)),
        name="outproj_even",
    )(x2, attn, ssm, wa, ws)


def _outproj_odd_kernel(x_ref, o_in_ref, w_ref, o_ref):
    o_ref[...] = x_ref[...] + _dot(o_in_ref[...], w_ref[...])


def _outproj_odd(x2, o_in, w, tm=1024):
    t, d = x2.shape
    return pl.pallas_call(
        _outproj_odd_kernel,
        grid=(t // tm,),
        in_specs=[pl.BlockSpec((tm, d), lambda i: (i, 0)),
                  pl.BlockSpec((tm, o_in.shape[1]), lambda i: (i, 0)),
                  _const_spec(w.shape)],
        out_specs=pl.BlockSpec((tm, d), lambda i: (i, 0)),
        out_shape=jax.ShapeDtypeStruct((t, d), F32),
        compiler_params=_params(("parallel",)),
        name="outproj_odd",
    )(x2, o_in, w)


def _ffn_ple_kernel(x_ref, halo_ref, p_ref, gf_ref, wg_ref, wu_ref, cw_ref, cb_ref, wd_ref,
                    gp_ref, wpg_ref, wpp_ref, o_ref, *, tm, seq):
    i = pl.program_id(0)
    x = x_ref[...]
    keep_halo = jnp.where((i * tm) % seq == 0, 0.0, 1.0)
    halo = halo_ref[...] * keep_halo
    hn = _rms_rows(jnp.concatenate([halo, x], axis=0), gf_ref[...]).astype(BF16)
    hn_cur = hn[FFN_HALO:]

    d_ff = wg_ref.shape[1]
    acc = x
    for j in range(d_ff // FFN_COLS):
        cs = slice(j * FFN_COLS, (j + 1) * FFN_COLS)
        g = _dot(hn, wg_ref[:, cs])
        gc = cb_ref[:, cs] + cw_ref[FFN_CONV - 1:FFN_CONV, cs] * g[FFN_HALO:]
        for shift in range(1, FFN_CONV):
            gc = gc + cw_ref[FFN_CONV - 1 - shift:FFN_CONV - shift, cs] * pltpu.roll(g, shift, axis=0)[FFN_HALO:]
        u = _dot(hn_cur, wu_ref[:, cs])
        acc = acc + _dot((_silu(gc) * u).astype(BF16), wd_ref[cs, :])
    x1 = acc
    hp = _rms_rows(x1, gp_ref[...]).astype(BF16)
    gate = _sigmoid(_dot(hp, wpg_ref[...]))
    o_ref[...] = x1 + gate * _dot(p_ref[...].astype(BF16), wpp_ref[...])


def _ffn_ple(x2, p2, seq, gain_ffn, wg, wu, conv_w, conv_b, wd, gain_ple, wpg, wpp, tm=1024):
    t, d = x2.shape
    halo_blocks = tm // FFN_HALO
    consts = (gain_ffn, wg, wu, conv_w, conv_b, wd, gain_ple, wpg, wpp)
    return pl.pallas_call(
        functools.partial(_ffn_ple_kernel, tm=tm, seq=seq),
        grid=(t // tm,),
        in_specs=[pl.BlockSpec((tm, d), lambda i: (i, 0)),
                  pl.BlockSpec((FFN_HALO, d), lambda i: (jnp.maximum(i * halo_blocks - 1, 0), 0)),
                  pl.BlockSpec((tm, p2.shape[1]), lambda i: (i, 0))]
        + [_const_spec(c.shape) for c in consts],
        out_specs=pl.BlockSpec((tm, d), lambda i: (i, 0)),
        out_shape=jax.ShapeDtypeStruct((t, d), F32),
        compiler_params=_params(("parallel",)),
        name="ffn_ple",
    )(x2, x2, p2, *consts)


def _inproj_odd_kernel(x_ref, g_ref, wqkv_ref, wz_ref, wba_ref, qkv_ref, z_ref, ba_ref):
    hn = _rms_rows(x_ref[...], g_ref[...]).astype(BF16)
    qkv_ref[...] = _dot(hn, wqkv_ref[...]).astype(BF16)
    z_ref[...] = _dot(hn, wz_ref[...]).astype(BF16)
    ba_ref[...] = _dot(hn, wba_ref[...])


def _inproj_odd(x2, gain, wqkv, wz, wba, tm=256):
    t, d = x2.shape
    outs = [(wqkv.shape[1], BF16), (wz.shape[1], BF16), (wba.shape[1], F32)]
    return pl.pallas_call(
        _inproj_odd_kernel,
        grid=(t // tm,),
        in_specs=[pl.BlockSpec((tm, d), lambda i: (i, 0)), _const_spec(gain.shape)]
        + [_const_spec(w.shape) for w in (wqkv, wz, wba)],
        out_specs=[pl.BlockSpec((tm, n), lambda i: (i, 0)) for n, _ in outs],
        out_shape=[jax.ShapeDtypeStruct((t, n), dt) for n, dt in outs],
        compiler_params=_params(("parallel",)),
        name="inproj_odd",
    )(x2, gain, wqkv, wz, wba)


def _bdot(a, b):
    return _dot(a.astype(BF16), b.astype(BF16))


def _unit_lower_inverses(mats):
    L = mats[0].shape[0]
    row = lax.broadcasted_iota(jnp.int32, (L, L), 0)
    col = lax.broadcasted_iota(jnp.int32, (L, L), 1)
    eye = jnp.where(row == col, 1.0, 0.0)
    base = 8
    same = (row // base) == (col // base)
    ns = [-jnp.where(same, a, 0.0) for a in mats]
    ts = [eye + n for n in ns]
    pws = [_bdot(n, n) for n in ns]
    ts = [t + _bdot(t, pw) for t, pw in zip(ts, pws)]
    pws = [_bdot(pw, pw) for pw in pws]
    ts = [t + _bdot(t, pw) for t, pw in zip(ts, pws)]
    size = base
    while size < L:
        wider = (row // (2 * size)) == (col // (2 * size))
        offs = [jnp.where(wider & jnp.logical_not(same), a, 0.0) for a in mats]
        ys = [_bdot(off, t) for off, t in zip(offs, ts)]
        ts = [t - _bdot(t, y) for t, y in zip(ts, ys)]
        same = wider
        size *= 2
    return ts


def _gdn_kernel(qkv_ref, z_ref, ba_ref, cw_ref, dtb_ref, alog_ref, nw_ref, out_ref, hist_s, state_s):
    c = pl.program_id(1)
    L = GDN_CHUNK
    dk = GDN_DIM
    qk_w = GDN_K_HEADS * dk
    rep = GDN_V_HEADS // GDN_K_HEADS
    nseq = qkv_ref.shape[0]

    @pl.when(c == 0)
    def _():
        hist_s[...] = jnp.zeros_like(hist_s)
        state_s[...] = jnp.zeros_like(state_s)

    row = lax.broadcasted_iota(jnp.int32, (L, L), 0)
    col = lax.broadcasted_iota(jnp.int32, (L, L), 1)
    lower = row >= col
    strict = row > col

    items = [(sq, hv) for sq in range(nseq) for hv in range(GDN_V_HEADS)]
    acts, qs, ks, kts, kk, qk0 = {}, {}, {}, {}, {}, {}
    beta_c, gc_c, gc_r = {}, {}, {}
    for sq in range(nseq):
        cur = qkv_ref[sq].astype(F32)
        ext = jnp.concatenate([hist_s[sq], cur], axis=0)
        hist_s[sq] = cur[L - CONV_HALO:, :]
        act = _silu(_causal_conv(ext, cw_ref, GDN_CONV, CONV_HALO))
        acts[sq] = act

        ba = ba_ref[sq]
        beta = _sigmoid(ba)
        g = -jnp.exp(alog_ref[...]) * _softplus(ba + dtb_ref[...])
        gc = _select_rows_dot(jnp.where(lower, 1.0, 0.0).astype(BF16), g)
        gc_t = gc.T
        for hv in range(GDN_V_HEADS):
            beta_c[sq, hv] = beta[:, hv:hv + 1]
            gc_c[sq, hv] = gc[:, GDN_V_HEADS + hv:GDN_V_HEADS + hv + 1]
            gc_r[sq, hv] = gc_t[GDN_V_HEADS + hv:GDN_V_HEADS + hv + 1, :]

        qk_act = act[:, :2 * qk_w]
        qk_n = qk_act * lax.rsqrt(_head_sums(qk_act * qk_act, dk) + EPS)
        for kh in range(GDN_K_HEADS):
            qn = qk_n[:, kh * dk:(kh + 1) * dk] * (dk ** -0.5)
            kn = qk_n[:, qk_w + kh * dk:qk_w + (kh + 1) * dk]
            kt = kn.T
            prod = _bdot(jnp.concatenate([kn, qn], axis=0), kt)
            qs[sq, kh], ks[sq, kh], kts[sq, kh] = qn, kn, kt
            kk[sq, kh], qk0[sq, kh] = prod[:L], prod[L:]

    def khead(it):
        return it[0], it[1] // rep

    g_last = {it: gc_r[it][:, L - 1:L] for it in items}
    egc = {it: jnp.exp(gc_c[it]) for it in items}
    decay = {it: jnp.exp(jnp.where(lower, gc_c[it] - gc_r[it], NEG)) for it in items}
    a_mats = [jnp.where(strict, kk[khead(it)] * beta_c[it] * decay[it], 0.0) for it in items]
    t_inv = dict(zip(items, _unit_lower_inverses(a_mats)))

    sols = {}
    for it in items:
        sq, hv = it
        v_h = acts[sq][:, 2 * qk_w + hv * dk:2 * qk_w + (hv + 1) * dk]
        kbeta = ks[khead(it)] * beta_c[it]
        sols[it] = _bdot(t_inv[it], jnp.concatenate([v_h * beta_c[it], kbeta * egc[it]], axis=1))

    states = {it: state_s[it[0], it[1]] for it in items}
    from_state = {it: _bdot(jnp.concatenate([sols[it][:, dk:], qs[khead(it)] * egc[it]], axis=0), states[it])
                  for it in items}
    v_new = {it: sols[it][:, :dk] - from_state[it][:L] for it in items}
    from_v = {it: _bdot(jnp.concatenate([qk0[khead(it)] * decay[it],
                                         kts[khead(it)] * jnp.exp(g_last[it] - gc_r[it])], axis=0), v_new[it])
              for it in items}
    for it in items:
        state_s[it[0], it[1]] = states[it] * jnp.exp(g_last[it]) + from_v[it][L:]
    for sq in range(nseq):
        o = jnp.concatenate([from_state[sq, hv][L:] + from_v[sq, hv][:L] for hv in range(GDN_V_HEADS)],
                            axis=1)
        o = o * lax.rsqrt(_head_sums(o * o, dk) * (1.0 / dk) + EPS) * nw_ref[...]
        out_ref[sq] = (o * _silu(z_ref[sq].astype(F32))).astype(BF16)


def _gdn(qkv, z, ba, conv_w, dt_bias_pad, a_log_pad, norm_w):
    b, s, cd = qkv.shape
    vw = z.shape[2]
    nc = s // GDN_CHUNK
    nseq = GDN_SEQS_PER_STEP
    return pl.pallas_call(
        _gdn_kernel,
        grid=(b // nseq, nc),
        in_specs=[
            pl.BlockSpec((nseq, GDN_CHUNK, cd), lambda bi, ci: (bi, ci, 0)),
            pl.BlockSpec((nseq, GDN_CHUNK, vw), lambda bi, ci: (bi, ci, 0)),
            pl.BlockSpec((nseq, GDN_CHUNK, LANES), lambda bi, ci: (bi, ci, 0)),
            _const_spec(conv_w.shape), _const_spec(dt_bias_pad.shape),
            _const_spec(a_log_pad.shape), _const_spec(norm_w.shape),
        ],
        out_specs=pl.BlockSpec((nseq, GDN_CHUNK, vw), lambda bi, ci: (bi, ci, 0)),
        out_shape=jax.ShapeDtypeStruct((b, s, vw), BF16),
        scratch_shapes=[
            pltpu.VMEM((nseq, CONV_HALO, cd), F32),
            pltpu.VMEM((nseq, GDN_V_HEADS, GDN_DIM, GDN_DIM), F32),
        ],
        compiler_params=_params(("parallel", "arbitrary")),
        name="gated_deltanet",
    )(qkv, z, ba, conv_w, dt_bias_pad, a_log_pad, norm_w)


def _pad_lanes(v, offset=0):
    out = jnp.zeros((1, LANES), F32)
    return out.at[0, offset:offset + v.shape[0]].set(v.astype(F32))


def _row(v):
    return v.astype(F32).reshape(1, -1)


def _even_mixer(x2, batch, seq, norm_w, w_in, q_norm, k_norm, conv_w, conv_b, dt_bias, a_log, d_skip,
                ssm_norm, w_out, slopes):
    mw = MOBA_HEADS * MOBA_HEAD_DIM
    inner = SSM_HEADS * SSM_HEAD_DIM
    conv_dim = inner + 2 * SSM_GROUPS * SSM_STATE
    w = w_in.astype(BF16)
    c0, c1, c2, c3, c4 = mw, 2 * mw, 3 * mw, 3 * mw + inner, 3 * mw + inner + conv_dim
    wdt = jnp.pad(w[:, c4:], ((0, 0), (0, LANES - SSM_HEADS)))
    q, k, v, z, xbc, dt = _inproj_even(x2, _row(norm_w), w[:, :c0], w[:, c0:c1], w[:, c1:c2],
                                       w[:, c2:c3], w[:, c3:c4], wdt)
    attn = _moba(q.reshape(batch, seq, mw), k.reshape(batch, seq, mw), v.reshape(batch, seq, mw),
                 _row(jnp.tile(q_norm, MOBA_HEADS_PER_STEP)), _row(jnp.tile(k_norm, MOBA_HEADS_PER_STEP)),
                 slopes).reshape(batch * seq, mw)

    ssm = _ssd(xbc.reshape(batch, seq, conv_dim), z.reshape(batch, seq, inner),
               dt.reshape(batch, seq, LANES), conv_w.astype(F32), _row(conv_b),
               _pad_lanes(dt_bias), _pad_lanes(a_log), _row(jnp.repeat(d_skip, SSM_HEAD_DIM)),
               _row(ssm_norm))
    wo = w_out.astype(BF16)
    return _outproj_even(x2, attn, ssm.reshape(batch * seq, inner), wo[:mw], wo[mw:])


def _gdn_mixer(x2, batch, seq, norm_w, w_in, conv_w, dt_bias, a_log, gdn_norm, w_out):
    conv_dim = 2 * GDN_K_HEADS * GDN_DIM + GDN_V_HEADS * GDN_DIM
    vw = GDN_V_HEADS * GDN_DIM
    w = w_in.astype(BF16)
    wba = jnp.pad(w[:, conv_dim + vw:], ((0, 0), (0, LANES - 2 * GDN_V_HEADS)))
    qkv, z, ba = _inproj_odd(x2, _row(norm_w), w[:, :conv_dim], w[:, conv_dim:conv_dim + vw], wba)
    o = _gdn(qkv.reshape(batch, seq, conv_dim), z.reshape(batch, seq, vw), ba.reshape(batch, seq, LANES),
             conv_w.astype(F32), _pad_lanes(dt_bias, GDN_V_HEADS), _pad_lanes(a_log, GDN_V_HEADS),
             _row(jnp.tile(gdn_norm, GDN_V_HEADS)))
    return _outproj_odd(x2, o.reshape(batch * seq, vw), w_out.astype(BF16))


def kernel(x, p, norm_mix, norm_ffn, norm_ple, w_in_even, moba_q_norm, moba_k_norm, ssm_conv_w, ssm_conv_b, ssm_dt_bias, ssm_a_log, ssm_d, ssm_norm, w_out_even, w_in_odd, gdn_conv_w, gdn_dt_bias, gdn_a_log, gdn_norm, w_out_odd, ffn_w_gate, ffn_w_up, ffn_conv_w, ffn_conv_b, ffn_w_down, ple_w_proj, ple_w_gate):
    batch, seq, d = x.shape
    depth = p.shape[0]
    slopes = jnp.exp2(-ALIBI_MAX_BIAS * jnp.arange(1, MOBA_HEADS + 1, dtype=F32) / MOBA_HEADS)
    x2 = x.reshape(batch * seq, d)
    for i in range(depth):
        j = i // 2
        if i % 2 == 0:
            x2 = _even_mixer(x2, batch, seq, norm_mix[i], w_in_even[j], moba_q_norm[j], moba_k_norm[j],
                             ssm_conv_w[j], ssm_conv_b[j], ssm_dt_bias[j], ssm_a_log[j], ssm_d[j],
                             ssm_norm[j], w_out_even[j], slopes)
        else:
            x2 = _gdn_mixer(x2, batch, seq, norm_mix[i], w_in_odd[j], gdn_conv_w[j], gdn_dt_bias[j],
                            gdn_a_log[j], gdn_norm[j], w_out_odd[j])
        x2 = _ffn_ple(x2, p[i].reshape(batch * seq, -1), seq, _row(norm_ffn[i]),
                      ffn_w_gate[i].astype(BF16), ffn_w_up[i].astype(BF16), ffn_conv_w[i].astype(F32),
                      _row(ffn_conv_b[i]), ffn_w_down[i].astype(BF16), _row(norm_ple[i]),
                      ple_w_gate[i].astype(BF16), ple_w_proj[i].astype(BF16))
    return x2.reshape(batch, seq, d)
```

```python
import functools

import jax
import jax.numpy as jnp
from jax import lax
from jax.experimental import pallas as pl
from jax.experimental.pallas import tpu as pltpu

F32 = jnp.float32
BF16 = jnp.bfloat16
HIGHEST = lax.Precision.HIGHEST

EPS = 1e-6
NEG = -1e30
LOG2E = 1.4426950408889634
LANES = 128
CONV_HALO = 8

MOBA_HEADS = 8
MOBA_HEAD_DIM = 64
MOBA_BLOCK = 256
MOBA_TOPK = 3
MOBA_HEADS_PER_STEP = 4
ALIBI_MAX_BIAS = 8.0

SSM_HEADS = 16
SSM_HEAD_DIM = 64
SSM_STATE = 128
SSM_GROUPS = 2
SSM_CONV = 4
SSM_CHUNK = 128

GDN_K_HEADS = 8
GDN_V_HEADS = 16
GDN_DIM = 128
GDN_CONV = 4
GDN_CHUNK = 64
GDN_SEQS_PER_STEP = 2

FFN_CONV = 3
FFN_HALO = 16
FFN_COLS = 256

VMEM_LIMIT = 56 * 1024 * 1024


def _rms_rows(x, gain):
    return x * lax.rsqrt(jnp.mean(x * x, axis=-1, keepdims=True) + EPS) * gain


def _sigmoid(x):
    return 0.5 * jnp.tanh(0.5 * x) + 0.5


def _silu(x):
    return x * _sigmoid(x)


def _softplus(x):
    return jnp.maximum(x, 0.0) + jnp.log(1.0 + jnp.exp(-jnp.abs(x)))


def _dot(a, b):
    return jnp.dot(a, b, preferred_element_type=F32)


def _dot_nt(a, b):
    return lax.dot_general(a, b, (((1,), (1,)), ((), ())), preferred_element_type=F32)


def _bf16_terms(x):
    hi = x.astype(BF16)
    rest = x - hi.astype(F32)
    mid = rest.astype(BF16)
    lo = (rest - mid.astype(F32)).astype(BF16)
    return [hi, mid, lo]


def _select_rows_dot(sel, x):
    return _dot(jnp.concatenate([sel] * 3, axis=1), jnp.concatenate(_bf16_terms(x), axis=0))


def _select_cols_dot(x, sel):
    return _dot(jnp.concatenate(_bf16_terms(x), axis=1), jnp.concatenate([sel] * 3, axis=0))


def _head_sums(x, head_w):
    rows, width = x.shape
    slab = 2 * LANES
    n = width // slab
    same = (lax.broadcasted_iota(jnp.int32, (slab, slab), 0) // head_w
            == lax.broadcasted_iota(jnp.int32, (slab, slab), 1) // head_w)
    ones = jnp.where(same, 1.0, 0.0).astype(BF16)
    stacked = jnp.concatenate([x[:, i * slab:(i + 1) * slab] for i in range(n)], axis=0)
    sums = _dot(stacked.astype(BF16), ones)
    return jnp.concatenate([sums[i * rows:(i + 1) * rows] for i in range(n)], axis=1)


def _const_spec(shape):
    zeros = (0,) * len(shape)
    return pl.BlockSpec(shape, lambda *_: zeros, pipeline_mode=pl.Buffered(1))


def _params(semantics):
    return pltpu.CompilerParams(dimension_semantics=semantics, vmem_limit_bytes=VMEM_LIMIT)


def _causal_conv(ext, w_ref, width, halo):
    y = w_ref[width - 1:width, :] * ext[halo:]
    for shift in range(1, width):
        y = y + w_ref[width - 1 - shift:width - shift, :] * pltpu.roll(ext, shift, axis=0)[halo:]
    return y


def _inproj_even_kernel(x_ref, g_ref, wq_ref, wk_ref, wv_ref, wz_ref, wxbc_ref, wdt_ref,
                        q_ref, k_ref, v_ref, z_ref, xbc_ref, dt_ref):
    hn = _rms_rows(x_ref[...], g_ref[...]).astype(BF16)
    q_ref[...] = _dot(hn, wq_ref[...]).astype(BF16)
    k_ref[...] = _dot(hn, wk_ref[...]).astype(BF16)
    v_ref[...] = _dot(hn, wv_ref[...]).astype(BF16)
    z_ref[...] = _dot(hn, wz_ref[...]).astype(BF16)
    xbc_ref[...] = _dot(hn, wxbc_ref[...]).astype(BF16)
    dt_ref[...] = _dot(hn, wdt_ref[...])


def _inproj_even(x2, gain, wq, wk, wv, wz, wxbc, wdt, tm=512):
    t, d = x2.shape
    outs = [(wq.shape[1], BF16), (wk.shape[1], BF16), (wv.shape[1], BF16),
            (wz.shape[1], BF16), (wxbc.shape[1], BF16), (wdt.shape[1], F32)]
    return pl.pallas_call(
        _inproj_even_kernel,
        grid=(t // tm,),
        in_specs=[pl.BlockSpec((tm, d), lambda i: (i, 0)), _const_spec(gain.shape)]
        + [_const_spec(w.shape) for w in (wq, wk, wv, wz, wxbc, wdt)],
        out_specs=[pl.BlockSpec((tm, n), lambda i: (i, 0)) for n, _ in outs],
        out_shape=[jax.ShapeDtypeStruct((t, n), dt) for n, dt in outs],
        compiler_params=_params(("parallel",)),
        name="inproj_even",
    )(x2, gain, wq, wk, wv, wz, wxbc, wdt)


def _moba_kernel(slopes_ref, q_ref, k_ref, v_ref, qg_ref, kg_ref, o_ref, kn_s, kbar_s, vt_s):
    hp = pl.program_id(1)
    qi = pl.program_id(2)
    seq = k_ref.shape[1]
    nb = seq // MOBA_BLOCK
    dh = MOBA_HEAD_DIM
    hps = MOBA_HEADS_PER_STEP
    heads = range(hps)

    lanes = hps * dh
    same_head = (lax.broadcasted_iota(jnp.int32, (lanes, lanes), 0) // dh
                 == lax.broadcasted_iota(jnp.int32, (lanes, lanes), 1) // dh)
    head_ones = jnp.where(same_head, 1.0, 0.0).astype(BF16)

    def head_rms(x, gain):
        sq = x * x
        hi = sq.astype(BF16)
        lo = (sq - hi.astype(F32)).astype(BF16)
        ms = (_dot(hi, head_ones) + _dot(lo, head_ones)) * (1.0 / dh)
        return x * lax.rsqrt(ms + EPS) * gain

    @pl.when(qi == 0)
    def _():
        kn = head_rms(k_ref[0].astype(F32), kg_ref[...])
        kn_s[...] = kn.astype(BF16)
        kbar_s[...] = jnp.mean(kn.reshape(nb, MOBA_BLOCK, lanes), axis=1)
        vt_s[...] = v_ref[0].astype(F32).T.astype(BF16)

    def attend(own):
        qn = head_rms(q_ref[0].astype(F32), qg_ref[...])
        lane_head = lax.broadcasted_iota(jnp.int32, (MOBA_BLOCK, lanes), 1) // dh
        slopes = [slopes_ref[hp * hps + hh] for hh in heads]
        q_heads = [jnp.where(lane_head == hh, qn, 0.0) for hh in heads]
        qs = [(q_h * (dh ** -0.5 * LOG2E)).astype(BF16) for q_h in q_heads]
        scores = {(hh, j): _dot_nt(kn_s[j * MOBA_BLOCK:(j + 1) * MOBA_BLOCK, :], qs[hh])
                  for hh in heads for j in range(own + 1)}

        blk = lax.broadcasted_iota(jnp.int32, (nb, MOBA_BLOCK), 0)
        kbar_head = lax.broadcasted_iota(jnp.int32, (nb, lanes), 1) // dh
        kbar = kbar_s[...]
        kbar_rows = jnp.concatenate([jnp.where(kbar_head == hh, kbar, 0.0) for hh in heads], axis=0)
        gates = lax.dot_general(kbar_rows, qn, (((1,), (1,)), ((), ())),
                                precision=HIGHEST, preferred_element_type=F32)
        selb = []
        for hh in heads:
            gate = gates[hh * nb:(hh + 1) * nb, :]
            rank = jnp.zeros(gate.shape, F32)
            for m in range(own):
                gm = gate[m:m + 1, :]
                beats = (gm > gate) | ((gm == gate) & (m < blk))
                rank = rank + jnp.where(beats, 1.0, 0.0)
            selected = (blk < own) & (rank < float(min(MOBA_TOPK, nb - 1)))
            selb.append(jnp.where(selected, 0.0, NEG))

        kidx = lax.broadcasted_iota(jnp.int32, (MOBA_BLOCK, MOBA_BLOCK), 0)
        qidx = lax.broadcasted_iota(jnp.int32, (MOBA_BLOCK, MOBA_BLOCK), 1)
        rel = (kidx - qidx).astype(F32)
        probs, denoms = [], []
        for hh in heads:
            slope2 = slopes[hh] * LOG2E
            alibi = slope2 * rel
            tiles = [scores[hh, j] + alibi for j in range(own)]
            tiles.append(jnp.where(rel <= 0.0, scores[hh, own] + alibi, NEG))
            col_bias = [selb[hh][j:j + 1, :] + slope2 * float((j - own) * MOBA_BLOCK) for j in range(own)]
            col_bias.append(jnp.zeros((1, MOBA_BLOCK), F32))
            m = jnp.max(tiles[0], axis=0, keepdims=True) + col_bias[0]
            for t, cb in zip(tiles[1:], col_bias[1:]):
                m = jnp.maximum(m, jnp.max(t, axis=0, keepdims=True) + cb)
            ps = [jnp.exp2(t + (cb - m)) for t, cb in zip(tiles, col_bias)]
            denom = jnp.sum(ps[0], axis=0, keepdims=True)
            for p in ps[1:]:
                denom = denom + jnp.sum(p, axis=0, keepdims=True)
            probs.append(jnp.concatenate([p.astype(BF16) for p in ps], axis=0))
            denoms.append(denom)
        keys = (own + 1) * MOBA_BLOCK
        outs = [_dot(vt_s[hh * dh:(hh + 1) * dh, 0:keys], probs[hh]) / denoms[hh] for hh in heads]
        o_ref[0] = jnp.concatenate(outs, axis=0).T.astype(BF16)

    for own in range(nb):
        pl.when(qi == own)(functools.partial(attend, own))


def _moba(q, k, v, q_gain, k_gain, slopes):
    b, s, width = q.shape
    dh = MOBA_HEAD_DIM
    hps = MOBA_HEADS_PER_STEP
    lanes = hps * dh
    nq = s // MOBA_BLOCK
    nb = s // MOBA_BLOCK
    return pl.pallas_call(
        _moba_kernel,
        grid=(b, width // lanes, nq),
        in_specs=[
            pl.BlockSpec(memory_space=pltpu.SMEM),
            pl.BlockSpec((1, MOBA_BLOCK, lanes), lambda bi, hi, qi: (bi, qi, hi)),
            pl.BlockSpec((1, s, lanes), lambda bi, hi, qi: (bi, 0, hi)),
            pl.BlockSpec((1, s, lanes), lambda bi, hi, qi: (bi, 0, hi)),
            pl.BlockSpec((1, lanes), lambda bi, hi, qi: (0, 0)),
            pl.BlockSpec((1, lanes), lambda bi, hi, qi: (0, 0)),
        ],
        out_specs=pl.BlockSpec((1, MOBA_BLOCK, lanes), lambda bi, hi, qi: (bi, qi, hi)),
        out_shape=jax.ShapeDtypeStruct((b, s, width), BF16),
        scratch_shapes=[
            pltpu.VMEM((s, lanes), BF16),
            pltpu.VMEM((nb, lanes), F32),
            pltpu.VMEM((lanes, s), BF16),
        ],
        compiler_params=_params(("parallel", "parallel", "arbitrary")),
        name="moba_attention",
    )(slopes, q, k, v, q_gain, k_gain)


def _ssd_kernel(xbc_ref, z_ref, dt_ref, cw_ref, cb_ref, dtb_ref, alog_ref, dskip_ref, nw_ref,
                out_ref, hist_s, state_s):
    c = pl.program_id(1)
    L = SSM_CHUNK
    inner = SSM_HEADS * SSM_HEAD_DIM
    gw = SSM_STATE
    pair_w = 2 * SSM_HEAD_DIM

    @pl.when(c == 0)
    def _():
        hist_s[...] = jnp.zeros_like(hist_s)
        state_s[...] = jnp.zeros_like(state_s)

    cur = xbc_ref[0].astype(F32)
    ext = jnp.concatenate([hist_s[...], cur], axis=0)
    hist_s[...] = cur[L - CONV_HALO:, :]
    act = _silu(_causal_conv(ext, cw_ref, SSM_CONV, CONV_HALO) + cb_ref[...])
    xs = act[:, :inner]

    dt = _softplus(dt_ref[0] + dtb_ref[...])
    a = dt * (-jnp.exp(alog_ref[...]))
    row = lax.broadcasted_iota(jnp.int32, (L, L), 0)
    col = lax.broadcasted_iota(jnp.int32, (L, L), 1)
    causal = row >= col
    acum = _select_rows_dot(jnp.where(causal, 1.0, 0.0).astype(BF16), a)
    acum_t = acum.T

    hrow = lax.broadcasted_iota(jnp.int32, (LANES, inner), 0)
    hcol = lax.broadcasted_iota(jnp.int32, (LANES, inner), 1)
    widen = jnp.where(hcol // SSM_HEAD_DIM == hrow, 1.0, 0.0).astype(BF16)
    widened = _select_cols_dot(jnp.concatenate([dt, acum], axis=0), widen)
    dt_w = widened[:L]
    acum_w = widened[L:]
    last_w = acum_w[L - 1:L, :]
    xdt = xs * dt_w
    xdt_bf = xdt.astype(BF16)
    xdt_dec_bf = (xdt * jnp.exp(last_w - acum_w)).astype(BF16)
    exp_acum_w = jnp.exp(acum_w)
    chunk_decay_w = jnp.exp(last_w)
    lane = lax.broadcasted_iota(jnp.int32, (L, pair_w), 1)
    first_head = lane < SSM_HEAD_DIM

    heads_per_group = SSM_HEADS // SSM_GROUPS
    ys = []
    for g in range(SSM_GROUPS):
        bg = act[:, inner + g * gw:inner + (g + 1) * gw]
        cg = act[:, inner + SSM_GROUPS * gw + g * gw:inner + SSM_GROUPS * gw + (g + 1) * gw]
        cg_bf = cg.astype(BF16)
        cb = _dot_nt(cg_bf, bg.astype(BF16))
        bg_t = bg.T.astype(BF16)
        for pp in range(heads_per_group // 2):
            h0 = g * heads_per_group + 2 * pp
            pi = h0 // 2
            sl = slice(h0 * SSM_HEAD_DIM, h0 * SSM_HEAD_DIM + pair_w)
            yd = []
            for hh in (h0, h0 + 1):
                diff = jnp.broadcast_to(acum[:, hh:hh + 1], (L, L)) - acum_t[hh:hh + 1, :]
                decay = jnp.exp(jnp.where(causal, diff, NEG))
                yd.append(_dot((cb * decay).astype(BF16), xdt_bf[:, sl]))
            y_diag = jnp.where(first_head, yd[0], yd[1])
            prev = state_s[pi]
            y_off = _dot(cg_bf, prev.astype(BF16)) * exp_acum_w[:, sl]
            state_s[pi] = prev * chunk_decay_w[:, sl] + _dot(bg_t, xdt_dec_bf[:, sl])
            ys.append(y_diag + y_off + dskip_ref[:, sl] * xs[:, sl])
    y = jnp.concatenate(ys, axis=1)

    zf = z_ref[0].astype(F32)
    yg = y * _silu(zf)
    gsz = inner // SSM_GROUPS
    parts = []
    for g in range(SSM_GROUPS):
        seg = yg[:, g * gsz:(g + 1) * gsz]
        parts.append(seg * lax.rsqrt(jnp.mean(seg * seg, axis=-1, keepdims=True) + EPS))
    out_ref[0] = (jnp.concatenate(parts, axis=1) * nw_ref[...]).astype(BF16)


def _ssd(xbc, z, dt, conv_w, conv_b, dt_bias, a_log, d_skip_w, norm_w):
    b, s, cd = xbc.shape
    inner = z.shape[2]
    nc = s // SSM_CHUNK
    return pl.pallas_call(
        _ssd_kernel,
        grid=(b, nc),
        in_specs=[
            pl.BlockSpec((1, SSM_CHUNK, cd), lambda bi, ci: (bi, ci, 0)),
            pl.BlockSpec((1, SSM_CHUNK, inner), lambda bi, ci: (bi, ci, 0)),
            pl.BlockSpec((1, SSM_CHUNK, LANES), lambda bi, ci: (bi, ci, 0)),
            _const_spec(conv_w.shape), _const_spec(conv_b.shape), _const_spec(dt_bias.shape),
            _const_spec(a_log.shape), _const_spec(d_skip_w.shape), _const_spec(norm_w.shape),
        ],
        out_specs=pl.BlockSpec((1, SSM_CHUNK, inner), lambda bi, ci: (bi, ci, 0)),
        out_shape=jax.ShapeDtypeStruct((b, s, inner), BF16),
        scratch_shapes=[
            pltpu.VMEM((CONV_HALO, cd), F32),
            pltpu.VMEM((SSM_HEADS // 2, SSM_STATE, 2 * SSM_HEAD_DIM), F32),
        ],
        compiler_params=_params(("parallel", "arbitrary")),
        name="ssd_heads",
    )(xbc, z, dt, conv_w, conv_b, dt_bias, a_log, d_skip_w, norm_w)


def _outproj_even_kernel(x_ref, a_ref, s_ref, wa_ref, ws_ref, o_ref):
    o_ref[...] = x_ref[...] + _dot(a_ref[...], wa_ref[...]) + _dot(s_ref[...], ws_ref[...])


def _outproj_even(x2, attn, ssm, wa, ws, tm=1024):
    t, d = x2.shape
    return pl.pallas_call(
        _outproj_even_kernel,
        grid=(t // tm,),
        in_specs=[pl.BlockSpec((tm, d), lambda i: (i, 0)),
                  pl.BlockSpec((tm, attn.shape[1]), lambda i: (i, 0)),
                  pl.BlockSpec((tm, ssm.shape[1]), lambda i: (i, 0)),
                  _const_spec(wa.shape), _const_spec(ws.shape)],
        out_specs=pl.BlockSpec((tm, d), lambda i: (i, 0)),
        out_shape=jax.ShapeDtypeStruct((t, d), F32),
        compiler_params=_params(("parallel",)),
        name="outproj_even",
    )(x2, attn, ssm, wa, ws)


def _outproj_odd_kernel(x_ref, o_in_ref, w_ref, o_ref):
    o_ref[...] = x_ref[...] + _dot(o_in_ref[...], w_ref[...])


def _outproj_odd(x2, o_in, w, tm=1024):
    t, d = x2.shape
    return pl.pallas_call(
        _outproj_odd_kernel,
        grid=(t // tm,),
        in_specs=[pl.BlockSpec((tm, d), lambda i: (i, 0)),
                  pl.BlockSpec((tm, o_in.shape[1]), lambda i: (i, 0)),
                  _const_spec(w.shape)],
        out_specs=pl.BlockSpec((tm, d), lambda i: (i, 0)),
        out_shape=jax.ShapeDtypeStruct((t, d), F32),
        compiler_params=_params(("parallel",)),
        name="outproj_odd",
    )(x2, o_in, w)


def _ffn_ple_kernel(x_ref, halo_ref, p_ref, gf_ref, wg_ref, wu_ref, cw_ref, cb_ref, wd_ref,
                    gp_ref, wpg_ref, wpp_ref, o_ref, *, tm, seq):
    i = pl.program_id(0)
    x = x_ref[...]
    keep_halo = jnp.where((i * tm) % seq == 0, 0.0, 1.0)
    halo = halo_ref[...] * keep_halo
    hn = _rms_rows(jnp.concatenate([halo, x], axis=0), gf_ref[...]).astype(BF16)
    hn_cur = hn[FFN_HALO:]

    d_ff = wg_ref.shape[1]
    acts = []
    for j in range(d_ff // FFN_COLS):
        cs = slice(j * FFN_COLS, (j + 1) * FFN_COLS)
        g = _dot(hn, wg_ref[:, cs])
        gc = cb_ref[:, cs] + cw_ref[FFN_CONV - 1:FFN_CONV, cs] * g[FFN_HALO:]
        for shift in range(1, FFN_CONV):
            gc = gc + cw_ref[FFN_CONV - 1 - shift:FFN_CONV - shift, cs] * pltpu.roll(g, shift, axis=0)[FFN_HALO:]
        u = _dot(hn_cur, wu_ref[:, cs])
        acts.append((_silu(gc) * u).astype(BF16))
    x1 = x + _dot(jnp.concatenate(acts, axis=1), wd_ref[...])
    hp = _rms_rows(x1, gp_ref[...]).astype(BF16)
    gate = _sigmoid(_dot(hp, wpg_ref[...]))
    o_ref[...] = x1 + gate * _dot(p_ref[...].astype(BF16), wpp_ref[...])


def _ffn_ple(x2, p2, seq, gain_ffn, wg, wu, conv_w, conv_b, wd, gain_ple, wpg, wpp, tm=1024):
    t, d = x2.shape
    halo_blocks = tm // FFN_HALO
    consts = (gain_ffn, wg, wu, conv_w, conv_b, wd, gain_ple, wpg, wpp)
    return pl.pallas_call(
        functools.partial(_ffn_ple_kernel, tm=tm, seq=seq),
        grid=(t // tm,),
        in_specs=[pl.BlockSpec((tm, d), lambda i: (i, 0)),
                  pl.BlockSpec((FFN_HALO, d), lambda i: (jnp.maximum(i * halo_blocks - 1, 0), 0)),
                  pl.BlockSpec((tm, p2.shape[1]), lambda i: (i, 0))]
        + [_const_spec(c.shape) for c in consts],
        out_specs=pl.BlockSpec((tm, d), lambda i: (i, 0)),
        out_shape=jax.ShapeDtypeStruct((t, d), F32),
        compiler_params=_params(("parallel",)),
        name="ffn_ple",
    )(x2, x2, p2, *consts)


def _inproj_odd_kernel(x_ref, g_ref, wqkv_ref, wz_ref, wba_ref, qkv_ref, z_ref, ba_ref):
    hn = _rms_rows(x_ref[...], g_ref[...]).astype(BF16)
    qkv_ref[...] = _dot(hn, wqkv_ref[...]).astype(BF16)
    z_ref[...] = _dot(hn, wz_ref[...]).astype(BF16)
    ba_ref[...] = _dot(hn, wba_ref[...])


def _inproj_odd(x2, gain, wqkv, wz, wba, tm=512):
    t, d = x2.shape
    outs = [(wqkv.shape[1], BF16), (wz.shape[1], BF16), (wba.shape[1], F32)]
    return pl.pallas_call(
        _inproj_odd_kernel,
        grid=(t // tm,),
        in_specs=[pl.BlockSpec((tm, d), lambda i: (i, 0)), _const_spec(gain.shape)]
        + [_const_spec(w.shape) for w in (wqkv, wz, wba)],
        out_specs=[pl.BlockSpec((tm, n), lambda i: (i, 0)) for n, _ in outs],
        out_shape=[jax.ShapeDtypeStruct((t, n), dt) for n, dt in outs],
        compiler_params=_params(("parallel",)),
        name="inproj_odd",
    )(x2, gain, wqkv, wz, wba)


def _bdot(a, b):
    return _dot(a.astype(BF16), b.astype(BF16))


def _unit_lower_inverses(mats):
    L = mats[0].shape[0]
    row = lax.broadcasted_iota(jnp.int32, (L, L), 0)
    col = lax.broadcasted_iota(jnp.int32, (L, L), 1)
    eye = jnp.where(row == col, 1.0, 0.0)
    base = 8
    same = (row // base) == (col // base)
    ns = [-jnp.where(same, a, 0.0) for a in mats]
    ts = [eye + n for n in ns]
    pws = [_bdot(n, n) for n in ns]
    ts = [t + _bdot(t, pw) for t, pw in zip(ts, pws)]
    pws = [_bdot(pw, pw) for pw in pws]
    ts = [t + _bdot(t, pw) for t, pw in zip(ts, pws)]
    size = base
    while size < L:
        wider = (row // (2 * size)) == (col // (2 * size))
        offs = [jnp.where(wider & jnp.logical_not(same), a, 0.0) for a in mats]
        ys = [_bdot(off, t) for off, t in zip(offs, ts)]
        ts = [t - _bdot(t, y) for t, y in zip(ts, ys)]
        same = wider
        size *= 2
    return ts


def _gdn_kernel(qkv_ref, z_ref, ba_ref, cw_ref, dtb_ref, alog_ref, nw_ref, out_ref, hist_s, state_s):
    c = pl.program_id(1)
    L = GDN_CHUNK
    dk = GDN_DIM
    qk_w = GDN_K_HEADS * dk
    rep = GDN_V_HEADS // GDN_K_HEADS
    nseq = qkv_ref.shape[0]

    @pl.when(c == 0)
    def _():
        hist_s[...] = jnp.zeros_like(hist_s)
        state_s[...] = jnp.zeros_like(state_s)

    row = lax.broadcasted_iota(jnp.int32, (L, L), 0)
    col = lax.broadcasted_iota(jnp.int32, (L, L), 1)
    lower = row >= col
    strict = row > col

    items = [(sq, hv) for sq in range(nseq) for hv in range(GDN_V_HEADS)]
    acts, qs, ks, kts, kk, qk0 = {}, {}, {}, {}, {}, {}
    beta_c, gc_c, gc_r = {}, {}, {}
    for sq in range(nseq):
        cur = qkv_ref[sq].astype(F32)
        ext = jnp.concatenate([hist_s[sq], cur], axis=0)
        hist_s[sq] = cur[L - CONV_HALO:, :]
        act = _silu(_causal_conv(ext, cw_ref, GDN_CONV, CONV_HALO))
        acts[sq] = act

        ba = ba_ref[sq]
        beta = _sigmoid(ba)
        g = -jnp.exp(alog_ref[...]) * _softplus(ba + dtb_ref[...])
        gc = _select_rows_dot(jnp.where(lower, 1.0, 0.0).astype(BF16), g)
        gc_t = gc.T
        for hv in range(GDN_V_HEADS):
            beta_c[sq, hv] = beta[:, hv:hv + 1]
            gc_c[sq, hv] = gc[:, GDN_V_HEADS + hv:GDN_V_HEADS + hv + 1]
            gc_r[sq, hv] = gc_t[GDN_V_HEADS + hv:GDN_V_HEADS + hv + 1, :]

        qk_act = act[:, :2 * qk_w]
        qk_n = qk_act * lax.rsqrt(_head_sums(qk_act * qk_act, dk) + EPS)
        for kh in range(GDN_K_HEADS):
            qn = qk_n[:, kh * dk:(kh + 1) * dk] * (dk ** -0.5)
            kn = qk_n[:, qk_w + kh * dk:qk_w + (kh + 1) * dk]
            kt = kn.T
            prod = _bdot(jnp.concatenate([kn, qn], axis=0), kt)
            qs[sq, kh], ks[sq, kh], kts[sq, kh] = qn, kn, kt
            kk[sq, kh], qk0[sq, kh] = prod[:L], prod[L:]

    def khead(it):
        return it[0], it[1] // rep

    g_last = {it: gc_r[it][:, L - 1:L] for it in items}
    egc = {it: jnp.exp(gc_c[it]) for it in items}
    decay = {it: jnp.exp(jnp.where(lower, gc_c[it] - gc_r[it], NEG)) for it in items}
    a_mats = [jnp.where(strict, kk[khead(it)] * beta_c[it] * decay[it], 0.0) for it in items]
    t_inv = dict(zip(items, _unit_lower_inverses(a_mats)))

    sols = {}
    for it in items:
        sq, hv = it
        v_h = acts[sq][:, 2 * qk_w + hv * dk:2 * qk_w + (hv + 1) * dk]
        kbeta = ks[khead(it)] * beta_c[it]
        sols[it] = _bdot(t_inv[it], jnp.concatenate([v_h * beta_c[it], kbeta * egc[it]], axis=1))

    states = {it: state_s[it[0], it[1]] for it in items}
    from_state = {it: _bdot(jnp.concatenate([sols[it][:, dk:], qs[khead(it)] * egc[it]], axis=0), states[it])
                  for it in items}
    v_new = {it: sols[it][:, :dk] - from_state[it][:L] for it in items}
    from_v = {it: _bdot(jnp.concatenate([qk0[khead(it)] * decay[it],
                                         kts[khead(it)] * jnp.exp(g_last[it] - gc_r[it])], axis=0), v_new[it])
              for it in items}
    for it in items:
        state_s[it[0], it[1]] = states[it] * jnp.exp(g_last[it]) + from_v[it][L:]
    for sq in range(nseq):
        o = jnp.concatenate([from_state[sq, hv][L:] + from_v[sq, hv][:L] for hv in range(GDN_V_HEADS)],
                            axis=1)
        o = o * lax.rsqrt(_head_sums(o * o, dk) * (1.0 / dk) + EPS) * nw_ref[...]
        out_ref[sq] = (o * _silu(z_ref[sq].astype(F32))).astype(BF16)


def _gdn(qkv, z, ba, conv_w, dt_bias_pad, a_log_pad, norm_w):
    b, s, cd = qkv.shape
    vw = z.shape[2]
    nc = s // GDN_CHUNK
    nseq = GDN_SEQS_PER_STEP
    return pl.pallas_call(
        _gdn_kernel,
        grid=(b // nseq, nc),
        in_specs=[
            pl.BlockSpec((nseq, GDN_CHUNK, cd), lambda bi, ci: (bi, ci, 0)),
            pl.BlockSpec((nseq, GDN_CHUNK, vw), lambda bi, ci: (bi, ci, 0)),
            pl.BlockSpec((nseq, GDN_CHUNK, LANES), lambda bi, ci: (bi, ci, 0)),
            _const_spec(conv_w.shape), _const_spec(dt_bias_pad.shape),
            _const_spec(a_log_pad.shape), _const_spec(norm_w.shape),
        ],
        out_specs=pl.BlockSpec((nseq, GDN_CHUNK, vw), lambda bi, ci: (bi, ci, 0)),
        out_shape=jax.ShapeDtypeStruct((b, s, vw), BF16),
        scratch_shapes=[
            pltpu.VMEM((nseq, CONV_HALO, cd), F32),
            pltpu.VMEM((nseq, GDN_V_HEADS, GDN_DIM, GDN_DIM), F32),
        ],
        compiler_params=_params(("parallel", "arbitrary")),
        name="gated_deltanet",
    )(qkv, z, ba, conv_w, dt_bias_pad, a_log_pad, norm_w)


def _pad_lanes(v, offset=0):
    out = jnp.zeros((1, LANES), F32)
    return out.at[0, offset:offset + v.shape[0]].set(v.astype(F32))


def _row(v):
    return v.astype(F32).reshape(1, -1)


def _even_mixer(x2, batch, seq, norm_w, w_in, q_norm, k_norm, conv_w, conv_b, dt_bias, a_log, d_skip,
                ssm_norm, w_out, slopes):
    mw = MOBA_HEADS * MOBA_HEAD_DIM
    inner = SSM_HEADS * SSM_HEAD_DIM
    conv_dim = inner + 2 * SSM_GROUPS * SSM_STATE
    w = w_in.astype(BF16)
    c0, c1, c2, c3, c4 = mw, 2 * mw, 3 * mw, 3 * mw + inner, 3 * mw + inner + conv_dim
    wdt = jnp.pad(w[:, c4:], ((0, 0), (0, LANES - SSM_HEADS)))
    q, k, v, z, xbc, dt = _inproj_even(x2, _row(norm_w), w[:, :c0], w[:, c0:c1], w[:, c1:c2],
                                       w[:, c2:c3], w[:, c3:c4], wdt)
    attn = _moba(q.reshape(batch, seq, mw), k.reshape(batch, seq, mw), v.reshape(batch, seq, mw),
                 _row(jnp.tile(q_norm, MOBA_HEADS_PER_STEP)), _row(jnp.tile(k_norm, MOBA_HEADS_PER_STEP)),
                 slopes).reshape(batch * seq, mw)

    ssm = _ssd(xbc.reshape(batch, seq, conv_dim), z.reshape(batch, seq, inner),
               dt.reshape(batch, seq, LANES), conv_w.astype(F32), _row(conv_b),
               _pad_lanes(dt_bias), _pad_lanes(a_log), _row(jnp.repeat(d_skip, SSM_HEAD_DIM)),
               _row(ssm_norm))
    wo = w_out.astype(BF16)
    return _outproj_even(x2, attn, ssm.reshape(batch * seq, inner), wo[:mw], wo[mw:])


def _gdn_mixer(x2, batch, seq, norm_w, w_in, conv_w, dt_bias, a_log, gdn_norm, w_out):
    conv_dim = 2 * GDN_K_HEADS * GDN_DIM + GDN_V_HEADS * GDN_DIM
    vw = GDN_V_HEADS * GDN_DIM
    w = w_in.astype(BF16)
    wba = jnp.pad(w[:, conv_dim + vw:], ((0, 0), (0, LANES - 2 * GDN_V_HEADS)))
    qkv, z, ba = _inproj_odd(x2, _row(norm_w), w[:, :conv_dim], w[:, conv_dim:conv_dim + vw], wba)
    o = _gdn(qkv.reshape(batch, seq, conv_dim), z.reshape(batch, seq, vw), ba.reshape(batch, seq, LANES),
             conv_w.astype(F32), _pad_lanes(dt_bias, GDN_V_HEADS), _pad_lanes(a_log, GDN_V_HEADS),
             _row(jnp.tile(gdn_norm, GDN_V_HEADS)))
    return _outproj_odd(x2, o.reshape(batch * seq, vw), w_out.astype(BF16))


def kernel(x, p, norm_mix, norm_ffn, norm_ple, w_in_even, moba_q_norm, moba_k_norm, ssm_conv_w, ssm_conv_b, ssm_dt_bias, ssm_a_log, ssm_d, ssm_norm, w_out_even, w_in_odd, gdn_conv_w, gdn_dt_bias, gdn_a_log, gdn_norm, w_out_odd, ffn_w_gate, ffn_w_up, ffn_conv_w, ffn_conv_b, ffn_w_down, ple_w_proj, ple_w_gate):
    batch, seq, d = x.shape
    depth = p.shape[0]
    slopes = jnp.exp2(-ALIBI_MAX_BIAS * jnp.arange(1, MOBA_HEADS + 1, dtype=F32) / MOBA_HEADS)
    x2 = x.reshape(batch * seq, d)
    for i in range(depth):
        j = i // 2
        if i % 2 == 0:
            x2 = _even_mixer(x2, batch, seq, norm_mix[i], w_in_even[j], moba_q_norm[j], moba_k_norm[j],
                             ssm_conv_w[j], ssm_conv_b[j], ssm_dt_bias[j], ssm_a_log[j], ssm_d[j],
                             ssm_norm[j], w_out_even[j], slopes)
        else:
            x2 = _gdn_mixer(x2, batch, seq, norm_mix[i], w_in_odd[j], gdn_conv_w[j], gdn_dt_bias[j],
                            gdn_a_log[j], gdn_norm[j], w_out_odd[j])
        x2 = _ffn_ple(x2, p[i].reshape(batch * seq, -1), seq, _row(norm_ffn[i]),
                      ffn_w_gate[i].astype(BF16), ffn_w_up[i].astype(BF16), ffn_conv_w[i].astype(F32),
                      _row(ffn_conv_b[i]), ffn_w_down[i].astype(BF16), _row(norm_ple[i]),
                      ple_w_gate[i].astype(BF16), ple_w_proj[i].astype(BF16))
    return x2.reshape(batch, seq, d)
```

```python
import functools

import jax
import jax.numpy as jnp
from jax import lax
from jax.experimental import pallas as pl
from jax.experimental.pallas import tpu as pltpu

F32 = jnp.float32
BF16 = jnp.bfloat16
HIGHEST = lax.Precision.HIGHEST

EPS = 1e-6
NEG = -1e30
LOG2E = 1.4426950408889634
LANES = 128
CONV_HALO = 8

MOBA_HEADS = 8
MOBA_HEAD_DIM = 64
MOBA_BLOCK = 256
MOBA_TOPK = 3
MOBA_HEADS_PER_STEP = 4
ALIBI_MAX_BIAS = 8.0

SSM_HEADS = 16
SSM_HEAD_DIM = 64
SSM_STATE = 128
SSM_GROUPS = 2
SSM_CONV = 4
SSM_CHUNK = 128

GDN_K_HEADS = 8
GDN_V_HEADS = 16
GDN_DIM = 128
GDN_CONV = 4
GDN_CHUNK = 64
GDN_SEQS_PER_STEP = 2

FFN_CONV = 3
FFN_HALO = 16
FFN_COLS = 256

VMEM_LIMIT = 56 * 1024 * 1024


def _rms_rows(x, gain):
    return x * lax.rsqrt(jnp.mean(x * x, axis=-1, keepdims=True) + EPS) * gain


def _sigmoid(x):
    return 0.5 * jnp.tanh(0.5 * x) + 0.5


def _silu(x):
    return x * _sigmoid(x)


def _softplus(x):
    return jnp.maximum(x, 0.0) + jnp.log(1.0 + jnp.exp(-jnp.abs(x)))


def _dot(a, b):
    return jnp.dot(a, b, preferred_element_type=F32)


def _dot_nt(a, b):
    return lax.dot_general(a, b, (((1,), (1,)), ((), ())), preferred_element_type=F32)


def _bf16_terms(x):
    hi = x.astype(BF16)
    rest = x - hi.astype(F32)
    mid = rest.astype(BF16)
    lo = (rest - mid.astype(F32)).astype(BF16)
    return [hi, mid, lo]


def _select_rows_dot(sel, x):
    return _dot(jnp.concatenate([sel] * 3, axis=1), jnp.concatenate(_bf16_terms(x), axis=0))


def _select_cols_dot(x, sel):
    return _dot(jnp.concatenate(_bf16_terms(x), axis=1), jnp.concatenate([sel] * 3, axis=0))


def _head_sums(x, head_w):
    rows, width = x.shape
    slab = 2 * LANES
    n = width // slab
    same = (lax.broadcasted_iota(jnp.int32, (slab, slab), 0) // head_w
            == lax.broadcasted_iota(jnp.int32, (slab, slab), 1) // head_w)
    ones = jnp.where(same, 1.0, 0.0).astype(BF16)
    stacked = jnp.concatenate([x[:, i * slab:(i + 1) * slab] for i in range(n)], axis=0)
    sums = _dot(stacked.astype(BF16), ones)
    return jnp.concatenate([sums[i * rows:(i + 1) * rows] for i in range(n)], axis=1)


def _const_spec(shape):
    zeros = (0,) * len(shape)
    return pl.BlockSpec(shape, lambda *_: zeros, pipeline_mode=pl.Buffered(1))


def _params(semantics):
    return pltpu.CompilerParams(dimension_semantics=semantics, vmem_limit_bytes=VMEM_LIMIT)


def _causal_conv(ext, w_ref, width, halo):
    y = w_ref[width - 1:width, :] * ext[halo:]
    for shift in range(1, width):
        y = y + w_ref[width - 1 - shift:width - shift, :] * pltpu.roll(ext, shift, axis=0)[halo:]
    return y


def _inproj_even_kernel(x_ref, g_ref, wq_ref, wk_ref, wv_ref, wz_ref, wxbc_ref, wdt_ref,
                        q_ref, k_ref, v_ref, z_ref, xbc_ref, dt_ref):
    hn = _rms_rows(x_ref[...], g_ref[...]).astype(BF16)
    q_ref[...] = _dot(hn, wq_ref[...]).astype(BF16)
    k_ref[...] = _dot(hn, wk_ref[...]).astype(BF16)
    v_ref[...] = _dot(hn, wv_ref[...]).astype(BF16)
    z_ref[...] = _dot(hn, wz_ref[...]).astype(BF16)
    xbc_ref[...] = _dot(hn, wxbc_ref[...]).astype(BF16)
    dt_ref[...] = _dot(hn, wdt_ref[...])


def _inproj_even(x2, gain, wq, wk, wv, wz, wxbc, wdt, tm=512):
    t, d = x2.shape
    outs = [(wq.shape[1], BF16), (wk.shape[1], BF16), (wv.shape[1], BF16),
            (wz.shape[1], BF16), (wxbc.shape[1], BF16), (wdt.shape[1], F32)]
    return pl.pallas_call(
        _inproj_even_kernel,
        grid=(t // tm,),
        in_specs=[pl.BlockSpec((tm, d), lambda i: (i, 0)), _const_spec(gain.shape)]
        + [_const_spec(w.shape) for w in (wq, wk, wv, wz, wxbc, wdt)],
        out_specs=[pl.BlockSpec((tm, n), lambda i: (i, 0)) for n, _ in outs],
        out_shape=[jax.ShapeDtypeStruct((t, n), dt) for n, dt in outs],
        compiler_params=_params(("parallel",)),
        name="inproj_even",
    )(x2, gain, wq, wk, wv, wz, wxbc, wdt)


def _moba_kernel(slopes_ref, q_ref, k_ref, v_ref, qg_ref, kg_ref, o_ref, kn_s, kbar_s, vt_s):
    hp = pl.program_id(1)
    seq = k_ref.shape[1]
    nb = seq // MOBA_BLOCK
    dh = MOBA_HEAD_DIM
    hps = MOBA_HEADS_PER_STEP
    heads = range(hps)

    lanes = hps * dh
    same_head = (lax.broadcasted_iota(jnp.int32, (lanes, lanes), 0) // dh
                 == lax.broadcasted_iota(jnp.int32, (lanes, lanes), 1) // dh)
    head_ones = jnp.where(same_head, 1.0, 0.0).astype(BF16)

    def head_rms(x, gain):
        ms = _dot((x * x).astype(BF16), head_ones) * (1.0 / dh)
        return x * lax.rsqrt(ms + EPS) * gain

    kn = head_rms(k_ref[0].astype(F32), kg_ref[...])
    kn_s[...] = kn.astype(BF16)
    kbar_s[...] = jnp.mean(kn.reshape(nb, MOBA_BLOCK, lanes), axis=1)
    vt_s[...] = v_ref[0].astype(F32).T.astype(BF16)

    lane_head = lax.broadcasted_iota(jnp.int32, (MOBA_BLOCK, lanes), 1) // dh
    blk = lax.broadcasted_iota(jnp.int32, (nb, MOBA_BLOCK), 0)
    kbar_head = lax.broadcasted_iota(jnp.int32, (nb, lanes), 1) // dh
    kbar = kbar_s[...]
    kbar_rows = jnp.concatenate([jnp.where(kbar_head == hh, kbar, 0.0) for hh in heads], axis=0)
    kidx = lax.broadcasted_iota(jnp.int32, (MOBA_BLOCK, MOBA_BLOCK), 0)
    qidx = lax.broadcasted_iota(jnp.int32, (MOBA_BLOCK, MOBA_BLOCK), 1)
    rel = (kidx - qidx).astype(F32)
    causal = rel <= 0.0
    slopes2 = [slopes_ref[hp * hps + hh] * LOG2E for hh in heads]
    alibis = [slope2 * rel for slope2 in slopes2]

    def score_stage(own):
        rows = slice(own * MOBA_BLOCK, (own + 1) * MOBA_BLOCK)
        qn = head_rms(q_ref[0, rows, :].astype(F32), qg_ref[...])
        qs = [(jnp.where(lane_head == hh, qn, 0.0) * (dh ** -0.5 * LOG2E)).astype(BF16) for hh in heads]
        scores = {(hh, j): _dot_nt(kn_s[j * MOBA_BLOCK:(j + 1) * MOBA_BLOCK, :], qs[hh])
                  for hh in heads for j in range(own + 1)}
        gates = lax.dot_general(kbar_rows, qn, (((1,), (1,)), ((), ())),
                                precision=HIGHEST, preferred_element_type=F32)
        selb = []
        for hh in heads:
            gate = gates[hh * nb:(hh + 1) * nb, :]
            rank = jnp.zeros(gate.shape, F32)
            for m in range(own):
                gm = gate[m:m + 1, :]
                beats = (gm > gate) | ((gm == gate) & (m < blk))
                rank = rank + jnp.where(beats, 1.0, 0.0)
            selected = (blk < own) & (rank < float(min(MOBA_TOPK, nb - 1)))
            selb.append(jnp.where(selected, 0.0, NEG))
        return scores, selb

    def softmax_pv_stage(own, scores, selb):
        rows = slice(own * MOBA_BLOCK, (own + 1) * MOBA_BLOCK)
        probs, denoms = [], []
        for hh in heads:
            tiles = [scores[hh, j] + alibis[hh] for j in range(own)]
            tiles.append(jnp.where(causal, scores[hh, own] + alibis[hh], NEG))
            col_bias = [selb[hh][j:j + 1, :] + slopes2[hh] * float((j - own) * MOBA_BLOCK) for j in range(own)]
            col_bias.append(jnp.zeros((1, MOBA_BLOCK), F32))
            m = jnp.max(tiles[0], axis=0, keepdims=True) + col_bias[0]
            for t, cb in zip(tiles[1:], col_bias[1:]):
                m = jnp.maximum(m, jnp.max(t, axis=0, keepdims=True) + cb)
            ps = [jnp.exp2(t + (cb - m)) for t, cb in zip(tiles, col_bias)]
            denom = jnp.sum(ps[0], axis=0, keepdims=True)
            for p in ps[1:]:
                denom = denom + jnp.sum(p, axis=0, keepdims=True)
            probs.append(jnp.concatenate([p.astype(BF16) for p in ps], axis=0))
            denoms.append(denom)
        keys = (own + 1) * MOBA_BLOCK
        outs = [_dot(vt_s[hh * dh:(hh + 1) * dh, 0:keys], probs[hh]) / denoms[hh] for hh in heads]
        o_ref[0, rows, :] = jnp.concatenate(outs, axis=0).T.astype(BF16)

    staged = score_stage(0)
    for own in range(nb):
        upcoming = score_stage(own + 1) if own + 1 < nb else None
        softmax_pv_stage(own, *staged)
        staged = upcoming


def _moba(q, k, v, q_gain, k_gain, slopes):
    b, s, width = q.shape
    dh = MOBA_HEAD_DIM
    hps = MOBA_HEADS_PER_STEP
    lanes = hps * dh
    nb = s // MOBA_BLOCK
    seq_spec = pl.BlockSpec((1, s, lanes), lambda bi, hi: (bi, 0, hi))
    return pl.pallas_call(
        _moba_kernel,
        grid=(b, width // lanes),
        in_specs=[
            pl.BlockSpec(memory_space=pltpu.SMEM),
            seq_spec, seq_spec, seq_spec,
            pl.BlockSpec((1, lanes), lambda bi, hi: (0, 0)),
            pl.BlockSpec((1, lanes), lambda bi, hi: (0, 0)),
        ],
        out_specs=seq_spec,
        out_shape=jax.ShapeDtypeStruct((b, s, width), BF16),
        scratch_shapes=[
            pltpu.VMEM((s, lanes), BF16),
            pltpu.VMEM((nb, lanes), F32),
            pltpu.VMEM((lanes, s), BF16),
        ],
        compiler_params=_params(("parallel", "parallel")),
        name="moba_attention",
    )(slopes, q, k, v, q_gain, k_gain)


def _ssd_kernel(xbc_ref, z_ref, dt_ref, cw_ref, cb_ref, dtb_ref, alog_ref, dskip_ref, nw_ref,
                out_ref, hist_s, state_s):
    c = pl.program_id(1)
    L = SSM_CHUNK
    inner = SSM_HEADS * SSM_HEAD_DIM
    gw = SSM_STATE
    pair_w = 2 * SSM_HEAD_DIM

    @pl.when(c == 0)
    def _():
        hist_s[...] = jnp.zeros_like(hist_s)
        state_s[...] = jnp.zeros_like(state_s)

    cur = xbc_ref[0].astype(F32)
    ext = jnp.concatenate([hist_s[...], cur], axis=0)
    hist_s[...] = cur[L - CONV_HALO:, :]
    act = _silu(_causal_conv(ext, cw_ref, SSM_CONV, CONV_HALO) + cb_ref[...])
    xs = act[:, :inner]

    dt = _softplus(dt_ref[0] + dtb_ref[...])
    a = dt * (-jnp.exp(alog_ref[...]))
    row = lax.broadcasted_iota(jnp.int32, (L, L), 0)
    col = lax.broadcasted_iota(jnp.int32, (L, L), 1)
    causal = row >= col
    acum = _select_rows_dot(jnp.where(causal, 1.0, 0.0).astype(BF16), a)
    acum_t = acum.T

    hrow = lax.broadcasted_iota(jnp.int32, (LANES, inner), 0)
    hcol = lax.broadcasted_iota(jnp.int32, (LANES, inner), 1)
    widen = jnp.where(hcol // SSM_HEAD_DIM == hrow, 1.0, 0.0).astype(BF16)
    widened = _select_cols_dot(jnp.concatenate([dt, acum], axis=0), widen)
    dt_w = widened[:L]
    acum_w = widened[L:]
    last_w = acum_w[L - 1:L, :]
    xdt = xs * dt_w
    xdt_bf = xdt.astype(BF16)
    xdt_dec_bf = (xdt * jnp.exp(last_w - acum_w)).astype(BF16)
    exp_acum_w = jnp.exp(acum_w)
    chunk_decay_w = jnp.exp(last_w)
    lane = lax.broadcasted_iota(jnp.int32, (L, pair_w), 1)
    first_head = lane < SSM_HEAD_DIM

    heads_per_group = SSM_HEADS // SSM_GROUPS
    ys = []
    for g in range(SSM_GROUPS):
        bg = act[:, inner + g * gw:inner + (g + 1) * gw]
        cg = act[:, inner + SSM_GROUPS * gw + g * gw:inner + SSM_GROUPS * gw + (g + 1) * gw]
        cg_bf = cg.astype(BF16)
        cb = _dot_nt(cg_bf, bg.astype(BF16))
        bg_t = bg.T.astype(BF16)
        for pp in range(heads_per_group // 2):
            h0 = g * heads_per_group + 2 * pp
            pi = h0 // 2
            sl = slice(h0 * SSM_HEAD_DIM, h0 * SSM_HEAD_DIM + pair_w)
            yd = []
            for hh in (h0, h0 + 1):
                diff = jnp.broadcast_to(acum[:, hh:hh + 1], (L, L)) - acum_t[hh:hh + 1, :]
                decay = jnp.exp(jnp.where(causal, diff, NEG))
                yd.append(_dot((cb * decay).astype(BF16), xdt_bf[:, sl]))
            y_diag = jnp.where(first_head, yd[0], yd[1])
            prev = state_s[pi]
            y_off = _dot(cg_bf, prev.astype(BF16)) * exp_acum_w[:, sl]
            state_s[pi] = prev * chunk_decay_w[:, sl] + _dot(bg_t, xdt_dec_bf[:, sl])
            ys.append(y_diag + y_off + dskip_ref[:, sl] * xs[:, sl])
    y = jnp.concatenate(ys, axis=1)

    zf = z_ref[0].astype(F32)
    yg = y * _silu(zf)
    gsz = inner // SSM_GROUPS
    parts = []
    for g in range(SSM_GROUPS):
        seg = yg[:, g * gsz:(g + 1) * gsz]
        parts.append(seg * lax.rsqrt(jnp.mean(seg * seg, axis=-1, keepdims=True) + EPS))
    out_ref[0] = (jnp.concatenate(parts, axis=1) * nw_ref[...]).astype(BF16)


def _ssd(xbc, z, dt, conv_w, conv_b, dt_bias, a_log, d_skip_w, norm_w):
    b, s, cd = xbc.shape
    inner = z.shape[2]
    nc = s // SSM_CHUNK
    return pl.pallas_call(
        _ssd_kernel,
        grid=(b, nc),
        in_specs=[
            pl.BlockSpec((1, SSM_CHUNK, cd), lambda bi, ci: (bi, ci, 0)),
            pl.BlockSpec((1, SSM_CHUNK, inner), lambda bi, ci: (bi, ci, 0)),
            pl.BlockSpec((1, SSM_CHUNK, LANES), lambda bi, ci: (bi, ci, 0)),
            _const_spec(conv_w.shape), _const_spec(conv_b.shape), _const_spec(dt_bias.shape),
            _const_spec(a_log.shape), _const_spec(d_skip_w.shape), _const_spec(norm_w.shape),
        ],
        out_specs=pl.BlockSpec((1, SSM_CHUNK, inner), lambda bi, ci: (bi, ci, 0)),
        out_shape=jax.ShapeDtypeStruct((b, s, inner), BF16),
        scratch_shapes=[
            pltpu.VMEM((CONV_HALO, cd), F32),
            pltpu.VMEM((SSM_HEADS // 2, SSM_STATE, 2 * SSM_HEAD_DIM), F32),
        ],
        compiler_params=_params(("parallel", "arbitrary")),
        name="ssd_heads",
    )(xbc, z, dt, conv_w, conv_b, dt_bias, a_log, d_skip_w, norm_w)


def _outproj_even_kernel(x_ref, a_ref, s_ref, wa_ref, ws_ref, o_ref):
    o_ref[...] = x_ref[...] + _dot(a_ref[...], wa_ref[...]) + _dot(s_ref[...], ws_ref[...])


def _outproj_even(x2, attn, ssm, wa, ws, tm=1024):
    t, d = x2.shape
    return pl.pallas_call(
        _outproj_even_kernel,
        grid=(t // tm,),
        in_specs=[pl.BlockSpec((tm, d), lambda i: (i, 0)),
                  pl.BlockSpec((tm, attn.shape[1]), lambda i: (i, 0)),
                  pl.BlockSpec((tm, ssm.shape[1]), lambda i: (i, 0)),
                  _const_spec(wa.shape), _const_spec(ws.shape)],
        out_specs=pl.BlockSpec((tm, d), lambda i: (i, 0)),
        out_shape=jax.ShapeDtypeStruct((t, d), F32),
        compiler_params=_params(("parallel",)),
        name="outproj_even",
    )(x2, attn, ssm, wa, ws)


def _outproj_odd_kernel(x_ref, o_in_ref, w_ref, o_ref):
    o_ref[...] = x_ref[...] + _dot(o_in_ref[...], w_ref[...])


def _outproj_odd(x2, o_in, w, tm=1024):
    t, d = x2.shape
    return pl.pallas_call(
        _outproj_odd_kernel,
        grid=(t // tm,),
        in_specs=[pl.BlockSpec((tm, d), lambda i: (i, 0)),
                  pl.BlockSpec((tm, o_in.shape[1]), lambda i: (i, 0)),
                  _const_spec(w.shape)],
        out_specs=pl.BlockSpec((tm, d), lambda i: (i, 0)),
        out_shape=jax.ShapeDtypeStruct((t, d), F32),
        compiler_params=_params(("parallel",)),
        name="outproj_odd",
    )(x2, o_in, w)


def _ffn_ple_kernel(x_ref, halo_ref, p_ref, gf_ref, wg_ref, wu_ref, cw_ref, cb_ref, wd_ref,
                    gp_ref, wpg_ref, wpp_ref, o_ref, *, tm, seq):
    i = pl.program_id(0)
    x = x_ref[...]
    keep_halo = jnp.where((i * tm) % seq == 0, 0.0, 1.0)
    halo = halo_ref[...] * keep_halo
    hn = _rms_rows(jnp.concatenate([halo, x], axis=0), gf_ref[...]).astype(BF16)
    hn_cur = hn[FFN_HALO:]

    d_ff = wg_ref.shape[1]
    acts = []
    for j in range(d_ff // FFN_COLS):
        cs = slice(j * FFN_COLS, (j + 1) * FFN_COLS)
        g = _dot(hn, wg_ref[:, cs])
        gc = cb_ref[:, cs] + cw_ref[FFN_CONV - 1:FFN_CONV, cs] * g[FFN_HALO:]
        for shift in range(1, FFN_CONV):
            gc = gc + cw_ref[FFN_CONV - 1 - shift:FFN_CONV - shift, cs] * pltpu.roll(g, shift, axis=0)[FFN_HALO:]
        u = _dot(hn_cur, wu_ref[:, cs])
        acts.append((_silu(gc) * u).astype(BF16))
    x1 = x + _dot(jnp.concatenate(acts, axis=1), wd_ref[...])
    hp = _rms_rows(x1, gp_ref[...]).astype(BF16)
    gate = _sigmoid(_dot(hp, wpg_ref[...]))
    o_ref[...] = x1 + gate * _dot(p_ref[...].astype(BF16), wpp_ref[...])


def _ffn_ple(x2, p2, seq, gain_ffn, wg, wu, conv_w, conv_b, wd, gain_ple, wpg, wpp, tm=1024):
    t, d = x2.shape
    halo_blocks = tm // FFN_HALO
    consts = (gain_ffn, wg, wu, conv_w, conv_b, wd, gain_ple, wpg, wpp)
    return pl.pallas_call(
        functools.partial(_ffn_ple_kernel, tm=tm, seq=seq),
        grid=(t // tm,),
        in_specs=[pl.BlockSpec((tm, d), lambda i: (i, 0)),
                  pl.BlockSpec((FFN_HALO, d), lambda i: (jnp.maximum(i * halo_blocks - 1, 0), 0)),
                  pl.BlockSpec((tm, p2.shape[1]), lambda i: (i, 0))]
        + [_const_spec(c.shape) for c in consts],
        out_specs=pl.BlockSpec((tm, d), lambda i: (i, 0)),
        out_shape=jax.ShapeDtypeStruct((t, d), F32),
        compiler_params=_params(("parallel",)),
        name="ffn_ple",
    )(x2, x2, p2, *consts)


def _inproj_odd_kernel(x_ref, g_ref, wqkv_ref, wz_ref, wba_ref, qkv_ref, z_ref, ba_ref):
    hn = _rms_rows(x_ref[...], g_ref[...]).astype(BF16)
    qkv_ref[...] = _dot(hn, wqkv_ref[...]).astype(BF16)
    z_ref[...] = _dot(hn, wz_ref[...]).astype(BF16)
    ba_ref[...] = _dot(hn, wba_ref[...])


def _inproj_odd(x2, gain, wqkv, wz, wba, tm=512):
    t, d = x2.shape
    outs = [(wqkv.shape[1], BF16), (wz.shape[1], BF16), (wba.shape[1], F32)]
    return pl.pallas_call(
        _inproj_odd_kernel,
        grid=(t // tm,),
        in_specs=[pl.BlockSpec((tm, d), lambda i: (i, 0)), _const_spec(gain.shape)]
        + [_const_spec(w.shape) for w in (wqkv, wz, wba)],
        out_specs=[pl.BlockSpec((tm, n), lambda i: (i, 0)) for n, _ in outs],
        out_shape=[jax.ShapeDtypeStruct((t, n), dt) for n, dt in outs],
        compiler_params=_params(("parallel",)),
        name="inproj_odd",
    )(x2, gain, wqkv, wz, wba)


def _bdot(a, b):
    return _dot(a.astype(BF16), b.astype(BF16))


def _unit_lower_inverses(mats):
    L = mats[0].shape[0]
    row = lax.broadcasted_iota(jnp.int32, (L, L), 0)
    col = lax.broadcasted_iota(jnp.int32, (L, L), 1)
    eye = jnp.where(row == col, 1.0, 0.0)
    base = 8
    same = (row // base) == (col // base)
    ns = [-jnp.where(same, a, 0.0) for a in mats]
    ts = [eye + n for n in ns]
    pws = [_bdot(n, n) for n in ns]
    ts = [t + _bdot(t, pw) for t, pw in zip(ts, pws)]
    pws = [_bdot(pw, pw) for pw in pws]
    ts = [t + _bdot(t, pw) for t, pw in zip(ts, pws)]
    size = base
    while size < L:
        wider = (row // (2 * size)) == (col // (2 * size))
        offs = [jnp.where(wider & jnp.logical_not(same), a, 0.0) for a in mats]
        ys = [_bdot(off, t) for off, t in zip(offs, ts)]
        ts = [t - _bdot(t, y) for t, y in zip(ts, ys)]
        same = wider
        size *= 2
    return ts


def _gdn_kernel(qkv_ref, z_ref, ba_ref, cw_ref, dtb_ref, alog_ref, nw_ref, out_ref, hist_s, state_s):
    c = pl.program_id(1)
    L = GDN_CHUNK
    dk = GDN_DIM
    qk_w = GDN_K_HEADS * dk
    rep = GDN_V_HEADS // GDN_K_HEADS
    nseq = qkv_ref.shape[0]

    @pl.when(c == 0)
    def _():
        hist_s[...] = jnp.zeros_like(hist_s)
        state_s[...] = jnp.zeros_like(state_s)

    row = lax.broadcasted_iota(jnp.int32, (L, L), 0)
    col = lax.broadcasted_iota(jnp.int32, (L, L), 1)
    lower = row >= col
    strict = row > col

    items = [(sq, hv) for sq in range(nseq) for hv in range(GDN_V_HEADS)]
    acts, qs, ks, kts, kk, qk0 = {}, {}, {}, {}, {}, {}
    beta_c, gc_c, gc_r = {}, {}, {}
    for sq in range(nseq):
        cur = qkv_ref[sq].astype(F32)
        ext = jnp.concatenate([hist_s[sq], cur], axis=0)
        hist_s[sq] = cur[L - CONV_HALO:, :]
        act = _silu(_causal_conv(ext, cw_ref, GDN_CONV, CONV_HALO))
        acts[sq] = act

        ba = ba_ref[sq]
        beta = _sigmoid(ba)
        g = -jnp.exp(alog_ref[...]) * _softplus(ba + dtb_ref[...])
        gc = _select_rows_dot(jnp.where(lower, 1.0, 0.0).astype(BF16), g)
        gc_t = gc.T
        for hv in range(GDN_V_HEADS):
            beta_c[sq, hv] = beta[:, hv:hv + 1]
            gc_c[sq, hv] = gc[:, GDN_V_HEADS + hv:GDN_V_HEADS + hv + 1]
            gc_r[sq, hv] = gc_t[GDN_V_HEADS + hv:GDN_V_HEADS + hv + 1, :]

        qk_act = act[:, :2 * qk_w]
        qk_n = qk_act * lax.rsqrt(_head_sums(qk_act * qk_act, dk) + EPS)
        for kh in range(GDN_K_HEADS):
            qn = qk_n[:, kh * dk:(kh + 1) * dk] * (dk ** -0.5)
            kn = qk_n[:, qk_w + kh * dk:qk_w + (kh + 1) * dk]
            kt = kn.T
            prod = _bdot(jnp.concatenate([kn, qn], axis=0), kt)
            qs[sq, kh], ks[sq, kh], kts[sq, kh] = qn, kn, kt
            kk[sq, kh], qk0[sq, kh] = prod[:L], prod[L:]

    def khead(it):
        return it[0], it[1] // rep

    g_last = {it: gc_r[it][:, L - 1:L] for it in items}
    egc = {it: jnp.exp(gc_c[it]) for it in items}
    decay = {it: jnp.exp(jnp.where(lower, gc_c[it] - gc_r[it], NEG)) for it in items}
    a_mats = [jnp.where(strict, kk[khead(it)] * beta_c[it] * decay[it], 0.0) for it in items]
    t_inv = dict(zip(items, _unit_lower_inverses(a_mats)))

    sols = {}
    for it in items:
        sq, hv = it
        v_h = acts[sq][:, 2 * qk_w + hv * dk:2 * qk_w + (hv + 1) * dk]
        kbeta = ks[khead(it)] * beta_c[it]
        sols[it] = _bdot(t_inv[it], jnp.concatenate([v_h * beta_c[it], kbeta * egc[it]], axis=1))

    states = {it: state_s[it[0], it[1]] for it in items}
    from_state = {it: _bdot(jnp.concatenate([sols[it][:, dk:], qs[khead(it)] * egc[it]], axis=0), states[it])
                  for it in items}
    v_new = {it: sols[it][:, :dk] - from_state[it][:L] for it in items}
    from_v = {it: _bdot(jnp.concatenate([qk0[khead(it)] * decay[it],
                                         kts[khead(it)] * jnp.exp(g_last[it] - gc_r[it])], axis=0), v_new[it])
              for it in items}
    for it in items:
        state_s[it[0], it[1]] = states[it] * jnp.exp(g_last[it]) + from_v[it][L:]
    for sq in range(nseq):
        o = jnp.concatenate([from_state[sq, hv][L:] + from_v[sq, hv][:L] for hv in range(GDN_V_HEADS)],
                            axis=1)
        o = o * lax.rsqrt(_head_sums(o * o, dk) * (1.0 / dk) + EPS) * nw_ref[...]
        out_ref[sq] = (o * _silu(z_ref[sq].astype(F32))).astype(BF16)


def _gdn(qkv, z, ba, conv_w, dt_bias_pad, a_log_pad, norm_w):
    b, s, cd = qkv.shape
    vw = z.shape[2]
    nc = s // GDN_CHUNK
    nseq = GDN_SEQS_PER_STEP
    return pl.pallas_call(
        _gdn_kernel,
        grid=(b // nseq, nc),
        in_specs=[
            pl.BlockSpec((nseq, GDN_CHUNK, cd), lambda bi, ci: (bi, ci, 0)),
            pl.BlockSpec((nseq, GDN_CHUNK, vw), lambda bi, ci: (bi, ci, 0)),
            pl.BlockSpec((nseq, GDN_CHUNK, LANES), lambda bi, ci: (bi, ci, 0)),
            _const_spec(conv_w.shape), _const_spec(dt_bias_pad.shape),
            _const_spec(a_log_pad.shape), _const_spec(norm_w.shape),
        ],
        out_specs=pl.BlockSpec((nseq, GDN_CHUNK, vw), lambda bi, ci: (bi, ci, 0)),
        out_shape=jax.ShapeDtypeStruct((b, s, vw), BF16),
        scratch_shapes=[
            pltpu.VMEM((nseq, CONV_HALO, cd), F32),
            pltpu.VMEM((nseq, GDN_V_HEADS, GDN_DIM, GDN_DIM), F32),
        ],
        compiler_params=_params(("parallel", "arbitrary")),
        name="gated_deltanet",
    )(qkv, z, ba, conv_w, dt_bias_pad, a_log_pad, norm_w)


def _pad_lanes(v, offset=0):
    out = jnp.zeros((1, LANES), F32)
    return out.at[0, offset:offset + v.shape[0]].set(v.astype(F32))


def _row(v):
    return v.astype(F32).reshape(1, -1)


def _even_mixer(x2, batch, seq, norm_w, w_in, q_norm, k_norm, conv_w, conv_b, dt_bias, a_log, d_skip,
                ssm_norm, w_out, slopes):
    mw = MOBA_HEADS * MOBA_HEAD_DIM
    inner = SSM_HEADS * SSM_HEAD_DIM
    conv_dim = inner + 2 * SSM_GROUPS * SSM_STATE
    w = w_in.astype(BF16)
    c0, c1, c2, c3, c4 = mw, 2 * mw, 3 * mw, 3 * mw + inner, 3 * mw + inner + conv_dim
    wdt = jnp.pad(w[:, c4:], ((0, 0), (0, LANES - SSM_HEADS)))
    q, k, v, z, xbc, dt = _inproj_even(x2, _row(norm_w), w[:, :c0], w[:, c0:c1], w[:, c1:c2],
                                       w[:, c2:c3], w[:, c3:c4], wdt)
    attn = _moba(q.reshape(batch, seq, mw), k.reshape(batch, seq, mw), v.reshape(batch, seq, mw),
                 _row(jnp.tile(q_norm, MOBA_HEADS_PER_STEP)), _row(jnp.tile(k_norm, MOBA_HEADS_PER_STEP)),
                 slopes).reshape(batch * seq, mw)

    ssm = _ssd(xbc.reshape(batch, seq, conv_dim), z.reshape(batch, seq, inner),
               dt.reshape(batch, seq, LANES), conv_w.astype(F32), _row(conv_b),
               _pad_lanes(dt_bias), _pad_lanes(a_log), _row(jnp.repeat(d_skip, SSM_HEAD_DIM)),
               _row(ssm_norm))
    wo = w_out.astype(BF16)
    return _outproj_even(x2, attn, ssm.reshape(batch * seq, inner), wo[:mw], wo[mw:])


def _gdn_mixer(x2, batch, seq, norm_w, w_in, conv_w, dt_bias, a_log, gdn_norm, w_out):
    conv_dim = 2 * GDN_K_HEADS * GDN_DIM + GDN_V_HEADS * GDN_DIM
    vw = GDN_V_HEADS * GDN_DIM
    w = w_in.astype(BF16)
    wba = jnp.pad(w[:, conv_dim + vw:], ((0, 0), (0, LANES - 2 * GDN_V_HEADS)))
    qkv, z, ba = _inproj_odd(x2, _row(norm_w), w[:, :conv_dim], w[:, conv_dim:conv_dim + vw], wba)
    o = _gdn(qkv.reshape(batch, seq, conv_dim), z.reshape(batch, seq, vw), ba.reshape(batch, seq, LANES),
             conv_w.astype(F32), _pad_lanes(dt_bias, GDN_V_HEADS), _pad_lanes(a_log, GDN_V_HEADS),
             _row(jnp.tile(gdn_norm, GDN_V_HEADS)))
    return _outproj_odd(x2, o.reshape(batch * seq, vw), w_out.astype(BF16))


def kernel(x, p, norm_mix, norm_ffn, norm_ple, w_in_even, moba_q_norm, moba_k_norm, ssm_conv_w, ssm_conv_b, ssm_dt_bias, ssm_a_log, ssm_d, ssm_norm, w_out_even, w_in_odd, gdn_conv_w, gdn_dt_bias, gdn_a_log, gdn_norm, w_out_odd, ffn_w_gate, ffn_w_up, ffn_conv_w, ffn_conv_b, ffn_w_down, ple_w_proj, ple_w_gate):
    batch, seq, d = x.shape
    depth = p.shape[0]
    slopes = jnp.exp2(-ALIBI_MAX_BIAS * jnp.arange(1, MOBA_HEADS + 1, dtype=F32) / MOBA_HEADS)
    x2 = x.reshape(batch * seq, d)
    for i in range(depth):
        j = i // 2
        if i % 2 == 0:
            x2 = _even_mixer(x2, batch, seq, norm_mix[i], w_in_even[j], moba_q_norm[j], moba_k_norm[j],
                             ssm_conv_w[j], ssm_conv_b[j], ssm_dt_bias[j], ssm_a_log[j], ssm_d[j],
                             ssm_norm[j], w_out_even[j], slopes)
        else:
            x2 = _gdn_mixer(x2, batch, seq, norm_mix[i], w_in_odd[j], gdn_conv_w[j], gdn_dt_bias[j],
                            gdn_a_log[j], gdn_norm[j], w_out_odd[j])
        x2 = _ffn_ple(x2, p[i].reshape(batch * seq, -1), seq, _row(norm_ffn[i]),
                      ffn_w_gate[i].astype(BF16), ffn_w_up[i].astype(BF16), ffn_conv_w[i].astype(F32),
                      _row(ffn_conv_b[i]), ffn_w_down[i].astype(BF16), _row(norm_ple[i]),
                      ple_w_gate[i].astype(BF16), ple_w_proj[i].astype(BF16))
    return x2.reshape(batch, seq, d)
```

```python
import functools

import jax
import jax.numpy as jnp
from jax import lax
from jax.experimental import pallas as pl
from jax.experimental.pallas import tpu as pltpu

F32 = jnp.float32
BF16 = jnp.bfloat16
HIGHEST = lax.Precision.HIGHEST

EPS = 1e-6
NEG = -1e30
LOG2E = 1.4426950408889634
LANES = 128
CONV_HALO = 8

MOBA_HEADS = 8
MOBA_HEAD_DIM = 64
MOBA_BLOCK = 256
MOBA_TOPK = 3
MOBA_HEADS_PER_STEP = 4
ALIBI_MAX_BIAS = 8.0

SSM_HEADS = 16
SSM_HEAD_DIM = 64
SSM_STATE = 128
SSM_GROUPS = 2
SSM_CONV = 4
SSM_CHUNK = 128

GDN_K_HEADS = 8
GDN_V_HEADS = 16
GDN_DIM = 128
GDN_CONV = 4
GDN_CHUNK = 64
GDN_SEQS_PER_STEP = 2
GDN_CHAINS_PER_GROUP = 32

FFN_CONV = 3
FFN_HALO = 16
FFN_COLS = 256

VMEM_LIMIT = 56 * 1024 * 1024


def _rms_rows(x, gain):
    return x * lax.rsqrt(jnp.mean(x * x, axis=-1, keepdims=True) + EPS) * gain


def _sigmoid(x):
    return 0.5 * jnp.tanh(0.5 * x) + 0.5


def _silu(x):
    half = 0.5 * x
    return half + half * jnp.tanh(half)


def _softplus(x):
    return jnp.maximum(x, 0.0) + jnp.log(1.0 + jnp.exp(-jnp.abs(x)))


def _dot(a, b):
    return jnp.dot(a, b, preferred_element_type=F32)


def _dot_nt(a, b):
    return lax.dot_general(a, b, (((1,), (1,)), ((), ())), preferred_element_type=F32)


def _bf16_terms(x):
    hi = x.astype(BF16)
    rest = x - hi.astype(F32)
    mid = rest.astype(BF16)
    lo = (rest - mid.astype(F32)).astype(BF16)
    return [hi, mid, lo]


def _select_rows_dot(sel, x):
    return _dot(jnp.concatenate([sel] * 3, axis=1), jnp.concatenate(_bf16_terms(x), axis=0))


def _select_cols_dot(x, sel):
    return _dot(jnp.concatenate(_bf16_terms(x), axis=1), jnp.concatenate([sel] * 3, axis=0))


def _head_sums(x, head_w):
    rows, width = x.shape
    slab = 2 * LANES
    n = width // slab
    same = (lax.broadcasted_iota(jnp.int32, (slab, slab), 0) // head_w
            == lax.broadcasted_iota(jnp.int32, (slab, slab), 1) // head_w)
    ones = jnp.where(same, 1.0, 0.0).astype(BF16)
    stacked = jnp.concatenate([x[:, i * slab:(i + 1) * slab] for i in range(n)], axis=0)
    sums = _dot(stacked.astype(BF16), ones)
    return jnp.concatenate([sums[i * rows:(i + 1) * rows] for i in range(n)], axis=1)


def _const_spec(shape):
    zeros = (0,) * len(shape)
    return pl.BlockSpec(shape, lambda *_: zeros, pipeline_mode=pl.Buffered(1))


def _params(semantics):
    return pltpu.CompilerParams(dimension_semantics=semantics, vmem_limit_bytes=VMEM_LIMIT)


def _causal_conv(ext, w_ref, width, halo):
    y = w_ref[width - 1:width, :] * ext[halo:]
    for shift in range(1, width):
        y = y + w_ref[width - 1 - shift:width - shift, :] * pltpu.roll(ext, shift, axis=0)[halo:]
    return y


def _inproj_even_kernel(x_ref, g_ref, wq_ref, wk_ref, wv_ref, wz_ref, wxbc_ref, wdt_ref,
                        q_ref, k_ref, v_ref, z_ref, xbc_ref, dt_ref):
    hn = _rms_rows(x_ref[...], g_ref[...]).astype(BF16)
    q_ref[...] = _dot(hn, wq_ref[...]).astype(BF16)
    k_ref[...] = _dot(hn, wk_ref[...]).astype(BF16)
    v_ref[...] = _dot(hn, wv_ref[...]).astype(BF16)
    z_ref[...] = _dot(hn, wz_ref[...]).astype(BF16)
    xbc_ref[...] = _dot(hn, wxbc_ref[...]).astype(BF16)
    dt_ref[...] = _dot(hn, wdt_ref[...])


def _inproj_even(x2, gain, wq, wk, wv, wz, wxbc, wdt, tm=512):
    t, d = x2.shape
    outs = [(wq.shape[1], BF16), (wk.shape[1], BF16), (wv.shape[1], BF16),
            (wz.shape[1], BF16), (wxbc.shape[1], BF16), (wdt.shape[1], F32)]
    return pl.pallas_call(
        _inproj_even_kernel,
        grid=(t // tm,),
        in_specs=[pl.BlockSpec((tm, d), lambda i: (i, 0)), _const_spec(gain.shape)]
        + [_const_spec(w.shape) for w in (wq, wk, wv, wz, wxbc, wdt)],
        out_specs=[pl.BlockSpec((tm, n), lambda i: (i, 0)) for n, _ in outs],
        out_shape=[jax.ShapeDtypeStruct((t, n), dt) for n, dt in outs],
        compiler_params=_params(("parallel",)),
        name="inproj_even",
    )(x2, gain, wq, wk, wv, wz, wxbc, wdt)


def _moba_kernel(slopes_ref, q_ref, k_ref, v_ref, qg_ref, kg_ref, o_ref, kn_s, kbar_s, vt_s):
    hp = pl.program_id(1)
    seq = k_ref.shape[1]
    nb = seq // MOBA_BLOCK
    dh = MOBA_HEAD_DIM
    hps = MOBA_HEADS_PER_STEP
    heads = range(hps)

    lanes = hps * dh
    same_head = (lax.broadcasted_iota(jnp.int32, (lanes, lanes), 0) // dh
                 == lax.broadcasted_iota(jnp.int32, (lanes, lanes), 1) // dh)
    head_ones = jnp.where(same_head, 1.0, 0.0).astype(BF16)

    def head_rms(x, gain):
        ms = _dot((x * x).astype(BF16), head_ones) * (1.0 / dh)
        return x * lax.rsqrt(ms + EPS) * gain

    kn = head_rms(k_ref[0].astype(F32), kg_ref[...])
    kn_s[...] = kn.astype(BF16)
    kbar_s[...] = jnp.mean(kn.reshape(nb, MOBA_BLOCK, lanes), axis=1)
    vt_s[...] = v_ref[0].astype(F32).T.astype(BF16)

    lane_head = lax.broadcasted_iota(jnp.int32, (MOBA_BLOCK, lanes), 1) // dh
    blk = lax.broadcasted_iota(jnp.int32, (nb, MOBA_BLOCK), 0)
    kbar_head = lax.broadcasted_iota(jnp.int32, (nb, lanes), 1) // dh
    kbar = kbar_s[...]
    kbar_rows = jnp.concatenate([jnp.where(kbar_head == hh, kbar, 0.0) for hh in heads], axis=0)
    kidx = lax.broadcasted_iota(jnp.int32, (MOBA_BLOCK, MOBA_BLOCK), 0)
    qidx = lax.broadcasted_iota(jnp.int32, (MOBA_BLOCK, MOBA_BLOCK), 1)
    rel = (kidx - qidx).astype(F32)
    causal = rel <= 0.0
    slopes2 = [slopes_ref[hp * hps + hh] * LOG2E for hh in heads]
    alibis = [slope2 * rel for slope2 in slopes2]

    def score_stage(own):
        rows = slice(own * MOBA_BLOCK, (own + 1) * MOBA_BLOCK)
        qn = head_rms(q_ref[0, rows, :].astype(F32), qg_ref[...])
        qs = [(jnp.where(lane_head == hh, qn, 0.0) * (dh ** -0.5 * LOG2E)).astype(BF16) for hh in heads]
        scores = {(hh, j): _dot_nt(kn_s[j * MOBA_BLOCK:(j + 1) * MOBA_BLOCK, :], qs[hh])
                  for hh in heads for j in range(own + 1)}
        gates = lax.dot_general(kbar_rows, qn, (((1,), (1,)), ((), ())),
                                precision=HIGHEST, preferred_element_type=F32)
        selb = []
        for hh in heads:
            gate = gates[hh * nb:(hh + 1) * nb, :]
            rank = jnp.zeros(gate.shape, F32)
            for m in range(own):
                gm = gate[m:m + 1, :]
                beats = (gm > gate) | ((gm == gate) & (m < blk))
                rank = rank + jnp.where(beats, 1.0, 0.0)
            selected = (blk < own) & (rank < float(min(MOBA_TOPK, nb - 1)))
            selb.append(jnp.where(selected, 0.0, NEG))
        return scores, selb

    def softmax_pv_stage(own, scores, selb):
        rows = slice(own * MOBA_BLOCK, (own + 1) * MOBA_BLOCK)
        probs, denoms = [], []
        for hh in heads:
            tiles = [scores[hh, j] + alibis[hh] for j in range(own)]
            tiles.append(jnp.where(causal, scores[hh, own] + alibis[hh], NEG))
            col_bias = [selb[hh][j:j + 1, :] + slopes2[hh] * float((j - own) * MOBA_BLOCK) for j in range(own)]
            col_bias.append(jnp.zeros((1, MOBA_BLOCK), F32))
            m = jnp.max(tiles[0], axis=0, keepdims=True) + col_bias[0]
            for t, cb in zip(tiles[1:], col_bias[1:]):
                m = jnp.maximum(m, jnp.max(t, axis=0, keepdims=True) + cb)
            ps = [jnp.exp2(t + (cb - m)) for t, cb in zip(tiles, col_bias)]
            denom = jnp.sum(ps[0], axis=0, keepdims=True)
            for p in ps[1:]:
                denom = denom + jnp.sum(p, axis=0, keepdims=True)
            probs.append(jnp.concatenate([p.astype(BF16) for p in ps], axis=0))
            denoms.append(denom)
        keys = (own + 1) * MOBA_BLOCK
        outs = [_dot(vt_s[hh * dh:(hh + 1) * dh, 0:keys], probs[hh]) / denoms[hh] for hh in heads]
        o_ref[0, rows, :] = jnp.concatenate(outs, axis=0).T.astype(BF16)

    staged = score_stage(0)
    for own in range(nb):
        upcoming = score_stage(own + 1) if own + 1 < nb else None
        softmax_pv_stage(own, *staged)
        staged = upcoming


def _moba(q, k, v, q_gain, k_gain, slopes):
    b, s, width = q.shape
    dh = MOBA_HEAD_DIM
    hps = MOBA_HEADS_PER_STEP
    lanes = hps * dh
    nb = s // MOBA_BLOCK
    seq_spec = pl.BlockSpec((1, s, lanes), lambda bi, hi: (bi, 0, hi))
    return pl.pallas_call(
        _moba_kernel,
        grid=(b, width // lanes),
        in_specs=[
            pl.BlockSpec(memory_space=pltpu.SMEM),
            seq_spec, seq_spec, seq_spec,
            pl.BlockSpec((1, lanes), lambda bi, hi: (0, 0)),
            pl.BlockSpec((1, lanes), lambda bi, hi: (0, 0)),
        ],
        out_specs=seq_spec,
        out_shape=jax.ShapeDtypeStruct((b, s, width), BF16),
        scratch_shapes=[
            pltpu.VMEM((s, lanes), BF16),
            pltpu.VMEM((nb, lanes), F32),
            pltpu.VMEM((lanes, s), BF16),
        ],
        compiler_params=_params(("parallel", "parallel")),
        name="moba_attention",
    )(slopes, q, k, v, q_gain, k_gain)


def _ssd_kernel(xbc_ref, z_ref, dt_ref, cw_ref, cb_ref, dtb_ref, alog_ref, dskip_ref, nw_ref,
                out_ref, hist_s, state_s):
    c = pl.program_id(1)
    L = SSM_CHUNK
    inner = SSM_HEADS * SSM_HEAD_DIM
    gw = SSM_STATE
    pair_w = 2 * SSM_HEAD_DIM

    @pl.when(c == 0)
    def _():
        hist_s[...] = jnp.zeros_like(hist_s)
        state_s[...] = jnp.zeros_like(state_s)

    cur = xbc_ref[0].astype(F32)
    ext = jnp.concatenate([hist_s[...], cur], axis=0)
    hist_s[...] = cur[L - CONV_HALO:, :]
    act = _silu(_causal_conv(ext, cw_ref, SSM_CONV, CONV_HALO) + cb_ref[...])
    xs = act[:, :inner]

    dt = _softplus(dt_ref[0] + dtb_ref[...])
    a = dt * (-jnp.exp(alog_ref[...]))
    row = lax.broadcasted_iota(jnp.int32, (L, L), 0)
    col = lax.broadcasted_iota(jnp.int32, (L, L), 1)
    causal = row >= col
    acum = _select_rows_dot(jnp.where(causal, 1.0, 0.0).astype(BF16), a)
    acum_t = acum.T

    hrow = lax.broadcasted_iota(jnp.int32, (LANES, inner), 0)
    hcol = lax.broadcasted_iota(jnp.int32, (LANES, inner), 1)
    widen = jnp.where(hcol // SSM_HEAD_DIM == hrow, 1.0, 0.0).astype(BF16)
    widened = _select_cols_dot(jnp.concatenate([dt, acum], axis=0), widen)
    dt_w = widened[:L]
    acum_w = widened[L:]
    last_w = acum_w[L - 1:L, :]
    xdt = xs * dt_w
    xdt_bf = xdt.astype(BF16)
    xdt_dec_bf = (xdt * jnp.exp(last_w - acum_w)).astype(BF16)
    exp_acum_w = jnp.exp(acum_w)
    chunk_decay_w = jnp.exp(last_w)
    lane = lax.broadcasted_iota(jnp.int32, (L, pair_w), 1)
    first_head = lane < SSM_HEAD_DIM

    heads_per_group = SSM_HEADS // SSM_GROUPS
    ys = []
    for g in range(SSM_GROUPS):
        bg = act[:, inner + g * gw:inner + (g + 1) * gw]
        cg = act[:, inner + SSM_GROUPS * gw + g * gw:inner + SSM_GROUPS * gw + (g + 1) * gw]
        cg_bf = cg.astype(BF16)
        cb = _dot_nt(cg_bf, bg.astype(BF16))
        bg_t = bg.T.astype(BF16)
        for pp in range(heads_per_group // 2):
            h0 = g * heads_per_group + 2 * pp
            pi = h0 // 2
            sl = slice(h0 * SSM_HEAD_DIM, h0 * SSM_HEAD_DIM + pair_w)
            yd = []
            for hh in (h0, h0 + 1):
                diff = jnp.broadcast_to(acum[:, hh:hh + 1], (L, L)) - acum_t[hh:hh + 1, :]
                decay = jnp.exp(jnp.where(causal, diff, NEG))
                yd.append(_dot((cb * decay).astype(BF16), xdt_bf[:, sl]))
            y_diag = jnp.where(first_head, yd[0], yd[1])
            prev = state_s[pi]
            y_off = _dot(cg_bf, prev.astype(BF16)) * exp_acum_w[:, sl]
            state_s[pi] = prev * chunk_decay_w[:, sl] + _dot(bg_t, xdt_dec_bf[:, sl])
            ys.append(y_diag + y_off + dskip_ref[:, sl] * xs[:, sl])
    y = jnp.concatenate(ys, axis=1)

    zf = z_ref[0].astype(F32)
    yg = y * _silu(zf)
    gsz = inner // SSM_GROUPS
    parts = []
    for g in range(SSM_GROUPS):
        seg = yg[:, g * gsz:(g + 1) * gsz]
        parts.append(seg * lax.rsqrt(jnp.mean(seg * seg, axis=-1, keepdims=True) + EPS))
    out_ref[0] = (jnp.concatenate(parts, axis=1) * nw_ref[...]).astype(BF16)


def _ssd(xbc, z, dt, conv_w, conv_b, dt_bias, a_log, d_skip_w, norm_w):
    b, s, cd = xbc.shape
    inner = z.shape[2]
    nc = s // SSM_CHUNK
    return pl.pallas_call(
        _ssd_kernel,
        grid=(b, nc),
        in_specs=[
            pl.BlockSpec((1, SSM_CHUNK, cd), lambda bi, ci: (bi, ci, 0)),
            pl.BlockSpec((1, SSM_CHUNK, inner), lambda bi, ci: (bi, ci, 0)),
            pl.BlockSpec((1, SSM_CHUNK, LANES), lambda bi, ci: (bi, ci, 0)),
            _const_spec(conv_w.shape), _const_spec(conv_b.shape), _const_spec(dt_bias.shape),
            _const_spec(a_log.shape), _const_spec(d_skip_w.shape), _const_spec(norm_w.shape),
        ],
        out_specs=pl.BlockSpec((1, SSM_CHUNK, inner), lambda bi, ci: (bi, ci, 0)),
        out_shape=jax.ShapeDtypeStruct((b, s, inner), BF16),
        scratch_shapes=[
            pltpu.VMEM((CONV_HALO, cd), F32),
            pltpu.VMEM((SSM_HEADS // 2, SSM_STATE, 2 * SSM_HEAD_DIM), F32),
        ],
        compiler_params=_params(("parallel", "arbitrary")),
        name="ssd_heads",
    )(xbc, z, dt, conv_w, conv_b, dt_bias, a_log, d_skip_w, norm_w)


def _outproj_even_kernel(x_ref, a_ref, s_ref, wa_ref, ws_ref, o_ref):
    o_ref[...] = x_ref[...] + _dot(a_ref[...], wa_ref[...]) + _dot(s_ref[...], ws_ref[...])


def _outproj_even(x2, attn, ssm, wa, ws, tm=1024):
    t, d = x2.shape
    return pl.pallas_call(
        _outproj_even_kernel,
        grid=(t // tm,),
        in_specs=[pl.BlockSpec((tm, d), lambda i: (i, 0)),
                  pl.BlockSpec((tm, attn.shape[1]), lambda i: (i, 0)),
                  pl.BlockSpec((tm, ssm.shape[1]), lambda i: (i, 0)),
                  _const_spec(wa.shape), _const_spec(ws.shape)],
        out_specs=pl.BlockSpec((tm, d), lambda i: (i, 0)),
        out_shape=jax.ShapeDtypeStruct((t, d), F32),
        compiler_params=_params(("parallel",)),
        name="outproj_even",
    )(x2, attn, ssm, wa, ws)


def _outproj_odd_kernel(x_ref, o_in_ref, w_ref, o_ref):
    o_ref[...] = x_ref[...] + _dot(o_in_ref[...], w_ref[...])


def _outproj_odd(x2, o_in, w, tm=1024):
    t, d = x2.shape
    return pl.pallas_call(
        _outproj_odd_kernel,
        grid=(t // tm,),
        in_specs=[pl.BlockSpec((tm, d), lambda i: (i, 0)),
                  pl.BlockSpec((tm, o_in.shape[1]), lambda i: (i, 0)),
                  _const_spec(w.shape)],
        out_specs=pl.BlockSpec((tm, d), lambda i: (i, 0)),
        out_shape=jax.ShapeDtypeStruct((t, d), F32),
        compiler_params=_params(("parallel",)),
        name="outproj_odd",
    )(x2, o_in, w)


def _ffn_ple_kernel(x_ref, halo_ref, p_ref, gf_ref, wg_ref, wu_ref, cw_ref, cb_ref, wd_ref,
                    gp_ref, wpg_ref, wpp_ref, o_ref, *, tm, seq):
    i = pl.program_id(0)
    x = x_ref[...]
    keep_halo = jnp.where((i * tm) % seq == 0, 0.0, 1.0)
    halo = halo_ref[...] * keep_halo
    hn = _rms_rows(jnp.concatenate([halo, x], axis=0), gf_ref[...]).astype(BF16)
    hn_cur = hn[FFN_HALO:]

    d_ff = wg_ref.shape[1]
    acts = []
    for j in range(d_ff // FFN_COLS):
        cs = slice(j * FFN_COLS, (j + 1) * FFN_COLS)
        g = _dot(hn, wg_ref[:, cs])
        gc = cb_ref[:, cs] + cw_ref[FFN_CONV - 1:FFN_CONV, cs] * g[FFN_HALO:]
        for shift in range(1, FFN_CONV):
            gc = gc + cw_ref[FFN_CONV - 1 - shift:FFN_CONV - shift, cs] * pltpu.roll(g, shift, axis=0)[FFN_HALO:]
        u = _dot(hn_cur, wu_ref[:, cs])
        acts.append((_silu(gc) * u).astype(BF16))
    x1 = x + _dot(jnp.concatenate(acts, axis=1), wd_ref[...])
    hp = _rms_rows(x1, gp_ref[...]).astype(BF16)
    gate = _sigmoid(_dot(hp, wpg_ref[...]))
    o_ref[...] = x1 + gate * _dot(p_ref[...].astype(BF16), wpp_ref[...])


def _ffn_ple(x2, p2, seq, gain_ffn, wg, wu, conv_w, conv_b, wd, gain_ple, wpg, wpp, tm=1024):
    t, d = x2.shape
    halo_blocks = tm // FFN_HALO
    consts = (gain_ffn, wg, wu, conv_w, conv_b, wd, gain_ple, wpg, wpp)
    return pl.pallas_call(
        functools.partial(_ffn_ple_kernel, tm=tm, seq=seq),
        grid=(t // tm,),
        in_specs=[pl.BlockSpec((tm, d), lambda i: (i, 0)),
                  pl.BlockSpec((FFN_HALO, d), lambda i: (jnp.maximum(i * halo_blocks - 1, 0), 0)),
                  pl.BlockSpec((tm, p2.shape[1]), lambda i: (i, 0))]
        + [_const_spec(c.shape) for c in consts],
        out_specs=pl.BlockSpec((tm, d), lambda i: (i, 0)),
        out_shape=jax.ShapeDtypeStruct((t, d), F32),
        compiler_params=_params(("parallel",)),
        name="ffn_ple",
    )(x2, x2, p2, *consts)


def _inproj_odd_kernel(x_ref, g_ref, wqkv_ref, wz_ref, wba_ref, qkv_ref, z_ref, ba_ref):
    hn = _rms_rows(x_ref[...], g_ref[...]).astype(BF16)
    qkv_ref[...] = _dot(hn, wqkv_ref[...]).astype(BF16)
    z_ref[...] = _dot(hn, wz_ref[...]).astype(BF16)
    ba_ref[...] = _dot(hn, wba_ref[...])


def _inproj_odd(x2, gain, wqkv, wz, wba, tm=512):
    t, d = x2.shape
    outs = [(wqkv.shape[1], BF16), (wz.shape[1], BF16), (wba.shape[1], F32)]
    return pl.pallas_call(
        _inproj_odd_kernel,
        grid=(t // tm,),
        in_specs=[pl.BlockSpec((tm, d), lambda i: (i, 0)), _const_spec(gain.shape)]
        + [_const_spec(w.shape) for w in (wqkv, wz, wba)],
        out_specs=[pl.BlockSpec((tm, n), lambda i: (i, 0)) for n, _ in outs],
        out_shape=[jax.ShapeDtypeStruct((t, n), dt) for n, dt in outs],
        compiler_params=_params(("parallel",)),
        name="inproj_odd",
    )(x2, gain, wqkv, wz, wba)


def _bdot(a, b):
    return _dot(a.astype(BF16), b.astype(BF16))


def _unit_lower_inverses(mats):
    L = mats[0].shape[0]
    row = lax.broadcasted_iota(jnp.int32, (L, L), 0)
    col = lax.broadcasted_iota(jnp.int32, (L, L), 1)
    eye = jnp.where(row == col, 1.0, 0.0)
    base = 8
    same = (row // base) == (col // base)
    ns = [-jnp.where(same, a, 0.0) for a in mats]
    ts = [eye + n for n in ns]
    ns_bf = [n.astype(BF16) for n in ns]
    pws_bf = [_dot(n, n).astype(BF16) for n in ns_bf]
    ts = [t + _dot(t.astype(BF16), pw) for t, pw in zip(ts, pws_bf)]
    pws_bf = [_dot(pw, pw).astype(BF16) for pw in pws_bf]
    ts = [t + _dot(t.astype(BF16), pw) for t, pw in zip(ts, pws_bf)]
    size = base
    while size < L:
        wider = (row // (2 * size)) == (col // (2 * size))
        offs = [jnp.where(wider & jnp.logical_not(same), a, 0.0).astype(BF16) for a in mats]
        ts_bf = [t.astype(BF16) for t in ts]
        ys = [_dot(off, t).astype(BF16) for off, t in zip(offs, ts_bf)]
        ts = [t - _dot(t_bf, y) for t, t_bf, y in zip(ts, ts_bf, ys)]
        same = wider
        size *= 2
    return ts


def _gdn_kernel(qkv_ref, z_ref, ba_ref, cw_ref, dtb_ref, alog_ref, nw_ref, out_ref, hist_s, state_s):
    c = pl.program_id(1)
    L = GDN_CHUNK
    dk = GDN_DIM
    qk_w = GDN_K_HEADS * dk
    rep = GDN_V_HEADS // GDN_K_HEADS
    nseq = qkv_ref.shape[0]

    @pl.when(c == 0)
    def _():
        hist_s[...] = jnp.zeros_like(hist_s)
        state_s[...] = jnp.zeros_like(state_s)

    row = lax.broadcasted_iota(jnp.int32, (L, L), 0)
    col = lax.broadcasted_iota(jnp.int32, (L, L), 1)
    lower = row >= col
    strict = row > col

    items = [(sq, hv) for sq in range(nseq) for hv in range(GDN_V_HEADS)]
    acts, qs, ks, kts, kk, qk0 = {}, {}, {}, {}, {}, {}
    beta_c, gc_c, gc_r = {}, {}, {}
    for sq in range(nseq):
        cur = qkv_ref[sq].astype(F32)
        ext = jnp.concatenate([hist_s[sq], cur], axis=0)
        hist_s[sq] = cur[L - CONV_HALO:, :]
        act = _silu(_causal_conv(ext, cw_ref, GDN_CONV, CONV_HALO))
        acts[sq] = act

        ba = ba_ref[sq]
        beta = _sigmoid(ba)
        g = -jnp.exp(alog_ref[...]) * _softplus(ba + dtb_ref[...])
        gc = _select_rows_dot(jnp.where(lower, 1.0, 0.0).astype(BF16), g)
        gc_t = gc.T
        for hv in range(GDN_V_HEADS):
            beta_c[sq, hv] = beta[:, hv:hv + 1]
            gc_c[sq, hv] = gc[:, GDN_V_HEADS + hv:GDN_V_HEADS + hv + 1]
            gc_r[sq, hv] = gc_t[GDN_V_HEADS + hv:GDN_V_HEADS + hv + 1, :]

        qk_act = act[:, :2 * qk_w]
        qk_n = qk_act * lax.rsqrt(_head_sums(qk_act * qk_act, dk) + EPS)
        for kh in range(GDN_K_HEADS):
            qn = qk_n[:, kh * dk:(kh + 1) * dk] * (dk ** -0.5)
            kn = qk_n[:, qk_w + kh * dk:qk_w + (kh + 1) * dk]
            kt = kn.T
            prod = _bdot(jnp.concatenate([kn, qn], axis=0), kt)
            qs[sq, kh], ks[sq, kh], kts[sq, kh] = qn, kn, kt
            kk[sq, kh], qk0[sq, kh] = prod[:L], prod[L:]

    def khead(it):
        return it[0], it[1] // rep

    outs = {}
    for g0 in range(0, len(items), GDN_CHAINS_PER_GROUP):
        group = items[g0:g0 + GDN_CHAINS_PER_GROUP]
        g_last = {it: gc_r[it][:, L - 1:L] for it in group}
        egc = {it: jnp.exp(gc_c[it]) for it in group}
        decay = {it: jnp.exp(jnp.where(lower, gc_c[it] - gc_r[it], NEG)) for it in group}
        a_mats = [jnp.where(strict, kk[khead(it)] * beta_c[it] * decay[it], 0.0) for it in group]
        t_inv = dict(zip(group, _unit_lower_inverses(a_mats)))

        sols = {}
        for it in group:
            sq, hv = it
            v_h = acts[sq][:, 2 * qk_w + hv * dk:2 * qk_w + (hv + 1) * dk]
            sols[it] = _bdot(t_inv[it], jnp.concatenate([v_h * beta_c[it],
                                                         ks[khead(it)] * (beta_c[it] * egc[it])], axis=1))

        states = {it: state_s[it[0], it[1]] for it in group}
        from_state = {it: _bdot(jnp.concatenate([sols[it][:, dk:], qs[khead(it)] * egc[it]], axis=0),
                                states[it]) for it in group}
        v_new = {it: sols[it][:, :dk] - from_state[it][:L] for it in group}
        from_v = {it: _bdot(jnp.concatenate([qk0[khead(it)] * decay[it],
                                             kts[khead(it)] * jnp.exp(g_last[it] - gc_r[it])], axis=0),
                            v_new[it]) for it in group}
        for it in group:
            state_s[it[0], it[1]] = states[it] * jnp.exp(g_last[it]) + from_v[it][L:]
            outs[it] = from_state[it][L:] + from_v[it][:L]
    for sq in range(nseq):
        o = jnp.concatenate([outs[sq, hv] for hv in range(GDN_V_HEADS)], axis=1)
        o = o * lax.rsqrt(_head_sums(o * o, dk) * (1.0 / dk) + EPS) * nw_ref[...]
        out_ref[sq] = (o * _silu(z_ref[sq].astype(F32))).astype(BF16)


def _gdn(qkv, z, ba, conv_w, dt_bias_pad, a_log_pad, norm_w):
    b, s, cd = qkv.shape
    vw = z.shape[2]
    nc = s // GDN_CHUNK
    nseq = GDN_SEQS_PER_STEP
    return pl.pallas_call(
        _gdn_kernel,
        grid=(b // nseq, nc),
        in_specs=[
            pl.BlockSpec((nseq, GDN_CHUNK, cd), lambda bi, ci: (bi, ci, 0)),
            pl.BlockSpec((nseq, GDN_CHUNK, vw), lambda bi, ci: (bi, ci, 0)),
            pl.BlockSpec((nseq, GDN_CHUNK, LANES), lambda bi, ci: (bi, ci, 0)),
            _const_spec(conv_w.shape), _const_spec(dt_bias_pad.shape),
            _const_spec(a_log_pad.shape), _const_spec(norm_w.shape),
        ],
        out_specs=pl.BlockSpec((nseq, GDN_CHUNK, vw), lambda bi, ci: (bi, ci, 0)),
        out_shape=jax.ShapeDtypeStruct((b, s, vw), BF16),
        scratch_shapes=[
            pltpu.VMEM((nseq, CONV_HALO, cd), F32),
            pltpu.VMEM((nseq, GDN_V_HEADS, GDN_DIM, GDN_DIM), F32),
        ],
        compiler_params=_params(("parallel", "arbitrary")),
        name="gated_deltanet",
    )(qkv, z, ba, conv_w, dt_bias_pad, a_log_pad, norm_w)


def _pad_lanes(v, offset=0):
    out = jnp.zeros((1, LANES), F32)
    return out.at[0, offset:offset + v.shape[0]].set(v.astype(F32))


def _row(v):
    return v.astype(F32).reshape(1, -1)


def _even_mixer(x2, batch, seq, norm_w, w_in, q_norm, k_norm, conv_w, conv_b, dt_bias, a_log, d_skip,
                ssm_norm, w_out, slopes):
    mw = MOBA_HEADS * MOBA_HEAD_DIM
    inner = SSM_HEADS * SSM_HEAD_DIM
    conv_dim = inner + 2 * SSM_GROUPS * SSM_STATE
    w = w_in.astype(BF16)
    c0, c1, c2, c3, c4 = mw, 2 * mw, 3 * mw, 3 * mw + inner, 3 * mw + inner + conv_dim
    wdt = jnp.pad(w[:, c4:], ((0, 0), (0, LANES - SSM_HEADS)))
    q, k, v, z, xbc, dt = _inproj_even(x2, _row(norm_w), w[:, :c0], w[:, c0:c1], w[:, c1:c2],
                                       w[:, c2:c3], w[:, c3:c4], wdt)
    attn = _moba(q.reshape(batch, seq, mw), k.reshape(batch, seq, mw), v.reshape(batch, seq, mw),
                 _row(jnp.tile(q_norm, MOBA_HEADS_PER_STEP)), _row(jnp.tile(k_norm, MOBA_HEADS_PER_STEP)),
                 slopes).reshape(batch * seq, mw)

    ssm = _ssd(xbc.reshape(batch, seq, conv_dim), z.reshape(batch, seq, inner),
               dt.reshape(batch, seq, LANES), conv_w.astype(F32), _row(conv_b),
               _pad_lanes(dt_bias), _pad_lanes(a_log), _row(jnp.repeat(d_skip, SSM_HEAD_DIM)),
               _row(ssm_norm))
    wo = w_out.astype(BF16)
    return _outproj_even(x2, attn, ssm.reshape(batch * seq, inner), wo[:mw], wo[mw:])


def _gdn_mixer(x2, batch, seq, norm_w, w_in, conv_w, dt_bias, a_log, gdn_norm, w_out):
    conv_dim = 2 * GDN_K_HEADS * GDN_DIM + GDN_V_HEADS * GDN_DIM
    vw = GDN_V_HEADS * GDN_DIM
    w = w_in.astype(BF16)
    wba = jnp.pad(w[:, conv_dim + vw:], ((0, 0), (0, LANES - 2 * GDN_V_HEADS)))
    qkv, z, ba = _inproj_odd(x2, _row(norm_w), w[:, :conv_dim], w[:, conv_dim:conv_dim + vw], wba)
    o = _gdn(qkv.reshape(batch, seq, conv_dim), z.reshape(batch, seq, vw), ba.reshape(batch, seq, LANES),
             conv_w.astype(F32), _pad_lanes(dt_bias, GDN_V_HEADS), _pad_lanes(a_log, GDN_V_HEADS),
             _row(jnp.tile(gdn_norm, GDN_V_HEADS)))
    return _outproj_odd(x2, o.reshape(batch * seq, vw), w_out.astype(BF16))


def kernel(x, p, norm_mix, norm_ffn, norm_ple, w_in_even, moba_q_norm, moba_k_norm, ssm_conv_w, ssm_conv_b, ssm_dt_bias, ssm_a_log, ssm_d, ssm_norm, w_out_even, w_in_odd, gdn_conv_w, gdn_dt_bias, gdn_a_log, gdn_norm, w_out_odd, ffn_w_gate, ffn_w_up, ffn_conv_w, ffn_conv_b, ffn_w_down, ple_w_proj, ple_w_gate):
    batch, seq, d = x.shape
    depth = p.shape[0]
    slopes = jnp.exp2(-ALIBI_MAX_BIAS * jnp.arange(1, MOBA_HEADS + 1, dtype=F32) / MOBA_HEADS)
    x2 = x.reshape(batch * seq, d)
    for i in range(depth):
        j = i // 2
        if i % 2 == 0:
            x2 = _even_mixer(x2, batch, seq, norm_mix[i], w_in_even[j], moba_q_norm[j], moba_k_norm[j],
                             ssm_conv_w[j], ssm_conv_b[j], ssm_dt_bias[j], ssm_a_log[j], ssm_d[j],
                             ssm_norm[j], w_out_even[j], slopes)
        else:
            x2 = _gdn_mixer(x2, batch, seq, norm_mix[i], w_in_odd[j], gdn_conv_w[j], gdn_dt_bias[j],
                            gdn_a_log[j], gdn_norm[j], w_out_odd[j])
        x2 = _ffn_ple(x2, p[i].reshape(batch * seq, -1), seq, _row(norm_ffn[i]),
                      ffn_w_gate[i].astype(BF16), ffn_w_up[i].astype(BF16), ffn_conv_w[i].astype(F32),
                      _row(ffn_conv_b[i]), ffn_w_down[i].astype(BF16), _row(norm_ple[i]),
                      ple_w_gate[i].astype(BF16), ple_w_proj[i].astype(BF16))
    return x2.reshape(batch, seq, d)
```

```python
import functools

import jax
import jax.numpy as jnp
from jax import lax
from jax.experimental import pallas as pl
from jax.experimental.pallas import tpu as pltpu

F32 = jnp.float32
BF16 = jnp.bfloat16
HIGHEST = lax.Precision.HIGHEST

EPS = 1e-6
NEG = -1e30
LOG2E = 1.4426950408889634
LANES = 128
CONV_HALO = 8

MOBA_HEADS = 8
MOBA_HEAD_DIM = 64
MOBA_BLOCK = 256
MOBA_TOPK = 3
MOBA_HEADS_PER_STEP = 4
MOBA_ONES_ROWS = 16
ALIBI_MAX_BIAS = 8.0

SSM_HEADS = 16
SSM_HEAD_DIM = 64
SSM_STATE = 128
SSM_GROUPS = 2
SSM_CONV = 4
SSM_CHUNK = 128

GDN_K_HEADS = 8
GDN_V_HEADS = 16
GDN_DIM = 128
GDN_CONV = 4
GDN_CHUNK = 64
GDN_SEQS_PER_STEP = 2
GDN_CHAINS_PER_GROUP = 32

FFN_CONV = 3
FFN_HALO = 16
FFN_COLS = 256

VMEM_LIMIT = 56 * 1024 * 1024


def _rms_rows(x, gain):
    return x * lax.rsqrt(jnp.mean(x * x, axis=-1, keepdims=True) + EPS) * gain


def _sigmoid(x):
    return 0.5 * jnp.tanh(0.5 * x) + 0.5


def _silu(x):
    half = 0.5 * x
    return half + half * jnp.tanh(half)


def _softplus(x):
    return jnp.maximum(x, 0.0) + jnp.log(1.0 + jnp.exp(-jnp.abs(x)))


def _dot(a, b):
    return jnp.dot(a, b, preferred_element_type=F32)


def _dot_nt(a, b):
    return lax.dot_general(a, b, (((1,), (1,)), ((), ())), preferred_element_type=F32)


def _bf16_terms(x):
    hi = x.astype(BF16)
    rest = x - hi.astype(F32)
    mid = rest.astype(BF16)
    lo = (rest - mid.astype(F32)).astype(BF16)
    return [hi, mid, lo]


def _select_rows_dot(sel, x):
    return _dot(jnp.concatenate([sel] * 3, axis=1), jnp.concatenate(_bf16_terms(x), axis=0))


def _select_cols_dot(x, sel):
    return _dot(jnp.concatenate(_bf16_terms(x), axis=1), jnp.concatenate([sel] * 3, axis=0))


def _head_sums(x, head_w):
    rows, width = x.shape
    slab = 2 * LANES
    n = width // slab
    same = (lax.broadcasted_iota(jnp.int32, (slab, slab), 0) // head_w
            == lax.broadcasted_iota(jnp.int32, (slab, slab), 1) // head_w)
    ones = jnp.where(same, 1.0, 0.0).astype(BF16)
    stacked = jnp.concatenate([x[:, i * slab:(i + 1) * slab] for i in range(n)], axis=0)
    sums = _dot(stacked.astype(BF16), ones)
    return jnp.concatenate([sums[i * rows:(i + 1) * rows] for i in range(n)], axis=1)


def _const_spec(shape):
    zeros = (0,) * len(shape)
    return pl.BlockSpec(shape, lambda *_: zeros, pipeline_mode=pl.Buffered(1))


def _params(semantics):
    return pltpu.CompilerParams(dimension_semantics=semantics, vmem_limit_bytes=VMEM_LIMIT)


def _causal_conv(ext, w_ref, width, halo):
    y = w_ref[width - 1:width, :] * ext[halo:]
    for shift in range(1, width):
        y = y + w_ref[width - 1 - shift:width - shift, :] * pltpu.roll(ext, shift, axis=0)[halo:]
    return y


def _inproj_even_kernel(x_ref, g_ref, wq_ref, wk_ref, wv_ref, wz_ref, wxbc_ref, wdt_ref,
                        q_ref, k_ref, v_ref, z_ref, xbc_ref, dt_ref):
    hn = _rms_rows(x_ref[...], g_ref[...]).astype(BF16)
    q_ref[...] = _dot(hn, wq_ref[...]).astype(BF16)
    k_ref[...] = _dot(hn, wk_ref[...]).astype(BF16)
    v_ref[...] = _dot(hn, wv_ref[...]).astype(BF16)
    z_ref[...] = _dot(hn, wz_ref[...]).astype(BF16)
    xbc_ref[...] = _dot(hn, wxbc_ref[...]).astype(BF16)
    dt_ref[...] = _dot(hn, wdt_ref[...])


def _inproj_even(x2, gain, wq, wk, wv, wz, wxbc, wdt, tm=512):
    t, d = x2.shape
    outs = [(wq.shape[1], BF16), (wk.shape[1], BF16), (wv.shape[1], BF16),
            (wz.shape[1], BF16), (wxbc.shape[1], BF16), (wdt.shape[1], F32)]
    return pl.pallas_call(
        _inproj_even_kernel,
        grid=(t // tm,),
        in_specs=[pl.BlockSpec((tm, d), lambda i: (i, 0)), _const_spec(gain.shape)]
        + [_const_spec(w.shape) for w in (wq, wk, wv, wz, wxbc, wdt)],
        out_specs=[pl.BlockSpec((tm, n), lambda i: (i, 0)) for n, _ in outs],
        out_shape=[jax.ShapeDtypeStruct((t, n), dt) for n, dt in outs],
        compiler_params=_params(("parallel",)),
        name="inproj_even",
    )(x2, gain, wq, wk, wv, wz, wxbc, wdt)


def _moba_kernel(slopes_ref, q_ref, k_ref, v_ref, qg_ref, kg_ref, o_ref, kn_s, kbar_s, vt_s):
    hp = pl.program_id(1)
    seq = k_ref.shape[1]
    nb = seq // MOBA_BLOCK
    dh = MOBA_HEAD_DIM
    hps = MOBA_HEADS_PER_STEP
    heads = range(hps)

    lanes = hps * dh
    same_head = (lax.broadcasted_iota(jnp.int32, (lanes, lanes), 0) // dh
                 == lax.broadcasted_iota(jnp.int32, (lanes, lanes), 1) // dh)
    head_mean = jnp.where(same_head, 1.0 / dh, 0.0).astype(BF16)

    def head_rms(x, gain):
        return x * lax.rsqrt(_dot((x * x).astype(BF16), head_mean) + EPS) * gain

    kn = head_rms(k_ref[0].astype(F32), kg_ref[...])
    kn_s[...] = kn.astype(BF16)
    kbar_s[...] = jnp.mean(kn.reshape(nb, MOBA_BLOCK, lanes), axis=1)
    vt_s[...] = v_ref[0].astype(F32).T.astype(BF16)

    lane_head = lax.broadcasted_iota(jnp.int32, (MOBA_BLOCK, lanes), 1) // dh
    blk = lax.broadcasted_iota(jnp.int32, (nb, MOBA_BLOCK), 0)
    kbar_head = lax.broadcasted_iota(jnp.int32, (nb, lanes), 1) // dh
    kbar = kbar_s[...]
    kbar_rows = jnp.concatenate([jnp.where(kbar_head == hh, kbar, 0.0) for hh in heads], axis=0)
    kidx = lax.broadcasted_iota(jnp.int32, (MOBA_BLOCK, MOBA_BLOCK), 0)
    qidx = lax.broadcasted_iota(jnp.int32, (MOBA_BLOCK, MOBA_BLOCK), 1)
    rel = (kidx - qidx).astype(F32)
    causal = rel <= 0.0
    slopes2 = [slopes_ref[hp * hps + hh] * LOG2E for hh in heads]
    q_gain = qg_ref[...] * (dh ** -0.5 * LOG2E)
    alibis = [slope2 * rel for slope2 in slopes2]

    def score_stage(own):
        rows = slice(own * MOBA_BLOCK, (own + 1) * MOBA_BLOCK)
        qn = head_rms(q_ref[0, rows, :].astype(F32), q_gain)
        qs = [jnp.where(lane_head == hh, qn, 0.0).astype(BF16) for hh in heads]
        scores = {(hh, j): _dot_nt(kn_s[j * MOBA_BLOCK:(j + 1) * MOBA_BLOCK, :], qs[hh])
                  for hh in heads for j in range(own + 1)}
        gates = lax.dot_general(kbar_rows, qn, (((1,), (1,)), ((), ())),
                                precision=HIGHEST, preferred_element_type=F32)
        selb = []
        for hh in heads:
            gate = gates[hh * nb:(hh + 1) * nb, :]
            rank = jnp.zeros(gate.shape, F32)
            for m in range(own):
                gm = gate[m:m + 1, :]
                beats = (gm > gate) | ((gm == gate) & (m < blk))
                rank = rank + jnp.where(beats, 1.0, 0.0)
            selected = (blk < own) & (rank < float(min(MOBA_TOPK, nb - 1)))
            selb.append(jnp.where(selected, 0.0, NEG))
        return scores, selb

    def softmax_pv_stage(own, scores, selb):
        rows = slice(own * MOBA_BLOCK, (own + 1) * MOBA_BLOCK)
        probs = []
        for hh in heads:
            tiles = [scores[hh, j] + alibis[hh] for j in range(own)]
            tiles.append(jnp.where(causal, scores[hh, own] + alibis[hh], NEG))
            col_bias = [selb[hh][j:j + 1, :] + slopes2[hh] * float((j - own) * MOBA_BLOCK) for j in range(own)]
            col_bias.append(jnp.zeros((1, MOBA_BLOCK), F32))
            m = jnp.max(tiles[0], axis=0, keepdims=True) + col_bias[0]
            for t, cb in zip(tiles[1:], col_bias[1:]):
                m = jnp.maximum(m, jnp.max(t, axis=0, keepdims=True) + cb)
            probs.append(jnp.concatenate([jnp.exp2(t + (cb - m)).astype(BF16) for t, cb in zip(tiles, col_bias)],
                                         axis=0))
        keys = (own + 1) * MOBA_BLOCK
        ones_rows = jnp.ones((MOBA_ONES_ROWS, keys), BF16)
        pv = [_dot(jnp.concatenate([vt_s[hh * dh:(hh + 1) * dh, 0:keys], ones_rows], axis=0), probs[hh])
              for hh in heads]
        outs = [r[:dh] / r[dh:dh + 1] for r in pv]
        o_ref[0, rows, :] = jnp.concatenate(outs, axis=0).T.astype(BF16)

    staged = score_stage(0)
    for own in range(nb):
        upcoming = score_stage(own + 1) if own + 1 < nb else None
        softmax_pv_stage(own, *staged)
        staged = upcoming


def _moba(q, k, v, q_gain, k_gain, slopes):
    b, s, width = q.shape
    dh = MOBA_HEAD_DIM
    hps = MOBA_HEADS_PER_STEP
    lanes = hps * dh
    nb = s // MOBA_BLOCK
    seq_spec = pl.BlockSpec((1, s, lanes), lambda bi, hi: (bi, 0, hi))
    return pl.pallas_call(
        _moba_kernel,
        grid=(b, width // lanes),
        in_specs=[
            pl.BlockSpec(memory_space=pltpu.SMEM),
            seq_spec, seq_spec, seq_spec,
            pl.BlockSpec((1, lanes), lambda bi, hi: (0, 0)),
            pl.BlockSpec((1, lanes), lambda bi, hi: (0, 0)),
        ],
        out_specs=seq_spec,
        out_shape=jax.ShapeDtypeStruct((b, s, width), BF16),
        scratch_shapes=[
            pltpu.VMEM((s, lanes), BF16),
            pltpu.VMEM((nb, lanes), F32),
            pltpu.VMEM((lanes, s), BF16),
        ],
        compiler_params=_params(("parallel", "parallel")),
        name="moba_attention",
    )(slopes, q, k, v, q_gain, k_gain)


def _ssd_kernel(xbc_ref, z_ref, dt_ref, cw_ref, cb_ref, dtb_ref, alog_ref, dskip_ref, nw_ref,
                out_ref, hist_s, state_s):
    c = pl.program_id(1)
    L = SSM_CHUNK
    inner = SSM_HEADS * SSM_HEAD_DIM
    gw = SSM_STATE
    pair_w = 2 * SSM_HEAD_DIM

    @pl.when(c == 0)
    def _():
        hist_s[...] = jnp.zeros_like(hist_s)
        state_s[...] = jnp.zeros_like(state_s)

    cur = xbc_ref[0].astype(F32)
    ext = jnp.concatenate([hist_s[...], cur], axis=0)
    hist_s[...] = cur[L - CONV_HALO:, :]
    act = _silu(_causal_conv(ext, cw_ref, SSM_CONV, CONV_HALO) + cb_ref[...])
    xs = act[:, :inner]

    dt = _softplus(dt_ref[0] + dtb_ref[...])
    a = dt * (-jnp.exp(alog_ref[...]))
    row = lax.broadcasted_iota(jnp.int32, (L, L), 0)
    col = lax.broadcasted_iota(jnp.int32, (L, L), 1)
    causal = row >= col
    acum = _select_rows_dot(jnp.where(causal, 1.0, 0.0).astype(BF16), a)
    acum_t = acum.T

    hrow = lax.broadcasted_iota(jnp.int32, (LANES, inner), 0)
    hcol = lax.broadcasted_iota(jnp.int32, (LANES, inner), 1)
    widen = jnp.where(hcol // SSM_HEAD_DIM == hrow, 1.0, 0.0).astype(BF16)
    widened = _select_cols_dot(jnp.concatenate([dt, acum], axis=0), widen)
    dt_w = widened[:L]
    acum_w = widened[L:]
    last_w = acum_w[L - 1:L, :]
    xdt = xs * dt_w
    xdt_bf = xdt.astype(BF16)
    xdt_dec_bf = (xdt * jnp.exp(last_w - acum_w)).astype(BF16)
    exp_acum_w = jnp.exp(acum_w)
    chunk_decay_w = jnp.exp(last_w)
    lane = lax.broadcasted_iota(jnp.int32, (L, pair_w), 1)
    first_head = lane < SSM_HEAD_DIM

    heads_per_group = SSM_HEADS // SSM_GROUPS
    ys = []
    for g in range(SSM_GROUPS):
        bg = act[:, inner + g * gw:inner + (g + 1) * gw]
        cg = act[:, inner + SSM_GROUPS * gw + g * gw:inner + SSM_GROUPS * gw + (g + 1) * gw]
        cg_bf = cg.astype(BF16)
        cb = _dot_nt(cg_bf, bg.astype(BF16))
        bg_t = bg.T.astype(BF16)
        for pp in range(heads_per_group // 2):
            h0 = g * heads_per_group + 2 * pp
            pi = h0 // 2
            sl = slice(h0 * SSM_HEAD_DIM, h0 * SSM_HEAD_DIM + pair_w)
            yd = []
            for hh in (h0, h0 + 1):
                diff = jnp.broadcast_to(acum[:, hh:hh + 1], (L, L)) - acum_t[hh:hh + 1, :]
                decay = jnp.exp(jnp.where(causal, diff, NEG))
                yd.append(_dot((cb * decay).astype(BF16), xdt_bf[:, sl]))
            y_diag = jnp.where(first_head, yd[0], yd[1])
            prev = state_s[pi]
            y_off = _dot(cg_bf, prev.astype(BF16)) * exp_acum_w[:, sl]
            state_s[pi] = prev * chunk_decay_w[:, sl] + _dot(bg_t, xdt_dec_bf[:, sl])
            ys.append(y_diag + y_off + dskip_ref[:, sl] * xs[:, sl])
    y = jnp.concatenate(ys, axis=1)

    zf = z_ref[0].astype(F32)
    yg = y * _silu(zf)
    gsz = inner // SSM_GROUPS
    parts = []
    for g in range(SSM_GROUPS):
        seg = yg[:, g * gsz:(g + 1) * gsz]
        parts.append(seg * lax.rsqrt(jnp.mean(seg * seg, axis=-1, keepdims=True) + EPS))
    out_ref[0] = (jnp.concatenate(parts, axis=1) * nw_ref[...]).astype(BF16)


def _ssd(xbc, z, dt, conv_w, conv_b, dt_bias, a_log, d_skip_w, norm_w):
    b, s, cd = xbc.shape
    inner = z.shape[2]
    nc = s // SSM_CHUNK
    return pl.pallas_call(
        _ssd_kernel,
        grid=(b, nc),
        in_specs=[
            pl.BlockSpec((1, SSM_CHUNK, cd), lambda bi, ci: (bi, ci, 0)),
            pl.BlockSpec((1, SSM_CHUNK, inner), lambda bi, ci: (bi, ci, 0)),
            pl.BlockSpec((1, SSM_CHUNK, LANES), lambda bi, ci: (bi, ci, 0)),
            _const_spec(conv_w.shape), _const_spec(conv_b.shape), _const_spec(dt_bias.shape),
            _const_spec(a_log.shape), _const_spec(d_skip_w.shape), _const_spec(norm_w.shape),
        ],
        out_specs=pl.BlockSpec((1, SSM_CHUNK, inner), lambda bi, ci: (bi, ci, 0)),
        out_shape=jax.ShapeDtypeStruct((b, s, inner), BF16),
        scratch_shapes=[
            pltpu.VMEM((CONV_HALO, cd), F32),
            pltpu.VMEM((SSM_HEADS // 2, SSM_STATE, 2 * SSM_HEAD_DIM), F32),
        ],
        compiler_params=_params(("parallel", "arbitrary")),
        name="ssd_heads",
    )(xbc, z, dt, conv_w, conv_b, dt_bias, a_log, d_skip_w, norm_w)


def _outproj_even_kernel(x_ref, a_ref, s_ref, wa_ref, ws_ref, o_ref):
    o_ref[...] = x_ref[...] + _dot(a_ref[...], wa_ref[...]) + _dot(s_ref[...], ws_ref[...])


def _outproj_even(x2, attn, ssm, wa, ws, tm=1024):
    t, d = x2.shape
    return pl.pallas_call(
        _outproj_even_kernel,
        grid=(t // tm,),
        in_specs=[pl.BlockSpec((tm, d), lambda i: (i, 0)),
                  pl.BlockSpec((tm, attn.shape[1]), lambda i: (i, 0)),
                  pl.BlockSpec((tm, ssm.shape[1]), lambda i: (i, 0)),
                  _const_spec(wa.shape), _const_spec(ws.shape)],
        out_specs=pl.BlockSpec((tm, d), lambda i: (i, 0)),
        out_shape=jax.ShapeDtypeStruct((t, d), F32),
        compiler_params=_params(("parallel",)),
        name="outproj_even",
    )(x2, attn, ssm, wa, ws)


def _outproj_odd_kernel(x_ref, o_in_ref, w_ref, o_ref):
    o_ref[...] = x_ref[...] + _dot(o_in_ref[...], w_ref[...])


def _outproj_odd(x2, o_in, w, tm=1024):
    t, d = x2.shape
    return pl.pallas_call(
        _outproj_odd_kernel,
        grid=(t // tm,),
        in_specs=[pl.BlockSpec((tm, d), lambda i: (i, 0)),
                  pl.BlockSpec((tm, o_in.shape[1]), lambda i: (i, 0)),
                  _const_spec(w.shape)],
        out_specs=pl.BlockSpec((tm, d), lambda i: (i, 0)),
        out_shape=jax.ShapeDtypeStruct((t, d), F32),
        compiler_params=_params(("parallel",)),
        name="outproj_odd",
    )(x2, o_in, w)


def _ffn_ple_kernel(x_ref, halo_ref, p_ref, gf_ref, wg_ref, wu_ref, cw_ref, cb_ref, wd_ref,
                    gp_ref, wpg_ref, wpp_ref, o_ref, *, tm, seq):
    i = pl.program_id(0)
    x = x_ref[...]
    keep_halo = jnp.where((i * tm) % seq == 0, 0.0, 1.0)
    halo = halo_ref[...] * keep_halo
    hn = _rms_rows(jnp.concatenate([halo, x], axis=0), gf_ref[...]).astype(BF16)
    hn_cur = hn[FFN_HALO:]

    d_ff = wg_ref.shape[1]
    acts = []
    for j in range(d_ff // FFN_COLS):
        cs = slice(j * FFN_COLS, (j + 1) * FFN_COLS)
        g = _dot(hn, wg_ref[:, cs])
        gc = cb_ref[:, cs] + cw_ref[FFN_CONV - 1:FFN_CONV, cs] * g[FFN_HALO:]
        for shift in range(1, FFN_CONV):
            gc = gc + cw_ref[FFN_CONV - 1 - shift:FFN_CONV - shift, cs] * pltpu.roll(g, shift, axis=0)[FFN_HALO:]
        u = _dot(hn_cur, wu_ref[:, cs])
        acts.append((_silu(gc) * u).astype(BF16))
    x1 = x + _dot(jnp.concatenate(acts, axis=1), wd_ref[...])
    hp = _rms_rows(x1, gp_ref[...]).astype(BF16)
    gate = _sigmoid(_dot(hp, wpg_ref[...]))
    o_ref[...] = x1 + gate * _dot(p_ref[...].astype(BF16), wpp_ref[...])


def _ffn_ple(x2, p2, seq, gain_ffn, wg, wu, conv_w, conv_b, wd, gain_ple, wpg, wpp, tm=1024):
    t, d = x2.shape
    halo_blocks = tm // FFN_HALO
    consts = (gain_ffn, wg, wu, conv_w, conv_b, wd, gain_ple, wpg, wpp)
    return pl.pallas_call(
        functools.partial(_ffn_ple_kernel, tm=tm, seq=seq),
        grid=(t // tm,),
        in_specs=[pl.BlockSpec((tm, d), lambda i: (i, 0)),
                  pl.BlockSpec((FFN_HALO, d), lambda i: (jnp.maximum(i * halo_blocks - 1, 0), 0)),
                  pl.BlockSpec((tm, p2.shape[1]), lambda i: (i, 0))]
        + [_const_spec(c.shape) for c in consts],
        out_specs=pl.BlockSpec((tm, d), lambda i: (i, 0)),
        out_shape=jax.ShapeDtypeStruct((t, d), F32),
        compiler_params=_params(("parallel",)),
        name="ffn_ple",
    )(x2, x2, p2, *consts)


def _inproj_odd_kernel(x_ref, g_ref, wqkv_ref, wz_ref, wba_ref, qkv_ref, z_ref, ba_ref):
    hn = _rms_rows(x_ref[...], g_ref[...]).astype(BF16)
    qkv_ref[...] = _dot(hn, wqkv_ref[...]).astype(BF16)
    z_ref[...] = _dot(hn, wz_ref[...]).astype(BF16)
    ba_ref[...] = _dot(hn, wba_ref[...])


def _inproj_odd(x2, gain, wqkv, wz, wba, tm=512):
    t, d = x2.shape
    outs = [(wqkv.shape[1], BF16), (wz.shape[1], BF16), (wba.shape[1], F32)]
    return pl.pallas_call(
        _inproj_odd_kernel,
        grid=(t // tm,),
        in_specs=[pl.BlockSpec((tm, d), lambda i: (i, 0)), _const_spec(gain.shape)]
        + [_const_spec(w.shape) for w in (wqkv, wz, wba)],
        out_specs=[pl.BlockSpec((tm, n), lambda i: (i, 0)) for n, _ in outs],
        out_shape=[jax.ShapeDtypeStruct((t, n), dt) for n, dt in outs],
        compiler_params=_params(("parallel",)),
        name="inproj_odd",
    )(x2, gain, wqkv, wz, wba)


def _bdot(a, b):
    return _dot(a.astype(BF16), b.astype(BF16))


def _unit_lower_inverses(mats):
    L = mats[0].shape[0]
    row = lax.broadcasted_iota(jnp.int32, (L, L), 0)
    col = lax.broadcasted_iota(jnp.int32, (L, L), 1)
    eye = jnp.where(row == col, 1.0, 0.0)
    base = 8
    same = (row // base) == (col // base)
    ns = [-jnp.where(same, a, 0.0) for a in mats]
    ts = [eye + n for n in ns]
    ns_bf = [n.astype(BF16) for n in ns]
    pws_bf = [_dot(n, n).astype(BF16) for n in ns_bf]
    ts = [t + _dot(t.astype(BF16), pw) for t, pw in zip(ts, pws_bf)]
    pws_bf = [_dot(pw, pw).astype(BF16) for pw in pws_bf]
    ts = [t + _dot(t.astype(BF16), pw) for t, pw in zip(ts, pws_bf)]
    size = base
    while size < L:
        wider = (row // (2 * size)) == (col // (2 * size))
        offs = [jnp.where(wider & jnp.logical_not(same), a, 0.0).astype(BF16) for a in mats]
        ts_bf = [t.astype(BF16) for t in ts]
        ys = [_dot(off, t).astype(BF16) for off, t in zip(offs, ts_bf)]
        ts = [t - _dot(t_bf, y) for t, t_bf, y in zip(ts, ts_bf, ys)]
        same = wider
        size *= 2
    return ts


def _gdn_kernel(qkv_ref, z_ref, ba_ref, cw_ref, dtb_ref, alog_ref, nw_ref, out_ref, hist_s, state_s):
    c = pl.program_id(1)
    L = GDN_CHUNK
    dk = GDN_DIM
    qk_w = GDN_K_HEADS * dk
    rep = GDN_V_HEADS // GDN_K_HEADS
    nseq = qkv_ref.shape[0]

    @pl.when(c == 0)
    def _():
        hist_s[...] = jnp.zeros_like(hist_s)
        state_s[...] = jnp.zeros_like(state_s)

    row = lax.broadcasted_iota(jnp.int32, (L, L), 0)
    col = lax.broadcasted_iota(jnp.int32, (L, L), 1)
    lower = row >= col
    strict = row > col

    items = [(sq, hv) for sq in range(nseq) for hv in range(GDN_V_HEADS)]
    acts, qs, ks, kts, kk, qk0 = {}, {}, {}, {}, {}, {}
    beta_c, gc_c, gc_r = {}, {}, {}
    for sq in range(nseq):
        cur = qkv_ref[sq].astype(F32)
        ext = jnp.concatenate([hist_s[sq], cur], axis=0)
        hist_s[sq] = cur[L - CONV_HALO:, :]
        act = _silu(_causal_conv(ext, cw_ref, GDN_CONV, CONV_HALO))
        acts[sq] = act

        ba = ba_ref[sq]
        beta = _sigmoid(ba)
        g = -jnp.exp(alog_ref[...]) * _softplus(ba + dtb_ref[...])
        gc = _select_rows_dot(jnp.where(lower, 1.0, 0.0).astype(BF16), g)
        gc_t = gc.T
        for hv in range(GDN_V_HEADS):
            beta_c[sq, hv] = beta[:, hv:hv + 1]
            gc_c[sq, hv] = gc[:, GDN_V_HEADS + hv:GDN_V_HEADS + hv + 1]
            gc_r[sq, hv] = gc_t[GDN_V_HEADS + hv:GDN_V_HEADS + hv + 1, :]

        qk_act = act[:, :2 * qk_w]
        qk_n = qk_act * lax.rsqrt(_head_sums(qk_act * qk_act, dk) + EPS)
        for kh in range(GDN_K_HEADS):
            qn = qk_n[:, kh * dk:(kh + 1) * dk] * (dk ** -0.5)
            kn = qk_n[:, qk_w + kh * dk:qk_w + (kh + 1) * dk]
            kt = kn.T
            prod = _bdot(jnp.concatenate([kn, qn], axis=0), kt)
            qs[sq, kh], ks[sq, kh], kts[sq, kh] = qn, kn, kt
            kk[sq, kh], qk0[sq, kh] = prod[:L], prod[L:]

    def khead(it):
        return it[0], it[1] // rep

    outs = {}
    for g0 in range(0, len(items), GDN_CHAINS_PER_GROUP):
        group = items[g0:g0 + GDN_CHAINS_PER_GROUP]
        g_last = {it: gc_r[it][:, L - 1:L] for it in group}
        egc = {it: jnp.exp(gc_c[it]) for it in group}
        decay = {it: jnp.exp(jnp.where(lower, gc_c[it] - gc_r[it], NEG)) for it in group}
        a_mats = [jnp.where(strict, kk[khead(it)] * beta_c[it] * decay[it], 0.0) for it in group]
        t_inv = dict(zip(group, _unit_lower_inverses(a_mats)))

        sols = {}
        for it in group:
            sq, hv = it
            v_h = acts[sq][:, 2 * qk_w + hv * dk:2 * qk_w + (hv + 1) * dk]
            sols[it] = _bdot(t_inv[it], jnp.concatenate([v_h * beta_c[it],
                                                         ks[khead(it)] * (beta_c[it] * egc[it])], axis=1))

        states = {it: state_s[it[0], it[1]] for it in group}
        from_state = {it: _bdot(jnp.concatenate([sols[it][:, dk:], qs[khead(it)] * egc[it]], axis=0),
                                states[it]) for it in group}
        v_new = {it: sols[it][:, :dk] - from_state[it][:L] for it in group}
        from_v = {it: _bdot(jnp.concatenate([qk0[khead(it)] * decay[it],
                                             kts[khead(it)] * jnp.exp(g_last[it] - gc_r[it])], axis=0),
                            v_new[it]) for it in group}
        for it in group:
            state_s[it[0], it[1]] = states[it] * jnp.exp(g_last[it]) + from_v[it][L:]
            outs[it] = from_state[it][L:] + from_v[it][:L]
    for sq in range(nseq):
        o = jnp.concatenate([outs[sq, hv] for hv in range(GDN_V_HEADS)], axis=1)
        o = o * lax.rsqrt(_head_sums(o * o, dk) * (1.0 / dk) + EPS) * nw_ref[...]
        out_ref[sq] = (o * _silu(z_ref[sq].astype(F32))).astype(BF16)


def _gdn(qkv, z, ba, conv_w, dt_bias_pad, a_log_pad, norm_w):
    b, s, cd = qkv.shape
    vw = z.shape[2]
    nc = s // GDN_CHUNK
    nseq = GDN_SEQS_PER_STEP
    return pl.pallas_call(
        _gdn_kernel,
        grid=(b // nseq, nc),
        in_specs=[
            pl.BlockSpec((nseq, GDN_CHUNK, cd), lambda bi, ci: (bi, ci, 0)),
            pl.BlockSpec((nseq, GDN_CHUNK, vw), lambda bi, ci: (bi, ci, 0)),
            pl.BlockSpec((nseq, GDN_CHUNK, LANES), lambda bi, ci: (bi, ci, 0)),
            _const_spec(conv_w.shape), _const_spec(dt_bias_pad.shape),
            _const_spec(a_log_pad.shape), _const_spec(norm_w.shape),
        ],
        out_specs=pl.BlockSpec((nseq, GDN_CHUNK, vw), lambda bi, ci: (bi, ci, 0)),
        out_shape=jax.ShapeDtypeStruct((b, s, vw), BF16),
        scratch_shapes=[
            pltpu.VMEM((nseq, CONV_HALO, cd), F32),
            pltpu.VMEM((nseq, GDN_V_HEADS, GDN_DIM, GDN_DIM), F32),
        ],
        compiler_params=_params(("parallel", "arbitrary")),
        name="gated_deltanet",
    )(qkv, z, ba, conv_w, dt_bias_pad, a_log_pad, norm_w)


def _pad_lanes(v, offset=0):
    out = jnp.zeros((1, LANES), F32)
    return out.at[0, offset:offset + v.shape[0]].set(v.astype(F32))


def _row(v):
    return v.astype(F32).reshape(1, -1)


def _even_mixer(x2, batch, seq, norm_w, w_in, q_norm, k_norm, conv_w, conv_b, dt_bias, a_log, d_skip,
                ssm_norm, w_out, slopes):
    mw = MOBA_HEADS * MOBA_HEAD_DIM
    inner = SSM_HEADS * SSM_HEAD_DIM
    conv_dim = inner + 2 * SSM_GROUPS * SSM_STATE
    w = w_in.astype(BF16)
    c0, c1, c2, c3, c4 = mw, 2 * mw, 3 * mw, 3 * mw + inner, 3 * mw + inner + conv_dim
    wdt = jnp.pad(w[:, c4:], ((0, 0), (0, LANES - SSM_HEADS)))
    q, k, v, z, xbc, dt = _inproj_even(x2, _row(norm_w), w[:, :c0], w[:, c0:c1], w[:, c1:c2],
                                       w[:, c2:c3], w[:, c3:c4], wdt)
    attn = _moba(q.reshape(batch, seq, mw), k.reshape(batch, seq, mw), v.reshape(batch, seq, mw),
                 _row(jnp.tile(q_norm, MOBA_HEADS_PER_STEP)), _row(jnp.tile(k_norm, MOBA_HEADS_PER_STEP)),
                 slopes).reshape(batch * seq, mw)

    ssm = _ssd(xbc.reshape(batch, seq, conv_dim), z.reshape(batch, seq, inner),
               dt.reshape(batch, seq, LANES), conv_w.astype(F32), _row(conv_b),
               _pad_lanes(dt_bias), _pad_lanes(a_log), _row(jnp.repeat(d_skip, SSM_HEAD_DIM)),
               _row(ssm_norm))
    wo = w_out.astype(BF16)
    return _outproj_even(x2, attn, ssm.reshape(batch * seq, inner), wo[:mw], wo[mw:])


def _gdn_mixer(x2, batch, seq, norm_w, w_in, conv_w, dt_bias, a_log, gdn_norm, w_out):
    conv_dim = 2 * GDN_K_HEADS * GDN_DIM + GDN_V_HEADS * GDN_DIM
    vw = GDN_V_HEADS * GDN_DIM
    w = w_in.astype(BF16)
    wba = jnp.pad(w[:, conv_dim + vw:], ((0, 0), (0, LANES - 2 * GDN_V_HEADS)))
    qkv, z, ba = _inproj_odd(x2, _row(norm_w), w[:, :conv_dim], w[:, conv_dim:conv_dim + vw], wba)
    o = _gdn(qkv.reshape(batch, seq, conv_dim), z.reshape(batch, seq, vw), ba.reshape(batch, seq, LANES),
             conv_w.astype(F32), _pad_lanes(dt_bias, GDN_V_HEADS), _pad_lanes(a_log, GDN_V_HEADS),
             _row(jnp.tile(gdn_norm, GDN_V_HEADS)))
    return _outproj_odd(x2, o.reshape(batch * seq, vw), w_out.astype(BF16))


def kernel(x, p, norm_mix, norm_ffn, norm_ple, w_in_even, moba_q_norm, moba_k_norm, ssm_conv_w, ssm_conv_b, ssm_dt_bias, ssm_a_log, ssm_d, ssm_norm, w_out_even, w_in_odd, gdn_conv_w, gdn_dt_bias, gdn_a_log, gdn_norm, w_out_odd, ffn_w_gate, ffn_w_up, ffn_conv_w, ffn_conv_b, ffn_w_down, ple_w_proj, ple_w_gate):
    batch, seq, d = x.shape
    depth = p.shape[0]
    slopes = jnp.exp2(-ALIBI_MAX_BIAS * jnp.arange(1, MOBA_HEADS + 1, dtype=F32) / MOBA_HEADS)
    x2 = x.reshape(batch * seq, d)
    for i in range(depth):
        j = i // 2
        if i % 2 == 0:
            x2 = _even_mixer(x2, batch, seq, norm_mix[i], w_in_even[j], moba_q_norm[j], moba_k_norm[j],
                             ssm_conv_w[j], ssm_conv_b[j], ssm_dt_bias[j], ssm_a_log[j], ssm_d[j],
                             ssm_norm[j], w_out_even[j], slopes)
        else:
            x2 = _gdn_mixer(x2, batch, seq, norm_mix[i], w_in_odd[j], gdn_conv_w[j], gdn_dt_bias[j],
                            gdn_a_log[j], gdn_norm[j], w_out_odd[j])
        x2 = _ffn_ple(x2, p[i].reshape(batch * seq, -1), seq, _row(norm_ffn[i]),
                      ffn_w_gate[i].astype(BF16), ffn_w_up[i].astype(BF16), ffn_conv_w[i].astype(F32),
                      _row(ffn_conv_b[i]), ffn_w_down[i].astype(BF16), _row(norm_ple[i]),
                      ple_w_gate[i].astype(BF16), ple_w_proj[i].astype(BF16))
    return x2.reshape(batch, seq, d)
```

```python
import functools

import jax
import jax.numpy as jnp
from jax import lax
from jax.experimental import pallas as pl
from jax.experimental.pallas import tpu as pltpu

F32 = jnp.float32
BF16 = jnp.bfloat16
HIGHEST = lax.Precision.HIGHEST

EPS = 1e-6
NEG = -1e30
LOG2E = 1.4426950408889634
LANES = 128
CONV_HALO = 8

MOBA_HEADS = 8
MOBA_HEAD_DIM = 64
MOBA_BLOCK = 256
MOBA_TOPK = 3
MOBA_HEADS_PER_STEP = 4
MOBA_ONES_ROWS = 16
ALIBI_MAX_BIAS = 8.0

SSM_HEADS = 16
SSM_HEAD_DIM = 64
SSM_STATE = 128
SSM_GROUPS = 2
SSM_CONV = 4
SSM_CHUNK = 128
SSM_SEQS_PER_STEP = 2

GDN_K_HEADS = 8
GDN_V_HEADS = 16
GDN_DIM = 128
GDN_CONV = 4
GDN_CHUNK = 64
GDN_SEQS_PER_STEP = 2
GDN_CHAINS_PER_GROUP = 32

FFN_CONV = 3
FFN_HALO = 16
FFN_COLS = 256

VMEM_LIMIT = 56 * 1024 * 1024


def _rms_rows(x, gain):
    return x * lax.rsqrt(jnp.mean(x * x, axis=-1, keepdims=True) + EPS) * gain


def _sigmoid(x):
    return 0.5 * jnp.tanh(0.5 * x) + 0.5


def _silu(x):
    half = 0.5 * x
    return half + half * jnp.tanh(half)


def _softplus(x):
    return jnp.maximum(x, 0.0) + jnp.log(1.0 + jnp.exp(-jnp.abs(x)))


def _dot(a, b):
    return jnp.dot(a, b, preferred_element_type=F32)


def _dot_nt(a, b):
    return lax.dot_general(a, b, (((1,), (1,)), ((), ())), preferred_element_type=F32)


def _bf16_terms(x):
    hi = x.astype(BF16)
    rest = x - hi.astype(F32)
    mid = rest.astype(BF16)
    lo = (rest - mid.astype(F32)).astype(BF16)
    return [hi, mid, lo]


def _select_rows_dot(sel, x):
    return _dot(jnp.concatenate([sel] * 3, axis=1), jnp.concatenate(_bf16_terms(x), axis=0))


def _select_cols_dot(x, sel):
    return _dot(jnp.concatenate(_bf16_terms(x), axis=1), jnp.concatenate([sel] * 3, axis=0))


def _head_sums(x, head_w):
    rows, width = x.shape
    slab = 2 * LANES
    n = width // slab
    same = (lax.broadcasted_iota(jnp.int32, (slab, slab), 0) // head_w
            == lax.broadcasted_iota(jnp.int32, (slab, slab), 1) // head_w)
    ones = jnp.where(same, 1.0, 0.0).astype(BF16)
    stacked = jnp.concatenate([x[:, i * slab:(i + 1) * slab] for i in range(n)], axis=0)
    sums = _dot(stacked.astype(BF16), ones)
    return jnp.concatenate([sums[i * rows:(i + 1) * rows] for i in range(n)], axis=1)


def _const_spec(shape):
    zeros = (0,) * len(shape)
    return pl.BlockSpec(shape, lambda *_: zeros, pipeline_mode=pl.Buffered(1))


def _params(semantics):
    return pltpu.CompilerParams(dimension_semantics=semantics, vmem_limit_bytes=VMEM_LIMIT)


def _causal_conv(ext, w_ref, width, halo):
    y = w_ref[width - 1:width, :] * ext[halo:]
    for shift in range(1, width):
        y = y + w_ref[width - 1 - shift:width - shift, :] * pltpu.roll(ext, shift, axis=0)[halo:]
    return y


def _inproj_even_kernel(x_ref, g_ref, wq_ref, wk_ref, wv_ref, wz_ref, wxbc_ref, wdt_ref,
                        q_ref, k_ref, v_ref, z_ref, xbc_ref, dt_ref):
    hn = _rms_rows(x_ref[...], g_ref[...]).astype(BF16)
    q_ref[...] = _dot(hn, wq_ref[...]).astype(BF16)
    k_ref[...] = _dot(hn, wk_ref[...]).astype(BF16)
    v_ref[...] = _dot(hn, wv_ref[...]).astype(BF16)
    z_ref[...] = _dot(hn, wz_ref[...]).astype(BF16)
    xbc_ref[...] = _dot(hn, wxbc_ref[...]).astype(BF16)
    dt_ref[...] = _dot(hn, wdt_ref[...])


def _inproj_even(x2, gain, wq, wk, wv, wz, wxbc, wdt, tm=512):
    t, d = x2.shape
    outs = [(wq.shape[1], BF16), (wk.shape[1], BF16), (wv.shape[1], BF16),
            (wz.shape[1], BF16), (wxbc.shape[1], BF16), (wdt.shape[1], F32)]
    return pl.pallas_call(
        _inproj_even_kernel,
        grid=(t // tm,),
        in_specs=[pl.BlockSpec((tm, d), lambda i: (i, 0)), _const_spec(gain.shape)]
        + [_const_spec(w.shape) for w in (wq, wk, wv, wz, wxbc, wdt)],
        out_specs=[pl.BlockSpec((tm, n), lambda i: (i, 0)) for n, _ in outs],
        out_shape=[jax.ShapeDtypeStruct((t, n), dt) for n, dt in outs],
        compiler_params=_params(("parallel",)),
        name="inproj_even",
    )(x2, gain, wq, wk, wv, wz, wxbc, wdt)


def _moba_kernel(slopes_ref, q_ref, k_ref, v_ref, qg_ref, kg_ref, o_ref, kn_s, kbar_s, vt_s):
    hp = pl.program_id(1)
    seq = k_ref.shape[1]
    nb = seq // MOBA_BLOCK
    dh = MOBA_HEAD_DIM
    hps = MOBA_HEADS_PER_STEP
    heads = range(hps)

    lanes = hps * dh
    same_head = (lax.broadcasted_iota(jnp.int32, (lanes, lanes), 0) // dh
                 == lax.broadcasted_iota(jnp.int32, (lanes, lanes), 1) // dh)
    head_mean = jnp.where(same_head, 1.0 / dh, 0.0).astype(BF16)

    def head_rms(x, gain):
        return x * lax.rsqrt(_dot((x * x).astype(BF16), head_mean) + EPS) * gain

    kn = head_rms(k_ref[0].astype(F32), kg_ref[...])
    kn_s[...] = kn.astype(BF16)
    kbar_s[...] = jnp.mean(kn.reshape(nb, MOBA_BLOCK, lanes), axis=1)
    vt_s[...] = v_ref[0].astype(F32).T.astype(BF16)

    lane_head = lax.broadcasted_iota(jnp.int32, (MOBA_BLOCK, lanes), 1) // dh
    blk = lax.broadcasted_iota(jnp.int32, (nb, MOBA_BLOCK), 0)
    kbar_head = lax.broadcasted_iota(jnp.int32, (nb, lanes), 1) // dh
    kbar = kbar_s[...]
    kbar_rows = jnp.concatenate([jnp.where(kbar_head == hh, kbar, 0.0) for hh in heads], axis=0)
    kidx = lax.broadcasted_iota(jnp.int32, (MOBA_BLOCK, MOBA_BLOCK), 0)
    qidx = lax.broadcasted_iota(jnp.int32, (MOBA_BLOCK, MOBA_BLOCK), 1)
    rel = (kidx - qidx).astype(F32)
    causal = rel <= 0.0
    slopes2 = [slopes_ref[hp * hps + hh] * LOG2E for hh in heads]
    q_gain = qg_ref[...] * (dh ** -0.5 * LOG2E)
    alibis = [slope2 * rel for slope2 in slopes2]

    def score_stage(own):
        rows = slice(own * MOBA_BLOCK, (own + 1) * MOBA_BLOCK)
        qn = head_rms(q_ref[0, rows, :].astype(F32), q_gain)
        qs = [jnp.where(lane_head == hh, qn, 0.0).astype(BF16) for hh in heads]
        scores = {(hh, j): _dot_nt(kn_s[j * MOBA_BLOCK:(j + 1) * MOBA_BLOCK, :], qs[hh])
                  for hh in heads for j in range(own + 1)}
        gates = lax.dot_general(kbar_rows, qn, (((1,), (1,)), ((), ())),
                                precision=HIGHEST, preferred_element_type=F32)
        selb = []
        for hh in heads:
            gate = gates[hh * nb:(hh + 1) * nb, :]
            rank = jnp.zeros(gate.shape, F32)
            for m in range(own):
                gm = gate[m:m + 1, :]
                beats = (gm > gate) | ((gm == gate) & (m < blk))
                rank = rank + jnp.where(beats, 1.0, 0.0)
            selected = (blk < own) & (rank < float(min(MOBA_TOPK, nb - 1)))
            selb.append(jnp.where(selected, 0.0, NEG))
        return scores, selb

    def softmax_pv_stage(own, scores, selb):
        rows = slice(own * MOBA_BLOCK, (own + 1) * MOBA_BLOCK)
        probs = []
        for hh in heads:
            tiles = [scores[hh, j] + alibis[hh] for j in range(own)]
            tiles.append(jnp.where(causal, scores[hh, own] + alibis[hh], NEG))
            col_bias = [selb[hh][j:j + 1, :] + slopes2[hh] * float((j - own) * MOBA_BLOCK) for j in range(own)]
            col_bias.append(jnp.zeros((1, MOBA_BLOCK), F32))
            m = jnp.max(tiles[0], axis=0, keepdims=True) + col_bias[0]
            for t, cb in zip(tiles[1:], col_bias[1:]):
                m = jnp.maximum(m, jnp.max(t, axis=0, keepdims=True) + cb)
            probs.append(jnp.concatenate([jnp.exp2(t + (cb - m)).astype(BF16) for t, cb in zip(tiles, col_bias)],
                                         axis=0))
        keys = (own + 1) * MOBA_BLOCK
        ones_rows = jnp.ones((MOBA_ONES_ROWS, keys), BF16)
        pv = [_dot(jnp.concatenate([vt_s[hh * dh:(hh + 1) * dh, 0:keys], ones_rows], axis=0), probs[hh])
              for hh in heads]
        outs = [r[:dh] / r[dh:dh + 1] for r in pv]
        o_ref[0, rows, :] = jnp.concatenate(outs, axis=0).T.astype(BF16)

    staged = score_stage(0)
    for own in range(nb):
        upcoming = score_stage(own + 1) if own + 1 < nb else None
        softmax_pv_stage(own, *staged)
        staged = upcoming


def _moba(q, k, v, q_gain, k_gain, slopes):
    b, s, width = q.shape
    dh = MOBA_HEAD_DIM
    hps = MOBA_HEADS_PER_STEP
    lanes = hps * dh
    nb = s // MOBA_BLOCK
    seq_spec = pl.BlockSpec((1, s, lanes), lambda bi, hi: (bi, 0, hi))
    return pl.pallas_call(
        _moba_kernel,
        grid=(b, width // lanes),
        in_specs=[
            pl.BlockSpec(memory_space=pltpu.SMEM),
            seq_spec, seq_spec, seq_spec,
            pl.BlockSpec((1, lanes), lambda bi, hi: (0, 0)),
            pl.BlockSpec((1, lanes), lambda bi, hi: (0, 0)),
        ],
        out_specs=seq_spec,
        out_shape=jax.ShapeDtypeStruct((b, s, width), BF16),
        scratch_shapes=[
            pltpu.VMEM((s, lanes), BF16),
            pltpu.VMEM((nb, lanes), F32),
            pltpu.VMEM((lanes, s), BF16),
        ],
        compiler_params=_params(("parallel", "parallel")),
        name="moba_attention",
    )(slopes, q, k, v, q_gain, k_gain)


def _ssd_kernel(xbc_ref, z_ref, dt_ref, cw_ref, cb_ref, dtb_ref, alog_ref, dskip_ref, nw_ref,
                out_ref, hist_s, state_s):
    c = pl.program_id(1)
    L = SSM_CHUNK
    inner = SSM_HEADS * SSM_HEAD_DIM
    gw = SSM_STATE
    pair_w = 2 * SSM_HEAD_DIM

    @pl.when(c == 0)
    def _():
        hist_s[...] = jnp.zeros_like(hist_s)
        state_s[...] = jnp.zeros_like(state_s)

    row = lax.broadcasted_iota(jnp.int32, (L, L), 0)
    col = lax.broadcasted_iota(jnp.int32, (L, L), 1)
    causal = row >= col
    causal_ones = jnp.where(causal, 1.0, 0.0).astype(BF16)
    hrow = lax.broadcasted_iota(jnp.int32, (LANES, inner), 0)
    hcol = lax.broadcasted_iota(jnp.int32, (LANES, inner), 1)
    widen = jnp.where(hcol // SSM_HEAD_DIM == hrow, 1.0, 0.0).astype(BF16)
    lane = lax.broadcasted_iota(jnp.int32, (L, pair_w), 1)
    first_head = lane < SSM_HEAD_DIM
    heads_per_group = SSM_HEADS // SSM_GROUPS
    gsz = inner // SSM_GROUPS

    for sq in range(xbc_ref.shape[0]):
        cur = xbc_ref[sq].astype(F32)
        ext = jnp.concatenate([hist_s[sq], cur], axis=0)
        hist_s[sq] = cur[L - CONV_HALO:, :]
        act = _silu(_causal_conv(ext, cw_ref, SSM_CONV, CONV_HALO) + cb_ref[...])
        xs = act[:, :inner]

        dt = _softplus(dt_ref[sq] + dtb_ref[...])
        a = dt * (-jnp.exp(alog_ref[...]) * LOG2E)
        acum = _select_rows_dot(causal_ones, a)
        acum_t = acum.T
        widened = _select_cols_dot(jnp.concatenate([dt, acum], axis=0), widen)
        dt_w = widened[:L]
        acum_w = widened[L:]
        last_w = acum_w[L - 1:L, :]
        xdt = xs * dt_w
        xdt_bf = xdt.astype(BF16)
        xdt_dec_bf = (xdt * jnp.exp2(last_w - acum_w)).astype(BF16)
        exp_acum_w = jnp.exp2(acum_w)
        chunk_decay_w = jnp.exp2(last_w)

        ys = []
        for g in range(SSM_GROUPS):
            bg = act[:, inner + g * gw:inner + (g + 1) * gw]
            cg = act[:, inner + SSM_GROUPS * gw + g * gw:inner + SSM_GROUPS * gw + (g + 1) * gw]
            cg_bf = cg.astype(BF16)
            cb = _dot_nt(cg_bf, bg.astype(BF16))
            bg_t = bg.T.astype(BF16)
            for pp in range(heads_per_group // 2):
                h0 = g * heads_per_group + 2 * pp
                pi = h0 // 2
                sl = slice(h0 * SSM_HEAD_DIM, h0 * SSM_HEAD_DIM + pair_w)
                yd = []
                for hh in (h0, h0 + 1):
                    diff = jnp.broadcast_to(acum[:, hh:hh + 1], (L, L)) - acum_t[hh:hh + 1, :]
                    decay = jnp.exp2(jnp.where(causal, diff, NEG))
                    yd.append(_dot((cb * decay).astype(BF16), xdt_bf[:, sl]))
                y_diag = jnp.where(first_head, yd[0], yd[1])
                prev = state_s[sq, pi]
                y_off = _dot(cg_bf, prev.astype(BF16)) * exp_acum_w[:, sl]
                state_s[sq, pi] = prev * chunk_decay_w[:, sl] + _dot(bg_t, xdt_dec_bf[:, sl])
                ys.append(y_diag + y_off + dskip_ref[:, sl] * xs[:, sl])
        y = jnp.concatenate(ys, axis=1)

        yg = y * _silu(z_ref[sq].astype(F32))
        parts = []
        for g in range(SSM_GROUPS):
            seg = yg[:, g * gsz:(g + 1) * gsz]
            parts.append(seg * lax.rsqrt(jnp.mean(seg * seg, axis=-1, keepdims=True) + EPS))
        out_ref[sq] = (jnp.concatenate(parts, axis=1) * nw_ref[...]).astype(BF16)


def _ssd(xbc, z, dt, conv_w, conv_b, dt_bias, a_log, d_skip_w, norm_w):
    b, s, cd = xbc.shape
    inner = z.shape[2]
    nc = s // SSM_CHUNK
    nseq = SSM_SEQS_PER_STEP
    return pl.pallas_call(
        _ssd_kernel,
        grid=(b // nseq, nc),
        in_specs=[
            pl.BlockSpec((nseq, SSM_CHUNK, cd), lambda bi, ci: (bi, ci, 0)),
            pl.BlockSpec((nseq, SSM_CHUNK, inner), lambda bi, ci: (bi, ci, 0)),
            pl.BlockSpec((nseq, SSM_CHUNK, LANES), lambda bi, ci: (bi, ci, 0)),
            _const_spec(conv_w.shape), _const_spec(conv_b.shape), _const_spec(dt_bias.shape),
            _const_spec(a_log.shape), _const_spec(d_skip_w.shape), _const_spec(norm_w.shape),
        ],
        out_specs=pl.BlockSpec((nseq, SSM_CHUNK, inner), lambda bi, ci: (bi, ci, 0)),
        out_shape=jax.ShapeDtypeStruct((b, s, inner), BF16),
        scratch_shapes=[
            pltpu.VMEM((nseq, CONV_HALO, cd), F32),
            pltpu.VMEM((nseq, SSM_HEADS // 2, SSM_STATE, 2 * SSM_HEAD_DIM), F32),
        ],
        compiler_params=_params(("parallel", "arbitrary")),
        name="ssd_heads",
    )(xbc, z, dt, conv_w, conv_b, dt_bias, a_log, d_skip_w, norm_w)


def _outproj_even_kernel(x_ref, a_ref, s_ref, wa_ref, ws_ref, o_ref):
    o_ref[...] = x_ref[...] + _dot(a_ref[...], wa_ref[...]) + _dot(s_ref[...], ws_ref[...])


def _outproj_even(x2, attn, ssm, wa, ws, tm=1024):
    t, d = x2.shape
    return pl.pallas_call(
        _outproj_even_kernel,
        grid=(t // tm,),
        in_specs=[pl.BlockSpec((tm, d), lambda i: (i, 0)),
                  pl.BlockSpec((tm, attn.shape[1]), lambda i: (i, 0)),
                  pl.BlockSpec((tm, ssm.shape[1]), lambda i: (i, 0)),
                  _const_spec(wa.shape), _const_spec(ws.shape)],
        out_specs=pl.BlockSpec((tm, d), lambda i: (i, 0)),
        out_shape=jax.ShapeDtypeStruct((t, d), F32),
        compiler_params=_params(("parallel",)),
        name="outproj_even",
    )(x2, attn, ssm, wa, ws)


def _outproj_odd_kernel(x_ref, o_in_ref, w_ref, o_ref):
    o_ref[...] = x_ref[...] + _dot(o_in_ref[...], w_ref[...])


def _outproj_odd(x2, o_in, w, tm=1024):
    t, d = x2.shape
    return pl.pallas_call(
        _outproj_odd_kernel,
        grid=(t // tm,),
        in_specs=[pl.BlockSpec((tm, d), lambda i: (i, 0)),
                  pl.BlockSpec((tm, o_in.shape[1]), lambda i: (i, 0)),
                  _const_spec(w.shape)],
        out_specs=pl.BlockSpec((tm, d), lambda i: (i, 0)),
        out_shape=jax.ShapeDtypeStruct((t, d), F32),
        compiler_params=_params(("parallel",)),
        name="outproj_odd",
    )(x2, o_in, w)


def _ffn_ple_kernel(x_ref, halo_ref, p_ref, gf_ref, wg_ref, wu_ref, cw_ref, cb_ref, wd_ref,
                    gp_ref, wpg_ref, wpp_ref, o_ref, *, tm, seq):
    i = pl.program_id(0)
    x = x_ref[...]
    keep_halo = jnp.where((i * tm) % seq == 0, 0.0, 1.0)
    halo = halo_ref[...] * keep_halo
    hn = _rms_rows(jnp.concatenate([halo, x], axis=0), gf_ref[...]).astype(BF16)
    hn_cur = hn[FFN_HALO:]

    d_ff = wg_ref.shape[1]
    acts = []
    for j in range(d_ff // FFN_COLS):
        cs = slice(j * FFN_COLS, (j + 1) * FFN_COLS)
        g = _dot(hn, wg_ref[:, cs])
        gc = cb_ref[:, cs] + cw_ref[FFN_CONV - 1:FFN_CONV, cs] * g[FFN_HALO:]
        for shift in range(1, FFN_CONV):
            gc = gc + cw_ref[FFN_CONV - 1 - shift:FFN_CONV - shift, cs] * pltpu.roll(g, shift, axis=0)[FFN_HALO:]
        u = _dot(hn_cur, wu_ref[:, cs])
        acts.append((_silu(gc) * u).astype(BF16))
    x1 = x + _dot(jnp.concatenate(acts, axis=1), wd_ref[...])
    hp = _rms_rows(x1, gp_ref[...]).astype(BF16)
    gate = _sigmoid(_dot(hp, wpg_ref[...]))
    o_ref[...] = x1 + gate * _dot(p_ref[...].astype(BF16), wpp_ref[...])


def _ffn_ple(x2, p2, seq, gain_ffn, wg, wu, conv_w, conv_b, wd, gain_ple, wpg, wpp, tm=1024):
    t, d = x2.shape
    halo_blocks = tm // FFN_HALO
    consts = (gain_ffn, wg, wu, conv_w, conv_b, wd, gain_ple, wpg, wpp)
    return pl.pallas_call(
        functools.partial(_ffn_ple_kernel, tm=tm, seq=seq),
        grid=(t // tm,),
        in_specs=[pl.BlockSpec((tm, d), lambda i: (i, 0)),
                  pl.BlockSpec((FFN_HALO, d), lambda i: (jnp.maximum(i * halo_blocks - 1, 0), 0)),
                  pl.BlockSpec((tm, p2.shape[1]), lambda i: (i, 0))]
        + [_const_spec(c.shape) for c in consts],
        out_specs=pl.BlockSpec((tm, d), lambda i: (i, 0)),
        out_shape=jax.ShapeDtypeStruct((t, d), F32),
        compiler_params=_params(("parallel",)),
        name="ffn_ple",
    )(x2, x2, p2, *consts)


def _inproj_odd_kernel(x_ref, g_ref, wqkv_ref, wz_ref, wba_ref, qkv_ref, z_ref, ba_ref):
    hn = _rms_rows(x_ref[...], g_ref[...]).astype(BF16)
    qkv_ref[...] = _dot(hn, wqkv_ref[...]).astype(BF16)
    z_ref[...] = _dot(hn, wz_ref[...]).astype(BF16)
    ba_ref[...] = _dot(hn, wba_ref[...])


def _inproj_odd(x2, gain, wqkv, wz, wba, tm=512):
    t, d = x2.shape
    outs = [(wqkv.shape[1], BF16), (wz.shape[1], BF16), (wba.shape[1], F32)]
    return pl.pallas_call(
        _inproj_odd_kernel,
        grid=(t // tm,),
        in_specs=[pl.BlockSpec((tm, d), lambda i: (i, 0)), _const_spec(gain.shape)]
        + [_const_spec(w.shape) for w in (wqkv, wz, wba)],
        out_specs=[pl.BlockSpec((tm, n), lambda i: (i, 0)) for n, _ in outs],
        out_shape=[jax.ShapeDtypeStruct((t, n), dt) for n, dt in outs],
        compiler_params=_params(("parallel",)),
        name="inproj_odd",
    )(x2, gain, wqkv, wz, wba)


def _bdot(a, b):
    return _dot(a.astype(BF16), b.astype(BF16))


def _unit_lower_inverses(mats):
    L = mats[0].shape[0]
    row = lax.broadcasted_iota(jnp.int32, (L, L), 0)
    col = lax.broadcasted_iota(jnp.int32, (L, L), 1)
    eye = jnp.where(row == col, 1.0, 0.0)
    base = 8
    same = (row // base) == (col // base)
    ns = [-jnp.where(same, a, 0.0) for a in mats]
    ts = [eye + n for n in ns]
    ns_bf = [n.astype(BF16) for n in ns]
    pws_bf = [_dot(n, n).astype(BF16) for n in ns_bf]
    ts = [t + _dot(t.astype(BF16), pw) for t, pw in zip(ts, pws_bf)]
    pws_bf = [_dot(pw, pw).astype(BF16) for pw in pws_bf]
    ts = [t + _dot(t.astype(BF16), pw) for t, pw in zip(ts, pws_bf)]
    size = base
    while size < L:
        wider = (row // (2 * size)) == (col // (2 * size))
        offs = [jnp.where(wider & jnp.logical_not(same), a, 0.0).astype(BF16) for a in mats]
        ts_bf = [t.astype(BF16) for t in ts]
        ys = [_dot(off, t).astype(BF16) for off, t in zip(offs, ts_bf)]
        ts = [t - _dot(t_bf, y) for t, t_bf, y in zip(ts, ts_bf, ys)]
        same = wider
        size *= 2
    return ts


def _gdn_kernel(qkv_ref, z_ref, ba_ref, cw_ref, dtb_ref, alog_ref, nw_ref, out_ref, hist_s, state_s):
    c = pl.program_id(1)
    L = GDN_CHUNK
    dk = GDN_DIM
    qk_w = GDN_K_HEADS * dk
    rep = GDN_V_HEADS // GDN_K_HEADS
    nseq = qkv_ref.shape[0]

    @pl.when(c == 0)
    def _():
        hist_s[...] = jnp.zeros_like(hist_s)
        state_s[...] = jnp.zeros_like(state_s)

    row = lax.broadcasted_iota(jnp.int32, (L, L), 0)
    col = lax.broadcasted_iota(jnp.int32, (L, L), 1)
    lower = row >= col
    strict = row > col

    items = [(sq, hv) for sq in range(nseq) for hv in range(GDN_V_HEADS)]
    acts, qs, ks, kts, kk, qk0 = {}, {}, {}, {}, {}, {}
    beta_c, gc_c, gc_r = {}, {}, {}
    for sq in range(nseq):
        cur = qkv_ref[sq].astype(F32)
        ext = jnp.concatenate([hist_s[sq], cur], axis=0)
        hist_s[sq] = cur[L - CONV_HALO:, :]
        act = _silu(_causal_conv(ext, cw_ref, GDN_CONV, CONV_HALO))
        acts[sq] = act

        ba = ba_ref[sq]
        beta = _sigmoid(ba)
        g = (-jnp.exp(alog_ref[...]) * LOG2E) * _softplus(ba + dtb_ref[...])
        gc = _select_rows_dot(jnp.where(lower, 1.0, 0.0).astype(BF16), g)
        gc_t = gc.T
        for hv in range(GDN_V_HEADS):
            beta_c[sq, hv] = beta[:, hv:hv + 1]
            gc_c[sq, hv] = gc[:, GDN_V_HEADS + hv:GDN_V_HEADS + hv + 1]
            gc_r[sq, hv] = gc_t[GDN_V_HEADS + hv:GDN_V_HEADS + hv + 1, :]

        qk_act = act[:, :2 * qk_w]
        qk_n = qk_act * lax.rsqrt(_head_sums(qk_act * qk_act, dk) + EPS)
        for kh in range(GDN_K_HEADS):
            qn = qk_n[:, kh * dk:(kh + 1) * dk] * (dk ** -0.5)
            kn = qk_n[:, qk_w + kh * dk:qk_w + (kh + 1) * dk]
            kt = kn.T
            prod = _bdot(jnp.concatenate([kn, qn], axis=0), kt)
            qs[sq, kh], ks[sq, kh], kts[sq, kh] = qn, kn, kt
            kk[sq, kh], qk0[sq, kh] = prod[:L], prod[L:]

    def khead(it):
        return it[0], it[1] // rep

    outs = {}
    for g0 in range(0, len(items), GDN_CHAINS_PER_GROUP):
        group = items[g0:g0 + GDN_CHAINS_PER_GROUP]
        g_last = {it: gc_r[it][:, L - 1:L] for it in group}
        egc = {it: jnp.exp2(gc_c[it]) for it in group}
        decay = {it: jnp.exp2(jnp.where(lower, gc_c[it] - gc_r[it], NEG)) for it in group}
        a_mats = [jnp.where(strict, kk[khead(it)] * beta_c[it] * decay[it], 0.0) for it in group]
        t_inv = dict(zip(group, _unit_lower_inverses(a_mats)))

        sols = {}
        for it in group:
            sq, hv = it
            v_h = acts[sq][:, 2 * qk_w + hv * dk:2 * qk_w + (hv + 1) * dk]
            sols[it] = _bdot(t_inv[it], jnp.concatenate([v_h * beta_c[it],
                                                         ks[khead(it)] * (beta_c[it] * egc[it])], axis=1))

        states = {it: state_s[it[0], it[1]] for it in group}
        from_state = {it: _bdot(jnp.concatenate([sols[it][:, dk:], qs[khead(it)] * egc[it]], axis=0),
                                states[it]) for it in group}
        v_new = {it: sols[it][:, :dk] - from_state[it][:L] for it in group}
        from_v = {it: _bdot(jnp.concatenate([qk0[khead(it)] * decay[it],
                                             kts[khead(it)] * jnp.exp2(g_last[it] - gc_r[it])], axis=0),
                            v_new[it]) for it in group}
        for it in group:
            state_s[it[0], it[1]] = states[it] * jnp.exp2(g_last[it]) + from_v[it][L:]
            outs[it] = from_state[it][L:] + from_v[it][:L]
    for sq in range(nseq):
        o = jnp.concatenate([outs[sq, hv] for hv in range(GDN_V_HEADS)], axis=1)
        o = o * lax.rsqrt(_head_sums(o * o, dk) * (1.0 / dk) + EPS) * nw_ref[...]
        out_ref[sq] = (o * _silu(z_ref[sq].astype(F32))).astype(BF16)


def _gdn(qkv, z, ba, conv_w, dt_bias_pad, a_log_pad, norm_w):
    b, s, cd = qkv.shape
    vw = z.shape[2]
    nc = s // GDN_CHUNK
    nseq = GDN_SEQS_PER_STEP
    return pl.pallas_call(
        _gdn_kernel,
        grid=(b // nseq, nc),
        in_specs=[
            pl.BlockSpec((nseq, GDN_CHUNK, cd), lambda bi, ci: (bi, ci, 0)),
            pl.BlockSpec((nseq, GDN_CHUNK, vw), lambda bi, ci: (bi, ci, 0)),
            pl.BlockSpec((nseq, GDN_CHUNK, LANES), lambda bi, ci: (bi, ci, 0)),
            _const_spec(conv_w.shape), _const_spec(dt_bias_pad.shape),
            _const_spec(a_log_pad.shape), _const_spec(norm_w.shape),
        ],
        out_specs=pl.BlockSpec((nseq, GDN_CHUNK, vw), lambda bi, ci: (bi, ci, 0)),
        out_shape=jax.ShapeDtypeStruct((b, s, vw), BF16),
        scratch_shapes=[
            pltpu.VMEM((nseq, CONV_HALO, cd), F32),
            pltpu.VMEM((nseq, GDN_V_HEADS, GDN_DIM, GDN_DIM), F32),
        ],
        compiler_params=_params(("parallel", "arbitrary")),
        name="gated_deltanet",
    )(qkv, z, ba, conv_w, dt_bias_pad, a_log_pad, norm_w)


def _pad_lanes(v, offset=0):
    out = jnp.zeros((1, LANES), F32)
    return out.at[0, offset:offset + v.shape[0]].set(v.astype(F32))


def _row(v):
    return v.astype(F32).reshape(1, -1)


def _even_mixer(x2, batch, seq, norm_w, w_in, q_norm, k_norm, conv_w, conv_b, dt_bias, a_log, d_skip,
                ssm_norm, w_out, slopes):
    mw = MOBA_HEADS * MOBA_HEAD_DIM
    inner = SSM_HEADS * SSM_HEAD_DIM
    conv_dim = inner + 2 * SSM_GROUPS * SSM_STATE
    w = w_in.astype(BF16)
    c0, c1, c2, c3, c4 = mw, 2 * mw, 3 * mw, 3 * mw + inner, 3 * mw + inner + conv_dim
    wdt = jnp.pad(w[:, c4:], ((0, 0), (0, LANES - SSM_HEADS)))
    q, k, v, z, xbc, dt = _inproj_even(x2, _row(norm_w), w[:, :c0], w[:, c0:c1], w[:, c1:c2],
                                       w[:, c2:c3], w[:, c3:c4], wdt)
    attn = _moba(q.reshape(batch, seq, mw), k.reshape(batch, seq, mw), v.reshape(batch, seq, mw),
                 _row(jnp.tile(q_norm, MOBA_HEADS_PER_STEP)), _row(jnp.tile(k_norm, MOBA_HEADS_PER_STEP)),
                 slopes).reshape(batch * seq, mw)

    ssm = _ssd(xbc.reshape(batch, seq, conv_dim), z.reshape(batch, seq, inner),
               dt.reshape(batch, seq, LANES), conv_w.astype(F32), _row(conv_b),
               _pad_lanes(dt_bias), _pad_lanes(a_log), _row(jnp.repeat(d_skip, SSM_HEAD_DIM)),
               _row(ssm_norm))
    wo = w_out.astype(BF16)
    return _outproj_even(x2, attn, ssm.reshape(batch * seq, inner), wo[:mw], wo[mw:])


def _gdn_mixer(x2, batch, seq, norm_w, w_in, conv_w, dt_bias, a_log, gdn_norm, w_out):
    conv_dim = 2 * GDN_K_HEADS * GDN_DIM + GDN_V_HEADS * GDN_DIM
    vw = GDN_V_HEADS * GDN_DIM
    w = w_in.astype(BF16)
    wba = jnp.pad(w[:, conv_dim + vw:], ((0, 0), (0, LANES - 2 * GDN_V_HEADS)))
    qkv, z, ba = _inproj_odd(x2, _row(norm_w), w[:, :conv_dim], w[:, conv_dim:conv_dim + vw], wba)
    o = _gdn(qkv.reshape(batch, seq, conv_dim), z.reshape(batch, seq, vw), ba.reshape(batch, seq, LANES),
             conv_w.astype(F32), _pad_lanes(dt_bias, GDN_V_HEADS), _pad_lanes(a_log, GDN_V_HEADS),
             _row(jnp.tile(gdn_norm, GDN_V_HEADS)))
    return _outproj_odd(x2, o.reshape(batch * seq, vw), w_out.astype(BF16))


def kernel(x, p, norm_mix, norm_ffn, norm_ple, w_in_even, moba_q_norm, moba_k_norm, ssm_conv_w, ssm_conv_b, ssm_dt_bias, ssm_a_log, ssm_d, ssm_norm, w_out_even, w_in_odd, gdn_conv_w, gdn_dt_bias, gdn_a_log, gdn_norm, w_out_odd, ffn_w_gate, ffn_w_up, ffn_conv_w, ffn_conv_b, ffn_w_down, ple_w_proj, ple_w_gate):
    batch, seq, d = x.shape
    depth = p.shape[0]
    slopes = jnp.exp2(-ALIBI_MAX_BIAS * jnp.arange(1, MOBA_HEADS + 1, dtype=F32) / MOBA_HEADS)
    x2 = x.reshape(batch * seq, d)
    for i in range(depth):
        j = i // 2
        if i % 2 == 0:
            x2 = _even_mixer(x2, batch, seq, norm_mix[i], w_in_even[j], moba_q_norm[j], moba_k_norm[j],
                             ssm_conv_w[j], ssm_conv_b[j], ssm_dt_bias[j], ssm_a_log[j], ssm_d[j],
                             ssm_norm[j], w_out_even[j], slopes)
        else:
            x2 = _gdn_mixer(x2, batch, seq, norm_mix[i], w_in_odd[j], gdn_conv_w[j], gdn_dt_bias[j],
                            gdn_a_log[j], gdn_norm[j], w_out_odd[j])
        x2 = _ffn_ple(x2, p[i].reshape(batch * seq, -1), seq, _row(norm_ffn[i]),
                      ffn_w_gate[i].astype(BF16), ffn_w_up[i].astype(BF16), ffn_conv_w[i].astype(F32),
                      _row(ffn_conv_b[i]), ffn_w_down[i].astype(BF16), _row(norm_ple[i]),
                      ple_w_gate[i].astype(BF16), ple_w_proj[i].astype(BF16))
    return x2.reshape(batch, seq, d)
```

```python
import functools

import jax
import jax.numpy as jnp
from jax import lax
from jax.experimental import pallas as pl
from jax.experimental.pallas import tpu as pltpu

F32 = jnp.float32
BF16 = jnp.bfloat16
HIGHEST = lax.Precision.HIGHEST

EPS = 1e-6
NEG = -1e30
LOG2E = 1.4426950408889634
LANES = 128
CONV_HALO = 8

MOBA_HEADS = 8
MOBA_HEAD_DIM = 64
MOBA_BLOCK = 256
MOBA_TOPK = 3
MOBA_HEADS_PER_STEP = 4
MOBA_ONES_ROWS = 16
ALIBI_MAX_BIAS = 8.0

SSM_HEADS = 16
SSM_HEAD_DIM = 64
SSM_STATE = 128
SSM_GROUPS = 2
SSM_CONV = 4
SSM_CHUNK = 128
SSM_SEQS_PER_STEP = 2

GDN_K_HEADS = 8
GDN_V_HEADS = 16
GDN_DIM = 128
GDN_CONV = 4
GDN_CHUNK = 64
GDN_SEQS_PER_STEP = 2
GDN_CHAINS_PER_GROUP = GDN_SEQS_PER_STEP * GDN_V_HEADS

FFN_CONV = 3
FFN_HALO = 16
FFN_COLS = 256

VMEM_LIMIT = 56 * 1024 * 1024


def _rms_rows(x, gain):
    return x * lax.rsqrt(jnp.mean(x * x, axis=-1, keepdims=True) + EPS) * gain


def _sigmoid(x):
    return 0.5 * jnp.tanh(0.5 * x) + 0.5


def _silu(x):
    half = 0.5 * x
    return half + half * jnp.tanh(half)


def _softplus(x):
    return jnp.maximum(x, 0.0) + jnp.log(1.0 + jnp.exp(-jnp.abs(x)))


def _dot(a, b):
    return jnp.dot(a, b, preferred_element_type=F32)


def _dot_nt(a, b):
    return lax.dot_general(a, b, (((1,), (1,)), ((), ())), preferred_element_type=F32)


def _bf16_terms(x):
    hi = x.astype(BF16)
    rest = x - hi.astype(F32)
    mid = rest.astype(BF16)
    lo = (rest - mid.astype(F32)).astype(BF16)
    return [hi, mid, lo]


def _select_rows_dot(sel, x):
    return _dot(jnp.concatenate([sel] * 3, axis=1), jnp.concatenate(_bf16_terms(x), axis=0))


def _select_cols_dot(x, sel):
    return _dot(jnp.concatenate(_bf16_terms(x), axis=1), jnp.concatenate([sel] * 3, axis=0))


def _head_sums(x, head_w):
    rows, width = x.shape
    slab = 2 * LANES
    n = width // slab
    same = (lax.broadcasted_iota(jnp.int32, (slab, slab), 0) // head_w
            == lax.broadcasted_iota(jnp.int32, (slab, slab), 1) // head_w)
    ones = jnp.where(same, 1.0, 0.0).astype(BF16)
    stacked = jnp.concatenate([x[:, i * slab:(i + 1) * slab] for i in range(n)], axis=0)
    sums = _dot(stacked.astype(BF16), ones)
    return jnp.concatenate([sums[i * rows:(i + 1) * rows] for i in range(n)], axis=1)


def _const_spec(shape):
    zeros = (0,) * len(shape)
    return pl.BlockSpec(shape, lambda *_: zeros, pipeline_mode=pl.Buffered(1))


def _params(semantics):
    return pltpu.CompilerParams(dimension_semantics=semantics, vmem_limit_bytes=VMEM_LIMIT)


def _causal_conv(ext, w_ref, width, halo):
    y = w_ref[width - 1:width, :] * ext[halo:]
    for shift in range(1, width):
        y = y + w_ref[width - 1 - shift:width - shift, :] * pltpu.roll(ext, shift, axis=0)[halo:]
    return y


def _project_columns(hn, w_ref, out_refs):
    c0 = 0
    for o_ref in out_refs:
        n = o_ref.shape[1]
        o_ref[...] = _dot(hn, w_ref[:, c0:c0 + n]).astype(BF16)
        c0 += n


def _inproj_even_kernel(x_ref, g_ref, w_ref, wdt_ref, q_ref, k_ref, v_ref, z_ref, xbc_ref, dt_ref):
    hn = _rms_rows(x_ref[...], g_ref[...]).astype(BF16)
    _project_columns(hn, w_ref, (q_ref, k_ref, v_ref, z_ref, xbc_ref))
    dt_ref[...] = _dot(hn, wdt_ref[...])


def _inproj_even(x2, gain, w, wdt, widths, tm=512):
    t, d = x2.shape
    outs = [(n, BF16) for n in widths] + [(wdt.shape[1], F32)]
    return pl.pallas_call(
        _inproj_even_kernel,
        grid=(t // tm,),
        in_specs=[pl.BlockSpec((tm, d), lambda i: (i, 0))] + [_const_spec(c.shape) for c in (gain, w, wdt)],
        out_specs=[pl.BlockSpec((tm, n), lambda i: (i, 0)) for n, _ in outs],
        out_shape=[jax.ShapeDtypeStruct((t, n), dt) for n, dt in outs],
        compiler_params=_params(("parallel",)),
        name="inproj_even",
    )(x2, gain, w, wdt)


def _moba_kernel(slopes_ref, q_ref, k_ref, v_ref, qg_ref, kg_ref, o_ref, kn_s, kbar_s, vt_s):
    hp = pl.program_id(1)
    seq = k_ref.shape[1]
    nb = seq // MOBA_BLOCK
    dh = MOBA_HEAD_DIM
    hps = MOBA_HEADS_PER_STEP
    heads = range(hps)

    lanes = hps * dh
    same_head = (lax.broadcasted_iota(jnp.int32, (lanes, lanes), 0) // dh
                 == lax.broadcasted_iota(jnp.int32, (lanes, lanes), 1) // dh)
    head_mean = jnp.where(same_head, 1.0 / dh, 0.0).astype(BF16)

    def head_rms(x, gain):
        return x * lax.rsqrt(_dot((x * x).astype(BF16), head_mean) + EPS) * gain

    kn = head_rms(k_ref[0].astype(F32), kg_ref[...])
    kn_s[...] = kn.astype(BF16)
    kbar_s[...] = jnp.mean(kn.reshape(nb, MOBA_BLOCK, lanes), axis=1)
    vt_s[...] = v_ref[0].astype(F32).T.astype(BF16)

    lane_head = lax.broadcasted_iota(jnp.int32, (MOBA_BLOCK, lanes), 1) // dh
    blk = lax.broadcasted_iota(jnp.int32, (nb, MOBA_BLOCK), 0)
    kbar_head = lax.broadcasted_iota(jnp.int32, (nb, lanes), 1) // dh
    kbar = kbar_s[...]
    kbar_rows = jnp.concatenate([jnp.where(kbar_head == hh, kbar, 0.0) for hh in heads], axis=0)
    kidx = lax.broadcasted_iota(jnp.int32, (MOBA_BLOCK, MOBA_BLOCK), 0)
    qidx = lax.broadcasted_iota(jnp.int32, (MOBA_BLOCK, MOBA_BLOCK), 1)
    rel = (kidx - qidx).astype(F32)
    causal = rel <= 0.0
    slopes2 = [slopes_ref[hp * hps + hh] * LOG2E for hh in heads]
    q_gain = qg_ref[...] * (dh ** -0.5 * LOG2E)
    alibis = [slope2 * rel for slope2 in slopes2]

    def score_stage(own):
        rows = slice(own * MOBA_BLOCK, (own + 1) * MOBA_BLOCK)
        qn = head_rms(q_ref[0, rows, :].astype(F32), q_gain)
        qs = [jnp.where(lane_head == hh, qn, 0.0).astype(BF16) for hh in heads]
        scores = {(hh, j): _dot_nt(kn_s[j * MOBA_BLOCK:(j + 1) * MOBA_BLOCK, :], qs[hh])
                  for hh in heads for j in range(own + 1)}
        gates = lax.dot_general(kbar_rows, qn, (((1,), (1,)), ((), ())),
                                precision=HIGHEST, preferred_element_type=F32)
        selb = []
        for hh in heads:
            gate = gates[hh * nb:(hh + 1) * nb, :]
            rank = jnp.zeros(gate.shape, F32)
            for m in range(own):
                gm = gate[m:m + 1, :]
                beats = (gm > gate) | ((gm == gate) & (m < blk))
                rank = rank + jnp.where(beats, 1.0, 0.0)
            selected = (blk < own) & (rank < float(min(MOBA_TOPK, nb - 1)))
            selb.append(jnp.where(selected, 0.0, NEG))
        return scores, selb

    def softmax_pv_stage(own, scores, selb):
        rows = slice(own * MOBA_BLOCK, (own + 1) * MOBA_BLOCK)
        probs = []
        for hh in heads:
            tiles = [scores[hh, j] + alibis[hh] for j in range(own)]
            tiles.append(jnp.where(causal, scores[hh, own] + alibis[hh], NEG))
            col_bias = [selb[hh][j:j + 1, :] + slopes2[hh] * float((j - own) * MOBA_BLOCK) for j in range(own)]
            col_bias.append(jnp.zeros((1, MOBA_BLOCK), F32))
            m = jnp.max(tiles[0], axis=0, keepdims=True) + col_bias[0]
            for t, cb in zip(tiles[1:], col_bias[1:]):
                m = jnp.maximum(m, jnp.max(t, axis=0, keepdims=True) + cb)
            probs.append(jnp.concatenate([jnp.exp2(t + (cb - m)).astype(BF16) for t, cb in zip(tiles, col_bias)],
                                         axis=0))
        keys = (own + 1) * MOBA_BLOCK
        ones_rows = jnp.ones((MOBA_ONES_ROWS, keys), BF16)
        pv = [_dot(jnp.concatenate([vt_s[hh * dh:(hh + 1) * dh, 0:keys], ones_rows], axis=0), probs[hh])
              for hh in heads]
        outs = [r[:dh] / r[dh:dh + 1] for r in pv]
        o_ref[0, rows, :] = jnp.concatenate(outs, axis=0).T.astype(BF16)

    staged = score_stage(0)
    for own in range(nb):
        upcoming = score_stage(own + 1) if own + 1 < nb else None
        softmax_pv_stage(own, *staged)
        staged = upcoming


def _moba(q, k, v, q_gain, k_gain, slopes):
    b, s, width = q.shape
    dh = MOBA_HEAD_DIM
    hps = MOBA_HEADS_PER_STEP
    lanes = hps * dh
    nb = s // MOBA_BLOCK
    seq_spec = pl.BlockSpec((1, s, lanes), lambda bi, hi: (bi, 0, hi))
    return pl.pallas_call(
        _moba_kernel,
        grid=(b, width // lanes),
        in_specs=[
            pl.BlockSpec(memory_space=pltpu.SMEM),
            seq_spec, seq_spec, seq_spec,
            pl.BlockSpec((1, lanes), lambda bi, hi: (0, 0)),
            pl.BlockSpec((1, lanes), lambda bi, hi: (0, 0)),
        ],
        out_specs=seq_spec,
        out_shape=jax.ShapeDtypeStruct((b, s, width), BF16),
        scratch_shapes=[
            pltpu.VMEM((s, lanes), BF16),
            pltpu.VMEM((nb, lanes), F32),
            pltpu.VMEM((lanes, s), BF16),
        ],
        compiler_params=_params(("parallel", "parallel")),
        name="moba_attention",
    )(slopes, q, k, v, q_gain, k_gain)


def _ssd_kernel(xbc_ref, z_ref, dt_ref, cw_ref, cb_ref, dtb_ref, alog_ref, dskip_ref, nw_ref,
                out_ref, hist_s, state_s):
    c = pl.program_id(1)
    L = SSM_CHUNK
    inner = SSM_HEADS * SSM_HEAD_DIM
    gw = SSM_STATE
    pair_w = 2 * SSM_HEAD_DIM

    @pl.when(c == 0)
    def _():
        hist_s[...] = jnp.zeros_like(hist_s)
        state_s[...] = jnp.zeros_like(state_s)

    row = lax.broadcasted_iota(jnp.int32, (L, L), 0)
    col = lax.broadcasted_iota(jnp.int32, (L, L), 1)
    causal = row >= col
    causal_ones = jnp.where(causal, 1.0, 0.0).astype(BF16)
    hrow = lax.broadcasted_iota(jnp.int32, (LANES, inner), 0)
    hcol = lax.broadcasted_iota(jnp.int32, (LANES, inner), 1)
    widen = jnp.where(hcol // SSM_HEAD_DIM == hrow, 1.0, 0.0).astype(BF16)
    lane = lax.broadcasted_iota(jnp.int32, (L, pair_w), 1)
    first_head = lane < SSM_HEAD_DIM
    heads_per_group = SSM_HEADS // SSM_GROUPS
    gsz = inner // SSM_GROUPS

    for sq in range(xbc_ref.shape[0]):
        cur = xbc_ref[sq].astype(F32)
        ext = jnp.concatenate([hist_s[sq], cur], axis=0)
        hist_s[sq] = cur[L - CONV_HALO:, :]
        act = _silu(_causal_conv(ext, cw_ref, SSM_CONV, CONV_HALO) + cb_ref[...])
        xs = act[:, :inner]

        dt = _softplus(dt_ref[sq] + dtb_ref[...])
        a = dt * (-jnp.exp(alog_ref[...]) * LOG2E)
        acum = _select_rows_dot(causal_ones, a)
        acum_t = acum.T
        widened = _select_cols_dot(jnp.concatenate([dt, acum], axis=0), widen)
        dt_w = widened[:L]
        acum_w = widened[L:]
        last_w = acum_w[L - 1:L, :]
        xdt = xs * dt_w
        xdt_bf = xdt.astype(BF16)
        xdt_dec_bf = (xdt * jnp.exp2(last_w - acum_w)).astype(BF16)
        exp_acum_w = jnp.exp2(acum_w)
        chunk_decay_w = jnp.exp2(last_w)

        ys = []
        for g in range(SSM_GROUPS):
            bg = act[:, inner + g * gw:inner + (g + 1) * gw]
            cg = act[:, inner + SSM_GROUPS * gw + g * gw:inner + SSM_GROUPS * gw + (g + 1) * gw]
            cg_bf = cg.astype(BF16)
            cb = _dot_nt(cg_bf, bg.astype(BF16))
            bg_t = bg.T.astype(BF16)
            for pp in range(heads_per_group // 2):
                h0 = g * heads_per_group + 2 * pp
                pi = h0 // 2
                sl = slice(h0 * SSM_HEAD_DIM, h0 * SSM_HEAD_DIM + pair_w)
                yd = []
                for hh in (h0, h0 + 1):
                    diff = jnp.broadcast_to(acum[:, hh:hh + 1], (L, L)) - acum_t[hh:hh + 1, :]
                    decay = jnp.exp2(jnp.where(causal, diff, NEG))
                    yd.append(_dot((cb * decay).astype(BF16), xdt_bf[:, sl]))
                y_diag = jnp.where(first_head, yd[0], yd[1])
                prev = state_s[sq, pi]
                y_off = _dot(cg_bf, prev.astype(BF16)) * exp_acum_w[:, sl]
                state_s[sq, pi] = prev * chunk_decay_w[:, sl] + _dot(bg_t, xdt_dec_bf[:, sl])
                ys.append(y_diag + y_off + dskip_ref[:, sl] * xs[:, sl])
        y = jnp.concatenate(ys, axis=1)

        yg = y * _silu(z_ref[sq].astype(F32))
        parts = []
        for g in range(SSM_GROUPS):
            seg = yg[:, g * gsz:(g + 1) * gsz]
            parts.append(seg * lax.rsqrt(jnp.mean(seg * seg, axis=-1, keepdims=True) + EPS))
        out_ref[sq] = (jnp.concatenate(parts, axis=1) * nw_ref[...]).astype(BF16)


def _ssd(xbc, z, dt, conv_w, conv_b, dt_bias, a_log, d_skip_w, norm_w):
    b, s, cd = xbc.shape
    inner = z.shape[2]
    nc = s // SSM_CHUNK
    nseq = SSM_SEQS_PER_STEP
    return pl.pallas_call(
        _ssd_kernel,
        grid=(b // nseq, nc),
        in_specs=[
            pl.BlockSpec((nseq, SSM_CHUNK, cd), lambda bi, ci: (bi, ci, 0)),
            pl.BlockSpec((nseq, SSM_CHUNK, inner), lambda bi, ci: (bi, ci, 0)),
            pl.BlockSpec((nseq, SSM_CHUNK, LANES), lambda bi, ci: (bi, ci, 0)),
            _const_spec(conv_w.shape), _const_spec(conv_b.shape), _const_spec(dt_bias.shape),
            _const_spec(a_log.shape), _const_spec(d_skip_w.shape), _const_spec(norm_w.shape),
        ],
        out_specs=pl.BlockSpec((nseq, SSM_CHUNK, inner), lambda bi, ci: (bi, ci, 0)),
        out_shape=jax.ShapeDtypeStruct((b, s, inner), BF16),
        scratch_shapes=[
            pltpu.VMEM((nseq, CONV_HALO, cd), F32),
            pltpu.VMEM((nseq, SSM_HEADS // 2, SSM_STATE, 2 * SSM_HEAD_DIM), F32),
        ],
        compiler_params=_params(("parallel", "arbitrary")),
        name="ssd_heads",
    )(xbc, z, dt, conv_w, conv_b, dt_bias, a_log, d_skip_w, norm_w)


def _outproj_even_kernel(x_ref, a_ref, s_ref, wa_ref, ws_ref, o_ref):
    o_ref[...] = x_ref[...] + _dot(a_ref[...], wa_ref[...]) + _dot(s_ref[...], ws_ref[...])


def _outproj_even(x2, attn, ssm, wa, ws, tm=1024):
    t, d = x2.shape
    return pl.pallas_call(
        _outproj_even_kernel,
        grid=(t // tm,),
        in_specs=[pl.BlockSpec((tm, d), lambda i: (i, 0)),
                  pl.BlockSpec((tm, attn.shape[1]), lambda i: (i, 0)),
                  pl.BlockSpec((tm, ssm.shape[1]), lambda i: (i, 0)),
                  _const_spec(wa.shape), _const_spec(ws.shape)],
        out_specs=pl.BlockSpec((tm, d), lambda i: (i, 0)),
        out_shape=jax.ShapeDtypeStruct((t, d), F32),
        compiler_params=_params(("parallel",)),
        name="outproj_even",
    )(x2, attn, ssm, wa, ws)


def _outproj_odd_kernel(x_ref, o_in_ref, w_ref, o_ref):
    o_ref[...] = x_ref[...] + _dot(o_in_ref[...], w_ref[...])


def _outproj_odd(x2, o_in, w, tm=1024):
    t, d = x2.shape
    return pl.pallas_call(
        _outproj_odd_kernel,
        grid=(t // tm,),
        in_specs=[pl.BlockSpec((tm, d), lambda i: (i, 0)),
                  pl.BlockSpec((tm, o_in.shape[1]), lambda i: (i, 0)),
                  _const_spec(w.shape)],
        out_specs=pl.BlockSpec((tm, d), lambda i: (i, 0)),
        out_shape=jax.ShapeDtypeStruct((t, d), F32),
        compiler_params=_params(("parallel",)),
        name="outproj_odd",
    )(x2, o_in, w)


def _ffn_ple_kernel(x_ref, halo_ref, p_ref, gf_ref, wg_ref, wu_ref, cw_ref, cb_ref, wd_ref,
                    gp_ref, wpg_ref, wpp_ref, o_ref, *, tm, seq):
    i = pl.program_id(0)
    x = x_ref[...]
    keep_halo = jnp.where((i * tm) % seq == 0, 0.0, 1.0)
    halo = halo_ref[...] * keep_halo
    hn = _rms_rows(jnp.concatenate([halo, x], axis=0), gf_ref[...]).astype(BF16)
    hn_cur = hn[FFN_HALO:]

    d_ff = wg_ref.shape[1]
    acts = []
    for j in range(d_ff // FFN_COLS):
        cs = slice(j * FFN_COLS, (j + 1) * FFN_COLS)
        g = _dot(hn, wg_ref[:, cs])
        gc = cb_ref[:, cs] + cw_ref[FFN_CONV - 1:FFN_CONV, cs] * g[FFN_HALO:]
        for shift in range(1, FFN_CONV):
            gc = gc + cw_ref[FFN_CONV - 1 - shift:FFN_CONV - shift, cs] * pltpu.roll(g, shift, axis=0)[FFN_HALO:]
        u = _dot(hn_cur, wu_ref[:, cs])
        acts.append((_silu(gc) * u).astype(BF16))
    x1 = x + _dot(jnp.concatenate(acts, axis=1), wd_ref[...])
    hp = _rms_rows(x1, gp_ref[...]).astype(BF16)
    gate = _sigmoid(_dot(hp, wpg_ref[...]))
    o_ref[...] = x1 + gate * _dot(p_ref[...].astype(BF16), wpp_ref[...])


def _ffn_ple(x2, p2, seq, gain_ffn, wg, wu, conv_w, conv_b, wd, gain_ple, wpg, wpp, tm=1024):
    t, d = x2.shape
    halo_blocks = tm // FFN_HALO
    consts = (gain_ffn, wg, wu, conv_w, conv_b, wd, gain_ple, wpg, wpp)
    return pl.pallas_call(
        functools.partial(_ffn_ple_kernel, tm=tm, seq=seq),
        grid=(t // tm,),
        in_specs=[pl.BlockSpec((tm, d), lambda i: (i, 0)),
                  pl.BlockSpec((FFN_HALO, d), lambda i: (jnp.maximum(i * halo_blocks - 1, 0), 0)),
                  pl.BlockSpec((tm, p2.shape[1]), lambda i: (i, 0))]
        + [_const_spec(c.shape) for c in consts],
        out_specs=pl.BlockSpec((tm, d), lambda i: (i, 0)),
        out_shape=jax.ShapeDtypeStruct((t, d), F32),
        compiler_params=_params(("parallel",)),
        name="ffn_ple",
    )(x2, x2, p2, *consts)


def _inproj_odd_kernel(x_ref, g_ref, w_ref, wba_ref, qkv_ref, z_ref, ba_ref):
    hn = _rms_rows(x_ref[...], g_ref[...]).astype(BF16)
    _project_columns(hn, w_ref, (qkv_ref, z_ref))
    ba_ref[...] = _dot(hn, wba_ref[...])


def _inproj_odd(x2, gain, w, wba, widths, tm=512):
    t, d = x2.shape
    outs = [(n, BF16) for n in widths] + [(wba.shape[1], F32)]
    return pl.pallas_call(
        _inproj_odd_kernel,
        grid=(t // tm,),
        in_specs=[pl.BlockSpec((tm, d), lambda i: (i, 0))] + [_const_spec(c.shape) for c in (gain, w, wba)],
        out_specs=[pl.BlockSpec((tm, n), lambda i: (i, 0)) for n, _ in outs],
        out_shape=[jax.ShapeDtypeStruct((t, n), dt) for n, dt in outs],
        compiler_params=_params(("parallel",)),
        name="inproj_odd",
    )(x2, gain, w, wba)


def _bdot(a, b):
    return _dot(a.astype(BF16), b.astype(BF16))


def _unit_lower_inverses(mats):
    L = mats[0].shape[0]
    row = lax.broadcasted_iota(jnp.int32, (L, L), 0)
    col = lax.broadcasted_iota(jnp.int32, (L, L), 1)
    eye = jnp.where(row == col, 1.0, 0.0)
    base = 8
    same = (row // base) == (col // base)
    ns = [-jnp.where(same, a, 0.0) for a in mats]
    ts = [eye + n for n in ns]
    ns_bf = [n.astype(BF16) for n in ns]
    pws_bf = [_dot(n, n).astype(BF16) for n in ns_bf]
    ts = [t + _dot(t.astype(BF16), pw) for t, pw in zip(ts, pws_bf)]
    pws_bf = [_dot(pw, pw).astype(BF16) for pw in pws_bf]
    ts = [t + _dot(t.astype(BF16), pw) for t, pw in zip(ts, pws_bf)]
    size = base
    while size < L:
        wider = (row // (2 * size)) == (col // (2 * size))
        offs = [jnp.where(wider & jnp.logical_not(same), a, 0.0).astype(BF16) for a in mats]
        ts_bf = [t.astype(BF16) for t in ts]
        ys = [_dot(off, t).astype(BF16) for off, t in zip(offs, ts_bf)]
        ts = [t - _dot(t_bf, y) for t, t_bf, y in zip(ts, ts_bf, ys)]
        same = wider
        size *= 2
    return ts


def _gdn_kernel(qkv_ref, z_ref, ba_ref, cw_ref, dtb_ref, alog_ref, nw_ref, out_ref, hist_s, state_s):
    c = pl.program_id(1)
    L = GDN_CHUNK
    dk = GDN_DIM
    qk_w = GDN_K_HEADS * dk
    rep = GDN_V_HEADS // GDN_K_HEADS
    nseq = qkv_ref.shape[0]

    @pl.when(c == 0)
    def _():
        hist_s[...] = jnp.zeros_like(hist_s)
        state_s[...] = jnp.zeros_like(state_s)

    row = lax.broadcasted_iota(jnp.int32, (L, L), 0)
    col = lax.broadcasted_iota(jnp.int32, (L, L), 1)
    lower = row >= col
    strict = row > col

    items = [(sq, hv) for sq in range(nseq) for hv in range(GDN_V_HEADS)]
    acts, qs, ks, kts, kk, qk0 = {}, {}, {}, {}, {}, {}
    beta_c, gc_c, gc_r = {}, {}, {}
    for sq in range(nseq):
        cur = qkv_ref[sq].astype(F32)
        ext = jnp.concatenate([hist_s[sq], cur], axis=0)
        hist_s[sq] = cur[L - CONV_HALO:, :]
        act = _silu(_causal_conv(ext, cw_ref, GDN_CONV, CONV_HALO))
        acts[sq] = act

        ba = ba_ref[sq]
        beta = _sigmoid(ba)
        g = (-jnp.exp(alog_ref[...]) * LOG2E) * _softplus(ba + dtb_ref[...])
        gc = _select_rows_dot(jnp.where(lower, 1.0, 0.0).astype(BF16), g)
        gc_t = gc.T
        for hv in range(GDN_V_HEADS):
            beta_c[sq, hv] = beta[:, hv:hv + 1]
            gc_c[sq, hv] = gc[:, GDN_V_HEADS + hv:GDN_V_HEADS + hv + 1]
            gc_r[sq, hv] = gc_t[GDN_V_HEADS + hv:GDN_V_HEADS + hv + 1, :]

        qk_act = act[:, :2 * qk_w]
        qk_n = qk_act * lax.rsqrt(_head_sums(qk_act * qk_act, dk) + EPS)
        for kh in range(GDN_K_HEADS):
            qn = qk_n[:, kh * dk:(kh + 1) * dk] * (dk ** -0.5)
            kn = qk_n[:, qk_w + kh * dk:qk_w + (kh + 1) * dk]
            kt = kn.T
            prod = _bdot(jnp.concatenate([kn, qn], axis=0), kt)
            qs[sq, kh], ks[sq, kh], kts[sq, kh] = qn, kn, kt
            kk[sq, kh], qk0[sq, kh] = prod[:L], prod[L:]

    def khead(it):
        return it[0], it[1] // rep

    outs = {}
    for g0 in range(0, len(items), GDN_CHAINS_PER_GROUP):
        group = items[g0:g0 + GDN_CHAINS_PER_GROUP]
        g_last = {it: gc_r[it][:, L - 1:L] for it in group}
        egc = {it: jnp.exp2(gc_c[it]) for it in group}
        decay = {it: jnp.exp2(jnp.where(lower, gc_c[it] - gc_r[it], NEG)) for it in group}
        a_mats = [jnp.where(strict, kk[khead(it)] * beta_c[it] * decay[it], 0.0) for it in group]
        t_inv = dict(zip(group, _unit_lower_inverses(a_mats)))

        sols = {}
        for it in group:
            sq, hv = it
            v_h = acts[sq][:, 2 * qk_w + hv * dk:2 * qk_w + (hv + 1) * dk]
            sols[it] = _bdot(t_inv[it], jnp.concatenate([v_h * beta_c[it],
                                                         ks[khead(it)] * (beta_c[it] * egc[it])], axis=1))

        states = {it: state_s[it[0], it[1]] for it in group}
        from_state = {it: _bdot(jnp.concatenate([sols[it][:, dk:], qs[khead(it)] * egc[it]], axis=0),
                                states[it]) for it in group}
        v_new = {it: sols[it][:, :dk] - from_state[it][:L] for it in group}
        from_v = {it: _bdot(jnp.concatenate([qk0[khead(it)] * decay[it],
                                             kts[khead(it)] * jnp.exp2(g_last[it] - gc_r[it])], axis=0),
                            v_new[it]) for it in group}
        for it in group:
            state_s[it[0], it[1]] = states[it] * jnp.exp2(g_last[it]) + from_v[it][L:]
            outs[it] = from_state[it][L:] + from_v[it][:L]
    for sq in range(nseq):
        o = jnp.concatenate([outs[sq, hv] for hv in range(GDN_V_HEADS)], axis=1)
        o = o * lax.rsqrt(_head_sums(o * o, dk) * (1.0 / dk) + EPS) * nw_ref[...]
        out_ref[sq] = (o * _silu(z_ref[sq].astype(F32))).astype(BF16)


def _gdn(qkv, z, ba, conv_w, dt_bias_pad, a_log_pad, norm_w):
    b, s, cd = qkv.shape
    vw = z.shape[2]
    nc = s // GDN_CHUNK
    nseq = GDN_SEQS_PER_STEP
    return pl.pallas_call(
        _gdn_kernel,
        grid=(b // nseq, nc),
        in_specs=[
            pl.BlockSpec((nseq, GDN_CHUNK, cd), lambda bi, ci: (bi, ci, 0)),
            pl.BlockSpec((nseq, GDN_CHUNK, vw), lambda bi, ci: (bi, ci, 0)),
            pl.BlockSpec((nseq, GDN_CHUNK, LANES), lambda bi, ci: (bi, ci, 0)),
            _const_spec(conv_w.shape), _const_spec(dt_bias_pad.shape),
            _const_spec(a_log_pad.shape), _const_spec(norm_w.shape),
        ],
        out_specs=pl.BlockSpec((nseq, GDN_CHUNK, vw), lambda bi, ci: (bi, ci, 0)),
        out_shape=jax.ShapeDtypeStruct((b, s, vw), BF16),
        scratch_shapes=[
            pltpu.VMEM((nseq, CONV_HALO, cd), F32),
            pltpu.VMEM((nseq, GDN_V_HEADS, GDN_DIM, GDN_DIM), F32),
        ],
        compiler_params=_params(("parallel", "arbitrary")),
        name="gated_deltanet",
    )(qkv, z, ba, conv_w, dt_bias_pad, a_log_pad, norm_w)


def _pad_lanes(v, offset=0):
    out = jnp.zeros((1, LANES), F32)
    return out.at[0, offset:offset + v.shape[0]].set(v.astype(F32))


def _row(v):
    return v.astype(F32).reshape(1, -1)


def _even_mixer(x2, batch, seq, norm_w, w_in, q_norm, k_norm, conv_w, conv_b, dt_bias, a_log, d_skip,
                ssm_norm, w_out, slopes):
    mw = MOBA_HEADS * MOBA_HEAD_DIM
    inner = SSM_HEADS * SSM_HEAD_DIM
    conv_dim = inner + 2 * SSM_GROUPS * SSM_STATE
    w = w_in.astype(BF16)
    widths = (mw, mw, mw, inner, conv_dim)
    wdt = jnp.pad(w[:, sum(widths):], ((0, 0), (0, LANES - SSM_HEADS)))
    q, k, v, z, xbc, dt = _inproj_even(x2, _row(norm_w), w, wdt, widths)
    attn = _moba(q.reshape(batch, seq, mw), k.reshape(batch, seq, mw), v.reshape(batch, seq, mw),
                 _row(jnp.tile(q_norm, MOBA_HEADS_PER_STEP)), _row(jnp.tile(k_norm, MOBA_HEADS_PER_STEP)),
                 slopes).reshape(batch * seq, mw)

    ssm = _ssd(xbc.reshape(batch, seq, conv_dim), z.reshape(batch, seq, inner),
               dt.reshape(batch, seq, LANES), conv_w.astype(F32), _row(conv_b),
               _pad_lanes(dt_bias), _pad_lanes(a_log), _row(jnp.repeat(d_skip, SSM_HEAD_DIM)),
               _row(ssm_norm))
    wo = w_out.astype(BF16)
    return _outproj_even(x2, attn, ssm.reshape(batch * seq, inner), wo[:mw], wo[mw:])


def _gdn_mixer(x2, batch, seq, norm_w, w_in, conv_w, dt_bias, a_log, gdn_norm, w_out):
    conv_dim = 2 * GDN_K_HEADS * GDN_DIM + GDN_V_HEADS * GDN_DIM
    vw = GDN_V_HEADS * GDN_DIM
    w = w_in.astype(BF16)
    wba = jnp.pad(w[:, conv_dim + vw:], ((0, 0), (0, LANES - 2 * GDN_V_HEADS)))
    qkv, z, ba = _inproj_odd(x2, _row(norm_w), w, wba, (conv_dim, vw))
    o = _gdn(qkv.reshape(batch, seq, conv_dim), z.reshape(batch, seq, vw), ba.reshape(batch, seq, LANES),
             conv_w.astype(F32), _pad_lanes(dt_bias, GDN_V_HEADS), _pad_lanes(a_log, GDN_V_HEADS),
             _row(jnp.tile(gdn_norm, GDN_V_HEADS)))
    return _outproj_odd(x2, o.reshape(batch * seq, vw), w_out.astype(BF16))


def kernel(x, p, norm_mix, norm_ffn, norm_ple, w_in_even, moba_q_norm, moba_k_norm, ssm_conv_w, ssm_conv_b, ssm_dt_bias, ssm_a_log, ssm_d, ssm_norm, w_out_even, w_in_odd, gdn_conv_w, gdn_dt_bias, gdn_a_log, gdn_norm, w_out_odd, ffn_w_gate, ffn_w_up, ffn_conv_w, ffn_conv_b, ffn_w_down, ple_w_proj, ple_w_gate):
    batch, seq, d = x.shape
    depth = p.shape[0]
    slopes = jnp.exp2(-ALIBI_MAX_BIAS * jnp.arange(1, MOBA_HEADS + 1, dtype=F32) / MOBA_HEADS)
    x2 = x.reshape(batch * seq, d)
    for i in range(depth):
        j = i // 2
        if i % 2 == 0:
            x2 = _even_mixer(x2, batch, seq, norm_mix[i], w_in_even[j], moba_q_norm[j], moba_k_norm[j],
                             ssm_conv_w[j], ssm_conv_b[j], ssm_dt_bias[j], ssm_a_log[j], ssm_d[j],
                             ssm_norm[j], w_out_even[j], slopes)
        else:
            x2 = _gdn_mixer(x2, batch, seq, norm_mix[i], w_in_odd[j], gdn_conv_w[j], gdn_dt_bias[j],
                            gdn_a_log[j], gdn_norm[j], w_out_odd[j])
        x2 = _ffn_ple(x2, p[i].reshape(batch * seq, -1), seq, _row(norm_ffn[i]),
                      ffn_w_gate[i].astype(BF16), ffn_w_up[i].astype(BF16), ffn_conv_w[i].astype(F32),
                      _row(ffn_conv_b[i]), ffn_w_down[i].astype(BF16), _row(norm_ple[i]),
                      ple_w_gate[i].astype(BF16), ple_w_proj[i].astype(BF16))
    return x2.reshape(batch, seq, d)
```

```python
import functools

import jax
import jax.numpy as jnp
from jax import lax
from jax.experimental import pallas as pl
from jax.experimental.pallas import tpu as pltpu

F32 = jnp.float32
BF16 = jnp.bfloat16
HIGHEST = lax.Precision.HIGHEST

EPS = 1e-6
NEG = -1e30
LOG2E = 1.4426950408889634
LANES = 128
CONV_HALO = 8

MOBA_HEADS = 8
MOBA_HEAD_DIM = 64
MOBA_BLOCK = 256
MOBA_TOPK = 3
MOBA_HEADS_PER_STEP = 4
MOBA_ONES_ROWS = 16
ALIBI_MAX_BIAS = 8.0

SSM_HEADS = 16
SSM_HEAD_DIM = 64
SSM_STATE = 128
SSM_GROUPS = 2
SSM_CONV = 4
SSM_CHUNK = 128
SSM_SEQS_PER_STEP = 2

GDN_K_HEADS = 8
GDN_V_HEADS = 16
GDN_DIM = 128
GDN_CONV = 4
GDN_CHUNK = 64
GDN_SEQS_PER_STEP = 2
GDN_CHAINS_PER_GROUP = GDN_SEQS_PER_STEP * GDN_V_HEADS

FFN_CONV = 3
FFN_HALO = 16
FFN_COLS = 256

VMEM_LIMIT = 56 * 1024 * 1024


def _rms_rows(x, gain):
    return x * lax.rsqrt(jnp.mean(x * x, axis=-1, keepdims=True) + EPS) * gain


def _sigmoid(x):
    return 0.5 * jnp.tanh(0.5 * x) + 0.5


def _silu(x):
    half = 0.5 * x
    return half + half * jnp.tanh(half)


def _softplus(x):
    return jnp.maximum(x, 0.0) + jnp.log(1.0 + jnp.exp(-jnp.abs(x)))


def _dot(a, b):
    return jnp.dot(a, b, preferred_element_type=F32)


def _dot_nt(a, b):
    return lax.dot_general(a, b, (((1,), (1,)), ((), ())), preferred_element_type=F32)


def _bf16_terms(x):
    hi = x.astype(BF16)
    rest = x - hi.astype(F32)
    mid = rest.astype(BF16)
    lo = (rest - mid.astype(F32)).astype(BF16)
    return [hi, mid, lo]


def _select_rows_dot(sel, x):
    return _dot(jnp.concatenate([sel] * 3, axis=1), jnp.concatenate(_bf16_terms(x), axis=0))


def _select_cols_dot(x, sel):
    return _dot(jnp.concatenate(_bf16_terms(x), axis=1), jnp.concatenate([sel] * 3, axis=0))


def _head_sums(x, head_w):
    rows, width = x.shape
    slab = 2 * LANES
    n = width // slab
    same = (lax.broadcasted_iota(jnp.int32, (slab, slab), 0) // head_w
            == lax.broadcasted_iota(jnp.int32, (slab, slab), 1) // head_w)
    ones = jnp.where(same, 1.0, 0.0).astype(BF16)
    stacked = jnp.concatenate([x[:, i * slab:(i + 1) * slab] for i in range(n)], axis=0)
    sums = _dot(stacked.astype(BF16), ones)
    return jnp.concatenate([sums[i * rows:(i + 1) * rows] for i in range(n)], axis=1)


def _const_spec(shape):
    zeros = (0,) * len(shape)
    return pl.BlockSpec(shape, lambda *_: zeros, pipeline_mode=pl.Buffered(1))


def _params(semantics):
    return pltpu.CompilerParams(dimension_semantics=semantics, vmem_limit_bytes=VMEM_LIMIT)


def _causal_conv(ext, w_ref, width, halo):
    y = w_ref[width - 1:width, :] * ext[halo:]
    for shift in range(1, width):
        y = y + w_ref[width - 1 - shift:width - shift, :] * pltpu.roll(ext, shift, axis=0)[halo:]
    return y


def _project_columns(hn, w_ref, out_refs):
    c0 = 0
    for o_ref in out_refs:
        n = o_ref.shape[1]
        o_ref[...] = _dot(hn, w_ref[:, c0:c0 + n]).astype(BF16)
        c0 += n


def _inproj_even_kernel(x_ref, g_ref, w_ref, wdt_ref, q_ref, k_ref, v_ref, z_ref, xbc_ref, dt_ref):
    hn = _rms_rows(x_ref[...], g_ref[...]).astype(BF16)
    _project_columns(hn, w_ref, (q_ref, k_ref, v_ref, z_ref, xbc_ref))
    dt_ref[...] = _dot(hn, wdt_ref[...])


def _inproj_even(x2, gain, w, wdt, widths, tm=512):
    t, d = x2.shape
    outs = [(n, BF16) for n in widths] + [(wdt.shape[1], F32)]
    return pl.pallas_call(
        _inproj_even_kernel,
        grid=(t // tm,),
        in_specs=[pl.BlockSpec((tm, d), lambda i: (i, 0))] + [_const_spec(c.shape) for c in (gain, w, wdt)],
        out_specs=[pl.BlockSpec((tm, n), lambda i: (i, 0)) for n, _ in outs],
        out_shape=[jax.ShapeDtypeStruct((t, n), dt) for n, dt in outs],
        compiler_params=_params(("parallel",)),
        name="inproj_even",
    )(x2, gain, w, wdt)


def _moba_kernel(slopes_ref, q_ref, k_ref, v_ref, qg_ref, kg_ref, o_ref, kn_s, kbar_s, vt_s):
    hp = pl.program_id(1)
    seq = k_ref.shape[1]
    nb = seq // MOBA_BLOCK
    dh = MOBA_HEAD_DIM
    hps = MOBA_HEADS_PER_STEP
    heads = range(hps)

    lanes = hps * dh
    same_head = (lax.broadcasted_iota(jnp.int32, (lanes, lanes), 0) // dh
                 == lax.broadcasted_iota(jnp.int32, (lanes, lanes), 1) // dh)
    head_mean = jnp.where(same_head, 1.0 / dh, 0.0).astype(BF16)

    def head_rms(x, gain):
        return x * lax.rsqrt(_dot((x * x).astype(BF16), head_mean) + EPS) * gain

    kn = head_rms(k_ref[0].astype(F32), kg_ref[...])
    kn_s[...] = kn.astype(BF16)
    kbar_s[...] = jnp.mean(kn.reshape(nb, MOBA_BLOCK, lanes), axis=1)
    vt_s[...] = v_ref[0].astype(F32).T.astype(BF16)

    lane_head = lax.broadcasted_iota(jnp.int32, (MOBA_BLOCK, lanes), 1) // dh
    blk = lax.broadcasted_iota(jnp.int32, (nb, MOBA_BLOCK), 0)
    kbar_head = lax.broadcasted_iota(jnp.int32, (nb, lanes), 1) // dh
    kbar = kbar_s[...]
    kbar_rows = jnp.concatenate([jnp.where(kbar_head == hh, kbar, 0.0) for hh in heads], axis=0)
    kidx = lax.broadcasted_iota(jnp.int32, (MOBA_BLOCK, MOBA_BLOCK), 0)
    qidx = lax.broadcasted_iota(jnp.int32, (MOBA_BLOCK, MOBA_BLOCK), 1)
    rel = (kidx - qidx).astype(F32)
    causal = rel <= 0.0
    slopes2 = [slopes_ref[hp * hps + hh] * LOG2E for hh in heads]
    q_gain = qg_ref[...] * (dh ** -0.5 * LOG2E)
    alibis = [slope2 * rel for slope2 in slopes2]

    def score_stage(own):
        rows = slice(own * MOBA_BLOCK, (own + 1) * MOBA_BLOCK)
        qn = head_rms(q_ref[0, rows, :].astype(F32), q_gain)
        qs = [jnp.where(lane_head == hh, qn, 0.0).astype(BF16) for hh in heads]
        scores = {(hh, j): _dot_nt(kn_s[j * MOBA_BLOCK:(j + 1) * MOBA_BLOCK, :], qs[hh])
                  for hh in heads for j in range(own + 1)}
        gates = lax.dot_general(kbar_rows, qn, (((1,), (1,)), ((), ())),
                                precision=HIGHEST, preferred_element_type=F32)
        selb = []
        for hh in heads:
            gate = gates[hh * nb:(hh + 1) * nb, :]
            rank = jnp.zeros(gate.shape, F32)
            for m in range(own):
                gm = gate[m:m + 1, :]
                beats = (gm > gate) | ((gm == gate) & (m < blk))
                rank = rank + jnp.where(beats, 1.0, 0.0)
            selected = (blk < own) & (rank < float(min(MOBA_TOPK, nb - 1)))
            selb.append(jnp.where(selected, 0.0, NEG))
        return scores, selb

    def softmax_pv_stage(own, scores, selb):
        rows = slice(own * MOBA_BLOCK, (own + 1) * MOBA_BLOCK)
        probs = []
        for hh in heads:
            tiles = [scores[hh, j] + alibis[hh] for j in range(own)]
            tiles.append(jnp.where(causal, scores[hh, own] + alibis[hh], NEG))
            col_bias = [selb[hh][j:j + 1, :] + slopes2[hh] * float((j - own) * MOBA_BLOCK) for j in range(own)]
            col_bias.append(jnp.zeros((1, MOBA_BLOCK), F32))
            m = jnp.max(tiles[0], axis=0, keepdims=True) + col_bias[0]
            for t, cb in zip(tiles[1:], col_bias[1:]):
                m = jnp.maximum(m, jnp.max(t, axis=0, keepdims=True) + cb)
            probs.append(jnp.concatenate([jnp.exp2(t + (cb - m)).astype(BF16) for t, cb in zip(tiles, col_bias)],
                                         axis=0))
        keys = (own + 1) * MOBA_BLOCK
        ones_rows = jnp.ones((MOBA_ONES_ROWS, keys), BF16)
        pv = [_dot(jnp.concatenate([vt_s[hh * dh:(hh + 1) * dh, 0:keys], ones_rows], axis=0), probs[hh])
              for hh in heads]
        outs = [r[:dh] / r[dh:dh + 1] for r in pv]
        o_ref[0, rows, :] = jnp.concatenate(outs, axis=0).T.astype(BF16)

    staged = score_stage(0)
    for own in range(nb):
        upcoming = score_stage(own + 1) if own + 1 < nb else None
        softmax_pv_stage(own, *staged)
        staged = upcoming


def _moba(q, k, v, q_gain, k_gain, slopes):
    b, s, width = q.shape
    dh = MOBA_HEAD_DIM
    hps = MOBA_HEADS_PER_STEP
    lanes = hps * dh
    nb = s // MOBA_BLOCK
    seq_spec = pl.BlockSpec((1, s, lanes), lambda bi, hi: (bi, 0, hi))
    return pl.pallas_call(
        _moba_kernel,
        grid=(b, width // lanes),
        in_specs=[
            pl.BlockSpec(memory_space=pltpu.SMEM),
            seq_spec, seq_spec, seq_spec,
            pl.BlockSpec((1, lanes), lambda bi, hi: (0, 0)),
            pl.BlockSpec((1, lanes), lambda bi, hi: (0, 0)),
        ],
        out_specs=seq_spec,
        out_shape=jax.ShapeDtypeStruct((b, s, width), BF16),
        scratch_shapes=[
            pltpu.VMEM((s, lanes), BF16),
            pltpu.VMEM((nb, lanes), F32),
            pltpu.VMEM((lanes, s), BF16),
        ],
        compiler_params=_params(("parallel", "parallel")),
        name="moba_attention",
    )(slopes, q, k, v, q_gain, k_gain)


def _ssd_kernel(xbc_ref, z_ref, dt_ref, cw_ref, cb_ref, dtb_ref, alog_ref, dskip_ref, nw_ref,
                out_ref, hist_s, state_s):
    c = pl.program_id(1)
    L = SSM_CHUNK
    inner = SSM_HEADS * SSM_HEAD_DIM
    gw = SSM_STATE
    pair_w = 2 * SSM_HEAD_DIM

    @pl.when(c == 0)
    def _():
        hist_s[...] = jnp.zeros_like(hist_s)
        state_s[...] = jnp.zeros_like(state_s)

    row = lax.broadcasted_iota(jnp.int32, (L, L), 0)
    col = lax.broadcasted_iota(jnp.int32, (L, L), 1)
    causal = row >= col
    causal_ones = jnp.where(causal, 1.0, 0.0).astype(BF16)
    hrow = lax.broadcasted_iota(jnp.int32, (LANES, inner), 0)
    hcol = lax.broadcasted_iota(jnp.int32, (LANES, inner), 1)
    widen = jnp.where(hcol // SSM_HEAD_DIM == hrow, 1.0, 0.0).astype(BF16)
    lane = lax.broadcasted_iota(jnp.int32, (L, pair_w), 1)
    first_head = lane < SSM_HEAD_DIM
    heads_per_group = SSM_HEADS // SSM_GROUPS
    gsz = inner // SSM_GROUPS

    for sq in range(xbc_ref.shape[0]):
        cur = xbc_ref[sq].astype(F32)
        ext = jnp.concatenate([hist_s[sq], cur], axis=0)
        hist_s[sq] = cur[L - CONV_HALO:, :]
        act = _silu(_causal_conv(ext, cw_ref, SSM_CONV, CONV_HALO) + cb_ref[...])
        xs = act[:, :inner]

        dt = _softplus(dt_ref[sq] + dtb_ref[...])
        a = dt * (-jnp.exp(alog_ref[...]) * LOG2E)
        acum = _select_rows_dot(causal_ones, a)
        acum_t = acum.T
        widened = _select_cols_dot(jnp.concatenate([dt, acum], axis=0), widen)
        dt_w = widened[:L]
        acum_w = widened[L:]
        last_w = acum_w[L - 1:L, :]
        xdt = xs * dt_w
        xdt_bf = xdt.astype(BF16)
        xdt_dec_bf = (xdt * jnp.exp2(last_w - acum_w)).astype(BF16)
        exp_acum_w = jnp.exp2(acum_w)
        chunk_decay_w = jnp.exp2(last_w)

        ys = []
        for g in range(SSM_GROUPS):
            bg = act[:, inner + g * gw:inner + (g + 1) * gw]
            cg = act[:, inner + SSM_GROUPS * gw + g * gw:inner + SSM_GROUPS * gw + (g + 1) * gw]
            cg_bf = cg.astype(BF16)
            cb = _dot_nt(cg_bf, bg.astype(BF16))
            bg_t = bg.T.astype(BF16)
            for pp in range(heads_per_group // 2):
                h0 = g * heads_per_group + 2 * pp
                pi = h0 // 2
                sl = slice(h0 * SSM_HEAD_DIM, h0 * SSM_HEAD_DIM + pair_w)
                yd = []
                for hh in (h0, h0 + 1):
                    diff = jnp.broadcast_to(acum[:, hh:hh + 1], (L, L)) - acum_t[hh:hh + 1, :]
                    decay = jnp.exp2(jnp.where(causal, diff, NEG))
                    yd.append(_dot((cb * decay).astype(BF16), xdt_bf[:, sl]))
                y_diag = jnp.where(first_head, yd[0], yd[1])
                prev = state_s[sq, pi]
                y_off = _dot(cg_bf, prev.astype(BF16)) * exp_acum_w[:, sl]
                state_s[sq, pi] = prev * chunk_decay_w[:, sl] + _dot(bg_t, xdt_dec_bf[:, sl])
                ys.append(y_diag + y_off + dskip_ref[:, sl] * xs[:, sl])
        y = jnp.concatenate(ys, axis=1)

        yg = y * _silu(z_ref[sq].astype(F32))
        parts = []
        for g in range(SSM_GROUPS):
            seg = yg[:, g * gsz:(g + 1) * gsz]
            parts.append(seg * lax.rsqrt(jnp.mean(seg * seg, axis=-1, keepdims=True) + EPS))
        out_ref[sq] = (jnp.concatenate(parts, axis=1) * nw_ref[...]).astype(BF16)


def _ssd(xbc, z, dt, conv_w, conv_b, dt_bias, a_log, d_skip_w, norm_w):
    b, s, cd = xbc.shape
    inner = z.shape[2]
    nc = s // SSM_CHUNK
    nseq = SSM_SEQS_PER_STEP
    assert b % nseq == 0 and s % SSM_CHUNK == 0, (b, s)
    return pl.pallas_call(
        _ssd_kernel,
        grid=(b // nseq, nc),
        in_specs=[
            pl.BlockSpec((nseq, SSM_CHUNK, cd), lambda bi, ci: (bi, ci, 0)),
            pl.BlockSpec((nseq, SSM_CHUNK, inner), lambda bi, ci: (bi, ci, 0)),
            pl.BlockSpec((nseq, SSM_CHUNK, LANES), lambda bi, ci: (bi, ci, 0)),
            _const_spec(conv_w.shape), _const_spec(conv_b.shape), _const_spec(dt_bias.shape),
            _const_spec(a_log.shape), _const_spec(d_skip_w.shape), _const_spec(norm_w.shape),
        ],
        out_specs=pl.BlockSpec((nseq, SSM_CHUNK, inner), lambda bi, ci: (bi, ci, 0)),
        out_shape=jax.ShapeDtypeStruct((b, s, inner), BF16),
        scratch_shapes=[
            pltpu.VMEM((nseq, CONV_HALO, cd), F32),
            pltpu.VMEM((nseq, SSM_HEADS // 2, SSM_STATE, 2 * SSM_HEAD_DIM), F32),
        ],
        compiler_params=_params(("parallel", "arbitrary")),
        name="ssd_heads",
    )(xbc, z, dt, conv_w, conv_b, dt_bias, a_log, d_skip_w, norm_w)


def _outproj_even_kernel(x_ref, a_ref, s_ref, w_ref, o_ref):
    na = a_ref.shape[1]
    o_ref[...] = x_ref[...] + _dot(a_ref[...], w_ref[:na, :]) + _dot(s_ref[...], w_ref[na:, :])


def _outproj_even(x2, attn, ssm, w, tm=1024):
    t, d = x2.shape
    return pl.pallas_call(
        _outproj_even_kernel,
        grid=(t // tm,),
        in_specs=[pl.BlockSpec((tm, d), lambda i: (i, 0)),
                  pl.BlockSpec((tm, attn.shape[1]), lambda i: (i, 0)),
                  pl.BlockSpec((tm, ssm.shape[1]), lambda i: (i, 0)),
                  _const_spec(w.shape)],
        out_specs=pl.BlockSpec((tm, d), lambda i: (i, 0)),
        out_shape=jax.ShapeDtypeStruct((t, d), F32),
        compiler_params=_params(("parallel",)),
        name="outproj_even",
    )(x2, attn, ssm, w)


def _outproj_odd_kernel(x_ref, o_in_ref, w_ref, o_ref):
    o_ref[...] = x_ref[...] + _dot(o_in_ref[...], w_ref[...])


def _outproj_odd(x2, o_in, w, tm=1024):
    t, d = x2.shape
    return pl.pallas_call(
        _outproj_odd_kernel,
        grid=(t // tm,),
        in_specs=[pl.BlockSpec((tm, d), lambda i: (i, 0)),
                  pl.BlockSpec((tm, o_in.shape[1]), lambda i: (i, 0)),
                  _const_spec(w.shape)],
        out_specs=pl.BlockSpec((tm, d), lambda i: (i, 0)),
        out_shape=jax.ShapeDtypeStruct((t, d), F32),
        compiler_params=_params(("parallel",)),
        name="outproj_odd",
    )(x2, o_in, w)


def _ffn_ple_kernel(x_ref, halo_ref, p_ref, gf_ref, wg_ref, wu_ref, cw_ref, cb_ref, wd_ref,
                    gp_ref, wpg_ref, wpp_ref, o_ref, *, tm, seq):
    i = pl.program_id(0)
    x = x_ref[...]
    keep_halo = jnp.where((i * tm) % seq == 0, 0.0, 1.0)
    halo = halo_ref[...] * keep_halo
    hn = _rms_rows(jnp.concatenate([halo, x], axis=0), gf_ref[...]).astype(BF16)
    hn_cur = hn[FFN_HALO:]

    d_ff = wg_ref.shape[1]
    acts = []
    for j in range(d_ff // FFN_COLS):
        cs = slice(j * FFN_COLS, (j + 1) * FFN_COLS)
        g = _dot(hn, wg_ref[:, cs])
        gc = cb_ref[:, cs] + cw_ref[FFN_CONV - 1:FFN_CONV, cs] * g[FFN_HALO:]
        for shift in range(1, FFN_CONV):
            gc = gc + cw_ref[FFN_CONV - 1 - shift:FFN_CONV - shift, cs] * pltpu.roll(g, shift, axis=0)[FFN_HALO:]
        u = _dot(hn_cur, wu_ref[:, cs])
        acts.append((_silu(gc) * u).astype(BF16))
    x1 = x + _dot(jnp.concatenate(acts, axis=1), wd_ref[...])
    hp = _rms_rows(x1, gp_ref[...]).astype(BF16)
    gate = _sigmoid(_dot(hp, wpg_ref[...]))
    o_ref[...] = x1 + gate * _dot(p_ref[...].astype(BF16), wpp_ref[...])


def _ffn_ple(x2, p_all, layer, seq, gain_ffn, wg, wu, conv_w, conv_b, wd, gain_ple, wpg, wpp, tm=1024):
    t, d = x2.shape
    halo_blocks = tm // FFN_HALO
    consts = (gain_ffn, wg, wu, conv_w, conv_b, wd, gain_ple, wpg, wpp)
    return pl.pallas_call(
        functools.partial(_ffn_ple_kernel, tm=tm, seq=seq),
        grid=(t // tm,),
        in_specs=[pl.BlockSpec((tm, d), lambda i: (i, 0)),
                  pl.BlockSpec((FFN_HALO, d), lambda i: (jnp.maximum(i * halo_blocks - 1, 0), 0)),
                  pl.BlockSpec((None, tm, p_all.shape[2]), lambda i: (layer, i, 0))]
        + [_const_spec(c.shape) for c in consts],
        out_specs=pl.BlockSpec((tm, d), lambda i: (i, 0)),
        out_shape=jax.ShapeDtypeStruct((t, d), F32),
        compiler_params=_params(("parallel",)),
        name="ffn_ple",
    )(x2, x2, p_all, *consts)


def _inproj_odd_kernel(x_ref, g_ref, w_ref, wba_ref, qkv_ref, z_ref, ba_ref):
    hn = _rms_rows(x_ref[...], g_ref[...]).astype(BF16)
    _project_columns(hn, w_ref, (qkv_ref, z_ref))
    ba_ref[...] = _dot(hn, wba_ref[...])


def _inproj_odd(x2, gain, w, wba, widths, tm=512):
    t, d = x2.shape
    outs = [(n, BF16) for n in widths] + [(wba.shape[1], F32)]
    return pl.pallas_call(
        _inproj_odd_kernel,
        grid=(t // tm,),
        in_specs=[pl.BlockSpec((tm, d), lambda i: (i, 0))] + [_const_spec(c.shape) for c in (gain, w, wba)],
        out_specs=[pl.BlockSpec((tm, n), lambda i: (i, 0)) for n, _ in outs],
        out_shape=[jax.ShapeDtypeStruct((t, n), dt) for n, dt in outs],
        compiler_params=_params(("parallel",)),
        name="inproj_odd",
    )(x2, gain, w, wba)


def _bdot(a, b):
    return _dot(a.astype(BF16), b.astype(BF16))


def _unit_lower_inverses(mats):
    L = mats[0].shape[0]
    row = lax.broadcasted_iota(jnp.int32, (L, L), 0)
    col = lax.broadcasted_iota(jnp.int32, (L, L), 1)
    eye = jnp.where(row == col, 1.0, 0.0)
    base = 8
    same = (row // base) == (col // base)
    ns = [-jnp.where(same, a, 0.0) for a in mats]
    ts = [eye + n for n in ns]
    ns_bf = [n.astype(BF16) for n in ns]
    pws_bf = [_dot(n, n).astype(BF16) for n in ns_bf]
    ts = [t + _dot(t.astype(BF16), pw) for t, pw in zip(ts, pws_bf)]
    pws_bf = [_dot(pw, pw).astype(BF16) for pw in pws_bf]
    ts = [t + _dot(t.astype(BF16), pw) for t, pw in zip(ts, pws_bf)]
    size = base
    while size < L:
        wider = (row // (2 * size)) == (col // (2 * size))
        offs = [jnp.where(wider & jnp.logical_not(same), a, 0.0).astype(BF16) for a in mats]
        ts_bf = [t.astype(BF16) for t in ts]
        ys = [_dot(off, t).astype(BF16) for off, t in zip(offs, ts_bf)]
        ts = [t - _dot(t_bf, y) for t, t_bf, y in zip(ts, ts_bf, ys)]
        same = wider
        size *= 2
    return ts


def _gdn_kernel(qkv_ref, z_ref, ba_ref, cw_ref, dtb_ref, alog_ref, nw_ref, out_ref, hist_s, state_s):
    c = pl.program_id(1)
    L = GDN_CHUNK
    dk = GDN_DIM
    qk_w = GDN_K_HEADS * dk
    rep = GDN_V_HEADS // GDN_K_HEADS
    nseq = qkv_ref.shape[0]

    @pl.when(c == 0)
    def _():
        hist_s[...] = jnp.zeros_like(hist_s)
        state_s[...] = jnp.zeros_like(state_s)

    row = lax.broadcasted_iota(jnp.int32, (L, L), 0)
    col = lax.broadcasted_iota(jnp.int32, (L, L), 1)
    lower = row >= col
    strict = row > col

    items = [(sq, hv) for sq in range(nseq) for hv in range(GDN_V_HEADS)]
    acts, qs, ks, kts, kk, qk0 = {}, {}, {}, {}, {}, {}
    beta_c, gc_c, gc_r = {}, {}, {}
    for sq in range(nseq):
        cur = qkv_ref[sq].astype(F32)
        ext = jnp.concatenate([hist_s[sq], cur], axis=0)
        hist_s[sq] = cur[L - CONV_HALO:, :]
        act = _silu(_causal_conv(ext, cw_ref, GDN_CONV, CONV_HALO))
        acts[sq] = act

        ba = ba_ref[sq]
        beta = _sigmoid(ba)
        g = (-jnp.exp(alog_ref[...]) * LOG2E) * _softplus(ba + dtb_ref[...])
        gc = _select_rows_dot(jnp.where(lower, 1.0, 0.0).astype(BF16), g)
        gc_t = gc.T
        for hv in range(GDN_V_HEADS):
            beta_c[sq, hv] = beta[:, hv:hv + 1]
            gc_c[sq, hv] = gc[:, GDN_V_HEADS + hv:GDN_V_HEADS + hv + 1]
            gc_r[sq, hv] = gc_t[GDN_V_HEADS + hv:GDN_V_HEADS + hv + 1, :]

        qk_act = act[:, :2 * qk_w]
        qk_n = qk_act * lax.rsqrt(_head_sums(qk_act * qk_act, dk) + EPS)
        for kh in range(GDN_K_HEADS):
            qn = qk_n[:, kh * dk:(kh + 1) * dk] * (dk ** -0.5)
            kn = qk_n[:, qk_w + kh * dk:qk_w + (kh + 1) * dk]
            kt = kn.T
            prod = _bdot(jnp.concatenate([kn, qn], axis=0), kt)
            qs[sq, kh], ks[sq, kh], kts[sq, kh] = qn, kn, kt
            kk[sq, kh], qk0[sq, kh] = prod[:L], prod[L:]

    def khead(it):
        return it[0], it[1] // rep

    outs = {}
    for g0 in range(0, len(items), GDN_CHAINS_PER_GROUP):
        group = items[g0:g0 + GDN_CHAINS_PER_GROUP]
        g_last = {it: gc_r[it][:, L - 1:L] for it in group}
        egc = {it: jnp.exp2(gc_c[it]) for it in group}
        decay = {it: jnp.exp2(jnp.where(lower, gc_c[it] - gc_r[it], NEG)) for it in group}
        a_mats = [jnp.where(strict, kk[khead(it)] * beta_c[it] * decay[it], 0.0) for it in group]
        t_inv = dict(zip(group, _unit_lower_inverses(a_mats)))

        sols = {}
        for it in group:
            sq, hv = it
            v_h = acts[sq][:, 2 * qk_w + hv * dk:2 * qk_w + (hv + 1) * dk]
            sols[it] = _bdot(t_inv[it], jnp.concatenate([v_h * beta_c[it],
                                                         ks[khead(it)] * (beta_c[it] * egc[it])], axis=1))

        states = {it: state_s[it[0], it[1]] for it in group}
        from_state = {it: _bdot(jnp.concatenate([sols[it][:, dk:], qs[khead(it)] * egc[it]], axis=0),
                                states[it]) for it in group}
        v_new = {it: sols[it][:, :dk] - from_state[it][:L] for it in group}
        from_v = {it: _bdot(jnp.concatenate([qk0[khead(it)] * decay[it],
                                             kts[khead(it)] * jnp.exp2(g_last[it] - gc_r[it])], axis=0),
                            v_new[it]) for it in group}
        for it in group:
            state_s[it[0], it[1]] = states[it] * jnp.exp2(g_last[it]) + from_v[it][L:]
            outs[it] = from_state[it][L:] + from_v[it][:L]
    for sq in range(nseq):
        o = jnp.concatenate([outs[sq, hv] for hv in range(GDN_V_HEADS)], axis=1)
        o = o * lax.rsqrt(_head_sums(o * o, dk) * (1.0 / dk) + EPS) * nw_ref[...]
        out_ref[sq] = (o * _silu(z_ref[sq].astype(F32))).astype(BF16)


def _gdn(qkv, z, ba, conv_w, dt_bias_pad, a_log_pad, norm_w):
    b, s, cd = qkv.shape
    vw = z.shape[2]
    nc = s // GDN_CHUNK
    nseq = GDN_SEQS_PER_STEP
    assert b % nseq == 0 and s % GDN_CHUNK == 0, (b, s)
    return pl.pallas_call(
        _gdn_kernel,
        grid=(b // nseq, nc),
        in_specs=[
            pl.BlockSpec((nseq, GDN_CHUNK, cd), lambda bi, ci: (bi, ci, 0)),
            pl.BlockSpec((nseq, GDN_CHUNK, vw), lambda bi, ci: (bi, ci, 0)),
            pl.BlockSpec((nseq, GDN_CHUNK, LANES), lambda bi, ci: (bi, ci, 0)),
            _const_spec(conv_w.shape), _const_spec(dt_bias_pad.shape),
            _const_spec(a_log_pad.shape), _const_spec(norm_w.shape),
        ],
        out_specs=pl.BlockSpec((nseq, GDN_CHUNK, vw), lambda bi, ci: (bi, ci, 0)),
        out_shape=jax.ShapeDtypeStruct((b, s, vw), BF16),
        scratch_shapes=[
            pltpu.VMEM((nseq, CONV_HALO, cd), F32),
            pltpu.VMEM((nseq, GDN_V_HEADS, GDN_DIM, GDN_DIM), F32),
        ],
        compiler_params=_params(("parallel", "arbitrary")),
        name="gated_deltanet",
    )(qkv, z, ba, conv_w, dt_bias_pad, a_log_pad, norm_w)


def _pad_lanes(v, offset=0):
    out = jnp.zeros((1, LANES), F32)
    return out.at[0, offset:offset + v.shape[0]].set(v.astype(F32))


def _row(v):
    return v.astype(F32).reshape(1, -1)


def _even_mixer(x2, batch, seq, norm_w, w_in, q_norm, k_norm, conv_w, conv_b, dt_bias, a_log, d_skip,
                ssm_norm, w_out, slopes):
    mw = MOBA_HEADS * MOBA_HEAD_DIM
    inner = SSM_HEADS * SSM_HEAD_DIM
    conv_dim = inner + 2 * SSM_GROUPS * SSM_STATE
    w = w_in.astype(BF16)
    widths = (mw, mw, mw, inner, conv_dim)
    wdt = jnp.pad(w[:, sum(widths):], ((0, 0), (0, LANES - SSM_HEADS)))
    q, k, v, z, xbc, dt = _inproj_even(x2, _row(norm_w), w, wdt, widths)
    attn = _moba(q.reshape(batch, seq, mw), k.reshape(batch, seq, mw), v.reshape(batch, seq, mw),
                 _row(jnp.tile(q_norm, MOBA_HEADS_PER_STEP)), _row(jnp.tile(k_norm, MOBA_HEADS_PER_STEP)),
                 slopes).reshape(batch * seq, mw)

    ssm = _ssd(xbc.reshape(batch, seq, conv_dim), z.reshape(batch, seq, inner),
               dt.reshape(batch, seq, LANES), conv_w.astype(F32), _row(conv_b),
               _pad_lanes(dt_bias), _pad_lanes(a_log), _row(jnp.repeat(d_skip, SSM_HEAD_DIM)),
               _row(ssm_norm))
    return _outproj_even(x2, attn, ssm.reshape(batch * seq, inner), w_out.astype(BF16))


def _gdn_mixer(x2, batch, seq, norm_w, w_in, conv_w, dt_bias, a_log, gdn_norm, w_out):
    conv_dim = 2 * GDN_K_HEADS * GDN_DIM + GDN_V_HEADS * GDN_DIM
    vw = GDN_V_HEADS * GDN_DIM
    w = w_in.astype(BF16)
    wba = jnp.pad(w[:, conv_dim + vw:], ((0, 0), (0, LANES - 2 * GDN_V_HEADS)))
    qkv, z, ba = _inproj_odd(x2, _row(norm_w), w, wba, (conv_dim, vw))
    o = _gdn(qkv.reshape(batch, seq, conv_dim), z.reshape(batch, seq, vw), ba.reshape(batch, seq, LANES),
             conv_w.astype(F32), _pad_lanes(dt_bias, GDN_V_HEADS), _pad_lanes(a_log, GDN_V_HEADS),
             _row(jnp.tile(gdn_norm, GDN_V_HEADS)))
    return _outproj_odd(x2, o.reshape(batch * seq, vw), w_out.astype(BF16))


def kernel(x, p, norm_mix, norm_ffn, norm_ple, w_in_even, moba_q_norm, moba_k_norm, ssm_conv_w, ssm_conv_b, ssm_dt_bias, ssm_a_log, ssm_d, ssm_norm, w_out_even, w_in_odd, gdn_conv_w, gdn_dt_bias, gdn_a_log, gdn_norm, w_out_odd, ffn_w_gate, ffn_w_up, ffn_conv_w, ffn_conv_b, ffn_w_down, ple_w_proj, ple_w_gate):
    batch, seq, d = x.shape
    depth = p.shape[0]
    slopes = jnp.exp2(-ALIBI_MAX_BIAS * jnp.arange(1, MOBA_HEADS + 1, dtype=F32) / MOBA_HEADS)
    x2 = x.reshape(batch * seq, d)
    p_all = p.reshape(depth, batch * seq, -1)
    for i in range(depth):
        j = i // 2
        if i % 2 == 0:
            x2 = _even_mixer(x2, batch, seq, norm_mix[i], w_in_even[j], moba_q_norm[j], moba_k_norm[j],
                             ssm_conv_w[j], ssm_conv_b[j], ssm_dt_bias[j], ssm_a_log[j], ssm_d[j],
                             ssm_norm[j], w_out_even[j], slopes)
        else:
            x2 = _gdn_mixer(x2, batch, seq, norm_mix[i], w_in_odd[j], gdn_conv_w[j], gdn_dt_bias[j],
                            gdn_a_log[j], gdn_norm[j], w_out_odd[j])
        x2 = _ffn_ple(x2, p_all, i, seq, _row(norm_ffn[i]),
                      ffn_w_gate[i].astype(BF16), ffn_w_up[i].astype(BF16), ffn_conv_w[i].astype(F32),
                      _row(ffn_conv_b[i]), ffn_w_down[i].astype(BF16), _row(norm_ple[i]),
                      ple_w_gate[i].astype(BF16), ple_w_proj[i].astype(BF16))
    return x2.reshape(batch, seq, d)
```

```python
import functools

import jax
import jax.numpy as jnp
from jax import lax
from jax.experimental import pallas as pl
from jax.experimental.pallas import tpu as pltpu

F32 = jnp.float32
BF16 = jnp.bfloat16
HIGHEST = lax.Precision.HIGHEST

EPS = 1e-6
NEG = -1e30
LOG2E = 1.4426950408889634
LANES = 128
CONV_HALO = 8

MOBA_HEADS = 8
MOBA_HEAD_DIM = 64
MOBA_BLOCK = 256
MOBA_TOPK = 3
MOBA_HEADS_PER_STEP = 4
MOBA_ONES_ROWS = 16
ALIBI_MAX_BIAS = 8.0

SSM_HEADS = 16
SSM_HEAD_DIM = 64
SSM_STATE = 128
SSM_GROUPS = 2
SSM_CONV = 4
SSM_CHUNK = 128
SSM_SEQS_PER_STEP = 4

GDN_K_HEADS = 8
GDN_V_HEADS = 16
GDN_DIM = 128
GDN_CONV = 4
GDN_CHUNK = 64
GDN_SEQS_PER_STEP = 2
GDN_CHAINS_PER_GROUP = GDN_SEQS_PER_STEP * GDN_V_HEADS

FFN_CONV = 3
FFN_HALO = 16
FFN_COLS = 256

VMEM_LIMIT = 56 * 1024 * 1024
ROW_TILE = 1024
INPROJ_ROW_TILE = 512


def _rms_rows(x, gain):
    return x * lax.rsqrt(jnp.mean(x * x, axis=-1, keepdims=True) + EPS) * gain


def _sigmoid(x):
    return 0.5 * jnp.tanh(0.5 * x) + 0.5


def _silu(x):
    half = 0.5 * x
    return half + half * jnp.tanh(half)


def _softplus(x):
    return jnp.maximum(x, 0.0) + jnp.log(1.0 + jnp.exp(-jnp.abs(x)))


def _dot(a, b):
    return jnp.dot(a, b, preferred_element_type=F32)


def _dot_nt(a, b):
    return lax.dot_general(a, b, (((1,), (1,)), ((), ())), preferred_element_type=F32)


def _bf16_terms(x):
    hi = x.astype(BF16)
    rest = x - hi.astype(F32)
    mid = rest.astype(BF16)
    lo = (rest - mid.astype(F32)).astype(BF16)
    return [hi, mid, lo]


def _select_rows_dot(sel, x):
    return _dot(jnp.concatenate([sel] * 3, axis=1), jnp.concatenate(_bf16_terms(x), axis=0))


def _select_cols_dot(x, sel):
    return _dot(jnp.concatenate(_bf16_terms(x), axis=1), jnp.concatenate([sel] * 3, axis=0))


def _head_sums(x, head_w):
    rows, width = x.shape
    slab = 2 * LANES
    n = width // slab
    same = (lax.broadcasted_iota(jnp.int32, (slab, slab), 0) // head_w
            == lax.broadcasted_iota(jnp.int32, (slab, slab), 1) // head_w)
    ones = jnp.where(same, 1.0, 0.0).astype(BF16)
    stacked = jnp.concatenate([x[:, i * slab:(i + 1) * slab] for i in range(n)], axis=0)
    sums = _dot(stacked.astype(BF16), ones)
    return jnp.concatenate([sums[i * rows:(i + 1) * rows] for i in range(n)], axis=1)


def _const_spec(shape):
    zeros = (0,) * len(shape)
    return pl.BlockSpec(shape, lambda *_: zeros, pipeline_mode=pl.Buffered(1))


def _params(semantics):
    return pltpu.CompilerParams(dimension_semantics=semantics, vmem_limit_bytes=VMEM_LIMIT)


def _causal_conv(ext, w_ref, width, halo):
    y = w_ref[width - 1:width, :] * ext[halo:]
    for shift in range(1, width):
        y = y + w_ref[width - 1 - shift:width - shift, :] * pltpu.roll(ext, shift, axis=0)[halo:]
    return y


def _project_columns(hn, w_ref, out_refs):
    c0 = 0
    for o_ref in out_refs:
        n = o_ref.shape[1]
        o_ref[...] = _dot(hn, w_ref[:, c0:c0 + n]).astype(BF16)
        c0 += n


def _inproj_even_kernel(x_ref, g_ref, w_ref, wdt_ref, q_ref, k_ref, v_ref, z_ref, xbc_ref, dt_ref):
    hn = _rms_rows(x_ref[...], g_ref[...]).astype(BF16)
    _project_columns(hn, w_ref, (q_ref, k_ref, v_ref, z_ref, xbc_ref))
    dt_ref[...] = _dot(hn, wdt_ref[...])


def _inproj_even(x2, gain, w, wdt, widths, tm=INPROJ_ROW_TILE):
    t, d = x2.shape
    outs = [(n, BF16) for n in widths] + [(wdt.shape[1], F32)]
    return pl.pallas_call(
        _inproj_even_kernel,
        grid=(t // tm,),
        in_specs=[pl.BlockSpec((tm, d), lambda i: (i, 0))] + [_const_spec(c.shape) for c in (gain, w, wdt)],
        out_specs=[pl.BlockSpec((tm, n), lambda i: (i, 0)) for n, _ in outs],
        out_shape=[jax.ShapeDtypeStruct((t, n), dt) for n, dt in outs],
        compiler_params=_params(("parallel",)),
        name="inproj_even",
    )(x2, gain, w, wdt)


def _moba_kernel(slopes_ref, q_ref, k_ref, v_ref, qg_ref, kg_ref, o_ref, kn_s, kbar_s, vt_s):
    hp = pl.program_id(1)
    seq = k_ref.shape[1]
    nb = seq // MOBA_BLOCK
    dh = MOBA_HEAD_DIM
    hps = MOBA_HEADS_PER_STEP
    heads = range(hps)

    lanes = hps * dh
    same_head = (lax.broadcasted_iota(jnp.int32, (lanes, lanes), 0) // dh
                 == lax.broadcasted_iota(jnp.int32, (lanes, lanes), 1) // dh)
    head_mean = jnp.where(same_head, 1.0 / dh, 0.0).astype(BF16)

    def head_rms(x, gain):
        return x * lax.rsqrt(_dot((x * x).astype(BF16), head_mean) + EPS) * gain

    kn = head_rms(k_ref[0].astype(F32), kg_ref[...])
    kn_s[...] = kn.astype(BF16)
    kbar_s[...] = jnp.mean(kn.reshape(nb, MOBA_BLOCK, lanes), axis=1)
    vt_s[...] = v_ref[0].astype(F32).T.astype(BF16)

    lane_head = lax.broadcasted_iota(jnp.int32, (MOBA_BLOCK, lanes), 1) // dh
    blk = lax.broadcasted_iota(jnp.int32, (nb, MOBA_BLOCK), 0)
    kbar_head = lax.broadcasted_iota(jnp.int32, (nb, lanes), 1) // dh
    kbar = kbar_s[...]
    kbar_rows = jnp.concatenate([jnp.where(kbar_head == hh, kbar, 0.0) for hh in heads], axis=0)
    kidx = lax.broadcasted_iota(jnp.int32, (MOBA_BLOCK, MOBA_BLOCK), 0)
    qidx = lax.broadcasted_iota(jnp.int32, (MOBA_BLOCK, MOBA_BLOCK), 1)
    rel = (kidx - qidx).astype(F32)
    causal = rel <= 0.0
    slopes2 = [slopes_ref[hp * hps + hh] * LOG2E for hh in heads]
    q_gain = qg_ref[...] * (dh ** -0.5 * LOG2E)
    alibis = [slope2 * rel for slope2 in slopes2]

    def score_stage(own):
        rows = slice(own * MOBA_BLOCK, (own + 1) * MOBA_BLOCK)
        qn = head_rms(q_ref[0, rows, :].astype(F32), q_gain)
        qs = [jnp.where(lane_head == hh, qn, 0.0).astype(BF16) for hh in heads]
        scores = {(hh, j): _dot_nt(kn_s[j * MOBA_BLOCK:(j + 1) * MOBA_BLOCK, :], qs[hh])
                  for hh in heads for j in range(own + 1)}
        gates = lax.dot_general(kbar_rows, qn, (((1,), (1,)), ((), ())),
                                precision=HIGHEST, preferred_element_type=F32)
        selb = []
        for hh in heads:
            gate = gates[hh * nb:(hh + 1) * nb, :]
            rank = jnp.zeros(gate.shape, F32)
            for m in range(own):
                gm = gate[m:m + 1, :]
                beats = (gm > gate) | ((gm == gate) & (m < blk))
                rank = rank + jnp.where(beats, 1.0, 0.0)
            selected = (blk < own) & (rank < float(min(MOBA_TOPK, nb - 1)))
            selb.append(jnp.where(selected, 0.0, NEG))
        return scores, selb

    def softmax_pv_stage(own, scores, selb):
        rows = slice(own * MOBA_BLOCK, (own + 1) * MOBA_BLOCK)
        probs = []
        for hh in heads:
            tiles = [scores[hh, j] + alibis[hh] for j in range(own)]
            tiles.append(jnp.where(causal, scores[hh, own] + alibis[hh], NEG))
            col_bias = [selb[hh][j:j + 1, :] + slopes2[hh] * float((j - own) * MOBA_BLOCK) for j in range(own)]
            col_bias.append(jnp.zeros((1, MOBA_BLOCK), F32))
            m = jnp.max(tiles[0], axis=0, keepdims=True) + col_bias[0]
            for t, cb in zip(tiles[1:], col_bias[1:]):
                m = jnp.maximum(m, jnp.max(t, axis=0, keepdims=True) + cb)
            probs.append(jnp.concatenate([jnp.exp2(t + (cb - m)).astype(BF16) for t, cb in zip(tiles, col_bias)],
                                         axis=0))
        keys = (own + 1) * MOBA_BLOCK
        ones_rows = jnp.ones((MOBA_ONES_ROWS, keys), BF16)
        pv = [_dot(jnp.concatenate([vt_s[hh * dh:(hh + 1) * dh, 0:keys], ones_rows], axis=0), probs[hh])
              for hh in heads]
        outs = [r[:dh] / r[dh:dh + 1] for r in pv]
        o_ref[0, rows, :] = jnp.concatenate(outs, axis=0).T.astype(BF16)

    staged = score_stage(0)
    for own in range(nb):
        upcoming = score_stage(own + 1) if own + 1 < nb else None
        softmax_pv_stage(own, *staged)
        staged = upcoming


def _moba(q, k, v, q_gain, k_gain, slopes):
    b, s, width = q.shape
    dh = MOBA_HEAD_DIM
    hps = MOBA_HEADS_PER_STEP
    lanes = hps * dh
    nb = s // MOBA_BLOCK
    seq_spec = pl.BlockSpec((1, s, lanes), lambda bi, hi: (bi, 0, hi))
    return pl.pallas_call(
        _moba_kernel,
        grid=(b, width // lanes),
        in_specs=[
            pl.BlockSpec(memory_space=pltpu.SMEM),
            seq_spec, seq_spec, seq_spec,
            pl.BlockSpec((1, lanes), lambda bi, hi: (0, 0)),
            pl.BlockSpec((1, lanes), lambda bi, hi: (0, 0)),
        ],
        out_specs=seq_spec,
        out_shape=jax.ShapeDtypeStruct((b, s, width), BF16),
        scratch_shapes=[
            pltpu.VMEM((s, lanes), BF16),
            pltpu.VMEM((nb, lanes), F32),
            pltpu.VMEM((lanes, s), BF16),
        ],
        compiler_params=_params(("parallel", "parallel")),
        name="moba_attention",
    )(slopes, q, k, v, q_gain, k_gain)


def _ssd_kernel(xbc_ref, z_ref, dt_ref, cw_ref, cb_ref, dtb_ref, alog_ref, dskip_ref, nw_ref,
                out_ref, hist_s, state_s):
    c = pl.program_id(1)
    L = SSM_CHUNK
    inner = SSM_HEADS * SSM_HEAD_DIM
    gw = SSM_STATE
    pair_w = 2 * SSM_HEAD_DIM

    @pl.when(c == 0)
    def _():
        hist_s[...] = jnp.zeros_like(hist_s)
        state_s[...] = jnp.zeros_like(state_s)

    row = lax.broadcasted_iota(jnp.int32, (L, L), 0)
    col = lax.broadcasted_iota(jnp.int32, (L, L), 1)
    causal = row >= col
    causal_ones = jnp.where(causal, 1.0, 0.0).astype(BF16)
    hrow = lax.broadcasted_iota(jnp.int32, (LANES, inner), 0)
    hcol = lax.broadcasted_iota(jnp.int32, (LANES, inner), 1)
    widen = jnp.where(hcol // SSM_HEAD_DIM == hrow, 1.0, 0.0).astype(BF16)
    lane = lax.broadcasted_iota(jnp.int32, (L, pair_w), 1)
    first_head = lane < SSM_HEAD_DIM
    heads_per_group = SSM_HEADS // SSM_GROUPS
    gsz = inner // SSM_GROUPS

    for sq in range(xbc_ref.shape[0]):
        cur = xbc_ref[sq].astype(F32)
        ext = jnp.concatenate([hist_s[sq], cur], axis=0)
        hist_s[sq] = cur[L - CONV_HALO:, :]
        act = _silu(_causal_conv(ext, cw_ref, SSM_CONV, CONV_HALO) + cb_ref[...])
        xs = act[:, :inner]

        dt = _softplus(dt_ref[sq] + dtb_ref[...])
        a = dt * (-jnp.exp(alog_ref[...]) * LOG2E)
        acum = _select_rows_dot(causal_ones, a)
        acum_t = acum.T
        widened = _select_cols_dot(jnp.concatenate([dt, acum], axis=0), widen)
        dt_w = widened[:L]
        acum_w = widened[L:]
        last_w = acum_w[L - 1:L, :]
        xdt = xs * dt_w
        xdt_bf = xdt.astype(BF16)
        xdt_dec_bf = (xdt * jnp.exp2(last_w - acum_w)).astype(BF16)
        exp_acum_w = jnp.exp2(acum_w)
        chunk_decay_w = jnp.exp2(last_w)

        ys = []
        for g in range(SSM_GROUPS):
            bg = act[:, inner + g * gw:inner + (g + 1) * gw]
            cg = act[:, inner + SSM_GROUPS * gw + g * gw:inner + SSM_GROUPS * gw + (g + 1) * gw]
            cg_bf = cg.astype(BF16)
            cb = _dot_nt(cg_bf, bg.astype(BF16))
            bg_t = bg.T.astype(BF16)
            for pp in range(heads_per_group // 2):
                h0 = g * heads_per_group + 2 * pp
                pi = h0 // 2
                sl = slice(h0 * SSM_HEAD_DIM, h0 * SSM_HEAD_DIM + pair_w)
                yd = []
                for hh in (h0, h0 + 1):
                    diff = jnp.broadcast_to(acum[:, hh:hh + 1], (L, L)) - acum_t[hh:hh + 1, :]
                    decay = jnp.exp2(jnp.where(causal, diff, NEG))
                    yd.append(_dot((cb * decay).astype(BF16), xdt_bf[:, sl]))
                y_diag = jnp.where(first_head, yd[0], yd[1])
                prev = state_s[sq, pi]
                y_off = _dot(cg_bf, prev.astype(BF16)) * exp_acum_w[:, sl]
                state_s[sq, pi] = prev * chunk_decay_w[:, sl] + _dot(bg_t, xdt_dec_bf[:, sl])
                ys.append(y_diag + y_off + dskip_ref[:, sl] * xs[:, sl])
        y = jnp.concatenate(ys, axis=1)

        yg = y * _silu(z_ref[sq].astype(F32))
        parts = []
        for g in range(SSM_GROUPS):
            seg = yg[:, g * gsz:(g + 1) * gsz]
            parts.append(seg * lax.rsqrt(jnp.mean(seg * seg, axis=-1, keepdims=True) + EPS))
        out_ref[sq] = (jnp.concatenate(parts, axis=1) * nw_ref[...]).astype(BF16)


def _ssd(xbc, z, dt, conv_w, conv_b, dt_bias, a_log, d_skip_w, norm_w):
    b, s, cd = xbc.shape
    inner = z.shape[2]
    nc = s // SSM_CHUNK
    nseq = SSM_SEQS_PER_STEP
    assert b % nseq == 0 and s % SSM_CHUNK == 0, (b, s)
    return pl.pallas_call(
        _ssd_kernel,
        grid=(b // nseq, nc),
        in_specs=[
            pl.BlockSpec((nseq, SSM_CHUNK, cd), lambda bi, ci: (bi, ci, 0)),
            pl.BlockSpec((nseq, SSM_CHUNK, inner), lambda bi, ci: (bi, ci, 0)),
            pl.BlockSpec((nseq, SSM_CHUNK, LANES), lambda bi, ci: (bi, ci, 0)),
            _const_spec(conv_w.shape), _const_spec(conv_b.shape), _const_spec(dt_bias.shape),
            _const_spec(a_log.shape), _const_spec(d_skip_w.shape), _const_spec(norm_w.shape),
        ],
        out_specs=pl.BlockSpec((nseq, SSM_CHUNK, inner), lambda bi, ci: (bi, ci, 0)),
        out_shape=jax.ShapeDtypeStruct((b, s, inner), BF16),
        scratch_shapes=[
            pltpu.VMEM((nseq, CONV_HALO, cd), F32),
            pltpu.VMEM((nseq, SSM_HEADS // 2, SSM_STATE, 2 * SSM_HEAD_DIM), F32),
        ],
        compiler_params=_params(("parallel", "arbitrary")),
        name="ssd_heads",
    )(xbc, z, dt, conv_w, conv_b, dt_bias, a_log, d_skip_w, norm_w)


def _outproj_even_kernel(x_ref, a_ref, s_ref, w_ref, o_ref):
    na = a_ref.shape[1]
    o_ref[...] = x_ref[...] + _dot(a_ref[...], w_ref[:na, :]) + _dot(s_ref[...], w_ref[na:, :])


def _outproj_even(x2, attn, ssm, w, tm=ROW_TILE):
    t, d = x2.shape
    return pl.pallas_call(
        _outproj_even_kernel,
        grid=(t // tm,),
        in_specs=[pl.BlockSpec((tm, d), lambda i: (i, 0)),
                  pl.BlockSpec((tm, attn.shape[1]), lambda i: (i, 0)),
                  pl.BlockSpec((tm, ssm.shape[1]), lambda i: (i, 0)),
                  _const_spec(w.shape)],
        out_specs=pl.BlockSpec((tm, d), lambda i: (i, 0)),
        out_shape=jax.ShapeDtypeStruct((t, d), F32),
        compiler_params=_params(("parallel",)),
        name="outproj_even",
    )(x2, attn, ssm, w)


def _outproj_odd_kernel(x_ref, o_in_ref, w_ref, o_ref):
    o_ref[...] = x_ref[...] + _dot(o_in_ref[...], w_ref[...])


def _outproj_odd(x2, o_in, w, tm=ROW_TILE):
    t, d = x2.shape
    return pl.pallas_call(
        _outproj_odd_kernel,
        grid=(t // tm,),
        in_specs=[pl.BlockSpec((tm, d), lambda i: (i, 0)),
                  pl.BlockSpec((tm, o_in.shape[1]), lambda i: (i, 0)),
                  _const_spec(w.shape)],
        out_specs=pl.BlockSpec((tm, d), lambda i: (i, 0)),
        out_shape=jax.ShapeDtypeStruct((t, d), F32),
        compiler_params=_params(("parallel",)),
        name="outproj_odd",
    )(x2, o_in, w)


def _ffn_ple_kernel(x_ref, halo_ref, p_ref, gf_ref, wg_ref, wu_ref, cw_ref, cb_ref, wd_ref,
                    gp_ref, wpg_ref, wpp_ref, o_ref, *, tm, seq):
    i = pl.program_id(0)
    x = x_ref[...]
    keep_halo = jnp.where((i * tm) % seq == 0, 0.0, 1.0)
    halo = halo_ref[...] * keep_halo
    hn = _rms_rows(jnp.concatenate([halo, x], axis=0), gf_ref[...]).astype(BF16)
    hn_cur = hn[FFN_HALO:]

    d_ff = wg_ref.shape[1]
    acts = []
    for j in range(d_ff // FFN_COLS):
        cs = slice(j * FFN_COLS, (j + 1) * FFN_COLS)
        g = _dot(hn, wg_ref[:, cs])
        gc = cb_ref[:, cs] + cw_ref[FFN_CONV - 1:FFN_CONV, cs] * g[FFN_HALO:]
        for shift in range(1, FFN_CONV):
            gc = gc + cw_ref[FFN_CONV - 1 - shift:FFN_CONV - shift, cs] * pltpu.roll(g, shift, axis=0)[FFN_HALO:]
        u = _dot(hn_cur, wu_ref[:, cs])
        acts.append((_silu(gc) * u).astype(BF16))
    x1 = x + _dot(jnp.concatenate(acts, axis=1), wd_ref[...])
    hp = _rms_rows(x1, gp_ref[...]).astype(BF16)
    gate = _sigmoid(_dot(hp, wpg_ref[...]))
    o_ref[...] = x1 + gate * _dot(p_ref[...].astype(BF16), wpp_ref[...])


def _ffn_ple(x2, p_all, layer, seq, gain_ffn, wg, wu, conv_w, conv_b, wd, gain_ple, wpg, wpp, tm=ROW_TILE):
    t, d = x2.shape
    halo_blocks = tm // FFN_HALO
    consts = (gain_ffn, wg, wu, conv_w, conv_b, wd, gain_ple, wpg, wpp)
    return pl.pallas_call(
        functools.partial(_ffn_ple_kernel, tm=tm, seq=seq),
        grid=(t // tm,),
        in_specs=[pl.BlockSpec((tm, d), lambda i: (i, 0)),
                  pl.BlockSpec((FFN_HALO, d), lambda i: (jnp.maximum(i * halo_blocks - 1, 0), 0)),
                  pl.BlockSpec((None, tm, p_all.shape[2]), lambda i: (layer, i, 0))]
        + [_const_spec(c.shape) for c in consts],
        out_specs=pl.BlockSpec((tm, d), lambda i: (i, 0)),
        out_shape=jax.ShapeDtypeStruct((t, d), F32),
        compiler_params=_params(("parallel",)),
        name="ffn_ple",
    )(x2, x2, p_all, *consts)


def _inproj_odd_kernel(x_ref, g_ref, w_ref, wba_ref, qkv_ref, z_ref, ba_ref):
    hn = _rms_rows(x_ref[...], g_ref[...]).astype(BF16)
    _project_columns(hn, w_ref, (qkv_ref, z_ref))
    ba_ref[...] = _dot(hn, wba_ref[...])


def _inproj_odd(x2, gain, w, wba, widths, tm=INPROJ_ROW_TILE):
    t, d = x2.shape
    outs = [(n, BF16) for n in widths] + [(wba.shape[1], F32)]
    return pl.pallas_call(
        _inproj_odd_kernel,
        grid=(t // tm,),
        in_specs=[pl.BlockSpec((tm, d), lambda i: (i, 0))] + [_const_spec(c.shape) for c in (gain, w, wba)],
        out_specs=[pl.BlockSpec((tm, n), lambda i: (i, 0)) for n, _ in outs],
        out_shape=[jax.ShapeDtypeStruct((t, n), dt) for n, dt in outs],
        compiler_params=_params(("parallel",)),
        name="inproj_odd",
    )(x2, gain, w, wba)


def _bdot(a, b):
    return _dot(a.astype(BF16), b.astype(BF16))


def _unit_lower_inverses(mats):
    L = mats[0].shape[0]
    row = lax.broadcasted_iota(jnp.int32, (L, L), 0)
    col = lax.broadcasted_iota(jnp.int32, (L, L), 1)
    eye = jnp.where(row == col, 1.0, 0.0)
    base = 8
    same = (row // base) == (col // base)
    ns = [-jnp.where(same, a, 0.0) for a in mats]
    ts = [eye + n for n in ns]
    ns_bf = [n.astype(BF16) for n in ns]
    pws_bf = [_dot(n, n).astype(BF16) for n in ns_bf]
    ts = [t + _dot(t.astype(BF16), pw) for t, pw in zip(ts, pws_bf)]
    pws_bf = [_dot(pw, pw).astype(BF16) for pw in pws_bf]
    ts = [t + _dot(t.astype(BF16), pw) for t, pw in zip(ts, pws_bf)]
    size = base
    while size < L:
        wider = (row // (2 * size)) == (col // (2 * size))
        offs = [jnp.where(wider & jnp.logical_not(same), a, 0.0).astype(BF16) for a in mats]
        ts_bf = [t.astype(BF16) for t in ts]
        ys = [_dot(off, t).astype(BF16) for off, t in zip(offs, ts_bf)]
        ts = [t - _dot(t_bf, y) for t, t_bf, y in zip(ts, ts_bf, ys)]
        same = wider
        size *= 2
    return ts


def _gdn_kernel(qkv_ref, z_ref, ba_ref, cw_ref, dtb_ref, alog_ref, nw_ref, out_ref, hist_s, state_s):
    c = pl.program_id(1)
    L = GDN_CHUNK
    dk = GDN_DIM
    qk_w = GDN_K_HEADS * dk
    rep = GDN_V_HEADS // GDN_K_HEADS
    nseq = qkv_ref.shape[0]

    @pl.when(c == 0)
    def _():
        hist_s[...] = jnp.zeros_like(hist_s)
        state_s[...] = jnp.zeros_like(state_s)

    row = lax.broadcasted_iota(jnp.int32, (L, L), 0)
    col = lax.broadcasted_iota(jnp.int32, (L, L), 1)
    lower = row >= col
    strict = row > col

    items = [(sq, hv) for sq in range(nseq) for hv in range(GDN_V_HEADS)]
    acts, qs, ks, kts, kk, qk0 = {}, {}, {}, {}, {}, {}
    beta_c, gc_c, gc_r = {}, {}, {}
    for sq in range(nseq):
        cur = qkv_ref[sq].astype(F32)
        ext = jnp.concatenate([hist_s[sq], cur], axis=0)
        hist_s[sq] = cur[L - CONV_HALO:, :]
        act = _silu(_causal_conv(ext, cw_ref, GDN_CONV, CONV_HALO))
        acts[sq] = act

        ba = ba_ref[sq]
        beta = _sigmoid(ba)
        g = (-jnp.exp(alog_ref[...]) * LOG2E) * _softplus(ba + dtb_ref[...])
        gc = _select_rows_dot(jnp.where(lower, 1.0, 0.0).astype(BF16), g)
        gc_t = gc.T
        for hv in range(GDN_V_HEADS):
            beta_c[sq, hv] = beta[:, hv:hv + 1]
            gc_c[sq, hv] = gc[:, GDN_V_HEADS + hv:GDN_V_HEADS + hv + 1]
            gc_r[sq, hv] = gc_t[GDN_V_HEADS + hv:GDN_V_HEADS + hv + 1, :]

        qk_act = act[:, :2 * qk_w]
        qk_n = qk_act * lax.rsqrt(_head_sums(qk_act * qk_act, dk) + EPS)
        for kh in range(GDN_K_HEADS):
            qn = qk_n[:, kh * dk:(kh + 1) * dk] * (dk ** -0.5)
            kn = qk_n[:, qk_w + kh * dk:qk_w + (kh + 1) * dk]
            kt = kn.T
            prod = _bdot(jnp.concatenate([kn, qn], axis=0), kt)
            qs[sq, kh], ks[sq, kh], kts[sq, kh] = qn, kn, kt
            kk[sq, kh], qk0[sq, kh] = prod[:L], prod[L:]

    def khead(it):
        return it[0], it[1] // rep

    outs = {}
    for g0 in range(0, len(items), GDN_CHAINS_PER_GROUP):
        group = items[g0:g0 + GDN_CHAINS_PER_GROUP]
        g_last = {it: gc_r[it][:, L - 1:L] for it in group}
        egc = {it: jnp.exp2(gc_c[it]) for it in group}
        decay = {it: jnp.exp2(jnp.where(lower, gc_c[it] - gc_r[it], NEG)) for it in group}
        a_mats = [jnp.where(strict, kk[khead(it)] * beta_c[it] * decay[it], 0.0) for it in group]
        t_inv = dict(zip(group, _unit_lower_inverses(a_mats)))

        sols = {}
        for it in group:
            sq, hv = it
            v_h = acts[sq][:, 2 * qk_w + hv * dk:2 * qk_w + (hv + 1) * dk]
            sols[it] = _bdot(t_inv[it], jnp.concatenate([v_h * beta_c[it],
                                                         ks[khead(it)] * (beta_c[it] * egc[it])], axis=1))

        states = {it: state_s[it[0], it[1]] for it in group}
        from_state = {it: _bdot(jnp.concatenate([sols[it][:, dk:], qs[khead(it)] * egc[it]], axis=0),
                                states[it]) for it in group}
        v_new = {it: sols[it][:, :dk] - from_state[it][:L] for it in group}
        from_v = {it: _bdot(jnp.concatenate([qk0[khead(it)] * decay[it],
                                             kts[khead(it)] * jnp.exp2(g_last[it] - gc_r[it])], axis=0),
                            v_new[it]) for it in group}
        for it in group:
            state_s[it[0], it[1]] = states[it] * jnp.exp2(g_last[it]) + from_v[it][L:]
            outs[it] = from_state[it][L:] + from_v[it][:L]
    for sq in range(nseq):
        o = jnp.concatenate([outs[sq, hv] for hv in range(GDN_V_HEADS)], axis=1)
        o = o * lax.rsqrt(_head_sums(o * o, dk) * (1.0 / dk) + EPS) * nw_ref[...]
        out_ref[sq] = (o * _silu(z_ref[sq].astype(F32))).astype(BF16)


def _gdn(qkv, z, ba, conv_w, dt_bias_pad, a_log_pad, norm_w):
    b, s, cd = qkv.shape
    vw = z.shape[2]
    nc = s // GDN_CHUNK
    nseq = GDN_SEQS_PER_STEP
    assert b % nseq == 0 and s % GDN_CHUNK == 0, (b, s)
    return pl.pallas_call(
        _gdn_kernel,
        grid=(b // nseq, nc),
        in_specs=[
            pl.BlockSpec((nseq, GDN_CHUNK, cd), lambda bi, ci: (bi, ci, 0)),
            pl.BlockSpec((nseq, GDN_CHUNK, vw), lambda bi, ci: (bi, ci, 0)),
            pl.BlockSpec((nseq, GDN_CHUNK, LANES), lambda bi, ci: (bi, ci, 0)),
            _const_spec(conv_w.shape), _const_spec(dt_bias_pad.shape),
            _const_spec(a_log_pad.shape), _const_spec(norm_w.shape),
        ],
        out_specs=pl.BlockSpec((nseq, GDN_CHUNK, vw), lambda bi, ci: (bi, ci, 0)),
        out_shape=jax.ShapeDtypeStruct((b, s, vw), BF16),
        scratch_shapes=[
            pltpu.VMEM((nseq, CONV_HALO, cd), F32),
            pltpu.VMEM((nseq, GDN_V_HEADS, GDN_DIM, GDN_DIM), F32),
        ],
        compiler_params=_params(("parallel", "arbitrary")),
        name="gated_deltanet",
    )(qkv, z, ba, conv_w, dt_bias_pad, a_log_pad, norm_w)


def _pad_lanes(v, offset=0):
    out = jnp.zeros((1, LANES), F32)
    return out.at[0, offset:offset + v.shape[0]].set(v.astype(F32))


def _row(v):
    return v.astype(F32).reshape(1, -1)


def _even_mixer(x2, batch, seq, norm_w, w_in, q_norm, k_norm, conv_w, conv_b, dt_bias, a_log, d_skip,
                ssm_norm, w_out, slopes):
    mw = MOBA_HEADS * MOBA_HEAD_DIM
    inner = SSM_HEADS * SSM_HEAD_DIM
    conv_dim = inner + 2 * SSM_GROUPS * SSM_STATE
    w = w_in.astype(BF16)
    widths = (mw, mw, mw, inner, conv_dim)
    wdt = jnp.pad(w[:, sum(widths):], ((0, 0), (0, LANES - SSM_HEADS)))
    q, k, v, z, xbc, dt = _inproj_even(x2, _row(norm_w), w, wdt, widths)
    attn = _moba(q.reshape(batch, seq, mw), k.reshape(batch, seq, mw), v.reshape(batch, seq, mw),
                 _row(jnp.tile(q_norm, MOBA_HEADS_PER_STEP)), _row(jnp.tile(k_norm, MOBA_HEADS_PER_STEP)),
                 slopes).reshape(batch * seq, mw)

    ssm = _ssd(xbc.reshape(batch, seq, conv_dim), z.reshape(batch, seq, inner),
               dt.reshape(batch, seq, LANES), conv_w.astype(F32), _row(conv_b),
               _pad_lanes(dt_bias), _pad_lanes(a_log), _row(jnp.repeat(d_skip, SSM_HEAD_DIM)),
               _row(ssm_norm))
    return _outproj_even(x2, attn, ssm.reshape(batch * seq, inner), w_out.astype(BF16))


def _gdn_mixer(x2, batch, seq, norm_w, w_in, conv_w, dt_bias, a_log, gdn_norm, w_out):
    conv_dim = 2 * GDN_K_HEADS * GDN_DIM + GDN_V_HEADS * GDN_DIM
    vw = GDN_V_HEADS * GDN_DIM
    w = w_in.astype(BF16)
    wba = jnp.pad(w[:, conv_dim + vw:], ((0, 0), (0, LANES - 2 * GDN_V_HEADS)))
    qkv, z, ba = _inproj_odd(x2, _row(norm_w), w, wba, (conv_dim, vw))
    o = _gdn(qkv.reshape(batch, seq, conv_dim), z.reshape(batch, seq, vw), ba.reshape(batch, seq, LANES),
             conv_w.astype(F32), _pad_lanes(dt_bias, GDN_V_HEADS), _pad_lanes(a_log, GDN_V_HEADS),
             _row(jnp.tile(gdn_norm, GDN_V_HEADS)))
    return _outproj_odd(x2, o.reshape(batch * seq, vw), w_out.astype(BF16))


def kernel(x, p, norm_mix, norm_ffn, norm_ple, w_in_even, moba_q_norm, moba_k_norm, ssm_conv_w, ssm_conv_b, ssm_dt_bias, ssm_a_log, ssm_d, ssm_norm, w_out_even, w_in_odd, gdn_conv_w, gdn_dt_bias, gdn_a_log, gdn_norm, w_out_odd, ffn_w_gate, ffn_w_up, ffn_conv_w, ffn_conv_b, ffn_w_down, ple_w_proj, ple_w_gate):
    batch, seq, d = x.shape
    depth = p.shape[0]
    assert seq % ROW_TILE == 0 and seq % MOBA_BLOCK == 0, seq
    slopes = jnp.exp2(-ALIBI_MAX_BIAS * jnp.arange(1, MOBA_HEADS + 1, dtype=F32) / MOBA_HEADS)
    x2 = x.reshape(batch * seq, d)
    p_all = p.reshape(depth, batch * seq, -1)
    for i in range(depth):
        j = i // 2
        if i % 2 == 0:
            x2 = _even_mixer(x2, batch, seq, norm_mix[i], w_in_even[j], moba_q_norm[j], moba_k_norm[j],
                             ssm_conv_w[j], ssm_conv_b[j], ssm_dt_bias[j], ssm_a_log[j], ssm_d[j],
                             ssm_norm[j], w_out_even[j], slopes)
        else:
            x2 = _gdn_mixer(x2, batch, seq, norm_mix[i], w_in_odd[j], gdn_conv_w[j], gdn_dt_bias[j],
                            gdn_a_log[j], gdn_norm[j], w_out_odd[j])
        x2 = _ffn_ple(x2, p_all, i, seq, _row(norm_ffn[i]),
                      ffn_w_gate[i].astype(BF16), ffn_w_up[i].astype(BF16), ffn_conv_w[i].astype(F32),
                      _row(ffn_conv_b[i]), ffn_w_down[i].astype(BF16), _row(norm_ple[i]),
                      ple_w_gate[i].astype(BF16), ple_w_proj[i].astype(BF16))
    return x2.reshape(batch, seq, d)
```

```python
import functools

import jax
import jax.numpy as jnp
from jax import lax
from jax.experimental import pallas as pl
from jax.experimental.pallas import tpu as pltpu

F32 = jnp.float32
BF16 = jnp.bfloat16
HIGHEST = lax.Precision.HIGHEST

EPS = 1e-6
NEG = -1e30
LOG2E = 1.4426950408889634
LANES = 128
CONV_HALO = 8

MOBA_HEADS = 8
MOBA_HEAD_DIM = 64
MOBA_BLOCK = 256
MOBA_TOPK = 3
MOBA_HEADS_PER_STEP = 4
MOBA_ONES_ROWS = 16
ALIBI_MAX_BIAS = 8.0

SSM_HEADS = 16
SSM_HEAD_DIM = 64
SSM_STATE = 128
SSM_GROUPS = 2
SSM_CONV = 4
SSM_CHUNK = 128
SSM_SEQS_PER_STEP = 4

GDN_K_HEADS = 8
GDN_V_HEADS = 16
GDN_DIM = 128
GDN_CONV = 4
GDN_CHUNK = 128
GDN_SEQS_PER_STEP = 2
GDN_CHAINS_PER_GROUP = GDN_SEQS_PER_STEP * GDN_V_HEADS

FFN_CONV = 3
FFN_HALO = 16
FFN_COLS = 256

VMEM_LIMIT = 56 * 1024 * 1024
ROW_TILE = 1024
INPROJ_ROW_TILE = 512


def _rms_rows(x, gain):
    return x * lax.rsqrt(jnp.mean(x * x, axis=-1, keepdims=True) + EPS) * gain


def _sigmoid(x):
    return 0.5 * jnp.tanh(0.5 * x) + 0.5


def _silu(x):
    half = 0.5 * x
    return half + half * jnp.tanh(half)


def _softplus(x):
    return jnp.maximum(x, 0.0) + jnp.log(1.0 + jnp.exp(-jnp.abs(x)))


def _dot(a, b):
    return jnp.dot(a, b, preferred_element_type=F32)


def _dot_nt(a, b):
    return lax.dot_general(a, b, (((1,), (1,)), ((), ())), preferred_element_type=F32)


def _bf16_terms(x):
    hi = x.astype(BF16)
    rest = x - hi.astype(F32)
    mid = rest.astype(BF16)
    lo = (rest - mid.astype(F32)).astype(BF16)
    return [hi, mid, lo]


def _select_rows_dot(sel, x):
    return _dot(jnp.concatenate([sel] * 3, axis=1), jnp.concatenate(_bf16_terms(x), axis=0))


def _select_cols_dot(x, sel):
    return _dot(jnp.concatenate(_bf16_terms(x), axis=1), jnp.concatenate([sel] * 3, axis=0))


def _head_sums(x, head_w):
    rows, width = x.shape
    slab = 2 * LANES
    n = width // slab
    same = (lax.broadcasted_iota(jnp.int32, (slab, slab), 0) // head_w
            == lax.broadcasted_iota(jnp.int32, (slab, slab), 1) // head_w)
    ones = jnp.where(same, 1.0, 0.0).astype(BF16)
    stacked = jnp.concatenate([x[:, i * slab:(i + 1) * slab] for i in range(n)], axis=0)
    sums = _dot(stacked.astype(BF16), ones)
    return jnp.concatenate([sums[i * rows:(i + 1) * rows] for i in range(n)], axis=1)


def _const_spec(shape):
    zeros = (0,) * len(shape)
    return pl.BlockSpec(shape, lambda *_: zeros, pipeline_mode=pl.Buffered(1))


def _params(semantics):
    return pltpu.CompilerParams(dimension_semantics=semantics, vmem_limit_bytes=VMEM_LIMIT)


def _causal_conv(ext, w_ref, width, halo):
    y = w_ref[width - 1:width, :] * ext[halo:]
    for shift in range(1, width):
        y = y + w_ref[width - 1 - shift:width - shift, :] * pltpu.roll(ext, shift, axis=0)[halo:]
    return y


def _project_columns(hn, w_ref, out_refs):
    c0 = 0
    for o_ref in out_refs:
        n = o_ref.shape[1]
        o_ref[...] = _dot(hn, w_ref[:, c0:c0 + n]).astype(BF16)
        c0 += n


def _inproj_even_kernel(x_ref, g_ref, w_ref, wdt_ref, q_ref, k_ref, v_ref, z_ref, xbc_ref, dt_ref):
    hn = _rms_rows(x_ref[...], g_ref[...]).astype(BF16)
    _project_columns(hn, w_ref, (q_ref, k_ref, v_ref, z_ref, xbc_ref))
    dt_ref[...] = _dot(hn, wdt_ref[...])


def _inproj_even(x2, gain, w, wdt, widths, tm=INPROJ_ROW_TILE):
    t, d = x2.shape
    outs = [(n, BF16) for n in widths] + [(wdt.shape[1], F32)]
    return pl.pallas_call(
        _inproj_even_kernel,
        grid=(t // tm,),
        in_specs=[pl.BlockSpec((tm, d), lambda i: (i, 0))] + [_const_spec(c.shape) for c in (gain, w, wdt)],
        out_specs=[pl.BlockSpec((tm, n), lambda i: (i, 0)) for n, _ in outs],
        out_shape=[jax.ShapeDtypeStruct((t, n), dt) for n, dt in outs],
        compiler_params=_params(("parallel",)),
        name="inproj_even",
    )(x2, gain, w, wdt)


def _moba_kernel(slopes_ref, q_ref, k_ref, v_ref, qg_ref, kg_ref, o_ref, kn_s, kbar_s, vt_s):
    hp = pl.program_id(1)
    seq = k_ref.shape[1]
    nb = seq // MOBA_BLOCK
    dh = MOBA_HEAD_DIM
    hps = MOBA_HEADS_PER_STEP
    heads = range(hps)

    lanes = hps * dh
    same_head = (lax.broadcasted_iota(jnp.int32, (lanes, lanes), 0) // dh
                 == lax.broadcasted_iota(jnp.int32, (lanes, lanes), 1) // dh)
    head_mean = jnp.where(same_head, 1.0 / dh, 0.0).astype(BF16)

    def head_rms(x, gain):
        return x * lax.rsqrt(_dot((x * x).astype(BF16), head_mean) + EPS) * gain

    kn = head_rms(k_ref[0].astype(F32), kg_ref[...])
    kn_s[...] = kn.astype(BF16)
    kbar_s[...] = jnp.mean(kn.reshape(nb, MOBA_BLOCK, lanes), axis=1)
    vt_s[...] = v_ref[0].astype(F32).T.astype(BF16)

    lane_head = lax.broadcasted_iota(jnp.int32, (MOBA_BLOCK, lanes), 1) // dh
    blk = lax.broadcasted_iota(jnp.int32, (nb, MOBA_BLOCK), 0)
    kbar_head = lax.broadcasted_iota(jnp.int32, (nb, lanes), 1) // dh
    kbar = kbar_s[...]
    kbar_rows = jnp.concatenate([jnp.where(kbar_head == hh, kbar, 0.0) for hh in heads], axis=0)
    kidx = lax.broadcasted_iota(jnp.int32, (MOBA_BLOCK, MOBA_BLOCK), 0)
    qidx = lax.broadcasted_iota(jnp.int32, (MOBA_BLOCK, MOBA_BLOCK), 1)
    rel = (kidx - qidx).astype(F32)
    causal = rel <= 0.0
    slopes2 = [slopes_ref[hp * hps + hh] * LOG2E for hh in heads]
    q_gain = qg_ref[...] * (dh ** -0.5 * LOG2E)
    alibis = [slope2 * rel for slope2 in slopes2]

    def score_stage(own):
        rows = slice(own * MOBA_BLOCK, (own + 1) * MOBA_BLOCK)
        qn = head_rms(q_ref[0, rows, :].astype(F32), q_gain)
        qs = [jnp.where(lane_head == hh, qn, 0.0).astype(BF16) for hh in heads]
        scores = {(hh, j): _dot_nt(kn_s[j * MOBA_BLOCK:(j + 1) * MOBA_BLOCK, :], qs[hh])
                  for hh in heads for j in range(own + 1)}
        gates = lax.dot_general(kbar_rows, qn, (((1,), (1,)), ((), ())),
                                precision=HIGHEST, preferred_element_type=F32)
        selb = []
        for hh in heads:
            gate = gates[hh * nb:(hh + 1) * nb, :]
            rank = jnp.zeros(gate.shape, F32)
            for m in range(own):
                gm = gate[m:m + 1, :]
                beats = (gm > gate) | ((gm == gate) & (m < blk))
                rank = rank + jnp.where(beats, 1.0, 0.0)
            selected = (blk < own) & (rank < float(min(MOBA_TOPK, nb - 1)))
            selb.append(jnp.where(selected, 0.0, NEG))
        return scores, selb

    def softmax_pv_stage(own, scores, selb):
        rows = slice(own * MOBA_BLOCK, (own + 1) * MOBA_BLOCK)
        probs = []
        for hh in heads:
            tiles = [scores[hh, j] + alibis[hh] for j in range(own)]
            tiles.append(jnp.where(causal, scores[hh, own] + alibis[hh], NEG))
            col_bias = [selb[hh][j:j + 1, :] + slopes2[hh] * float((j - own) * MOBA_BLOCK) for j in range(own)]
            col_bias.append(jnp.zeros((1, MOBA_BLOCK), F32))
            m = jnp.max(tiles[0], axis=0, keepdims=True) + col_bias[0]
            for t, cb in zip(tiles[1:], col_bias[1:]):
                m = jnp.maximum(m, jnp.max(t, axis=0, keepdims=True) + cb)
            probs.append(jnp.concatenate([jnp.exp2(t + (cb - m)).astype(BF16) for t, cb in zip(tiles, col_bias)],
                                         axis=0))
        keys = (own + 1) * MOBA_BLOCK
        ones_rows = jnp.ones((MOBA_ONES_ROWS, keys), BF16)
        pv = [_dot(jnp.concatenate([vt_s[hh * dh:(hh + 1) * dh, 0:keys], ones_rows], axis=0), probs[hh])
              for hh in heads]
        outs = [r[:dh] / r[dh:dh + 1] for r in pv]
        o_ref[0, rows, :] = jnp.concatenate(outs, axis=0).T.astype(BF16)

    staged = score_stage(0)
    for own in range(nb):
        upcoming = score_stage(own + 1) if own + 1 < nb else None
        softmax_pv_stage(own, *staged)
        staged = upcoming


def _moba(q, k, v, q_gain, k_gain, slopes):
    b, s, width = q.shape
    dh = MOBA_HEAD_DIM
    hps = MOBA_HEADS_PER_STEP
    lanes = hps * dh
    nb = s // MOBA_BLOCK
    seq_spec = pl.BlockSpec((1, s, lanes), lambda bi, hi: (bi, 0, hi))
    return pl.pallas_call(
        _moba_kernel,
        grid=(b, width // lanes),
        in_specs=[
            pl.BlockSpec(memory_space=pltpu.SMEM),
            seq_spec, seq_spec, seq_spec,
            pl.BlockSpec((1, lanes), lambda bi, hi: (0, 0)),
            pl.BlockSpec((1, lanes), lambda bi, hi: (0, 0)),
        ],
        out_specs=seq_spec,
        out_shape=jax.ShapeDtypeStruct((b, s, width), BF16),
        scratch_shapes=[
            pltpu.VMEM((s, lanes), BF16),
            pltpu.VMEM((nb, lanes), F32),
            pltpu.VMEM((lanes, s), BF16),
        ],
        compiler_params=_params(("parallel", "parallel")),
        name="moba_attention",
    )(slopes, q, k, v, q_gain, k_gain)


def _ssd_kernel(xbc_ref, z_ref, dt_ref, cw_ref, cb_ref, dtb_ref, alog_ref, dskip_ref, nw_ref,
                out_ref, hist_s, state_s):
    c = pl.program_id(1)
    L = SSM_CHUNK
    inner = SSM_HEADS * SSM_HEAD_DIM
    gw = SSM_STATE
    pair_w = 2 * SSM_HEAD_DIM

    @pl.when(c == 0)
    def _():
        hist_s[...] = jnp.zeros_like(hist_s)
        state_s[...] = jnp.zeros_like(state_s)

    row = lax.broadcasted_iota(jnp.int32, (L, L), 0)
    col = lax.broadcasted_iota(jnp.int32, (L, L), 1)
    causal = row >= col
    causal_ones = jnp.where(causal, 1.0, 0.0).astype(BF16)
    hrow = lax.broadcasted_iota(jnp.int32, (LANES, inner), 0)
    hcol = lax.broadcasted_iota(jnp.int32, (LANES, inner), 1)
    widen = jnp.where(hcol // SSM_HEAD_DIM == hrow, 1.0, 0.0).astype(BF16)
    lane = lax.broadcasted_iota(jnp.int32, (L, pair_w), 1)
    first_head = lane < SSM_HEAD_DIM
    heads_per_group = SSM_HEADS // SSM_GROUPS
    gsz = inner // SSM_GROUPS

    for sq in range(xbc_ref.shape[0]):
        cur = xbc_ref[sq].astype(F32)
        ext = jnp.concatenate([hist_s[sq], cur], axis=0)
        hist_s[sq] = cur[L - CONV_HALO:, :]
        act = _silu(_causal_conv(ext, cw_ref, SSM_CONV, CONV_HALO) + cb_ref[...])
        xs = act[:, :inner]

        dt = _softplus(dt_ref[sq] + dtb_ref[...])
        a = dt * (-jnp.exp(alog_ref[...]) * LOG2E)
        acum = _select_rows_dot(causal_ones, a)
        acum_t = acum.T
        widened = _select_cols_dot(jnp.concatenate([dt, acum], axis=0), widen)
        dt_w = widened[:L]
        acum_w = widened[L:]
        last_w = acum_w[L - 1:L, :]
        xdt = xs * dt_w
        xdt_bf = xdt.astype(BF16)
        xdt_dec_bf = (xdt * jnp.exp2(last_w - acum_w)).astype(BF16)
        exp_acum_w = jnp.exp2(acum_w)
        chunk_decay_w = jnp.exp2(last_w)

        ys = []
        for g in range(SSM_GROUPS):
            bg = act[:, inner + g * gw:inner + (g + 1) * gw]
            cg = act[:, inner + SSM_GROUPS * gw + g * gw:inner + SSM_GROUPS * gw + (g + 1) * gw]
            cg_bf = cg.astype(BF16)
            cb = _dot_nt(cg_bf, bg.astype(BF16))
            bg_t = bg.T.astype(BF16)
            for pp in range(heads_per_group // 2):
                h0 = g * heads_per_group + 2 * pp
                pi = h0 // 2
                sl = slice(h0 * SSM_HEAD_DIM, h0 * SSM_HEAD_DIM + pair_w)
                yd = []
                for hh in (h0, h0 + 1):
                    diff = jnp.broadcast_to(acum[:, hh:hh + 1], (L, L)) - acum_t[hh:hh + 1, :]
                    decay = jnp.exp2(jnp.where(causal, diff, NEG))
                    yd.append(_dot((cb * decay).astype(BF16), xdt_bf[:, sl]))
                y_diag = jnp.where(first_head, yd[0], yd[1])
                prev = state_s[sq, pi]
                y_off = _dot(cg_bf, prev.astype(BF16)) * exp_acum_w[:, sl]
                state_s[sq, pi] = prev * chunk_decay_w[:, sl] + _dot(bg_t, xdt_dec_bf[:, sl])
                ys.append(y_diag + y_off + dskip_ref[:, sl] * xs[:, sl])
        y = jnp.concatenate(ys, axis=1)

        yg = y * _silu(z_ref[sq].astype(F32))
        parts = []
        for g in range(SSM_GROUPS):
            seg = yg[:, g * gsz:(g + 1) * gsz]
            parts.append(seg * lax.rsqrt(jnp.mean(seg * seg, axis=-1, keepdims=True) + EPS))
        out_ref[sq] = (jnp.concatenate(parts, axis=1) * nw_ref[...]).astype(BF16)


def _ssd(xbc, z, dt, conv_w, conv_b, dt_bias, a_log, d_skip_w, norm_w):
    b, s, cd = xbc.shape
    inner = z.shape[2]
    nc = s // SSM_CHUNK
    nseq = SSM_SEQS_PER_STEP
    assert b % nseq == 0 and s % SSM_CHUNK == 0, (b, s)
    return pl.pallas_call(
        _ssd_kernel,
        grid=(b // nseq, nc),
        in_specs=[
            pl.BlockSpec((nseq, SSM_CHUNK, cd), lambda bi, ci: (bi, ci, 0)),
            pl.BlockSpec((nseq, SSM_CHUNK, inner), lambda bi, ci: (bi, ci, 0)),
            pl.BlockSpec((nseq, SSM_CHUNK, LANES), lambda bi, ci: (bi, ci, 0)),
            _const_spec(conv_w.shape), _const_spec(conv_b.shape), _const_spec(dt_bias.shape),
            _const_spec(a_log.shape), _const_spec(d_skip_w.shape), _const_spec(norm_w.shape),
        ],
        out_specs=pl.BlockSpec((nseq, SSM_CHUNK, inner), lambda bi, ci: (bi, ci, 0)),
        out_shape=jax.ShapeDtypeStruct((b, s, inner), BF16),
        scratch_shapes=[
            pltpu.VMEM((nseq, CONV_HALO, cd), F32),
            pltpu.VMEM((nseq, SSM_HEADS // 2, SSM_STATE, 2 * SSM_HEAD_DIM), F32),
        ],
        compiler_params=_params(("parallel", "arbitrary")),
        name="ssd_heads",
    )(xbc, z, dt, conv_w, conv_b, dt_bias, a_log, d_skip_w, norm_w)


def _outproj_even_kernel(x_ref, a_ref, s_ref, w_ref, o_ref):
    na = a_ref.shape[1]
    o_ref[...] = x_ref[...] + _dot(a_ref[...], w_ref[:na, :]) + _dot(s_ref[...], w_ref[na:, :])


def _outproj_even(x2, attn, ssm, w, tm=ROW_TILE):
    t, d = x2.shape
    return pl.pallas_call(
        _outproj_even_kernel,
        grid=(t // tm,),
        in_specs=[pl.BlockSpec((tm, d), lambda i: (i, 0)),
                  pl.BlockSpec((tm, attn.shape[1]), lambda i: (i, 0)),
                  pl.BlockSpec((tm, ssm.shape[1]), lambda i: (i, 0)),
                  _const_spec(w.shape)],
        out_specs=pl.BlockSpec((tm, d), lambda i: (i, 0)),
        out_shape=jax.ShapeDtypeStruct((t, d), F32),
        compiler_params=_params(("parallel",)),
        name="outproj_even",
    )(x2, attn, ssm, w)


def _outproj_odd_kernel(x_ref, o_in_ref, w_ref, o_ref):
    o_ref[...] = x_ref[...] + _dot(o_in_ref[...], w_ref[...])


def _outproj_odd(x2, o_in, w, tm=ROW_TILE):
    t, d = x2.shape
    return pl.pallas_call(
        _outproj_odd_kernel,
        grid=(t // tm,),
        in_specs=[pl.BlockSpec((tm, d), lambda i: (i, 0)),
                  pl.BlockSpec((tm, o_in.shape[1]), lambda i: (i, 0)),
                  _const_spec(w.shape)],
        out_specs=pl.BlockSpec((tm, d), lambda i: (i, 0)),
        out_shape=jax.ShapeDtypeStruct((t, d), F32),
        compiler_params=_params(("parallel",)),
        name="outproj_odd",
    )(x2, o_in, w)


def _ffn_ple_kernel(x_ref, halo_ref, p_ref, gf_ref, wg_ref, wu_ref, cw_ref, cb_ref, wd_ref,
                    gp_ref, wpg_ref, wpp_ref, o_ref, *, tm, seq):
    i = pl.program_id(0)
    x = x_ref[...]
    keep_halo = jnp.where((i * tm) % seq == 0, 0.0, 1.0)
    halo = halo_ref[...] * keep_halo
    hn = _rms_rows(jnp.concatenate([halo, x], axis=0), gf_ref[...]).astype(BF16)
    hn_cur = hn[FFN_HALO:]

    d_ff = wg_ref.shape[1]
    acts = []
    for j in range(d_ff // FFN_COLS):
        cs = slice(j * FFN_COLS, (j + 1) * FFN_COLS)
        g = _dot(hn, wg_ref[:, cs])
        gc = cb_ref[:, cs] + cw_ref[FFN_CONV - 1:FFN_CONV, cs] * g[FFN_HALO:]
        for shift in range(1, FFN_CONV):
            gc = gc + cw_ref[FFN_CONV - 1 - shift:FFN_CONV - shift, cs] * pltpu.roll(g, shift, axis=0)[FFN_HALO:]
        u = _dot(hn_cur, wu_ref[:, cs])
        acts.append((_silu(gc) * u).astype(BF16))
    x1 = x + _dot(jnp.concatenate(acts, axis=1), wd_ref[...])
    hp = _rms_rows(x1, gp_ref[...]).astype(BF16)
    gate = _sigmoid(_dot(hp, wpg_ref[...]))
    o_ref[...] = x1 + gate * _dot(p_ref[...].astype(BF16), wpp_ref[...])


def _ffn_ple(x2, p_all, layer, seq, gain_ffn, wg, wu, conv_w, conv_b, wd, gain_ple, wpg, wpp, tm=ROW_TILE):
    t, d = x2.shape
    halo_blocks = tm // FFN_HALO
    consts = (gain_ffn, wg, wu, conv_w, conv_b, wd, gain_ple, wpg, wpp)
    return pl.pallas_call(
        functools.partial(_ffn_ple_kernel, tm=tm, seq=seq),
        grid=(t // tm,),
        in_specs=[pl.BlockSpec((tm, d), lambda i: (i, 0)),
                  pl.BlockSpec((FFN_HALO, d), lambda i: (jnp.maximum(i * halo_blocks - 1, 0), 0)),
                  pl.BlockSpec((None, tm, p_all.shape[2]), lambda i: (layer, i, 0))]
        + [_const_spec(c.shape) for c in consts],
        out_specs=pl.BlockSpec((tm, d), lambda i: (i, 0)),
        out_shape=jax.ShapeDtypeStruct((t, d), F32),
        compiler_params=_params(("parallel",)),
        name="ffn_ple",
    )(x2, x2, p_all, *consts)


def _inproj_odd_kernel(x_ref, g_ref, w_ref, wba_ref, qkv_ref, z_ref, ba_ref):
    hn = _rms_rows(x_ref[...], g_ref[...]).astype(BF16)
    _project_columns(hn, w_ref, (qkv_ref, z_ref))
    ba_ref[...] = _dot(hn, wba_ref[...])


def _inproj_odd(x2, gain, w, wba, widths, tm=INPROJ_ROW_TILE):
    t, d = x2.shape
    outs = [(n, BF16) for n in widths] + [(wba.shape[1], F32)]
    return pl.pallas_call(
        _inproj_odd_kernel,
        grid=(t // tm,),
        in_specs=[pl.BlockSpec((tm, d), lambda i: (i, 0))] + [_const_spec(c.shape) for c in (gain, w, wba)],
        out_specs=[pl.BlockSpec((tm, n), lambda i: (i, 0)) for n, _ in outs],
        out_shape=[jax.ShapeDtypeStruct((t, n), dt) for n, dt in outs],
        compiler_params=_params(("parallel",)),
        name="inproj_odd",
    )(x2, gain, w, wba)


def _bdot(a, b):
    return _dot(a.astype(BF16), b.astype(BF16))


def _unit_lower_inverses(mats):
    L = mats[0].shape[0]
    row = lax.broadcasted_iota(jnp.int32, (L, L), 0)
    col = lax.broadcasted_iota(jnp.int32, (L, L), 1)
    eye = jnp.where(row == col, 1.0, 0.0)
    base = 8
    same = (row // base) == (col // base)
    ns = [-jnp.where(same, a, 0.0) for a in mats]
    ts = [eye + n for n in ns]
    ns_bf = [n.astype(BF16) for n in ns]
    pws_bf = [_dot(n, n).astype(BF16) for n in ns_bf]
    ts = [t + _dot(t.astype(BF16), pw) for t, pw in zip(ts, pws_bf)]
    pws_bf = [_dot(pw, pw).astype(BF16) for pw in pws_bf]
    ts = [t + _dot(t.astype(BF16), pw) for t, pw in zip(ts, pws_bf)]
    size = base
    while size < L:
        wider = (row // (2 * size)) == (col // (2 * size))
        offs = [jnp.where(wider & jnp.logical_not(same), a, 0.0).astype(BF16) for a in mats]
        ts_bf = [t.astype(BF16) for t in ts]
        ys = [_dot(off, t).astype(BF16) for off, t in zip(offs, ts_bf)]
        ts = [t - _dot(t_bf, y) for t, t_bf, y in zip(ts, ts_bf, ys)]
        same = wider
        size *= 2
    return ts


def _gdn_kernel(qkv_ref, z_ref, ba_ref, cw_ref, dtb_ref, alog_ref, nw_ref, out_ref, hist_s, state_s):
    c = pl.program_id(1)
    L = GDN_CHUNK
    dk = GDN_DIM
    qk_w = GDN_K_HEADS * dk
    rep = GDN_V_HEADS // GDN_K_HEADS
    nseq = qkv_ref.shape[0]

    @pl.when(c == 0)
    def _():
        hist_s[...] = jnp.zeros_like(hist_s)
        state_s[...] = jnp.zeros_like(state_s)

    row = lax.broadcasted_iota(jnp.int32, (L, L), 0)
    col = lax.broadcasted_iota(jnp.int32, (L, L), 1)
    lower = row >= col
    strict = row > col

    items = [(sq, hv) for sq in range(nseq) for hv in range(GDN_V_HEADS)]
    acts, qs, ks, kts, kk, qk0 = {}, {}, {}, {}, {}, {}
    beta_c, gc_c, gc_r = {}, {}, {}
    for sq in range(nseq):
        cur = qkv_ref[sq].astype(F32)
        ext = jnp.concatenate([hist_s[sq], cur], axis=0)
        hist_s[sq] = cur[L - CONV_HALO:, :]
        act = _silu(_causal_conv(ext, cw_ref, GDN_CONV, CONV_HALO))
        acts[sq] = act

        ba = ba_ref[sq]
        beta = _sigmoid(ba)
        g = (-jnp.exp(alog_ref[...]) * LOG2E) * _softplus(ba + dtb_ref[...])
        gc = _select_rows_dot(jnp.where(lower, 1.0, 0.0).astype(BF16), g)
        gc_t = gc.T
        for hv in range(GDN_V_HEADS):
            beta_c[sq, hv] = beta[:, hv:hv + 1]
            gc_c[sq, hv] = gc[:, GDN_V_HEADS + hv:GDN_V_HEADS + hv + 1]
            gc_r[sq, hv] = gc_t[GDN_V_HEADS + hv:GDN_V_HEADS + hv + 1, :]

        qk_act = act[:, :2 * qk_w]
        qk_n = qk_act * lax.rsqrt(_head_sums(qk_act * qk_act, dk) + EPS)
        for kh in range(GDN_K_HEADS):
            qn = qk_n[:, kh * dk:(kh + 1) * dk] * (dk ** -0.5)
            kn = qk_n[:, qk_w + kh * dk:qk_w + (kh + 1) * dk]
            kt = kn.T
            prod = _bdot(jnp.concatenate([kn, qn], axis=0), kt)
            qs[sq, kh], ks[sq, kh], kts[sq, kh] = qn, kn, kt
            kk[sq, kh], qk0[sq, kh] = prod[:L], prod[L:]

    def khead(it):
        return it[0], it[1] // rep

    outs = {}
    for g0 in range(0, len(items), GDN_CHAINS_PER_GROUP):
        group = items[g0:g0 + GDN_CHAINS_PER_GROUP]
        g_last = {it: gc_r[it][:, L - 1:L] for it in group}
        egc = {it: jnp.exp2(gc_c[it]) for it in group}
        decay = {it: jnp.exp2(jnp.where(lower, gc_c[it] - gc_r[it], NEG)) for it in group}
        a_mats = [jnp.where(strict, kk[khead(it)] * beta_c[it] * decay[it], 0.0) for it in group]
        t_inv = dict(zip(group, _unit_lower_inverses(a_mats)))

        sols = {}
        for it in group:
            sq, hv = it
            v_h = acts[sq][:, 2 * qk_w + hv * dk:2 * qk_w + (hv + 1) * dk]
            sols[it] = _bdot(t_inv[it], jnp.concatenate([v_h * beta_c[it],
                                                         ks[khead(it)] * (beta_c[it] * egc[it])], axis=1))

        states = {it: state_s[it[0], it[1]] for it in group}
        from_state = {it: _bdot(jnp.concatenate([sols[it][:, dk:], qs[khead(it)] * egc[it]], axis=0),
                                states[it]) for it in group}
        v_new = {it: sols[it][:, :dk] - from_state[it][:L] for it in group}
        from_v = {it: _bdot(jnp.concatenate([qk0[khead(it)] * decay[it],
                                             kts[khead(it)] * jnp.exp2(g_last[it] - gc_r[it])], axis=0),
                            v_new[it]) for it in group}
        for it in group:
            state_s[it[0], it[1]] = states[it] * jnp.exp2(g_last[it]) + from_v[it][L:]
            outs[it] = from_state[it][L:] + from_v[it][:L]
    for sq in range(nseq):
        o = jnp.concatenate([outs[sq, hv] for hv in range(GDN_V_HEADS)], axis=1)
        o = o * lax.rsqrt(_head_sums(o * o, dk) * (1.0 / dk) + EPS) * nw_ref[...]
        out_ref[sq] = (o * _silu(z_ref[sq].astype(F32))).astype(BF16)


def _gdn(qkv, z, ba, conv_w, dt_bias_pad, a_log_pad, norm_w):
    b, s, cd = qkv.shape
    vw = z.shape[2]
    nc = s // GDN_CHUNK
    nseq = GDN_SEQS_PER_STEP
    assert b % nseq == 0 and s % GDN_CHUNK == 0, (b, s)
    return pl.pallas_call(
        _gdn_kernel,
        grid=(b // nseq, nc),
        in_specs=[
            pl.BlockSpec((nseq, GDN_CHUNK, cd), lambda bi, ci: (bi, ci, 0)),
            pl.BlockSpec((nseq, GDN_CHUNK, vw), lambda bi, ci: (bi, ci, 0)),
            pl.BlockSpec((nseq, GDN_CHUNK, LANES), lambda bi, ci: (bi, ci, 0)),
            _const_spec(conv_w.shape), _const_spec(dt_bias_pad.shape),
            _const_spec(a_log_pad.shape), _const_spec(norm_w.shape),
        ],
        out_specs=pl.BlockSpec((nseq, GDN_CHUNK, vw), lambda bi, ci: (bi, ci, 0)),
        out_shape=jax.ShapeDtypeStruct((b, s, vw), BF16),
        scratch_shapes=[
            pltpu.VMEM((nseq, CONV_HALO, cd), F32),
            pltpu.VMEM((nseq, GDN_V_HEADS, GDN_DIM, GDN_DIM), F32),
        ],
        compiler_params=_params(("parallel", "arbitrary")),
        name="gated_deltanet",
    )(qkv, z, ba, conv_w, dt_bias_pad, a_log_pad, norm_w)


def _pad_lanes(v, offset=0):
    out = jnp.zeros((1, LANES), F32)
    return out.at[0, offset:offset + v.shape[0]].set(v.astype(F32))


def _row(v):
    return v.astype(F32).reshape(1, -1)


def _even_mixer(x2, batch, seq, norm_w, w_in, q_norm, k_norm, conv_w, conv_b, dt_bias, a_log, d_skip,
                ssm_norm, w_out, slopes):
    mw = MOBA_HEADS * MOBA_HEAD_DIM
    inner = SSM_HEADS * SSM_HEAD_DIM
    conv_dim = inner + 2 * SSM_GROUPS * SSM_STATE
    w = w_in.astype(BF16)
    widths = (mw, mw, mw, inner, conv_dim)
    wdt = jnp.pad(w[:, sum(widths):], ((0, 0), (0, LANES - SSM_HEADS)))
    q, k, v, z, xbc, dt = _inproj_even(x2, _row(norm_w), w, wdt, widths)
    attn = _moba(q.reshape(batch, seq, mw), k.reshape(batch, seq, mw), v.reshape(batch, seq, mw),
                 _row(jnp.tile(q_norm, MOBA_HEADS_PER_STEP)), _row(jnp.tile(k_norm, MOBA_HEADS_PER_STEP)),
                 slopes).reshape(batch * seq, mw)

    ssm = _ssd(xbc.reshape(batch, seq, conv_dim), z.reshape(batch, seq, inner),
               dt.reshape(batch, seq, LANES), conv_w.astype(F32), _row(conv_b),
               _pad_lanes(dt_bias), _pad_lanes(a_log), _row(jnp.repeat(d_skip, SSM_HEAD_DIM)),
               _row(ssm_norm))
    return _outproj_even(x2, attn, ssm.reshape(batch * seq, inner), w_out.astype(BF16))


def _gdn_mixer(x2, batch, seq, norm_w, w_in, conv_w, dt_bias, a_log, gdn_norm, w_out):
    conv_dim = 2 * GDN_K_HEADS * GDN_DIM + GDN_V_HEADS * GDN_DIM
    vw = GDN_V_HEADS * GDN_DIM
    w = w_in.astype(BF16)
    wba = jnp.pad(w[:, conv_dim + vw:], ((0, 0), (0, LANES - 2 * GDN_V_HEADS)))
    qkv, z, ba = _inproj_odd(x2, _row(norm_w), w, wba, (conv_dim, vw))
    o = _gdn(qkv.reshape(batch, seq, conv_dim), z.reshape(batch, seq, vw), ba.reshape(batch, seq, LANES),
             conv_w.astype(F32), _pad_lanes(dt_bias, GDN_V_HEADS), _pad_lanes(a_log, GDN_V_HEADS),
             _row(jnp.tile(gdn_norm, GDN_V_HEADS)))
    return _outproj_odd(x2, o.reshape(batch * seq, vw), w_out.astype(BF16))


def kernel(x, p, norm_mix, norm_ffn, norm_ple, w_in_even, moba_q_norm, moba_k_norm, ssm_conv_w, ssm_conv_b, ssm_dt_bias, ssm_a_log, ssm_d, ssm_norm, w_out_even, w_in_odd, gdn_conv_w, gdn_dt_bias, gdn_a_log, gdn_norm, w_out_odd, ffn_w_gate, ffn_w_up, ffn_conv_w, ffn_conv_b, ffn_w_down, ple_w_proj, ple_w_gate):
    batch, seq, d = x.shape
    depth = p.shape[0]
    assert seq % ROW_TILE == 0 and seq % MOBA_BLOCK == 0, seq
    slopes = jnp.exp2(-ALIBI_MAX_BIAS * jnp.arange(1, MOBA_HEADS + 1, dtype=F32) / MOBA_HEADS)
    x2 = x.reshape(batch * seq, d)
    p_all = p.reshape(depth, batch * seq, -1)
    for i in range(depth):
        j = i // 2
        if i % 2 == 0:
            x2 = _even_mixer(x2, batch, seq, norm_mix[i], w_in_even[j], moba_q_norm[j], moba_k_norm[j],
                             ssm_conv_w[j], ssm_conv_b[j], ssm_dt_bias[j], ssm_a_log[j], ssm_d[j],
                             ssm_norm[j], w_out_even[j], slopes)
        else:
            x2 = _gdn_mixer(x2, batch, seq, norm_mix[i], w_in_odd[j], gdn_conv_w[j], gdn_dt_bias[j],
                            gdn_a_log[j], gdn_norm[j], w_out_odd[j])
        x2 = _ffn_ple(x2, p_all, i, seq, _row(norm_ffn[i]),
                      ffn_w_gate[i].astype(BF16), ffn_w_up[i].astype(BF16), ffn_conv_w[i].astype(F32),
                      _row(ffn_conv_b[i]), ffn_w_down[i].astype(BF16), _row(norm_ple[i]),
                      ple_w_gate[i].astype(BF16), ple_w_proj[i].astype(BF16))
    return x2.reshape(batch, seq, d)
```

```python
import functools

import jax
import jax.numpy as jnp
from jax import lax
from jax.experimental import pallas as pl
from jax.experimental.pallas import tpu as pltpu

F32 = jnp.float32
BF16 = jnp.bfloat16
HIGHEST = lax.Precision.HIGHEST

EPS = 1e-6
NEG = -1e30
LOG2E = 1.4426950408889634
LANES = 128
CONV_HALO = 8

MOBA_HEADS = 8
MOBA_HEAD_DIM = 64
MOBA_BLOCK = 256
MOBA_TOPK = 3
MOBA_HEADS_PER_STEP = 4
MOBA_ONES_ROWS = 16
ALIBI_MAX_BIAS = 8.0

SSM_HEADS = 16
SSM_HEAD_DIM = 64
SSM_STATE = 128
SSM_GROUPS = 2
SSM_CONV = 4
SSM_CHUNK = 128
SSM_SEQS_PER_STEP = 4

GDN_K_HEADS = 8
GDN_V_HEADS = 16
GDN_DIM = 128
GDN_CONV = 4
GDN_CHUNK = 128
GDN_SEQS_PER_STEP = 2
GDN_CHAINS_PER_GROUP = GDN_SEQS_PER_STEP * GDN_V_HEADS

FFN_CONV = 3
FFN_HALO = 16
FFN_COLS = 256

VMEM_LIMIT = 56 * 1024 * 1024
ROW_TILE = 1024
INPROJ_ROW_TILE = 512


def _rms_rows(x, gain):
    return x * lax.rsqrt(jnp.mean(x * x, axis=-1, keepdims=True) + EPS) * gain


def _sigmoid(x):
    return 0.5 * jnp.tanh(0.5 * x) + 0.5


def _silu(x):
    half = 0.5 * x
    return half + half * jnp.tanh(half)


def _softplus(x):
    return jnp.maximum(x, 0.0) + jnp.log(1.0 + jnp.exp(-jnp.abs(x)))


def _dot(a, b):
    return jnp.dot(a, b, preferred_element_type=F32)


def _dot_nt(a, b):
    return lax.dot_general(a, b, (((1,), (1,)), ((), ())), preferred_element_type=F32)


def _bf16_terms(x):
    hi = x.astype(BF16)
    rest = x - hi.astype(F32)
    mid = rest.astype(BF16)
    lo = (rest - mid.astype(F32)).astype(BF16)
    return [hi, mid, lo]


def _select_rows_dot(sel, x):
    return _dot(jnp.concatenate([sel] * 3, axis=1), jnp.concatenate(_bf16_terms(x), axis=0))


def _select_cols_dot(x, sel):
    return _dot(jnp.concatenate(_bf16_terms(x), axis=1), jnp.concatenate([sel] * 3, axis=0))


def _head_sums(x, head_w):
    rows, width = x.shape
    slab = 2 * LANES
    n = width // slab
    same = (lax.broadcasted_iota(jnp.int32, (slab, slab), 0) // head_w
            == lax.broadcasted_iota(jnp.int32, (slab, slab), 1) // head_w)
    ones = jnp.where(same, 1.0, 0.0).astype(BF16)
    stacked = jnp.concatenate([x[:, i * slab:(i + 1) * slab] for i in range(n)], axis=0)
    sums = _dot(stacked.astype(BF16), ones)
    return jnp.concatenate([sums[i * rows:(i + 1) * rows] for i in range(n)], axis=1)


def _const_spec(shape):
    zeros = (0,) * len(shape)
    return pl.BlockSpec(shape, lambda *_: zeros, pipeline_mode=pl.Buffered(1))


def _params(semantics):
    return pltpu.CompilerParams(dimension_semantics=semantics, vmem_limit_bytes=VMEM_LIMIT)


def _causal_conv(ext, w_ref, width, halo):
    y = w_ref[width - 1:width, :] * ext[halo:]
    for shift in range(1, width):
        y = y + w_ref[width - 1 - shift:width - shift, :] * pltpu.roll(ext, shift, axis=0)[halo:]
    return y


def _project_columns(hn, w_ref, out_refs):
    c0 = 0
    for o_ref in out_refs:
        n = o_ref.shape[1]
        o_ref[...] = _dot(hn, w_ref[:, c0:c0 + n]).astype(BF16)
        c0 += n


def _inproj_even_kernel(x_ref, g_ref, w_ref, wdt_ref, q_ref, k_ref, v_ref, z_ref, xbc_ref, dt_ref):
    hn = _rms_rows(x_ref[...], g_ref[...]).astype(BF16)
    _project_columns(hn, w_ref, (q_ref, k_ref, v_ref, z_ref, xbc_ref))
    dt_ref[...] = _dot(hn, wdt_ref[...])


def _inproj_even(x2, gain, w, wdt, widths, tm=INPROJ_ROW_TILE):
    t, d = x2.shape
    outs = [(n, BF16) for n in widths] + [(wdt.shape[1], F32)]
    return pl.pallas_call(
        _inproj_even_kernel,
        grid=(t // tm,),
        in_specs=[pl.BlockSpec((tm, d), lambda i: (i, 0))] + [_const_spec(c.shape) for c in (gain, w, wdt)],
        out_specs=[pl.BlockSpec((tm, n), lambda i: (i, 0)) for n, _ in outs],
        out_shape=[jax.ShapeDtypeStruct((t, n), dt) for n, dt in outs],
        compiler_params=_params(("parallel",)),
        name="inproj_even",
    )(x2, gain, w, wdt)


def _moba_kernel(slopes_ref, q_ref, k_ref, v_ref, qg_ref, kg_ref, o_ref, kn_s, kbar_s, vt_s):
    hp = pl.program_id(1)
    seq = k_ref.shape[1]
    nb = seq // MOBA_BLOCK
    dh = MOBA_HEAD_DIM
    hps = MOBA_HEADS_PER_STEP
    heads = range(hps)

    lanes = hps * dh
    same_head = (lax.broadcasted_iota(jnp.int32, (lanes, lanes), 0) // dh
                 == lax.broadcasted_iota(jnp.int32, (lanes, lanes), 1) // dh)
    head_mean = jnp.where(same_head, 1.0 / dh, 0.0).astype(BF16)

    def head_rms(x, gain):
        return x * lax.rsqrt(_dot((x * x).astype(BF16), head_mean) + EPS) * gain

    kn = head_rms(k_ref[0].astype(F32), kg_ref[...])
    kn_s[...] = kn.astype(BF16)
    kbar_s[...] = jnp.mean(kn.reshape(nb, MOBA_BLOCK, lanes), axis=1)
    vt_s[...] = v_ref[0].astype(F32).T.astype(BF16)

    lane_head = lax.broadcasted_iota(jnp.int32, (MOBA_BLOCK, lanes), 1) // dh
    blk = lax.broadcasted_iota(jnp.int32, (nb, MOBA_BLOCK), 0)
    kbar_head = lax.broadcasted_iota(jnp.int32, (nb, lanes), 1) // dh
    kbar = kbar_s[...]
    kbar_rows = jnp.concatenate([jnp.where(kbar_head == hh, kbar, 0.0) for hh in heads], axis=0)
    kidx = lax.broadcasted_iota(jnp.int32, (MOBA_BLOCK, MOBA_BLOCK), 0)
    qidx = lax.broadcasted_iota(jnp.int32, (MOBA_BLOCK, MOBA_BLOCK), 1)
    rel = (kidx - qidx).astype(F32)
    causal = rel <= 0.0
    slopes2 = [slopes_ref[hp * hps + hh] * LOG2E for hh in heads]
    q_gain = qg_ref[...] * (dh ** -0.5 * LOG2E)
    alibis = [slope2 * rel for slope2 in slopes2]

    def score_stage(own):
        rows = slice(own * MOBA_BLOCK, (own + 1) * MOBA_BLOCK)
        qn = head_rms(q_ref[0, rows, :].astype(F32), q_gain)
        qs = [jnp.where(lane_head == hh, qn, 0.0).astype(BF16) for hh in heads]
        scores = {(hh, j): _dot_nt(kn_s[j * MOBA_BLOCK:(j + 1) * MOBA_BLOCK, :], qs[hh])
                  for hh in heads for j in range(own + 1)}
        gates = lax.dot_general(kbar_rows, qn, (((1,), (1,)), ((), ())),
                                precision=HIGHEST, preferred_element_type=F32)
        selb = []
        for hh in heads:
            gate = gates[hh * nb:(hh + 1) * nb, :]
            rank = jnp.zeros(gate.shape, F32)
            for m in range(own):
                gm = gate[m:m + 1, :]
                beats = (gm > gate) | ((gm == gate) & (m < blk))
                rank = rank + jnp.where(beats, 1.0, 0.0)
            selected = (blk < own) & (rank < float(min(MOBA_TOPK, nb - 1)))
            selb.append(jnp.where(selected, 0.0, NEG))
        return scores, selb

    def softmax_pv_stage(own, scores, selb):
        rows = slice(own * MOBA_BLOCK, (own + 1) * MOBA_BLOCK)
        probs = []
        for hh in heads:
            tiles = [scores[hh, j] + alibis[hh] for j in range(own)]
            tiles.append(jnp.where(causal, scores[hh, own] + alibis[hh], NEG))
            col_bias = [selb[hh][j:j + 1, :] + slopes2[hh] * float((j - own) * MOBA_BLOCK) for j in range(own)]
            col_bias.append(jnp.zeros((1, MOBA_BLOCK), F32))
            m = jnp.max(tiles[0], axis=0, keepdims=True) + col_bias[0]
            for t, cb in zip(tiles[1:], col_bias[1:]):
                m = jnp.maximum(m, jnp.max(t, axis=0, keepdims=True) + cb)
            probs.append(jnp.concatenate([jnp.exp2(t + (cb - m)).astype(BF16) for t, cb in zip(tiles, col_bias)],
                                         axis=0))
        keys = (own + 1) * MOBA_BLOCK
        ones_rows = jnp.ones((MOBA_ONES_ROWS, keys), BF16)
        pv = [_dot(jnp.concatenate([vt_s[hh * dh:(hh + 1) * dh, 0:keys], ones_rows], axis=0), probs[hh])
              for hh in heads]
        outs = [r[:dh] / r[dh:dh + 1] for r in pv]
        o_ref[0, rows, :] = jnp.concatenate(outs, axis=0).T.astype(BF16)

    staged = score_stage(0)
    for own in range(nb):
        upcoming = score_stage(own + 1) if own + 1 < nb else None
        softmax_pv_stage(own, *staged)
        staged = upcoming


def _moba(q, k, v, q_gain, k_gain, slopes):
    b, s, width = q.shape
    dh = MOBA_HEAD_DIM
    hps = MOBA_HEADS_PER_STEP
    lanes = hps * dh
    nb = s // MOBA_BLOCK
    seq_spec = pl.BlockSpec((1, s, lanes), lambda bi, hi: (bi, 0, hi))
    return pl.pallas_call(
        _moba_kernel,
        grid=(b, width // lanes),
        in_specs=[
            pl.BlockSpec(memory_space=pltpu.SMEM),
            seq_spec, seq_spec, seq_spec,
            pl.BlockSpec((1, lanes), lambda bi, hi: (0, 0)),
            pl.BlockSpec((1, lanes), lambda bi, hi: (0, 0)),
        ],
        out_specs=seq_spec,
        out_shape=jax.ShapeDtypeStruct((b, s, width), BF16),
        scratch_shapes=[
            pltpu.VMEM((s, lanes), BF16),
            pltpu.VMEM((nb, lanes), F32),
            pltpu.VMEM((lanes, s), BF16),
        ],
        compiler_params=_params(("parallel", "parallel")),
        name="moba_attention",
    )(slopes, q, k, v, q_gain, k_gain)


def _ssd_kernel(xbc_ref, z_ref, dt_ref, cw_ref, cb_ref, dtb_ref, alog_ref, dskip_ref, nw_ref,
                out_ref, hist_s, state_s):
    c = pl.program_id(1)
    L = SSM_CHUNK
    inner = SSM_HEADS * SSM_HEAD_DIM
    gw = SSM_STATE
    pair_w = 2 * SSM_HEAD_DIM

    @pl.when(c == 0)
    def _():
        hist_s[...] = jnp.zeros_like(hist_s)
        state_s[...] = jnp.zeros_like(state_s)

    row = lax.broadcasted_iota(jnp.int32, (L, L), 0)
    col = lax.broadcasted_iota(jnp.int32, (L, L), 1)
    causal = row >= col
    causal_ones = jnp.where(causal, 1.0, 0.0).astype(BF16)
    hrow = lax.broadcasted_iota(jnp.int32, (LANES, inner), 0)
    hcol = lax.broadcasted_iota(jnp.int32, (LANES, inner), 1)
    widen = jnp.where(hcol // SSM_HEAD_DIM == hrow, 1.0, 0.0).astype(BF16)
    lane = lax.broadcasted_iota(jnp.int32, (L, pair_w), 1)
    first_head = lane < SSM_HEAD_DIM
    heads_per_group = SSM_HEADS // SSM_GROUPS
    gsz = inner // SSM_GROUPS

    for sq in range(xbc_ref.shape[0]):
        cur = xbc_ref[sq].astype(F32)
        ext = jnp.concatenate([hist_s[sq], cur], axis=0)
        hist_s[sq] = cur[L - CONV_HALO:, :]
        act = _silu(_causal_conv(ext, cw_ref, SSM_CONV, CONV_HALO) + cb_ref[...])
        xs = act[:, :inner]

        dt = _softplus(dt_ref[sq] + dtb_ref[...])
        a = dt * (-jnp.exp(alog_ref[...]) * LOG2E)
        acum = _select_rows_dot(causal_ones, a)
        acum_t = acum.T
        widened = _select_cols_dot(jnp.concatenate([dt, acum], axis=0), widen)
        dt_w = widened[:L]
        acum_w = widened[L:]
        last_w = acum_w[L - 1:L, :]
        xdt = xs * dt_w
        xdt_bf = xdt.astype(BF16)
        xdt_dec_bf = (xdt * jnp.exp2(last_w - acum_w)).astype(BF16)
        exp_acum_w = jnp.exp2(acum_w)
        chunk_decay_w = jnp.exp2(last_w)

        ys = []
        for g in range(SSM_GROUPS):
            bg = act[:, inner + g * gw:inner + (g + 1) * gw]
            cg = act[:, inner + SSM_GROUPS * gw + g * gw:inner + SSM_GROUPS * gw + (g + 1) * gw]
            cg_bf = cg.astype(BF16)
            cb = _dot_nt(cg_bf, bg.astype(BF16))
            bg_t = bg.T.astype(BF16)
            for pp in range(heads_per_group // 2):
                h0 = g * heads_per_group + 2 * pp
                pi = h0 // 2
                sl = slice(h0 * SSM_HEAD_DIM, h0 * SSM_HEAD_DIM + pair_w)
                yd = []
                for hh in (h0, h0 + 1):
                    diff = jnp.broadcast_to(acum[:, hh:hh + 1], (L, L)) - acum_t[hh:hh + 1, :]
                    decay = jnp.exp2(jnp.where(causal, diff, NEG))
                    yd.append(_dot((cb * decay).astype(BF16), xdt_bf[:, sl]))
                y_diag = jnp.where(first_head, yd[0], yd[1])
                prev = state_s[sq, pi]
                y_off = _dot(cg_bf, prev.astype(BF16)) * exp_acum_w[:, sl]
                state_s[sq, pi] = prev * chunk_decay_w[:, sl] + _dot(bg_t, xdt_dec_bf[:, sl])
                ys.append(y_diag + y_off + dskip_ref[:, sl] * xs[:, sl])
        y = jnp.concatenate(ys, axis=1)

        yg = y * _silu(z_ref[sq].astype(F32))
        parts = []
        for g in range(SSM_GROUPS):
            seg = yg[:, g * gsz:(g + 1) * gsz]
            parts.append(seg * lax.rsqrt(jnp.mean(seg * seg, axis=-1, keepdims=True) + EPS))
        out_ref[sq] = (jnp.concatenate(parts, axis=1) * nw_ref[...]).astype(BF16)


def _ssd(xbc, z, dt, conv_w, conv_b, dt_bias, a_log, d_skip_w, norm_w):
    b, s, cd = xbc.shape
    inner = z.shape[2]
    nc = s // SSM_CHUNK
    nseq = SSM_SEQS_PER_STEP
    assert b % nseq == 0 and s % SSM_CHUNK == 0, (b, s)
    return pl.pallas_call(
        _ssd_kernel,
        grid=(b // nseq, nc),
        in_specs=[
            pl.BlockSpec((nseq, SSM_CHUNK, cd), lambda bi, ci: (bi, ci, 0)),
            pl.BlockSpec((nseq, SSM_CHUNK, inner), lambda bi, ci: (bi, ci, 0)),
            pl.BlockSpec((nseq, SSM_CHUNK, LANES), lambda bi, ci: (bi, ci, 0)),
            _const_spec(conv_w.shape), _const_spec(conv_b.shape), _const_spec(dt_bias.shape),
            _const_spec(a_log.shape), _const_spec(d_skip_w.shape), _const_spec(norm_w.shape),
        ],
        out_specs=pl.BlockSpec((nseq, SSM_CHUNK, inner), lambda bi, ci: (bi, ci, 0)),
        out_shape=jax.ShapeDtypeStruct((b, s, inner), BF16),
        scratch_shapes=[
            pltpu.VMEM((nseq, CONV_HALO, cd), F32),
            pltpu.VMEM((nseq, SSM_HEADS // 2, SSM_STATE, 2 * SSM_HEAD_DIM), F32),
        ],
        compiler_params=_params(("parallel", "arbitrary")),
        name="ssd_heads",
    )(xbc, z, dt, conv_w, conv_b, dt_bias, a_log, d_skip_w, norm_w)


def _outproj_even_kernel(x_ref, a_ref, s_ref, w_ref, o_ref):
    mixed = jnp.concatenate([a_ref[...], s_ref[...]], axis=1)
    o_ref[...] = x_ref[...] + _dot(mixed, w_ref[...])


def _outproj_even(x2, attn, ssm, w, tm=ROW_TILE):
    t, d = x2.shape
    return pl.pallas_call(
        _outproj_even_kernel,
        grid=(t // tm,),
        in_specs=[pl.BlockSpec((tm, d), lambda i: (i, 0)),
                  pl.BlockSpec((tm, attn.shape[1]), lambda i: (i, 0)),
                  pl.BlockSpec((tm, ssm.shape[1]), lambda i: (i, 0)),
                  _const_spec(w.shape)],
        out_specs=pl.BlockSpec((tm, d), lambda i: (i, 0)),
        out_shape=jax.ShapeDtypeStruct((t, d), F32),
        compiler_params=_params(("parallel",)),
        name="outproj_even",
    )(x2, attn, ssm, w)


def _outproj_odd_kernel(x_ref, o_in_ref, w_ref, o_ref):
    o_ref[...] = x_ref[...] + _dot(o_in_ref[...], w_ref[...])


def _outproj_odd(x2, o_in, w, tm=ROW_TILE):
    t, d = x2.shape
    return pl.pallas_call(
        _outproj_odd_kernel,
        grid=(t // tm,),
        in_specs=[pl.BlockSpec((tm, d), lambda i: (i, 0)),
                  pl.BlockSpec((tm, o_in.shape[1]), lambda i: (i, 0)),
                  _const_spec(w.shape)],
        out_specs=pl.BlockSpec((tm, d), lambda i: (i, 0)),
        out_shape=jax.ShapeDtypeStruct((t, d), F32),
        compiler_params=_params(("parallel",)),
        name="outproj_odd",
    )(x2, o_in, w)


def _ffn_ple_kernel(x_ref, halo_ref, p_ref, gf_ref, wg_ref, wu_ref, cw_ref, cb_ref, wd_ref,
                    gp_ref, wpg_ref, wpp_ref, o_ref, *, tm, seq):
    i = pl.program_id(0)
    x = x_ref[...]
    keep_halo = jnp.where((i * tm) % seq == 0, 0.0, 1.0)
    halo = halo_ref[...] * keep_halo
    hn = _rms_rows(jnp.concatenate([halo, x], axis=0), gf_ref[...]).astype(BF16)
    hn_cur = hn[FFN_HALO:]

    d_ff = wg_ref.shape[1]
    acts = []
    for j in range(d_ff // FFN_COLS):
        cs = slice(j * FFN_COLS, (j + 1) * FFN_COLS)
        g = _dot(hn, wg_ref[:, cs])
        gc = cb_ref[:, cs] + cw_ref[FFN_CONV - 1:FFN_CONV, cs] * g[FFN_HALO:]
        for shift in range(1, FFN_CONV):
            gc = gc + cw_ref[FFN_CONV - 1 - shift:FFN_CONV - shift, cs] * pltpu.roll(g, shift, axis=0)[FFN_HALO:]
        u = _dot(hn_cur, wu_ref[:, cs])
        acts.append((_silu(gc) * u).astype(BF16))
    x1 = x + _dot(jnp.concatenate(acts, axis=1), wd_ref[...])
    hp = _rms_rows(x1, gp_ref[...]).astype(BF16)
    gate = _sigmoid(_dot(hp, wpg_ref[...]))
    o_ref[...] = x1 + gate * _dot(p_ref[...].astype(BF16), wpp_ref[...])


def _ffn_ple(x2, p_all, layer, seq, gain_ffn, wg, wu, conv_w, conv_b, wd, gain_ple, wpg, wpp, tm=ROW_TILE):
    t, d = x2.shape
    halo_blocks = tm // FFN_HALO
    consts = (gain_ffn, wg, wu, conv_w, conv_b, wd, gain_ple, wpg, wpp)
    return pl.pallas_call(
        functools.partial(_ffn_ple_kernel, tm=tm, seq=seq),
        grid=(t // tm,),
        in_specs=[pl.BlockSpec((tm, d), lambda i: (i, 0)),
                  pl.BlockSpec((FFN_HALO, d), lambda i: (jnp.maximum(i * halo_blocks - 1, 0), 0)),
                  pl.BlockSpec((None, tm, p_all.shape[2]), lambda i: (layer, i, 0))]
        + [_const_spec(c.shape) for c in consts],
        out_specs=pl.BlockSpec((tm, d), lambda i: (i, 0)),
        out_shape=jax.ShapeDtypeStruct((t, d), F32),
        compiler_params=_params(("parallel",)),
        name="ffn_ple",
    )(x2, x2, p_all, *consts)


def _inproj_odd_kernel(x_ref, g_ref, w_ref, wba_ref, qkv_ref, z_ref, ba_ref):
    hn = _rms_rows(x_ref[...], g_ref[...]).astype(BF16)
    _project_columns(hn, w_ref, (qkv_ref, z_ref))
    ba_ref[...] = _dot(hn, wba_ref[...])


def _inproj_odd(x2, gain, w, wba, widths, tm=INPROJ_ROW_TILE):
    t, d = x2.shape
    outs = [(n, BF16) for n in widths] + [(wba.shape[1], F32)]
    return pl.pallas_call(
        _inproj_odd_kernel,
        grid=(t // tm,),
        in_specs=[pl.BlockSpec((tm, d), lambda i: (i, 0))] + [_const_spec(c.shape) for c in (gain, w, wba)],
        out_specs=[pl.BlockSpec((tm, n), lambda i: (i, 0)) for n, _ in outs],
        out_shape=[jax.ShapeDtypeStruct((t, n), dt) for n, dt in outs],
        compiler_params=_params(("parallel",)),
        name="inproj_odd",
    )(x2, gain, w, wba)


def _bdot(a, b):
    return _dot(a.astype(BF16), b.astype(BF16))


def _unit_lower_inverses(mats):
    L = mats[0].shape[0]
    row = lax.broadcasted_iota(jnp.int32, (L, L), 0)
    col = lax.broadcasted_iota(jnp.int32, (L, L), 1)
    eye = jnp.where(row == col, 1.0, 0.0)
    base = 8
    same = (row // base) == (col // base)
    ns = [-jnp.where(same, a, 0.0) for a in mats]
    ts = [eye + n for n in ns]
    ns_bf = [n.astype(BF16) for n in ns]
    pws_bf = [_dot(n, n).astype(BF16) for n in ns_bf]
    ts = [t + _dot(t.astype(BF16), pw) for t, pw in zip(ts, pws_bf)]
    pws_bf = [_dot(pw, pw).astype(BF16) for pw in pws_bf]
    ts = [t + _dot(t.astype(BF16), pw) for t, pw in zip(ts, pws_bf)]
    size = base
    while size < L:
        wider = (row // (2 * size)) == (col // (2 * size))
        offs = [jnp.where(wider & jnp.logical_not(same), a, 0.0).astype(BF16) for a in mats]
        ts_bf = [t.astype(BF16) for t in ts]
        ys = [_dot(off, t).astype(BF16) for off, t in zip(offs, ts_bf)]
        ts = [t - _dot(t_bf, y) for t, t_bf, y in zip(ts, ts_bf, ys)]
        same = wider
        size *= 2
    return ts


def _gdn_kernel(qkv_ref, z_ref, ba_ref, cw_ref, dtb_ref, alog_ref, nw_ref, out_ref, hist_s, state_s):
    c = pl.program_id(1)
    L = GDN_CHUNK
    dk = GDN_DIM
    qk_w = GDN_K_HEADS * dk
    rep = GDN_V_HEADS // GDN_K_HEADS
    nseq = qkv_ref.shape[0]

    @pl.when(c == 0)
    def _():
        hist_s[...] = jnp.zeros_like(hist_s)
        state_s[...] = jnp.zeros_like(state_s)

    row = lax.broadcasted_iota(jnp.int32, (L, L), 0)
    col = lax.broadcasted_iota(jnp.int32, (L, L), 1)
    lower = row >= col
    strict = row > col

    items = [(sq, hv) for sq in range(nseq) for hv in range(GDN_V_HEADS)]
    acts, qs, ks, kts, kk, qk0 = {}, {}, {}, {}, {}, {}
    beta_c, gc_c, gc_r = {}, {}, {}
    for sq in range(nseq):
        cur = qkv_ref[sq].astype(F32)
        ext = jnp.concatenate([hist_s[sq], cur], axis=0)
        hist_s[sq] = cur[L - CONV_HALO:, :]
        act = _silu(_causal_conv(ext, cw_ref, GDN_CONV, CONV_HALO))
        acts[sq] = act

        ba = ba_ref[sq]
        beta = _sigmoid(ba)
        g = (-jnp.exp(alog_ref[...]) * LOG2E) * _softplus(ba + dtb_ref[...])
        gc = _select_rows_dot(jnp.where(lower, 1.0, 0.0).astype(BF16), g)
        gc_t = gc.T
        for hv in range(GDN_V_HEADS):
            beta_c[sq, hv] = beta[:, hv:hv + 1]
            gc_c[sq, hv] = gc[:, GDN_V_HEADS + hv:GDN_V_HEADS + hv + 1]
            gc_r[sq, hv] = gc_t[GDN_V_HEADS + hv:GDN_V_HEADS + hv + 1, :]

        qk_act = act[:, :2 * qk_w]
        qk_n = qk_act * lax.rsqrt(_head_sums(qk_act * qk_act, dk) + EPS)
        for kh in range(GDN_K_HEADS):
            qn = qk_n[:, kh * dk:(kh + 1) * dk] * (dk ** -0.5)
            kn = qk_n[:, qk_w + kh * dk:qk_w + (kh + 1) * dk]
            kt = kn.T
            prod = _bdot(jnp.concatenate([kn, qn], axis=0), kt)
            qs[sq, kh], ks[sq, kh], kts[sq, kh] = qn, kn, kt
            kk[sq, kh], qk0[sq, kh] = prod[:L], prod[L:]

    def khead(it):
        return it[0], it[1] // rep

    outs = {}
    for g0 in range(0, len(items), GDN_CHAINS_PER_GROUP):
        group = items[g0:g0 + GDN_CHAINS_PER_GROUP]
        g_last = {it: gc_r[it][:, L - 1:L] for it in group}
        egc = {it: jnp.exp2(gc_c[it]) for it in group}
        decay = {it: jnp.exp2(jnp.where(lower, gc_c[it] - gc_r[it], NEG)) for it in group}
        a_mats = [jnp.where(strict, kk[khead(it)] * beta_c[it] * decay[it], 0.0) for it in group]
        t_inv = dict(zip(group, _unit_lower_inverses(a_mats)))

        sols = {}
        for it in group:
            sq, hv = it
            v_h = acts[sq][:, 2 * qk_w + hv * dk:2 * qk_w + (hv + 1) * dk]
            sols[it] = _bdot(t_inv[it], jnp.concatenate([v_h * beta_c[it],
                                                         ks[khead(it)] * (beta_c[it] * egc[it])], axis=1))

        states = {it: state_s[it[0], it[1]] for it in group}
        from_state = {it: _bdot(jnp.concatenate([sols[it][:, dk:], qs[khead(it)] * egc[it]], axis=0),
                                states[it]) for it in group}
        v_new = {it: sols[it][:, :dk] - from_state[it][:L] for it in group}
        from_v = {it: _bdot(jnp.concatenate([qk0[khead(it)] * decay[it],
                                             kts[khead(it)] * jnp.exp2(g_last[it] - gc_r[it])], axis=0),
                            v_new[it]) for it in group}
        for it in group:
            state_s[it[0], it[1]] = states[it] * jnp.exp2(g_last[it]) + from_v[it][L:]
            outs[it] = from_state[it][L:] + from_v[it][:L]
    for sq in range(nseq):
        o = jnp.concatenate([outs[sq, hv] for hv in range(GDN_V_HEADS)], axis=1)
        o = o * lax.rsqrt(_head_sums(o * o, dk) * (1.0 / dk) + EPS) * nw_ref[...]
        out_ref[sq] = (o * _silu(z_ref[sq].astype(F32))).astype(BF16)


def _gdn(qkv, z, ba, conv_w, dt_bias_pad, a_log_pad, norm_w):
    b, s, cd = qkv.shape
    vw = z.shape[2]
    nc = s // GDN_CHUNK
    nseq = GDN_SEQS_PER_STEP
    assert b % nseq == 0 and s % GDN_CHUNK == 0, (b, s)
    return pl.pallas_call(
        _gdn_kernel,
        grid=(b // nseq, nc),
        in_specs=[
            pl.BlockSpec((nseq, GDN_CHUNK, cd), lambda bi, ci: (bi, ci, 0)),
            pl.BlockSpec((nseq, GDN_CHUNK, vw), lambda bi, ci: (bi, ci, 0)),
            pl.BlockSpec((nseq, GDN_CHUNK, LANES), lambda bi, ci: (bi, ci, 0)),
            _const_spec(conv_w.shape), _const_spec(dt_bias_pad.shape),
            _const_spec(a_log_pad.shape), _const_spec(norm_w.shape),
        ],
        out_specs=pl.BlockSpec((nseq, GDN_CHUNK, vw), lambda bi, ci: (bi, ci, 0)),
        out_shape=jax.ShapeDtypeStruct((b, s, vw), BF16),
        scratch_shapes=[
            pltpu.VMEM((nseq, CONV_HALO, cd), F32),
            pltpu.VMEM((nseq, GDN_V_HEADS, GDN_DIM, GDN_DIM), F32),
        ],
        compiler_params=_params(("parallel", "arbitrary")),
        name="gated_deltanet",
    )(qkv, z, ba, conv_w, dt_bias_pad, a_log_pad, norm_w)


def _pad_lanes(v, offset=0):
    out = jnp.zeros((1, LANES), F32)
    return out.at[0, offset:offset + v.shape[0]].set(v.astype(F32))


def _row(v):
    return v.astype(F32).reshape(1, -1)


def _even_mixer(x2, batch, seq, norm_w, w_in, q_norm, k_norm, conv_w, conv_b, dt_bias, a_log, d_skip,
                ssm_norm, w_out, slopes):
    mw = MOBA_HEADS * MOBA_HEAD_DIM
    inner = SSM_HEADS * SSM_HEAD_DIM
    conv_dim = inner + 2 * SSM_GROUPS * SSM_STATE
    w = w_in.astype(BF16)
    widths = (mw, mw, mw, inner, conv_dim)
    wdt = jnp.pad(w[:, sum(widths):], ((0, 0), (0, LANES - SSM_HEADS)))
    q, k, v, z, xbc, dt = _inproj_even(x2, _row(norm_w), w, wdt, widths)
    attn = _moba(q.reshape(batch, seq, mw), k.reshape(batch, seq, mw), v.reshape(batch, seq, mw),
                 _row(jnp.tile(q_norm, MOBA_HEADS_PER_STEP)), _row(jnp.tile(k_norm, MOBA_HEADS_PER_STEP)),
                 slopes).reshape(batch * seq, mw)

    ssm = _ssd(xbc.reshape(batch, seq, conv_dim), z.reshape(batch, seq, inner),
               dt.reshape(batch, seq, LANES), conv_w.astype(F32), _row(conv_b),
               _pad_lanes(dt_bias), _pad_lanes(a_log), _row(jnp.repeat(d_skip, SSM_HEAD_DIM)),
               _row(ssm_norm))
    return _outproj_even(x2, attn, ssm.reshape(batch * seq, inner), w_out.astype(BF16))


def _gdn_mixer(x2, batch, seq, norm_w, w_in, conv_w, dt_bias, a_log, gdn_norm, w_out):
    conv_dim = 2 * GDN_K_HEADS * GDN_DIM + GDN_V_HEADS * GDN_DIM
    vw = GDN_V_HEADS * GDN_DIM
    w = w_in.astype(BF16)
    wba = jnp.pad(w[:, conv_dim + vw:], ((0, 0), (0, LANES - 2 * GDN_V_HEADS)))
    qkv, z, ba = _inproj_odd(x2, _row(norm_w), w, wba, (conv_dim, vw))
    o = _gdn(qkv.reshape(batch, seq, conv_dim), z.reshape(batch, seq, vw), ba.reshape(batch, seq, LANES),
             conv_w.astype(F32), _pad_lanes(dt_bias, GDN_V_HEADS), _pad_lanes(a_log, GDN_V_HEADS),
             _row(jnp.tile(gdn_norm, GDN_V_HEADS)))
    return _outproj_odd(x2, o.reshape(batch * seq, vw), w_out.astype(BF16))


def kernel(x, p, norm_mix, norm_ffn, norm_ple, w_in_even, moba_q_norm, moba_k_norm, ssm_conv_w, ssm_conv_b, ssm_dt_bias, ssm_a_log, ssm_d, ssm_norm, w_out_even, w_in_odd, gdn_conv_w, gdn_dt_bias, gdn_a_log, gdn_norm, w_out_odd, ffn_w_gate, ffn_w_up, ffn_conv_w, ffn_conv_b, ffn_w_down, ple_w_proj, ple_w_gate):
    batch, seq, d = x.shape
    depth = p.shape[0]
    assert seq % ROW_TILE == 0 and seq % MOBA_BLOCK == 0, seq
    slopes = jnp.exp2(-ALIBI_MAX_BIAS * jnp.arange(1, MOBA_HEADS + 1, dtype=F32) / MOBA_HEADS)
    x2 = x.reshape(batch * seq, d)
    p_all = p.reshape(depth, batch * seq, -1)
    for i in range(depth):
        j = i // 2
        if i % 2 == 0:
            x2 = _even_mixer(x2, batch, seq, norm_mix[i], w_in_even[j], moba_q_norm[j], moba_k_norm[j],
                             ssm_conv_w[j], ssm_conv_b[j], ssm_dt_bias[j], ssm_a_log[j], ssm_d[j],
                             ssm_norm[j], w_out_even[j], slopes)
        else:
            x2 = _gdn_mixer(x2, batch, seq, norm_mix[i], w_in_odd[j], gdn_conv_w[j], gdn_dt_bias[j],
                            gdn_a_log[j], gdn_norm[j], w_out_odd[j])
        x2 = _ffn_ple(x2, p_all, i, seq, _row(norm_ffn[i]),
                      ffn_w_gate[i].astype(BF16), ffn_w_up[i].astype(BF16), ffn_conv_w[i].astype(F32),
                      _row(ffn_conv_b[i]), ffn_w_down[i].astype(BF16), _row(norm_ple[i]),
                      ple_w_gate[i].astype(BF16), ple_w_proj[i].astype(BF16))
    return x2.reshape(batch, seq, d)
```

```python
import functools

import jax
import jax.numpy as jnp
from jax import lax
from jax.experimental import pallas as pl
from jax.experimental.pallas import tpu as pltpu

F32 = jnp.float32
BF16 = jnp.bfloat16
HIGHEST = lax.Precision.HIGHEST

EPS = 1e-6
NEG = -1e30
LOG2E = 1.4426950408889634
LANES = 128
CONV_HALO = 8

MOBA_HEADS = 8
MOBA_HEAD_DIM = 64
MOBA_BLOCK = 256
MOBA_TOPK = 3
MOBA_HEADS_PER_STEP = 4
MOBA_ONES_ROWS = 16
ALIBI_MAX_BIAS = 8.0

SSM_HEADS = 16
SSM_HEAD_DIM = 64
SSM_STATE = 128
SSM_GROUPS = 2
SSM_CONV = 4
SSM_CHUNK = 128
SSM_SEQS_PER_STEP = 4

GDN_K_HEADS = 8
GDN_V_HEADS = 16
GDN_DIM = 128
GDN_CONV = 4
GDN_CHUNK = 128
GDN_SEQS_PER_STEP = 1
GDN_CHAINS_PER_GROUP = GDN_SEQS_PER_STEP * GDN_V_HEADS

FFN_CONV = 3
FFN_HALO = 16
FFN_COLS = 256

VMEM_LIMIT = 56 * 1024 * 1024
ROW_TILE = 1024
INPROJ_ROW_TILE = 512


def _rms_rows(x, gain):
    return x * lax.rsqrt(jnp.mean(x * x, axis=-1, keepdims=True) + EPS) * gain


def _sigmoid(x):
    return 0.5 * jnp.tanh(0.5 * x) + 0.5


def _silu(x):
    half = 0.5 * x
    return half + half * jnp.tanh(half)


def _softplus(x):
    return jnp.maximum(x, 0.0) + jnp.log(1.0 + jnp.exp(-jnp.abs(x)))


def _dot(a, b):
    return jnp.dot(a, b, preferred_element_type=F32)


def _dot_nt(a, b):
    return lax.dot_general(a, b, (((1,), (1,)), ((), ())), preferred_element_type=F32)


def _bf16_terms(x):
    hi = x.astype(BF16)
    rest = x - hi.astype(F32)
    mid = rest.astype(BF16)
    lo = (rest - mid.astype(F32)).astype(BF16)
    return [hi, mid, lo]


def _select_rows_dot(sel, x):
    return _dot(jnp.concatenate([sel] * 3, axis=1), jnp.concatenate(_bf16_terms(x), axis=0))


def _select_cols_dot(x, sel):
    return _dot(jnp.concatenate(_bf16_terms(x), axis=1), jnp.concatenate([sel] * 3, axis=0))


def _head_sums(x, head_w):
    rows, width = x.shape
    slab = 2 * LANES
    n = width // slab
    same = (lax.broadcasted_iota(jnp.int32, (slab, slab), 0) // head_w
            == lax.broadcasted_iota(jnp.int32, (slab, slab), 1) // head_w)
    ones = jnp.where(same, 1.0, 0.0).astype(BF16)
    stacked = jnp.concatenate([x[:, i * slab:(i + 1) * slab] for i in range(n)], axis=0)
    sums = _dot(stacked.astype(BF16), ones)
    return jnp.concatenate([sums[i * rows:(i + 1) * rows] for i in range(n)], axis=1)


def _const_spec(shape):
    zeros = (0,) * len(shape)
    return pl.BlockSpec(shape, lambda *_: zeros, pipeline_mode=pl.Buffered(1))


def _params(semantics):
    return pltpu.CompilerParams(dimension_semantics=semantics, vmem_limit_bytes=VMEM_LIMIT)


def _causal_conv(ext, w_ref, width, halo):
    y = w_ref[width - 1:width, :] * ext[halo:]
    for shift in range(1, width):
        y = y + w_ref[width - 1 - shift:width - shift, :] * pltpu.roll(ext, shift, axis=0)[halo:]
    return y


def _project_columns(hn, w_ref, out_refs):
    c0 = 0
    for o_ref in out_refs:
        n = o_ref.shape[1]
        o_ref[...] = _dot(hn, w_ref[:, c0:c0 + n]).astype(BF16)
        c0 += n


def _inproj_even_kernel(x_ref, g_ref, w_ref, wdt_ref, q_ref, k_ref, v_ref, z_ref, xbc_ref, dt_ref):
    hn = _rms_rows(x_ref[...], g_ref[...]).astype(BF16)
    _project_columns(hn, w_ref, (q_ref, k_ref, v_ref, z_ref, xbc_ref))
    dt_ref[...] = _dot(hn, wdt_ref[...])


def _inproj_even(x2, gain, w, wdt, widths, tm=INPROJ_ROW_TILE):
    t, d = x2.shape
    outs = [(n, BF16) for n in widths] + [(wdt.shape[1], F32)]
    return pl.pallas_call(
        _inproj_even_kernel,
        grid=(t // tm,),
        in_specs=[pl.BlockSpec((tm, d), lambda i: (i, 0))] + [_const_spec(c.shape) for c in (gain, w, wdt)],
        out_specs=[pl.BlockSpec((tm, n), lambda i: (i, 0)) for n, _ in outs],
        out_shape=[jax.ShapeDtypeStruct((t, n), dt) for n, dt in outs],
        compiler_params=_params(("parallel",)),
        name="inproj_even",
    )(x2, gain, w, wdt)


def _moba_kernel(slopes_ref, q_ref, k_ref, v_ref, qg_ref, kg_ref, o_ref, kn_s, kbar_s, vt_s):
    hp = pl.program_id(1)
    seq = k_ref.shape[1]
    nb = seq // MOBA_BLOCK
    dh = MOBA_HEAD_DIM
    hps = MOBA_HEADS_PER_STEP
    heads = range(hps)

    lanes = hps * dh
    same_head = (lax.broadcasted_iota(jnp.int32, (lanes, lanes), 0) // dh
                 == lax.broadcasted_iota(jnp.int32, (lanes, lanes), 1) // dh)
    head_mean = jnp.where(same_head, 1.0 / dh, 0.0).astype(BF16)

    def head_rms(x, gain):
        return x * lax.rsqrt(_dot((x * x).astype(BF16), head_mean) + EPS) * gain

    kn = head_rms(k_ref[0].astype(F32), kg_ref[...])
    kn_s[...] = kn.astype(BF16)
    kbar_s[...] = jnp.mean(kn.reshape(nb, MOBA_BLOCK, lanes), axis=1)
    vt_s[...] = v_ref[0].astype(F32).T.astype(BF16)

    lane_head = lax.broadcasted_iota(jnp.int32, (MOBA_BLOCK, lanes), 1) // dh
    blk = lax.broadcasted_iota(jnp.int32, (nb, MOBA_BLOCK), 0)
    kbar_head = lax.broadcasted_iota(jnp.int32, (nb, lanes), 1) // dh
    kbar = kbar_s[...]
    kbar_rows = jnp.concatenate([jnp.where(kbar_head == hh, kbar, 0.0) for hh in heads], axis=0)
    kidx = lax.broadcasted_iota(jnp.int32, (MOBA_BLOCK, MOBA_BLOCK), 0)
    qidx = lax.broadcasted_iota(jnp.int32, (MOBA_BLOCK, MOBA_BLOCK), 1)
    rel = (kidx - qidx).astype(F32)
    causal = rel <= 0.0
    slopes2 = [slopes_ref[hp * hps + hh] * LOG2E for hh in heads]
    q_gain = qg_ref[...] * (dh ** -0.5 * LOG2E)
    alibis = [slope2 * rel for slope2 in slopes2]

    def score_stage(own):
        rows = slice(own * MOBA_BLOCK, (own + 1) * MOBA_BLOCK)
        qn = head_rms(q_ref[0, rows, :].astype(F32), q_gain)
        qs = [jnp.where(lane_head == hh, qn, 0.0).astype(BF16) for hh in heads]
        scores = {(hh, j): _dot_nt(kn_s[j * MOBA_BLOCK:(j + 1) * MOBA_BLOCK, :], qs[hh])
                  for hh in heads for j in range(own + 1)}
        gates = lax.dot_general(kbar_rows, qn, (((1,), (1,)), ((), ())),
                                precision=HIGHEST, preferred_element_type=F32)
        selb = []
        for hh in heads:
            gate = gates[hh * nb:(hh + 1) * nb, :]
            rank = jnp.zeros(gate.shape, F32)
            for m in range(own):
                gm = gate[m:m + 1, :]
                beats = (gm > gate) | ((gm == gate) & (m < blk))
                rank = rank + jnp.where(beats, 1.0, 0.0)
            selected = (blk < own) & (rank < float(min(MOBA_TOPK, nb - 1)))
            selb.append(jnp.where(selected, 0.0, NEG))
        return scores, selb

    def softmax_pv_stage(own, scores, selb):
        rows = slice(own * MOBA_BLOCK, (own + 1) * MOBA_BLOCK)
        probs = []
        for hh in heads:
            tiles = [scores[hh, j] + alibis[hh] for j in range(own)]
            tiles.append(jnp.where(causal, scores[hh, own] + alibis[hh], NEG))
            col_bias = [selb[hh][j:j + 1, :] + slopes2[hh] * float((j - own) * MOBA_BLOCK) for j in range(own)]
            col_bias.append(jnp.zeros((1, MOBA_BLOCK), F32))
            m = jnp.max(tiles[0], axis=0, keepdims=True) + col_bias[0]
            for t, cb in zip(tiles[1:], col_bias[1:]):
                m = jnp.maximum(m, jnp.max(t, axis=0, keepdims=True) + cb)
            probs.append(jnp.concatenate([jnp.exp2(t + (cb - m)).astype(BF16) for t, cb in zip(tiles, col_bias)],
                                         axis=0))
        keys = (own + 1) * MOBA_BLOCK
        ones_rows = jnp.ones((MOBA_ONES_ROWS, keys), BF16)
        pv = [_dot(jnp.concatenate([vt_s[hh * dh:(hh + 1) * dh, 0:keys], ones_rows], axis=0), probs[hh])
              for hh in heads]
        outs = [r[:dh] / r[dh:dh + 1] for r in pv]
        o_ref[0, rows, :] = jnp.concatenate(outs, axis=0).T.astype(BF16)

    staged = score_stage(0)
    for own in range(nb):
        upcoming = score_stage(own + 1) if own + 1 < nb else None
        softmax_pv_stage(own, *staged)
        staged = upcoming


def _moba(q, k, v, q_gain, k_gain, slopes):
    b, s, width = q.shape
    dh = MOBA_HEAD_DIM
    hps = MOBA_HEADS_PER_STEP
    lanes = hps * dh
    nb = s // MOBA_BLOCK
    seq_spec = pl.BlockSpec((1, s, lanes), lambda bi, hi: (bi, 0, hi))
    return pl.pallas_call(
        _moba_kernel,
        grid=(b, width // lanes),
        in_specs=[
            pl.BlockSpec(memory_space=pltpu.SMEM),
            seq_spec, seq_spec, seq_spec,
            pl.BlockSpec((1, lanes), lambda bi, hi: (0, 0)),
            pl.BlockSpec((1, lanes), lambda bi, hi: (0, 0)),
        ],
        out_specs=seq_spec,
        out_shape=jax.ShapeDtypeStruct((b, s, width), BF16),
        scratch_shapes=[
            pltpu.VMEM((s, lanes), BF16),
            pltpu.VMEM((nb, lanes), F32),
            pltpu.VMEM((lanes, s), BF16),
        ],
        compiler_params=_params(("parallel", "parallel")),
        name="moba_attention",
    )(slopes, q, k, v, q_gain, k_gain)


def _ssd_kernel(xbc_ref, z_ref, dt_ref, cw_ref, cb_ref, dtb_ref, alog_ref, dskip_ref, nw_ref,
                out_ref, hist_s, state_s):
    c = pl.program_id(1)
    L = SSM_CHUNK
    inner = SSM_HEADS * SSM_HEAD_DIM
    gw = SSM_STATE
    pair_w = 2 * SSM_HEAD_DIM

    @pl.when(c == 0)
    def _():
        hist_s[...] = jnp.zeros_like(hist_s)
        state_s[...] = jnp.zeros_like(state_s)

    row = lax.broadcasted_iota(jnp.int32, (L, L), 0)
    col = lax.broadcasted_iota(jnp.int32, (L, L), 1)
    causal = row >= col
    causal_ones = jnp.where(causal, 1.0, 0.0).astype(BF16)
    hrow = lax.broadcasted_iota(jnp.int32, (LANES, inner), 0)
    hcol = lax.broadcasted_iota(jnp.int32, (LANES, inner), 1)
    widen = jnp.where(hcol // SSM_HEAD_DIM == hrow, 1.0, 0.0).astype(BF16)
    lane = lax.broadcasted_iota(jnp.int32, (L, pair_w), 1)
    first_head = lane < SSM_HEAD_DIM
    heads_per_group = SSM_HEADS // SSM_GROUPS
    gsz = inner // SSM_GROUPS

    for sq in range(xbc_ref.shape[0]):
        cur = xbc_ref[sq].astype(F32)
        ext = jnp.concatenate([hist_s[sq], cur], axis=0)
        hist_s[sq] = cur[L - CONV_HALO:, :]
        act = _silu(_causal_conv(ext, cw_ref, SSM_CONV, CONV_HALO) + cb_ref[...])
        xs = act[:, :inner]

        dt = _softplus(dt_ref[sq] + dtb_ref[...])
        a = dt * (-jnp.exp(alog_ref[...]) * LOG2E)
        acum = _select_rows_dot(causal_ones, a)
        acum_t = acum.T
        widened = _select_cols_dot(jnp.concatenate([dt, acum], axis=0), widen)
        dt_w = widened[:L]
        acum_w = widened[L:]
        last_w = acum_w[L - 1:L, :]
        xdt = xs * dt_w
        xdt_bf = xdt.astype(BF16)
        xdt_dec_bf = (xdt * jnp.exp2(last_w - acum_w)).astype(BF16)
        exp_acum_w = jnp.exp2(acum_w)
        chunk_decay_w = jnp.exp2(last_w)

        ys = []
        for g in range(SSM_GROUPS):
            bg = act[:, inner + g * gw:inner + (g + 1) * gw]
            cg = act[:, inner + SSM_GROUPS * gw + g * gw:inner + SSM_GROUPS * gw + (g + 1) * gw]
            cg_bf = cg.astype(BF16)
            cb = _dot_nt(cg_bf, bg.astype(BF16))
            bg_t = bg.T.astype(BF16)
            for pp in range(heads_per_group // 2):
                h0 = g * heads_per_group + 2 * pp
                pi = h0 // 2
                sl = slice(h0 * SSM_HEAD_DIM, h0 * SSM_HEAD_DIM + pair_w)
                yd = []
                for hh in (h0, h0 + 1):
                    diff = jnp.broadcast_to(acum[:, hh:hh + 1], (L, L)) - acum_t[hh:hh + 1, :]
                    decay = jnp.exp2(jnp.where(causal, diff, NEG))
                    yd.append(_dot((cb * decay).astype(BF16), xdt_bf[:, sl]))
                y_diag = jnp.where(first_head, yd[0], yd[1])
                prev = state_s[sq, pi]
                y_off = _dot(cg_bf, prev.astype(BF16)) * exp_acum_w[:, sl]
                state_s[sq, pi] = prev * chunk_decay_w[:, sl] + _dot(bg_t, xdt_dec_bf[:, sl])
                ys.append(y_diag + y_off + dskip_ref[:, sl] * xs[:, sl])
        y = jnp.concatenate(ys, axis=1)

        yg = y * _silu(z_ref[sq].astype(F32))
        parts = []
        for g in range(SSM_GROUPS):
            seg = yg[:, g * gsz:(g + 1) * gsz]
            parts.append(seg * lax.rsqrt(jnp.mean(seg * seg, axis=-1, keepdims=True) + EPS))
        out_ref[sq] = (jnp.concatenate(parts, axis=1) * nw_ref[...]).astype(BF16)


def _ssd(xbc, z, dt, conv_w, conv_b, dt_bias, a_log, d_skip_w, norm_w):
    b, s, cd = xbc.shape
    inner = z.shape[2]
    nc = s // SSM_CHUNK
    nseq = SSM_SEQS_PER_STEP
    assert b % nseq == 0 and s % SSM_CHUNK == 0, (b, s)
    return pl.pallas_call(
        _ssd_kernel,
        grid=(b // nseq, nc),
        in_specs=[
            pl.BlockSpec((nseq, SSM_CHUNK, cd), lambda bi, ci: (bi, ci, 0)),
            pl.BlockSpec((nseq, SSM_CHUNK, inner), lambda bi, ci: (bi, ci, 0)),
            pl.BlockSpec((nseq, SSM_CHUNK, LANES), lambda bi, ci: (bi, ci, 0)),
            _const_spec(conv_w.shape), _const_spec(conv_b.shape), _const_spec(dt_bias.shape),
            _const_spec(a_log.shape), _const_spec(d_skip_w.shape), _const_spec(norm_w.shape),
        ],
        out_specs=pl.BlockSpec((nseq, SSM_CHUNK, inner), lambda bi, ci: (bi, ci, 0)),
        out_shape=jax.ShapeDtypeStruct((b, s, inner), BF16),
        scratch_shapes=[
            pltpu.VMEM((nseq, CONV_HALO, cd), F32),
            pltpu.VMEM((nseq, SSM_HEADS // 2, SSM_STATE, 2 * SSM_HEAD_DIM), F32),
        ],
        compiler_params=_params(("parallel", "arbitrary")),
        name="ssd_heads",
    )(xbc, z, dt, conv_w, conv_b, dt_bias, a_log, d_skip_w, norm_w)


def _outproj_even_kernel(x_ref, a_ref, s_ref, w_ref, o_ref):
    na = a_ref.shape[1]
    o_ref[...] = x_ref[...] + _dot(a_ref[...], w_ref[:na, :]) + _dot(s_ref[...], w_ref[na:, :])


def _outproj_even(x2, attn, ssm, w, tm=ROW_TILE):
    t, d = x2.shape
    return pl.pallas_call(
        _outproj_even_kernel,
        grid=(t // tm,),
        in_specs=[pl.BlockSpec((tm, d), lambda i: (i, 0)),
                  pl.BlockSpec((tm, attn.shape[1]), lambda i: (i, 0)),
                  pl.BlockSpec((tm, ssm.shape[1]), lambda i: (i, 0)),
                  _const_spec(w.shape)],
        out_specs=pl.BlockSpec((tm, d), lambda i: (i, 0)),
        out_shape=jax.ShapeDtypeStruct((t, d), F32),
        compiler_params=_params(("parallel",)),
        name="outproj_even",
    )(x2, attn, ssm, w)


def _outproj_odd_kernel(x_ref, o_in_ref, w_ref, o_ref):
    o_ref[...] = x_ref[...] + _dot(o_in_ref[...], w_ref[...])


def _outproj_odd(x2, o_in, w, tm=ROW_TILE):
    t, d = x2.shape
    return pl.pallas_call(
        _outproj_odd_kernel,
        grid=(t // tm,),
        in_specs=[pl.BlockSpec((tm, d), lambda i: (i, 0)),
                  pl.BlockSpec((tm, o_in.shape[1]), lambda i: (i, 0)),
                  _const_spec(w.shape)],
        out_specs=pl.BlockSpec((tm, d), lambda i: (i, 0)),
        out_shape=jax.ShapeDtypeStruct((t, d), F32),
        compiler_params=_params(("parallel",)),
        name="outproj_odd",
    )(x2, o_in, w)


def _ffn_ple_kernel(x_ref, halo_ref, p_ref, gf_ref, wg_ref, wu_ref, cw_ref, cb_ref, wd_ref,
                    gp_ref, wpg_ref, wpp_ref, o_ref, *, tm, seq):
    i = pl.program_id(0)
    x = x_ref[...]
    keep_halo = jnp.where((i * tm) % seq == 0, 0.0, 1.0)
    halo = halo_ref[...] * keep_halo
    hn = _rms_rows(jnp.concatenate([halo, x], axis=0), gf_ref[...]).astype(BF16)
    hn_cur = hn[FFN_HALO:]

    d_ff = wg_ref.shape[1]
    acts = []
    for j in range(d_ff // FFN_COLS):
        cs = slice(j * FFN_COLS, (j + 1) * FFN_COLS)
        g = _dot(hn, wg_ref[:, cs])
        gc = cb_ref[:, cs] + cw_ref[FFN_CONV - 1:FFN_CONV, cs] * g[FFN_HALO:]
        for shift in range(1, FFN_CONV):
            gc = gc + cw_ref[FFN_CONV - 1 - shift:FFN_CONV - shift, cs] * pltpu.roll(g, shift, axis=0)[FFN_HALO:]
        u = _dot(hn_cur, wu_ref[:, cs])
        acts.append((_silu(gc) * u).astype(BF16))
    x1 = x + _dot(jnp.concatenate(acts, axis=1), wd_ref[...])
    hp = _rms_rows(x1, gp_ref[...]).astype(BF16)
    gate = _sigmoid(_dot(hp, wpg_ref[...]))
    o_ref[...] = x1 + gate * _dot(p_ref[...].astype(BF16), wpp_ref[...])


def _ffn_ple(x2, p_all, layer, seq, gain_ffn, wg, wu, conv_w, conv_b, wd, gain_ple, wpg, wpp, tm=ROW_TILE):
    t, d = x2.shape
    halo_blocks = tm // FFN_HALO
    consts = (gain_ffn, wg, wu, conv_w, conv_b, wd, gain_ple, wpg, wpp)
    return pl.pallas_call(
        functools.partial(_ffn_ple_kernel, tm=tm, seq=seq),
        grid=(t // tm,),
        in_specs=[pl.BlockSpec((tm, d), lambda i: (i, 0)),
                  pl.BlockSpec((FFN_HALO, d), lambda i: (jnp.maximum(i * halo_blocks - 1, 0), 0)),
                  pl.BlockSpec((None, tm, p_all.shape[2]), lambda i: (layer, i, 0))]
        + [_const_spec(c.shape) for c in consts],
        out_specs=pl.BlockSpec((tm, d), lambda i: (i, 0)),
        out_shape=jax.ShapeDtypeStruct((t, d), F32),
        compiler_params=_params(("parallel",)),
        name="ffn_ple",
    )(x2, x2, p_all, *consts)


def _inproj_odd_kernel(x_ref, g_ref, w_ref, wba_ref, qkv_ref, z_ref, ba_ref):
    hn = _rms_rows(x_ref[...], g_ref[...]).astype(BF16)
    _project_columns(hn, w_ref, (qkv_ref, z_ref))
    ba_ref[...] = _dot(hn, wba_ref[...])


def _inproj_odd(x2, gain, w, wba, widths, tm=INPROJ_ROW_TILE):
    t, d = x2.shape
    outs = [(n, BF16) for n in widths] + [(wba.shape[1], F32)]
    return pl.pallas_call(
        _inproj_odd_kernel,
        grid=(t // tm,),
        in_specs=[pl.BlockSpec((tm, d), lambda i: (i, 0))] + [_const_spec(c.shape) for c in (gain, w, wba)],
        out_specs=[pl.BlockSpec((tm, n), lambda i: (i, 0)) for n, _ in outs],
        out_shape=[jax.ShapeDtypeStruct((t, n), dt) for n, dt in outs],
        compiler_params=_params(("parallel",)),
        name="inproj_odd",
    )(x2, gain, w, wba)


def _bdot(a, b):
    return _dot(a.astype(BF16), b.astype(BF16))


def _unit_lower_inverses(mats):
    L = mats[0].shape[0]
    row = lax.broadcasted_iota(jnp.int32, (L, L), 0)
    col = lax.broadcasted_iota(jnp.int32, (L, L), 1)
    eye = jnp.where(row == col, 1.0, 0.0)
    base = 8
    same = (row // base) == (col // base)
    ns = [-jnp.where(same, a, 0.0) for a in mats]
    ts = [eye + n for n in ns]
    ns_bf = [n.astype(BF16) for n in ns]
    pws_bf = [_dot(n, n).astype(BF16) for n in ns_bf]
    ts = [t + _dot(t.astype(BF16), pw) for t, pw in zip(ts, pws_bf)]
    pws_bf = [_dot(pw, pw).astype(BF16) for pw in pws_bf]
    ts = [t + _dot(t.astype(BF16), pw) for t, pw in zip(ts, pws_bf)]
    size = base
    while size < L:
        wider = (row // (2 * size)) == (col // (2 * size))
        offs = [jnp.where(wider & jnp.logical_not(same), a, 0.0).astype(BF16) for a in mats]
        ts_bf = [t.astype(BF16) for t in ts]
        ys = [_dot(off, t).astype(BF16) for off, t in zip(offs, ts_bf)]
        ts = [t - _dot(t_bf, y) for t, t_bf, y in zip(ts, ts_bf, ys)]
        same = wider
        size *= 2
    return ts


def _gdn_kernel(qkv_ref, z_ref, ba_ref, cw_ref, dtb_ref, alog_ref, nw_ref, out_ref, hist_s, state_s):
    c = pl.program_id(1)
    L = GDN_CHUNK
    dk = GDN_DIM
    qk_w = GDN_K_HEADS * dk
    rep = GDN_V_HEADS // GDN_K_HEADS
    nseq = qkv_ref.shape[0]

    @pl.when(c == 0)
    def _():
        hist_s[...] = jnp.zeros_like(hist_s)
        state_s[...] = jnp.zeros_like(state_s)

    row = lax.broadcasted_iota(jnp.int32, (L, L), 0)
    col = lax.broadcasted_iota(jnp.int32, (L, L), 1)
    lower = row >= col
    strict = row > col

    items = [(sq, hv) for sq in range(nseq) for hv in range(GDN_V_HEADS)]
    acts, qs, ks, kts, kk, qk0 = {}, {}, {}, {}, {}, {}
    beta_c, gc_c, gc_r = {}, {}, {}
    for sq in range(nseq):
        cur = qkv_ref[sq].astype(F32)
        ext = jnp.concatenate([hist_s[sq], cur], axis=0)
        hist_s[sq] = cur[L - CONV_HALO:, :]
        act = _silu(_causal_conv(ext, cw_ref, GDN_CONV, CONV_HALO))
        acts[sq] = act

        ba = ba_ref[sq]
        beta = _sigmoid(ba)
        g = (-jnp.exp(alog_ref[...]) * LOG2E) * _softplus(ba + dtb_ref[...])
        gc = _select_rows_dot(jnp.where(lower, 1.0, 0.0).astype(BF16), g)
        gc_t = gc.T
        for hv in range(GDN_V_HEADS):
            beta_c[sq, hv] = beta[:, hv:hv + 1]
            gc_c[sq, hv] = gc[:, GDN_V_HEADS + hv:GDN_V_HEADS + hv + 1]
            gc_r[sq, hv] = gc_t[GDN_V_HEADS + hv:GDN_V_HEADS + hv + 1, :]

        qk_act = act[:, :2 * qk_w]
        qk_n = qk_act * lax.rsqrt(_head_sums(qk_act * qk_act, dk) + EPS)
        for kh in range(GDN_K_HEADS):
            qn = qk_n[:, kh * dk:(kh + 1) * dk] * (dk ** -0.5)
            kn = qk_n[:, qk_w + kh * dk:qk_w + (kh + 1) * dk]
            kt = kn.T
            prod = _bdot(jnp.concatenate([kn, qn], axis=0), kt)
            qs[sq, kh], ks[sq, kh], kts[sq, kh] = qn, kn, kt
            kk[sq, kh], qk0[sq, kh] = prod[:L], prod[L:]

    def khead(it):
        return it[0], it[1] // rep

    outs = {}
    for g0 in range(0, len(items), GDN_CHAINS_PER_GROUP):
        group = items[g0:g0 + GDN_CHAINS_PER_GROUP]
        g_last = {it: gc_r[it][:, L - 1:L] for it in group}
        egc = {it: jnp.exp2(gc_c[it]) for it in group}
        decay = {it: jnp.exp2(jnp.where(lower, gc_c[it] - gc_r[it], NEG)) for it in group}
        a_mats = [jnp.where(strict, kk[khead(it)] * beta_c[it] * decay[it], 0.0) for it in group]
        t_inv = dict(zip(group, _unit_lower_inverses(a_mats)))

        sols = {}
        for it in group:
            sq, hv = it
            v_h = acts[sq][:, 2 * qk_w + hv * dk:2 * qk_w + (hv + 1) * dk]
            sols[it] = _bdot(t_inv[it], jnp.concatenate([v_h * beta_c[it],
                                                         ks[khead(it)] * (beta_c[it] * egc[it])], axis=1))

        states = {it: state_s[it[0], it[1]] for it in group}
        from_state = {it: _bdot(jnp.concatenate([sols[it][:, dk:], qs[khead(it)] * egc[it]], axis=0),
                                states[it]) for it in group}
        v_new = {it: sols[it][:, :dk] - from_state[it][:L] for it in group}
        from_v = {it: _bdot(jnp.concatenate([qk0[khead(it)] * decay[it],
                                             kts[khead(it)] * jnp.exp2(g_last[it] - gc_r[it])], axis=0),
                            v_new[it]) for it in group}
        for it in group:
            state_s[it[0], it[1]] = states[it] * jnp.exp2(g_last[it]) + from_v[it][L:]
            outs[it] = from_state[it][L:] + from_v[it][:L]
    for sq in range(nseq):
        o = jnp.concatenate([outs[sq, hv] for hv in range(GDN_V_HEADS)], axis=1)
        o = o * lax.rsqrt(_head_sums(o * o, dk) * (1.0 / dk) + EPS) * nw_ref[...]
        out_ref[sq] = (o * _silu(z_ref[sq].astype(F32))).astype(BF16)


def _gdn(qkv, z, ba, conv_w, dt_bias_pad, a_log_pad, norm_w):
    b, s, cd = qkv.shape
    vw = z.shape[2]
    nc = s // GDN_CHUNK
    nseq = GDN_SEQS_PER_STEP
    assert b % nseq == 0 and s % GDN_CHUNK == 0, (b, s)
    return pl.pallas_call(
        _gdn_kernel,
        grid=(b // nseq, nc),
        in_specs=[
            pl.BlockSpec((nseq, GDN_CHUNK, cd), lambda bi, ci: (bi, ci, 0)),
            pl.BlockSpec((nseq, GDN_CHUNK, vw), lambda bi, ci: (bi, ci, 0)),
            pl.BlockSpec((nseq, GDN_CHUNK, LANES), lambda bi, ci: (bi, ci, 0)),
            _const_spec(conv_w.shape), _const_spec(dt_bias_pad.shape),
            _const_spec(a_log_pad.shape), _const_spec(norm_w.shape),
        ],
        out_specs=pl.BlockSpec((nseq, GDN_CHUNK, vw), lambda bi, ci: (bi, ci, 0)),
        out_shape=jax.ShapeDtypeStruct((b, s, vw), BF16),
        scratch_shapes=[
            pltpu.VMEM((nseq, CONV_HALO, cd), F32),
            pltpu.VMEM((nseq, GDN_V_HEADS, GDN_DIM, GDN_DIM), F32),
        ],
        compiler_params=_params(("parallel", "arbitrary")),
        name="gated_deltanet",
    )(qkv, z, ba, conv_w, dt_bias_pad, a_log_pad, norm_w)


def _pad_lanes(v, offset=0):
    out = jnp.zeros((1, LANES), F32)
    return out.at[0, offset:offset + v.shape[0]].set(v.astype(F32))


def _row(v):
    return v.astype(F32).reshape(1, -1)


def _even_mixer(x2, batch, seq, norm_w, w_in, q_norm, k_norm, conv_w, conv_b, dt_bias, a_log, d_skip,
                ssm_norm, w_out, slopes):
    mw = MOBA_HEADS * MOBA_HEAD_DIM
    inner = SSM_HEADS * SSM_HEAD_DIM
    conv_dim = inner + 2 * SSM_GROUPS * SSM_STATE
    w = w_in.astype(BF16)
    widths = (mw, mw, mw, inner, conv_dim)
    wdt = jnp.pad(w[:, sum(widths):], ((0, 0), (0, LANES - SSM_HEADS)))
    q, k, v, z, xbc, dt = _inproj_even(x2, _row(norm_w), w, wdt, widths)
    attn = _moba(q.reshape(batch, seq, mw), k.reshape(batch, seq, mw), v.reshape(batch, seq, mw),
                 _row(jnp.tile(q_norm, MOBA_HEADS_PER_STEP)), _row(jnp.tile(k_norm, MOBA_HEADS_PER_STEP)),
                 slopes).reshape(batch * seq, mw)

    ssm = _ssd(xbc.reshape(batch, seq, conv_dim), z.reshape(batch, seq, inner),
               dt.reshape(batch, seq, LANES), conv_w.astype(F32), _row(conv_b),
               _pad_lanes(dt_bias), _pad_lanes(a_log), _row(jnp.repeat(d_skip, SSM_HEAD_DIM)),
               _row(ssm_norm))
    return _outproj_even(x2, attn, ssm.reshape(batch * seq, inner), w_out.astype(BF16))


def _gdn_mixer(x2, batch, seq, norm_w, w_in, conv_w, dt_bias, a_log, gdn_norm, w_out):
    conv_dim = 2 * GDN_K_HEADS * GDN_DIM + GDN_V_HEADS * GDN_DIM
    vw = GDN_V_HEADS * GDN_DIM
    w = w_in.astype(BF16)
    wba = jnp.pad(w[:, conv_dim + vw:], ((0, 0), (0, LANES - 2 * GDN_V_HEADS)))
    qkv, z, ba = _inproj_odd(x2, _row(norm_w), w, wba, (conv_dim, vw))
    o = _gdn(qkv.reshape(batch, seq, conv_dim), z.reshape(batch, seq, vw), ba.reshape(batch, seq, LANES),
             conv_w.astype(F32), _pad_lanes(dt_bias, GDN_V_HEADS), _pad_lanes(a_log, GDN_V_HEADS),
             _row(jnp.tile(gdn_norm, GDN_V_HEADS)))
    return _outproj_odd(x2, o.reshape(batch * seq, vw), w_out.astype(BF16))


def kernel(x, p, norm_mix, norm_ffn, norm_ple, w_in_even, moba_q_norm, moba_k_norm, ssm_conv_w, ssm_conv_b, ssm_dt_bias, ssm_a_log, ssm_d, ssm_norm, w_out_even, w_in_odd, gdn_conv_w, gdn_dt_bias, gdn_a_log, gdn_norm, w_out_odd, ffn_w_gate, ffn_w_up, ffn_conv_w, ffn_conv_b, ffn_w_down, ple_w_proj, ple_w_gate):
    batch, seq, d = x.shape
    depth = p.shape[0]
    assert seq % ROW_TILE == 0 and seq % MOBA_BLOCK == 0, seq
    slopes = jnp.exp2(-ALIBI_MAX_BIAS * jnp.arange(1, MOBA_HEADS + 1, dtype=F32) / MOBA_HEADS)
    x2 = x.reshape(batch * seq, d)
    p_all = p.reshape(depth, batch * seq, -1)
    for i in range(depth):
        j = i // 2
        if i % 2 == 0:
            x2 = _even_mixer(x2, batch, seq, norm_mix[i], w_in_even[j], moba_q_norm[j], moba_k_norm[j],
                             ssm_conv_w[j], ssm_conv_b[j], ssm_dt_bias[j], ssm_a_log[j], ssm_d[j],
                             ssm_norm[j], w_out_even[j], slopes)
        else:
            x2 = _gdn_mixer(x2, batch, seq, norm_mix[i], w_in_odd[j], gdn_conv_w[j], gdn_dt_bias[j],
                            gdn_a_log[j], gdn_norm[j], w_out_odd[j])
        x2 = _ffn_ple(x2, p_all, i, seq, _row(norm_ffn[i]),
                      ffn_w_gate[i].astype(BF16), ffn_w_up[i].astype(BF16), ffn_conv_w[i].astype(F32),
                      _row(ffn_conv_b[i]), ffn_w_down[i].astype(BF16), _row(norm_ple[i]),
                      ple_w_gate[i].astype(BF16), ple_w_proj[i].astype(BF16))
    return x2.reshape(batch, seq, d)
```
